```python
import math
import jax, jax.numpy as jnp
from jax import lax
import numpy as np

D_MODEL = 1024
BATCH = 2
SEQ = 8192
DEPTH = 1

N_META = 16
RET_HEADS = 4
RET_QK_HEAD = 128
RET_V_HEAD = 256
RET_CHUNK = 128
ROPE_BASE = 10000.0
HY_WIDTH = D_MODEL
HY_SHORT_CONV = 3
HY_EMB_DIM = 33
HY_FILTER_ORDER = 64
HY_FAST_DECAY_PCT = 0.3
HY_SLOW_DECAY_PCT = 1.5
HY_DECAY_TARGET = 1e-2
N_EXPERTS = 16
EC_CAPACITY = 2
D_FF_EXPERT = 2 * D_MODEL
RMS_EPS = 1e-6

RET_QK_W = RET_HEADS * RET_QK_HEAD
RET_V_W = RET_HEADS * RET_V_HEAD
IN_SPLITS = (RET_QK_W, RET_QK_W, RET_V_W, RET_V_W, 3 * HY_WIDTH, D_MODEL, D_MODEL)
IN_PROJ_W = sum(IN_SPLITS)

kernel_name = 'hybrid_retention_hyena_ecmoe_encoder'


def rms_norm(x, g):
    xf = x.astype(jnp.float32)
    y = xf * lax.rsqrt(jnp.mean(xf * xf, axis=-1, keepdims=True) + RMS_EPS)
    return (y * g.astype(jnp.float32)).astype(x.dtype)


def rotary(x, pos):
    half = x.shape[-1] // 2
    inv = ROPE_BASE ** (-jnp.arange(half, dtype=jnp.float32) / half)
    ang = pos[:, None] * inv[None, :]
    cos = jnp.cos(ang)[:, None, :].astype(x.dtype)
    sin = jnp.sin(ang)[:, None, :].astype(x.dtype)
    x1, x2 = x[..., :half], x[..., half:]
    return jnp.concatenate([x1 * cos - x2 * sin, x1 * sin + x2 * cos], axis=-1)


def bidir_retention(q, k, v, log_gf, log_gb):
    B, T, H, _ = q.shape
    dt = q.dtype
    C = RET_CHUNK
    pad = (-T) % C
    n = (T + pad) // C

    def to_chunks(a):
        a = jnp.pad(a, ((0, 0), (pad, 0), (0, 0), (0, 0)))
        return a.reshape(B, n, C, H, a.shape[-1]).transpose(0, 3, 1, 2, 4)

    qc, kc, vc = to_chunks(q), to_chunks(k), to_chunks(v)
    i = jnp.arange(C, dtype=jnp.float32)
    diff = i[:, None] - i[None, :]
    lf = log_gf[:, None, None]
    lb = log_gb[:, None, None]
    mask = jnp.exp(jnp.where(diff >= 0, lf * diff, -lb * diff)).astype(dt)
    scores = jnp.einsum('bhncd,bhned->bhnce', qc, kc) * mask[None, :, None]
    intra = jnp.einsum('bhnce,bhned->bhncd', scores, vc)

    w_end = jnp.exp(log_gf[:, None] * (C - 1 - i)[None, :]).astype(dt)
    w_start = jnp.exp(log_gb[:, None] * i[None, :]).astype(dt)
    a_f = jnp.einsum('bhncd,hc,bhnce->nbhde', kc, w_end, vc)
    a_b = jnp.einsum('bhncd,hc,bhnce->nbhde', kc, w_start, vc)
    dec_f = jnp.exp(log_gf * C).astype(dt)
    dec_b = jnp.exp(log_gb * C).astype(dt)

    def states(a, dec, reverse):
        def step(s, a_n):
            return s * dec[None, :, None, None] + a_n, s
        _, ss = lax.scan(step, jnp.zeros_like(a[0]), a, reverse=reverse)
        return ss

    s_f = states(a_f, dec_f, False)
    s_b = states(a_b, dec_b, True)
    qw_f = jnp.exp(log_gf[:, None] * (i + 1.0)[None, :]).astype(dt)
    qw_b = jnp.exp(log_gb[:, None] * (C - i)[None, :]).astype(dt)
    cross = (jnp.einsum('bhncd,hc,nbhde->bhnce', qc, qw_f, s_f)
             + jnp.einsum('bhncd,hc,nbhde->bhnce', qc, qw_b, s_b))
    o = (intra + cross).transpose(0, 2, 3, 1, 4).reshape(B, n * C, H, -1)
    return o[:, pad:]


def head_rms_norm(o, g):
    of = o.astype(jnp.float32)
    y = of * lax.rsqrt(jnp.mean(of * of, axis=-1, keepdims=True) + RMS_EPS)
    return (y * g.astype(jnp.float32).reshape(o.shape[2], o.shape[3])).astype(o.dtype)


def short_conv_centred(u, w, b):
    T = u.shape[1]
    half = HY_SHORT_CONV // 2
    up = jnp.pad(u, ((0, 0), (half, half), (0, 0)))
    return sum(up[:, j:j + T] * w[j] for j in range(HY_SHORT_CONV)) + b


def hyena_filters(T, w1, b1, w2, b2, w3, b3, freq, w4):
    f32 = jnp.float32
    t = jnp.arange(T, dtype=f32)
    t_norm = t / (T - 1)
    bands = (HY_EMB_DIM - 1) // 2
    fr = jnp.linspace(1e-4, bands - 1, bands, dtype=f32)
    ang = (2.0 * math.pi * t / T)[:, None] * fr[None, :]
    feat = jnp.concatenate([t_norm[:, None], jnp.cos(ang), -jnp.sin(ang)], axis=-1)
    fq = freq.astype(f32)
    hdn = jnp.sin(fq * (feat @ w1.astype(f32) + b1.astype(f32)))
    hdn = jnp.sin(fq * (hdn @ w2.astype(f32) + b2.astype(f32)))
    hdn = jnp.sin(fq * (hdn @ w3.astype(f32) + b3.astype(f32)))
    filt = (hdn @ w4.astype(f32)).reshape(T, 2, HY_WIDTH)
    max_decay = math.log(HY_DECAY_TARGET) / HY_FAST_DECAY_PCT
    min_decay = math.log(HY_DECAY_TARGET) / HY_SLOW_DECAY_PCT
    deltas = jnp.linspace(min_decay, max_decay, HY_WIDTH, dtype=f32)
    window = jnp.exp(-t_norm[:, None] * jnp.abs(deltas)[None, :])
    filt = filt * window[:, None, :]
    return filt[:, 0], filt[:, 1]


def bidir_fftconv(z, h_f, h_b):
    T = z.shape[1]
    n = 2 * T
    g = jnp.concatenate([h_f, jnp.zeros((1, h_f.shape[1]), jnp.float32), h_b[:0:-1]], axis=0)
    zf = jnp.fft.rfft(z.astype(jnp.float32), n=n, axis=1)
    gf = jnp.fft.rfft(g, n=n, axis=0)
    y = jnp.fft.irfft(zf * gf[None], n=n, axis=1)[:, :T]
    return y.astype(z.dtype)


def expert_choice_moe(xn, w_router, w_gate, w_up, w_down):
    B, T, _ = xn.shape
    cap = EC_CAPACITY * T // N_EXPERTS
    logits = jnp.einsum('btd,de->bte', xn, w_router).astype(jnp.float32)
    aff = jax.nn.softmax(logits, axis=-1)
    gate, idx = lax.top_k(jnp.swapaxes(aff, 1, 2), cap)
    bidx = jnp.arange(B)[:, None, None]
    xe = xn[bidx, idx]
    hid = jax.nn.silu(jnp.einsum('becd,edf->becf', xe, w_gate)) * jnp.einsum('becd,edf->becf', xe, w_up)
    ye = jnp.einsum('becf,efd->becd', hid, w_down) * gate[..., None].astype(xn.dtype)
    return jnp.zeros_like(xn).at[bidx, idx].add(ye)


def setup_inputs(seed: int = 0) -> dict:
    key = jax.random.key(seed)
    ks = iter(jax.random.split(key, 40))
    f32 = jnp.float32
    L = DEPTH

    def nrm(shape, scale):
        return jax.random.normal(next(ks), shape, f32) * scale

    hh = jnp.arange(RET_HEADS, dtype=f32)
    decay_base = jnp.log(2.0 ** (5.0 + hh) - 1.0)
    return {
        'x': nrm((BATCH, SEQ, D_MODEL), 1.0),
        'meta_tokens': nrm((N_META, D_MODEL), 1.0),
        'norm1_g': 1.0 + nrm((L, D_MODEL), 0.01),
        'w_in': nrm((L, D_MODEL, IN_PROJ_W), D_MODEL ** -0.5),
        'ret_decay_fwd': decay_base[None] + nrm((L, RET_HEADS), 0.1),
        'ret_decay_bwd': decay_base[None] + nrm((L, RET_HEADS), 0.1),
        'ret_head_norm_g': 1.0 + nrm((L, RET_V_W), 0.01),
        'w_ret_out': nrm((L, RET_V_W, D_MODEL), RET_V_W ** -0.5),
        'hy_conv_w': nrm((L, HY_SHORT_CONV, 3 * HY_WIDTH), HY_SHORT_CONV ** -0.5),
        'hy_conv_b': nrm((L, 3 * HY_WIDTH), 0.01),
        'hy_filt_w1': nrm((L, HY_EMB_DIM, HY_FILTER_ORDER), HY_EMB_DIM ** -0.5),
        'hy_filt_b1': nrm((L, HY_FILTER_ORDER), 0.01),
        'hy_filt_w2': nrm((L, HY_FILTER_ORDER, HY_FILTER_ORDER), HY_FILTER_ORDER ** -0.5),
        'hy_filt_b2': nrm((L, HY_FILTER_ORDER), 0.01),
        'hy_filt_w3': nrm((L, HY_FILTER_ORDER, HY_FILTER_ORDER), HY_FILTER_ORDER ** -0.5),
        'hy_filt_b3': nrm((L, HY_FILTER_ORDER), 0.01),
        'hy_filt_freq': 1.0 + nrm((L, HY_FILTER_ORDER), 0.01),
        'hy_filt_w4': nrm((L, HY_FILTER_ORDER, 2 * HY_WIDTH), 0.1 * HY_FILTER_ORDER ** -0.5),
        'hy_skip': nrm((L, HY_WIDTH), 1.0),
        'w_hy_out': nrm((L, HY_WIDTH, D_MODEL), HY_WIDTH ** -0.5),
        'w_o': nrm((L, D_MODEL, D_MODEL), D_MODEL ** -0.5),
        'norm2_g': 1.0 + nrm((L, D_MODEL), 0.01),
        'w_router': nrm((L, D_MODEL, N_EXPERTS), D_MODEL ** -0.5),
        'w_exp_gate': nrm((L, N_EXPERTS, D_MODEL, D_FF_EXPERT), D_MODEL ** -0.5),
        'w_exp_up': nrm((L, N_EXPERTS, D_MODEL, D_FF_EXPERT), D_MODEL ** -0.5),
        'w_exp_down': nrm((L, N_EXPERTS, D_FF_EXPERT, D_MODEL), D_FF_EXPERT ** -0.5),
        'final_norm_g': 1.0 + nrm((D_MODEL,), 0.01),
    }


def reference(x, meta_tokens, norm1_g, w_in, ret_decay_fwd, ret_decay_bwd, ret_head_norm_g, w_ret_out,
              hy_conv_w, hy_conv_b, hy_filt_w1, hy_filt_b1, hy_filt_w2, hy_filt_b2, hy_filt_w3, hy_filt_b3,
              hy_filt_freq, hy_filt_w4, hy_skip, w_hy_out, w_o, norm2_g, w_router, w_exp_gate, w_exp_up,
              w_exp_down, final_norm_g):
    B = x.shape[0]
    meta = jnp.broadcast_to(meta_tokens[None].astype(x.dtype), (B, N_META, D_MODEL))
    h = jnp.concatenate([meta, x], axis=1)
    T = h.shape[1]
    pos = jnp.arange(T, dtype=jnp.float32)
    offs = np.cumsum(IN_SPLITS)[:-1].tolist()
    for l in range(DEPTH):
        xn = rms_norm(h, norm1_g[l])
        proj = xn @ w_in[l]
        q, k, v, g_ret, u_hy, gate_a, gate_b = jnp.split(proj, offs, axis=-1)

        q = rotary(q.reshape(B, T, RET_HEADS, RET_QK_HEAD), pos)
        k = rotary(k.reshape(B, T, RET_HEADS, RET_QK_HEAD), pos) * (RET_QK_HEAD ** -0.5)
        v = v.reshape(B, T, RET_HEADS, RET_V_HEAD)
        o = bidir_retention(q, k, v,
                            jax.nn.log_sigmoid(ret_decay_fwd[l].astype(jnp.float32)),
                            jax.nn.log_sigmoid(ret_decay_bwd[l].astype(jnp.float32)))
        o = head_rms_norm(o, ret_head_norm_g[l]).reshape(B, T, RET_V_W)
        y_a = (jax.nn.silu(g_ret) * o) @ w_ret_out[l]

        uc = short_conv_centred(u_hy, hy_conv_w[l], hy_conv_b[l])
        x0, x1, vh = jnp.split(uc, 3, axis=-1)
        h_f, h_b = hyena_filters(T, hy_filt_w1[l], hy_filt_b1[l], hy_filt_w2[l], hy_filt_b2[l],
                                 hy_filt_w3[l], hy_filt_b3[l], hy_filt_freq[l], hy_filt_w4[l])
        z = x1 * vh
        y_b = (x0 * (bidir_fftconv(z, h_f, h_b) + z * hy_skip[l])) @ w_hy_out[l]

        mixed = jax.nn.sigmoid(gate_a) * y_a + jax.nn.sigmoid(gate_b) * y_b
        h = h + mixed @ w_o[l]

        h = h + expert_choice_moe(rms_norm(h, norm2_g[l]), w_router[l], w_exp_gate[l],
                                  w_exp_up[l], w_exp_down[l])
    return rms_norm(h, final_norm_g)[:, N_META:]
```

```python
import functools
import math

import numpy as np
import jax
import jax.numpy as jnp
from jax import lax
from jax.experimental import pallas as pl
from jax.experimental.pallas import tpu as pltpu

F32 = jnp.float32
BF16 = jnp.bfloat16
I32 = jnp.int32
HIGHEST = lax.Precision.HIGHEST

D_MODEL = 1024
N_META = 16
RET_HEADS = 4
RET_QK_HEAD = 128
RET_V_HEAD = 256
ROPE_BASE = 10000.0
HY_EMB_DIM = 33
HY_FILTER_ORDER = 64
HY_FAST_DECAY_PCT = 0.3
HY_SLOW_DECAY_PCT = 1.5
HY_DECAY_TARGET = 1e-2
N_EXPERTS = 16
EC_CAPACITY = 2
D_FF = 2 * D_MODEL
RMS_EPS = 1e-6
IN_PROJ_W = 8192

CH = 128
PAD = CH - N_META
NFFT = 16384
ZCH = 80
FAR = 16

VMEM_LIMIT_BIG = 56 * 1024 * 1024
VMEM_LIMIT_MID = 40 * 1024 * 1024


def _cparams(sem, vmem=VMEM_LIMIT_MID):
    return pltpu.CompilerParams(dimension_semantics=sem, vmem_limit_bytes=vmem)


IP_TM = 640
IP_TN = 512


def _inproj_kernel(x_ref, g_ref, w_ref, cs_ref, sn_ref, o_ref, xn_ref):
    j = pl.program_id(1)

    @pl.when(j == 0)
    def _():
        x = x_ref[...]
        ms = jnp.mean(x * x, axis=-1, keepdims=True)
        xn_ref[...] = (x * lax.rsqrt(ms + RMS_EPS) * g_ref[...]).astype(BF16)

    acc = jnp.dot(xn_ref[...], w_ref[...], preferred_element_type=F32)

    @pl.when(j < 2)
    def _():
        scale = jnp.where(j == 1, RET_QK_HEAD ** -0.5, 1.0).astype(F32)
        cs = cs_ref[...]
        sn = sn_ref[...]
        for hh in range(IP_TN // RET_QK_HEAD):
            xh = acc[:, hh * RET_QK_HEAD:(hh + 1) * RET_QK_HEAD]
            rot = xh * cs + pltpu.roll(xh, RET_QK_HEAD // 2, axis=1) * sn
            o_ref[:, hh * RET_QK_HEAD:(hh + 1) * RET_QK_HEAD] = (rot * scale).astype(BF16)

    @pl.when(jnp.logical_or(jnp.logical_and(j >= 2, j < 4), jnp.logical_and(j >= 6, j < 12)))
    def _():
        o_ref[...] = acc.astype(BF16)

    @pl.when(jnp.logical_and(j >= 4, j < 6))
    def _():
        o_ref[...] = (acc * jax.nn.sigmoid(acc)).astype(BF16)

    @pl.when(j >= 12)
    def _():
        o_ref[...] = jax.nn.sigmoid(acc).astype(BF16)


def _in_proj(h0, g1, w_in_bf, cs, sn, tp):
    n_rows = h0.shape[0]
    nt_b = tp // IP_TM
    return pl.pallas_call(
        _inproj_kernel,
        grid=(n_rows // IP_TM, IN_PROJ_W // IP_TN),
        in_specs=[
            pl.BlockSpec((IP_TM, D_MODEL), lambda i, j: (i, 0)),
            pl.BlockSpec((1, D_MODEL), lambda i, j: (0, 0)),
            pl.BlockSpec((D_MODEL, IP_TN), lambda i, j: (0, j)),
            pl.BlockSpec((IP_TM, RET_QK_HEAD), lambda i, j: (i % nt_b, 0)),
            pl.BlockSpec((IP_TM, RET_QK_HEAD), lambda i, j: (i % nt_b, 0)),
        ],
        out_specs=pl.BlockSpec((IP_TM, IP_TN), lambda i, j: (i, j)),
        out_shape=jax.ShapeDtypeStruct((n_rows, IN_PROJ_W), BF16),
        scratch_shapes=[pltpu.VMEM((IP_TM, D_MODEL), BF16)],
        compiler_params=_cparams(("arbitrary", "arbitrary")),
        name="in_proj",
    )(h0, g1, w_in_bf, cs, sn)


def _ret_kernel(lf_ref, lb_ref, q_ref, k_ref, v_ref, gr_ref, gn_ref, o_ref, ob_ref, s_ref, *, nch):
    h = pl.program_id(1)
    lf = lf_ref[h]
    lb = lb_ref[h]
    ri = lax.broadcasted_iota(I32, (CH, CH), 0).astype(F32)
    ci = lax.broadcasted_iota(I32, (CH, CH), 1).astype(F32)
    diff = ri - ci
    mask = jnp.exp(jnp.where(diff >= 0, lf * diff, -lb * diff))
    w_end = jnp.exp(lf * (CH - 1.0 - ri))
    w_start = jnp.exp(lb * ri)
    qw_f = jnp.exp(lf * (ri + 1.0))
    qw_b = jnp.exp(lb * (CH - ri))
    dec_f = jnp.exp(jnp.full((CH, RET_V_HEAD), lf * CH, F32))
    dec_b = jnp.exp(jnp.full((CH, RET_V_HEAD), lb * CH, F32))
    tn_dims = (((0,), (0,)), ((), ()))
    nt_dims = (((1,), (1,)), ((), ()))

    s_ref[...] = jnp.zeros_like(s_ref)

    def bwd(ii, carry):
        n = nch - 1 - ii
        r0 = pl.multiple_of(n * CH, CH)
        q = q_ref[0, pl.ds(r0, CH), :].astype(F32)
        k = k_ref[0, pl.ds(r0, CH), :].astype(F32)
        v = v_ref[0, pl.ds(r0, CH), :]
        s = s_ref[...]
        ob_ref[pl.ds(r0, CH), :] = jnp.dot((q * qw_b).astype(BF16), s.astype(BF16),
                                           preferred_element_type=F32)
        a = lax.dot_general((k * w_start).astype(BF16), v, tn_dims, preferred_element_type=F32)
        s_ref[...] = s * dec_b + a
        return carry

    lax.fori_loop(0, nch, bwd, 0)

    s_ref[...] = jnp.zeros_like(s_ref)
    gn = gn_ref[...]

    def fwd(n, carry):
        r0 = pl.multiple_of(n * CH, CH)
        qb = q_ref[0, pl.ds(r0, CH), :]
        kb = k_ref[0, pl.ds(r0, CH), :]
        v = v_ref[0, pl.ds(r0, CH), :]
        q = qb.astype(F32)
        k = kb.astype(F32)
        s = s_ref[...]
        scores = lax.dot_general(qb, kb, nt_dims, preferred_element_type=F32) * mask
        o = jnp.dot(scores.astype(BF16), v, preferred_element_type=F32)
        o = o + jnp.dot((q * qw_f).astype(BF16), s.astype(BF16), preferred_element_type=F32)
        o = o + ob_ref[pl.ds(r0, CH), :]
        a = lax.dot_general((k * w_end).astype(BF16), v, tn_dims, preferred_element_type=F32)
        s_ref[...] = s * dec_f + a
        y = o * lax.rsqrt(jnp.mean(o * o, axis=-1, keepdims=True) + RMS_EPS) * gn
        o_ref[0, pl.ds(r0, CH), :] = (y * gr_ref[0, pl.ds(r0, CH), :].astype(F32)).astype(BF16)
        return carry

    lax.fori_loop(0, nch, fwd, 0)


def _retention(proj3, lf, lb, gn, tp):
    b = proj3.shape[0]
    nch = tp // CH
    qk_blocks = (RET_HEADS * RET_QK_HEAD) // RET_QK_HEAD
    v_blk0 = (2 * RET_HEADS * RET_QK_HEAD) // RET_V_HEAD
    g_blk0 = v_blk0 + RET_HEADS
    smem = pl.BlockSpec(memory_space=pltpu.SMEM)
    return pl.pallas_call(
        functools.partial(_ret_kernel, nch=nch),
        grid=(b, RET_HEADS),
        in_specs=[
            smem, smem,
            pl.BlockSpec((1, tp, RET_QK_HEAD), lambda bi, h: (bi, 0, h)),
            pl.BlockSpec((1, tp, RET_QK_HEAD), lambda bi, h: (bi, 0, qk_blocks + h)),
            pl.BlockSpec((1, tp, RET_V_HEAD), lambda bi, h: (bi, 0, v_blk0 + h)),
            pl.BlockSpec((1, tp, RET_V_HEAD), lambda bi, h: (bi, 0, g_blk0 + h)),
            pl.BlockSpec((1, RET_V_HEAD), lambda bi, h: (0, h)),
        ],
        out_specs=pl.BlockSpec((1, tp, RET_V_HEAD), lambda bi, h: (bi, 0, h)),
        out_shape=jax.ShapeDtypeStruct((b, tp, RET_HEADS * RET_V_HEAD), BF16),
        scratch_shapes=[pltpu.VMEM((tp, RET_V_HEAD), F32), pltpu.VMEM((RET_QK_HEAD, RET_V_HEAD), F32)],
        compiler_params=_cparams(("arbitrary", "arbitrary"), VMEM_LIMIT_BIG),
        name="retention",
    )(lf, lb, proj3, proj3, proj3, proj3, gn)


HP_CW = 128


def _hyprep_kernel(u0_ref, u1_ref, u2_ref, w0_ref, w1_ref, w2_ref, b0_ref, b1_ref, b2_ref,
                   x0_ref, z_ref, *, nch, tz):
    rows = lax.broadcasted_iota(I32, (CH, HP_CW), 0)
    halo = 16

    def conv(u_ref, w_ref, b_ref, n, r0):
        cur = u_ref[0, pl.ds(r0, CH), :].astype(F32)
        rp = pl.multiple_of(jnp.maximum(r0 - halo, 0), halo)
        rn = pl.multiple_of(jnp.minimum(r0 + CH, (nch - 1) * CH), halo)
        prev = u_ref[0, pl.ds(rp, halo), :].astype(F32)[halo - 1:halo, :]
        nxt = u_ref[0, pl.ds(rn, halo), :].astype(F32)[0:1, :]
        prev = jnp.where(n > 0, prev, 0.0)
        nxt = jnp.where(n < nch - 1, nxt, 0.0)
        up = jnp.where(rows == 0, prev, pltpu.roll(cur, 1, axis=0))
        dn = jnp.where(rows == CH - 1, nxt, pltpu.roll(cur, CH - 1, axis=0))
        w = w_ref[...]
        return up * w[0:1, :] + cur * w[1:2, :] + dn * w[2:3, :] + b_ref[...]

    def body(n, carry):
        r0 = pl.multiple_of(n * CH, CH)
        x0 = conv(u0_ref, w0_ref, b0_ref, n, r0)
        x1 = conv(u1_ref, w1_ref, b1_ref, n, r0)
        vv = conv(u2_ref, w2_ref, b2_ref, n, r0)
        z = jnp.where(rows + r0 >= PAD, x1 * vv, 0.0)
        x0_ref[0, pl.ds(r0, CH), :] = x0.astype(BF16)
        z_ref[0, pl.ds(r0, CH), :] = z.astype(BF16)
        return carry

    lax.fori_loop(0, nch, body, 0)
    z_ref[0, nch * CH:tz, :] = jnp.zeros((tz - nch * CH, HP_CW), BF16)


def _hy_prep(proj3, conv_w, conv_b, tp):
    b = proj3.shape[0]
    nch = tp // CH
    tz = ZCH * CH
    ncb = D_MODEL // HP_CW
    u_blk0 = 3072 // HP_CW
    uspec = lambda s: pl.BlockSpec((1, tp, HP_CW), lambda bi, c: (bi, 0, u_blk0 + s * ncb + c))
    wspec = lambda s: pl.BlockSpec((3, HP_CW), lambda bi, c: (0, s * ncb + c))
    bspec = lambda s: pl.BlockSpec((1, HP_CW), lambda bi, c: (0, s * ncb + c))
    return pl.pallas_call(
        functools.partial(_hyprep_kernel, nch=nch, tz=tz),
        grid=(b, ncb),
        in_specs=[uspec(0), uspec(1), uspec(2), wspec(0), wspec(1), wspec(2), bspec(0), bspec(1), bspec(2)],
        out_specs=[pl.BlockSpec((1, tp, HP_CW), lambda bi, c: (bi, 0, c)),
                   pl.BlockSpec((1, tz, HP_CW), lambda bi, c: (bi, 0, c))],
        out_shape=[jax.ShapeDtypeStruct((b, tp, D_MODEL), BF16),
                   jax.ShapeDtypeStruct((b, tz, D_MODEL), BF16)],
        compiler_params=_cparams(("arbitrary", "arbitrary")),
        name="hy_prep",
    )(proj3, proj3, proj3, conv_w, conv_w, conv_w, conv_b, conv_b, conv_b)


FEAT_W = 128


def _filter_kernel(feat_ref, w1_ref, b1_ref, w2_ref, b2_ref, w3_ref, b3_ref, fq_ref, w4_ref, dl_ref, o_ref):
    feat = feat_ref[...]
    fq = fq_ref[...]
    hdn = jnp.sin(fq * (jnp.dot(feat, w1_ref[...], precision=HIGHEST, preferred_element_type=F32) + b1_ref[...]))
    hdn = jnp.sin(fq * (jnp.dot(hdn, w2_ref[...], precision=HIGHEST, preferred_element_type=F32) + b2_ref[...]))
    hdn = jnp.sin(fq * (jnp.dot(hdn, w3_ref[...], precision=HIGHEST, preferred_element_type=F32) + b3_ref[...]))
    filt = jnp.dot(hdn, w4_ref[...], precision=HIGHEST, preferred_element_type=F32)
    t_norm = feat[:, 0:1]
    valid = feat[:, HY_EMB_DIM:HY_EMB_DIM + 1]
    window = jnp.exp(-t_norm * dl_ref[...])
    o_ref[...] = filt * window * valid


def _filters(feat, w1p, b1, w2, b2, w3, b3, fq, w4, dl, dir_of_tile):
    n_rows = feat.shape[0]
    full = lambda a: pl.BlockSpec(a.shape, lambda i: (0,) * a.ndim)
    return pl.pallas_call(
        _filter_kernel,
        grid=(n_rows // CH,),
        in_specs=[pl.BlockSpec((CH, FEAT_W), lambda i: (i, 0)),
                  full(w1p), full(b1), full(w2), full(b2), full(w3), full(b3), full(fq),
                  pl.BlockSpec((HY_FILTER_ORDER, D_MODEL), lambda i: (0, dir_of_tile(i))),
                  full(dl)],
        out_specs=pl.BlockSpec((CH, D_MODEL), lambda i: (i, 0)),
        out_shape=jax.ShapeDtypeStruct((n_rows, D_MODEL), F32),
        compiler_params=_cparams(("arbitrary",)),
        name="hy_filter",
    )(feat, w1p, b1, w2, b2, w3, b3, fq, w4, dl)


def _fft_a_data_kernel(m_ref, z_ref, o_ref):
    x = jnp.concatenate([z_ref[0], z_ref[1]], axis=0)
    o_ref[0] = jnp.dot(m_ref[0], x, preferred_element_type=F32).astype(BF16)


def _fft_a_filt_kernel(m_ref, g_ref, o_ref):
    o_ref[0] = jnp.dot(m_ref[0], g_ref[...].astype(BF16), preferred_element_type=F32).astype(BF16)


def _fft_b_filt_kernel(f_ref, yr_ref, yi_ref, o_ref):
    y = jnp.concatenate([yr_ref[...], yi_ref[...]], axis=0)
    o_ref[0] = jnp.dot(f_ref[...], y, preferred_element_type=F32) * (1.0 / NFFT)


def _fft_b_data_kernel(f_ref, fi_ref, yr_ref, yi_ref, g_ref, o_ref):
    y = jnp.concatenate([yr_ref[...], yi_ref[...]], axis=0)
    x = jnp.dot(f_ref[...], y, preferred_element_type=F32)
    xr, xi = x[:CH], x[CH:]
    gr, gi = g_ref[0, :CH], g_ref[0, CH:]
    p = jnp.concatenate([xr * gr - xi * gi, xr * gi + xi * gr], axis=0).astype(BF16)
    o_ref[0] = jnp.dot(fi_ref[...], p, preferred_element_type=F32).astype(BF16)


def _fft_a_inv_kernel(m_ref, ur_ref, ui_ref, o_ref):
    u = jnp.concatenate([ur_ref[...], ui_ref[...]], axis=0)
    y = jnp.dot(m_ref[0], u, preferred_element_type=F32).astype(BF16)
    o_ref[0] = y[:ZCH]
    o_ref[1] = y[ZCH:]


def _dft_tables():
    n2 = np.arange(CH)[:, None, None]
    k1 = np.arange(CH)[None, :, None]

    def theta(n1_count):
        n1 = np.arange(n1_count)[None, None, :]
        return 2.0 * np.pi * ((k1 * (CH * n1 + n2)) % NFFT) / NFFT

    th = theta(ZCH)
    c, s = np.cos(th), np.sin(th)
    m_a = np.concatenate([np.concatenate([c, s], axis=2), np.concatenate([-s, c], axis=2)], axis=1)
    m_ainv = np.transpose(m_a, (0, 2, 1))
    th = theta(CH)
    m_af = np.concatenate([np.cos(th), -np.sin(th)], axis=1)
    a = 2.0 * np.pi * ((np.arange(CH)[:, None] * np.arange(CH)[None, :]) % CH) / CH
    c, s = np.cos(a), np.sin(a)
    f2 = np.block([[c, s], [-s, c]])
    f2i = np.block([[c, -s], [s, c]])
    f = lambda t: jnp.asarray(t.astype(np.float32)).astype(BF16)
    return f(m_a), f(m_ainv), f(m_af), f(f2), f(f2i)


def _fft_conv(z, g):
    c = z.shape[-1]
    m_a, m_ainv, m_af, f2, f2i = _dft_tables()
    cp = _cparams(("arbitrary",))
    full2 = pl.BlockSpec((2 * CH, 2 * CH), lambda i: (0, 0))
    col_re = pl.BlockSpec((CH, c), lambda i: (0, i))
    col_im = pl.BlockSpec((CH, c), lambda i: (0, CH + i))
    blk = pl.BlockSpec((1, 2 * CH, c), lambda i: (i, 0, 0))
    spec_shape = jax.ShapeDtypeStruct((CH, 2 * CH, c), BF16)

    yg = pl.pallas_call(
        _fft_a_filt_kernel, grid=(CH,),
        in_specs=[pl.BlockSpec((1, 2 * CH, CH), lambda i: (i, 0, 0)), col_re],
        out_specs=blk, out_shape=spec_shape, compiler_params=cp, name="fft_a_filt",
    )(m_af, g.reshape(CH, CH * c))
    yg2 = yg.reshape(CH, 2 * CH * c)
    gh = pl.pallas_call(
        _fft_b_filt_kernel, grid=(CH,),
        in_specs=[full2, col_re, col_im],
        out_specs=blk, out_shape=jax.ShapeDtypeStruct((CH, 2 * CH, c), F32),
        compiler_params=cp, name="fft_b_filt",
    )(f2, yg2, yg2)

    y = pl.pallas_call(
        _fft_a_data_kernel, grid=(CH,),
        in_specs=[pl.BlockSpec((1, 2 * CH, 2 * ZCH), lambda i: (i, 0, 0)),
                  pl.BlockSpec((2, ZCH, c), lambda i: (0, 0, i))],
        out_specs=blk, out_shape=spec_shape, compiler_params=cp, name="fft_a_data",
    )(m_a, z.reshape(2, ZCH, CH * c))
    y2 = y.reshape(CH, 2 * CH * c)
    u = pl.pallas_call(
        _fft_b_data_kernel, grid=(CH,),
        in_specs=[full2, full2, col_re, col_im, blk],
        out_specs=blk, out_shape=spec_shape, compiler_params=cp, name="fft_b_data",
    )(f2, f2i, y2, y2, gh)
    u2 = u.reshape(CH, 2 * CH * c)
    yc = pl.pallas_call(
        _fft_a_inv_kernel, grid=(CH,),
        in_specs=[pl.BlockSpec((1, 2 * ZCH, 2 * CH), lambda i: (i, 0, 0)), col_re, col_im],
        out_specs=pl.BlockSpec((2, ZCH, c), lambda i: (0, 0, i)),
        out_shape=jax.ShapeDtypeStruct((2, ZCH, CH * c), BF16),
        compiler_params=cp, name="fft_a_inv",
    )(m_ainv, u2, u2)
    return yc.reshape(2, ZCH * CH, c)


def _far_kernel(glo_ref, ghi_ref, hf_ref, hb_ref, zm_ref, zl_ref, o_ref):
    g_hi = ghi_ref[0:FAR, :]
    ef = hf_ref[0:FAR, :] - g_hi
    g_lo = glo_ref[CH - FAR:CH, :]
    hb = hb_ref[0:FAR, :]
    zm = zm_ref[0, PAD:CH, :].astype(F32)
    zl = zl_ref[0, PAD:CH, :].astype(F32)
    row = lambda a, i: a[i:i + 1, :]
    eb = [row(hb, 0) - row(g_hi, 0)] + [row(hb, m) - row(g_lo, FAR - m) for m in range(1, FAR)]
    for j in range(FAR):
        acc_f = row(ef, j) * row(zm, 0)
        for i in range(1, j + 1):
            acc_f = acc_f + row(ef, j - i) * row(zm, i)
        o_ref[0, 0, j:j + 1, :] = acc_f
        acc_b = eb[0] * row(zl, j)
        for i in range(j + 1, FAR):
            acc_b = acc_b + eb[i - j] * row(zl, i)
        o_ref[0, 1, j:j + 1, :] = acc_b


def _far_correction(g, h_far, z, tp):
    b, _, c = z.shape
    last = tp // CH - 1
    half = NFFT // 2 // CH
    return pl.pallas_call(
        _far_kernel, grid=(b,),
        in_specs=[pl.BlockSpec((CH, c), lambda bi: (half - 1, 0)),
                  pl.BlockSpec((CH, c), lambda bi: (half, 0)),
                  pl.BlockSpec((CH, c), lambda bi: (0, 0)),
                  pl.BlockSpec((CH, c), lambda bi: (1, 0)),
                  pl.BlockSpec((1, CH, c), lambda bi: (bi, 0, 0)),
                  pl.BlockSpec((1, CH, c), lambda bi: (bi, last, 0))],
        out_specs=pl.BlockSpec((1, 2, FAR, c), lambda bi: (bi, 0, 0, 0)),
        out_shape=jax.ShapeDtypeStruct((b, 2, FAR, c), F32),
        compiler_params=_cparams(("arbitrary",)),
        name="hy_far",
    )(g, g, h_far, h_far, z, z)


MG_TM = 640
ROUTER_W = 128


def _merge_kernel(h_ref, og_ref, x0_ref, z_ref, yc_ref, ga_ref, gb_ref, corr_ref, skip_ref,
                  wr_ref, wh_ref, wo_ref, g2_ref, wrt_ref,
                  h1_ref, xn_ref, aff_ref, pre_ref, *, nt_b):
    i = pl.program_id(0)
    ib = i % nt_b
    pre_ref[...] = yc_ref[0].astype(F32) + z_ref[0].astype(F32) * skip_ref[...]

    @pl.when(ib == 0)
    def _():
        pre_ref[PAD:CH, :] += corr_ref[0, 1]

    @pl.when(ib == nt_b - 1)
    def _():
        pre_ref[MG_TM - FAR:MG_TM, :] += corr_ref[0, 0]

    pre = (x0_ref[0].astype(F32) * pre_ref[...]).astype(BF16)
    ya = jnp.dot(og_ref[0], wr_ref[...], preferred_element_type=F32)
    yb = jnp.dot(pre, wh_ref[...], preferred_element_type=F32)
    mixed = (ga_ref[0].astype(F32) * ya + gb_ref[0].astype(F32) * yb).astype(BF16)
    h1 = h_ref[0] + jnp.dot(mixed, wo_ref[...], preferred_element_type=F32)
    h1_ref[0] = h1
    xn = h1 * lax.rsqrt(jnp.mean(h1 * h1, axis=-1, keepdims=True) + RMS_EPS) * g2_ref[...]
    xn_ref[0] = xn
    logits = jnp.dot(xn, wrt_ref[...], precision=HIGHEST, preferred_element_type=F32)
    lane = lax.broadcasted_iota(I32, logits.shape, 1)
    logits = jnp.where(lane < N_EXPERTS, logits, -jnp.inf)
    m = jnp.max(logits, axis=-1, keepdims=True)
    e = jnp.exp(logits - m)
    aff = e / jnp.sum(e, axis=-1, keepdims=True)
    rows = lax.broadcasted_iota(I32, logits.shape, 0) + ib * MG_TM
    aff_ref[0] = jnp.where(rows >= PAD, aff, -1.0)


def _merge(h0, og, x0c, z, yc, proj3, corr, skip, w_ret, w_hy, w_o, g2, w_router_p, tp):
    b = h0.shape[0]
    nt_b = tp // MG_TM
    ga_blk0 = 6144 // D_MODEL
    row = lambda w: pl.BlockSpec((1, MG_TM, w), lambda i: (i // nt_b, i % nt_b, 0))
    full = lambda a: pl.BlockSpec(a.shape, lambda i: (0,) * a.ndim)
    return pl.pallas_call(
        functools.partial(_merge_kernel, nt_b=nt_b),
        grid=(b * nt_b,),
        in_specs=[row(D_MODEL), row(D_MODEL), row(D_MODEL), row(D_MODEL), row(D_MODEL),
                  pl.BlockSpec((1, MG_TM, D_MODEL), lambda i: (i // nt_b, i % nt_b, ga_blk0)),
                  pl.BlockSpec((1, MG_TM, D_MODEL), lambda i: (i // nt_b, i % nt_b, ga_blk0 + 1)),
                  pl.BlockSpec((1, 2, FAR, D_MODEL), lambda i: (i // nt_b, 0, 0, 0)),
                  full(skip), full(w_ret), full(w_hy), full(w_o), full(g2), full(w_router_p)],
        out_specs=[row(D_MODEL), row(D_MODEL), row(ROUTER_W)],
        out_shape=[jax.ShapeDtypeStruct((b, tp, D_MODEL), F32),
                   jax.ShapeDtypeStruct((b, tp, D_MODEL), F32),
                   jax.ShapeDtypeStruct((b, tp, ROUTER_W), F32)],
        scratch_shapes=[pltpu.VMEM((MG_TM, D_MODEL), F32)],
        compiler_params=_cparams(("arbitrary",), VMEM_LIMIT_BIG),
        name="merge",
    )(h0, og, x0c, z, yc, proj3, proj3, corr, skip, w_ret, w_hy, w_o, g2, w_router_p)


TK_W = 128


def _select_kernel(aff_ref, low_ref, slot_ref, offs_ref, *, cap, nch):
    def chunk(c):
        r0 = pl.multiple_of(c * CH, CH)
        return aff_ref[0, pl.ds(r0, CH), :]

    def count(pred):
        def body(c, acc):
            return acc + pred(chunk(c)).astype(I32)

        acc = lax.fori_loop(0, nch, body, jnp.zeros((CH, ROUTER_W), I32))
        return jnp.sum(acc, axis=0, keepdims=True)

    def search(it, bits):
        cand = bits | jnp.left_shift(1, 29 - it)
        cand_f = pltpu.bitcast(cand, F32)
        return jnp.where(count(lambda a: a >= cand_f) >= cap, cand, bits)

    thr = pltpu.bitcast(lax.fori_loop(0, 30, search, jnp.zeros((1, ROUTER_W), I32)), F32)
    need = (cap - count(lambda a: a > thr)).astype(F32)
    low = low_ref[...]

    def scan(c, carry):
        c_eq, c_sel = carry
        a = chunk(c)
        eq = a == thr
        eq_f = eq.astype(F32)
        eq_rank = jnp.dot(low, eq_f, preferred_element_type=F32) + c_eq
        sel = jnp.logical_or(a > thr, jnp.logical_and(eq, eq_rank < need))
        sel_f = sel.astype(F32)
        slot = jnp.dot(low, sel_f, preferred_element_type=F32) + c_sel
        r0 = pl.multiple_of(c * CH, CH)
        slot_ref[0, pl.ds(r0, CH), :] = jnp.where(sel, slot, -1.0).astype(I32)
        offs_ref[0, c] = c_sel.astype(I32)
        return (c_eq + jnp.sum(eq_f, axis=0, keepdims=True), c_sel + jnp.sum(sel_f, axis=0, keepdims=True))

    zero = jnp.zeros((1, ROUTER_W), F32)
    lax.fori_loop(0, nch, scan, (zero, zero))


def _extract_kernel(offs_ref, slot_ref, aff_ref, idx_ref, gate_ref, *, nch, nwin):
    idx_ref[...] = jnp.zeros_like(idx_ref)
    gate_ref[...] = jnp.zeros_like(gate_ref)
    lane = lax.broadcasted_iota(I32, (CH, TK_W), 1)
    trow = lax.broadcasted_iota(I32, (CH, TK_W), 0)

    def per_chunk(c, carry):
        r0 = pl.multiple_of(c * CH, CH)
        slots = slot_ref[0, pl.ds(r0, CH), :]
        affs = aff_ref[0, pl.ds(r0, CH), :]
        tpos = (trow + r0).astype(F32)
        for e in range(N_EXPERTS):
            col = jnp.broadcast_to(slots[:, e:e + 1], (CH, TK_W))
            gcol = jnp.broadcast_to(affs[:, e:e + 1], (CH, TK_W))
            w0 = offs_ref[0, 0, c * N_EXPERTS + e] // TK_W
            for dw in range(2):
                w = w0 + dw
                base = jnp.where(w < nwin, w * TK_W, -2 * TK_W)
                hit = col == lane + base
                row = e * nwin + jnp.minimum(w, nwin - 1)
                idx_ref[row] += jnp.sum(jnp.where(hit, tpos, 0.0), axis=0, keepdims=True)
                gate_ref[row] += jnp.sum(jnp.where(hit, gcol, 0.0), axis=0, keepdims=True)
        return carry

    lax.fori_loop(0, nch, per_chunk, 0)


def _topk(aff, cap, slots):
    b, tp, _ = aff.shape
    nch = tp // CH
    nwin = -(-slots // TK_W)
    low = jnp.asarray(np.tril(np.ones((CH, CH), np.float32), k=-1))
    slot, offs = pl.pallas_call(
        functools.partial(_select_kernel, cap=cap, nch=nch),
        grid=(b,),
        in_specs=[pl.BlockSpec((1, tp, ROUTER_W), lambda bi: (bi, 0, 0)),
                  pl.BlockSpec((CH, CH), lambda bi: (0, 0))],
        out_specs=[pl.BlockSpec((1, tp, ROUTER_W), lambda bi: (bi, 0, 0)),
                   pl.BlockSpec((1, nch, 1, ROUTER_W), lambda bi: (bi, 0, 0, 0))],
        out_shape=[jax.ShapeDtypeStruct((b, tp, ROUTER_W), I32),
                   jax.ShapeDtypeStruct((b, nch, 1, ROUTER_W), I32)],
        compiler_params=_cparams(("arbitrary",)),
        name="topk_select",
    )(aff, low)
    offs_s = offs[:, :, 0, :N_EXPERTS].reshape(b, 1, nch * N_EXPERTS)
    rows = pl.BlockSpec((None, N_EXPERTS * nwin, 1, TK_W), lambda bi: (bi, 0, 0, 0))
    out = jax.ShapeDtypeStruct((b, N_EXPERTS * nwin, 1, TK_W), F32)
    idx, gate = pl.pallas_call(
        functools.partial(_extract_kernel, nch=nch, nwin=nwin),
        grid=(b,),
        in_specs=[pl.BlockSpec((1, 1, nch * N_EXPERTS), lambda bi: (bi, 0, 0), memory_space=pltpu.SMEM),
                  pl.BlockSpec((1, tp, ROUTER_W), lambda bi: (bi, 0, 0)),
                  pl.BlockSpec((1, tp, ROUTER_W), lambda bi: (bi, 0, 0))],
        out_specs=[rows, rows],
        out_shape=[out, out],
        compiler_params=_cparams(("arbitrary",)),
        name="topk_extract",
    )(offs_s, slot, aff)
    idx = idx.reshape(b * N_EXPERTS, 1, nwin * TK_W)[:, :, :slots].astype(I32)
    gate = gate.reshape(b * N_EXPERTS, nwin * TK_W, 1)[:, :slots]
    return idx, gate


FF_TF = 512


def _moe_kernel(idx_ref, xn_hbm, gate_ref, wg_ref, wu_ref, wd_ref, o_ref, xe32_ref, xe_ref, sem, *, slots):
    f = pl.program_id(1)

    def row_copy(s):
        return pltpu.make_async_copy(xn_hbm.at[pl.ds(idx_ref[0, 0, s], 1), :],
                                     xe32_ref.at[pl.ds(s, 1), :], sem)

    @pl.when(f == 0)
    def _():
        def issue(s, carry):
            row_copy(s).start()
            return carry

        lax.fori_loop(0, slots, issue, 0)

        def drain(s, carry):
            row_copy(s).wait()
            return carry

        lax.fori_loop(0, slots, drain, 0)
        xe_ref[...] = xe32_ref[...].astype(BF16)

    xe = xe_ref[...]
    gg = jnp.dot(xe, wg_ref[0].astype(BF16), preferred_element_type=F32)
    uu = jnp.dot(xe, wu_ref[0].astype(BF16), preferred_element_type=F32)
    hid = (gg * jax.nn.sigmoid(gg) * uu).astype(BF16)
    part = jnp.dot(hid, wd_ref[0].astype(BF16), preferred_element_type=F32)

    @pl.when(f == 0)
    def _():
        o_ref[0] = part

    @pl.when(f > 0)
    def _():
        o_ref[0] += part

    @pl.when(f == pl.num_programs(1) - 1)
    def _():
        o_ref[0] = o_ref[0] * gate_ref[0]


def _moe_ffn(idx, gate, xn_flat, w_gate, w_up, w_down, slots):
    be = idx.shape[0]
    nf = D_FF // FF_TF
    return pl.pallas_call(
        functools.partial(_moe_kernel, slots=slots),
        grid=(be, nf),
        in_specs=[pl.BlockSpec((1, 1, slots), lambda i, f: (i, 0, 0), memory_space=pltpu.SMEM),
                  pl.BlockSpec(memory_space=pl.ANY),
                  pl.BlockSpec((1, slots, 1), lambda i, f: (i, 0, 0)),
                  pl.BlockSpec((1, D_MODEL, FF_TF), lambda i, f: (i % N_EXPERTS, 0, f)),
                  pl.BlockSpec((1, D_MODEL, FF_TF), lambda i, f: (i % N_EXPERTS, 0, f)),
                  pl.BlockSpec((1, FF_TF, D_MODEL), lambda i, f: (i % N_EXPERTS, f, 0))],
        out_specs=pl.BlockSpec((1, slots, D_MODEL), lambda i, f: (i, 0, 0)),
        out_shape=jax.ShapeDtypeStruct((be, slots, D_MODEL), F32),
        scratch_shapes=[pltpu.VMEM((slots, D_MODEL), F32), pltpu.VMEM((slots, D_MODEL), BF16),
                        pltpu.SemaphoreType.DMA(())],
        compiler_params=_cparams(("arbitrary", "arbitrary"), VMEM_LIMIT_BIG),
        name="moe_ffn",
    )(idx, xn_flat, gate, w_gate, w_up, w_down)


CB_U = 6
CB_NB = 64


def _combine_kernel(idx_ref, h1_hbm, ye_ref, g_ref, o_hbm, acc_ref, sem, *, cap, tp):
    b = pl.program_id(0)
    e = pl.program_id(1)

    @pl.when(e == 0)
    def _():
        cp = pltpu.make_async_copy(h1_hbm.at[b], acc_ref, sem)
        cp.start()
        cp.wait()

    def rmw(g, carry):
        ts = [idx_ref[0, 0, g * CB_U + u] for u in range(CB_U)]
        vals = [acc_ref[ts[u]] + ye_ref[g * CB_U + u] for u in range(CB_U)]
        for u in range(CB_U):
            acc_ref[ts[u]] = vals[u]
        return carry

    lax.fori_loop(0, cap // CB_U, rmw, 0)

    @pl.when(e == pl.num_programs(1) - 1)
    def _():
        gamma = g_ref[...]

        def norm(r, carry):
            r0 = pl.multiple_of(r * CB_NB, CB_NB)
            x = acc_ref[pl.ds(r0, CB_NB)]
            ms = jnp.mean(x * x, axis=-1, keepdims=True)
            acc_ref[pl.ds(r0, CB_NB)] = x * lax.rsqrt(ms + RMS_EPS) * gamma
            return carry

        lax.fori_loop(CH // CB_NB, tp // CB_NB, norm, 0)
        cp = pltpu.make_async_copy(acc_ref.at[pl.ds(CH, tp - CH)], o_hbm.at[b], sem)
        cp.start()
        cp.wait()


def _combine(idx_local, h1, ye, gf, cap, slots):
    b, tp, _ = h1.shape
    assert cap % CB_U == 0
    return pl.pallas_call(
        functools.partial(_combine_kernel, cap=cap, tp=tp),
        grid=(b, N_EXPERTS),
        in_specs=[pl.BlockSpec((1, 1, slots), lambda bi, e: (bi * N_EXPERTS + e, 0, 0), memory_space=pltpu.SMEM),
                  pl.BlockSpec(memory_space=pl.ANY),
                  pl.BlockSpec((None, slots, 1, D_MODEL), lambda bi, e: (bi * N_EXPERTS + e, 0, 0, 0)),
                  pl.BlockSpec((1, 1, D_MODEL), lambda bi, e: (0, 0, 0))],
        out_specs=pl.BlockSpec(memory_space=pl.ANY),
        out_shape=jax.ShapeDtypeStruct((b, tp - CH, 1, D_MODEL), F32),
        scratch_shapes=[pltpu.VMEM((tp, 1, D_MODEL), F32), pltpu.SemaphoreType.DMA(())],
        compiler_params=_cparams(("arbitrary", "arbitrary"), VMEM_LIMIT_BIG),
        name="combine",
    )(idx_local, h1.reshape(b, tp, 1, D_MODEL), ye.reshape(ye.shape[0], slots, 1, D_MODEL),
      gf.reshape(1, 1, D_MODEL))


def _rope_tables(tp):
    half = RET_QK_HEAD // 2
    pos = jnp.arange(tp, dtype=F32) - float(PAD)
    inv = ROPE_BASE ** (-jnp.arange(half, dtype=F32) / half)
    ang = pos[:, None] * inv[None, :]
    cos, sin = jnp.cos(ang), jnp.sin(ang)
    return jnp.concatenate([cos, cos], axis=1), jnp.concatenate([-sin, sin], axis=1)


def _filter_features(t_len):
    half = NFFT // 2
    r = np.arange(NFFT)
    p_main = np.where(r < half, r, NFFT - r)
    valid_main = (r != half).astype(np.float32)
    p_far = half + np.arange(CH)
    p = jnp.asarray(np.concatenate([p_main, p_far, p_far]).astype(np.float32))
    valid = jnp.asarray(np.concatenate([valid_main, np.ones(2 * CH, np.float32)]))
    t_norm = p / (t_len - 1)
    bands = (HY_EMB_DIM - 1) // 2
    fr = jnp.linspace(1e-4, bands - 1, bands, dtype=F32)
    ang = (2.0 * math.pi * p / t_len)[:, None] * fr[None, :]
    feat = jnp.concatenate([t_norm[:, None], jnp.cos(ang), -jnp.sin(ang), valid[:, None]], axis=-1)
    return jnp.pad(feat, ((0, 0), (0, FEAT_W - feat.shape[1])))


def kernel(x, meta_tokens, norm1_g, w_in, ret_decay_fwd, ret_decay_bwd, ret_head_norm_g, w_ret_out,
           hy_conv_w, hy_conv_b, hy_filt_w1, hy_filt_b1, hy_filt_w2, hy_filt_b2, hy_filt_w3, hy_filt_b3,
           hy_filt_freq, hy_filt_w4, hy_skip, w_hy_out, w_o, norm2_g, w_router, w_exp_gate, w_exp_up,
           w_exp_down, final_norm_g):
    b, seq, d = x.shape
    t_len = seq + N_META
    tp = PAD + t_len
    assert d == D_MODEL and tp % IP_TM == 0 and tp % CH == 0 and t_len - NFFT // 2 == FAR
    cap = EC_CAPACITY * t_len // N_EXPERTS
    slots = -(-cap // 16) * 16
    l = 0

    meta = jnp.broadcast_to(meta_tokens[None].astype(x.dtype), (b, N_META, d))
    h0 = jnp.concatenate([jnp.zeros((b, PAD, d), x.dtype), meta, x], axis=1)

    cs, sn = _rope_tables(tp)
    proj = _in_proj(h0.reshape(b * tp, d), norm1_g[l][None], w_in[l].astype(BF16), cs, sn, tp)
    proj3 = proj.reshape(b, tp, IN_PROJ_W)

    lf = jax.nn.log_sigmoid(ret_decay_fwd[l].astype(F32))
    lb = jax.nn.log_sigmoid(ret_decay_bwd[l].astype(F32))
    og = _retention(proj3, lf, lb, ret_head_norm_g[l][None], tp)

    x0c, z = _hy_prep(proj3, hy_conv_w[l], hy_conv_b[l][None], tp)

    feat = _filter_features(t_len)
    w1p = jnp.pad(hy_filt_w1[l].astype(F32), ((0, FEAT_W - HY_EMB_DIM), (0, 0)))
    max_decay = math.log(HY_DECAY_TARGET) / HY_FAST_DECAY_PCT
    min_decay = math.log(HY_DECAY_TARGET) / HY_SLOW_DECAY_PCT
    dl = jnp.abs(jnp.linspace(min_decay, max_decay, D_MODEL, dtype=F32))[None]
    fargs = (w1p, hy_filt_b1[l][None].astype(F32), hy_filt_w2[l].astype(F32), hy_filt_b2[l][None].astype(F32),
             hy_filt_w3[l].astype(F32), hy_filt_b3[l][None].astype(F32), hy_filt_freq[l][None].astype(F32),
             hy_filt_w4[l].astype(F32), dl)
    half_tiles = NFFT // 2 // CH
    g = _filters(feat[:NFFT], *fargs, lambda i: i // half_tiles)
    h_far = _filters(feat[NFFT:], *fargs, lambda i: i)
    yc = _fft_conv(z, g)
    corr = _far_correction(g, h_far, z, tp)

    w_router_p = jnp.pad(w_router[l].astype(F32), ((0, 0), (0, ROUTER_W - N_EXPERTS)))
    h1, xn2, aff = _merge(h0, og, x0c, z, yc, proj3, corr, hy_skip[l][None].astype(F32),
                          w_ret_out[l].astype(BF16), w_hy_out[l].astype(BF16), w_o[l].astype(BF16),
                          norm2_g[l][None].astype(F32), w_router_p, tp)

    idx, gate = _topk(aff, cap, slots)
    live = (jnp.arange(slots) < cap)[None, None, :]
    idx_local = jnp.where(live, idx, PAD)
    idx_flat = idx_local + (jnp.arange(b * N_EXPERTS, dtype=I32) // N_EXPERTS * tp)[:, None, None]
    ye = _moe_ffn(idx_flat, gate, xn2.reshape(b * tp, d),
                  w_exp_gate[l], w_exp_up[l], w_exp_down[l], slots)
    out = _combine(idx_local, h1, ye, final_norm_g.astype(F32), cap, slots)
    return out.reshape(b, seq, d)
```

```python
import functools
import math

import numpy as np
import jax
import jax.numpy as jnp
from jax import lax
from jax.experimental import pallas as pl
from jax.experimental.pallas import tpu as pltpu

F32 = jnp.float32
BF16 = jnp.bfloat16
I32 = jnp.int32
HIGHEST = lax.Precision.HIGHEST

D_MODEL = 1024
N_META = 16
RET_HEADS = 4
RET_QK_HEAD = 128
RET_V_HEAD = 256
ROPE_BASE = 10000.0
HY_EMB_DIM = 33
HY_FILTER_ORDER = 64
HY_FAST_DECAY_PCT = 0.3
HY_SLOW_DECAY_PCT = 1.5
HY_DECAY_TARGET = 1e-2
N_EXPERTS = 16
EC_CAPACITY = 2
D_FF = 2 * D_MODEL
RMS_EPS = 1e-6
IN_PROJ_W = 8192

CH = 128
PAD = CH - N_META
NFFT = 16384
ZCH = 80
FAR = 16

VMEM_LIMIT_BIG = 56 * 1024 * 1024
VMEM_LIMIT_MID = 40 * 1024 * 1024


def _cparams(sem, vmem=VMEM_LIMIT_MID):
    return pltpu.CompilerParams(dimension_semantics=sem, vmem_limit_bytes=vmem)


IP_TM = 640
IP_TN = 512


def _inproj_kernel(x_ref, g_ref, w_ref, cs_ref, sn_ref, o_ref, xn_ref):
    j = pl.program_id(1)

    @pl.when(j == 0)
    def _():
        x = x_ref[...]
        ms = jnp.mean(x * x, axis=-1, keepdims=True)
        xn_ref[...] = (x * lax.rsqrt(ms + RMS_EPS) * g_ref[...]).astype(BF16)

    acc = jnp.dot(xn_ref[...], w_ref[...], preferred_element_type=F32)

    @pl.when(j < 2)
    def _():
        scale = jnp.where(j == 1, RET_QK_HEAD ** -0.5, 1.0).astype(F32)
        cs = cs_ref[...]
        sn = sn_ref[...]
        for hh in range(IP_TN // RET_QK_HEAD):
            xh = acc[:, hh * RET_QK_HEAD:(hh + 1) * RET_QK_HEAD]
            rot = xh * cs + pltpu.roll(xh, RET_QK_HEAD // 2, axis=1) * sn
            o_ref[:, hh * RET_QK_HEAD:(hh + 1) * RET_QK_HEAD] = (rot * scale).astype(BF16)

    @pl.when(jnp.logical_or(jnp.logical_and(j >= 2, j < 4), jnp.logical_and(j >= 6, j < 12)))
    def _():
        o_ref[...] = acc.astype(BF16)

    @pl.when(jnp.logical_and(j >= 4, j < 6))
    def _():
        o_ref[...] = (acc * jax.nn.sigmoid(acc)).astype(BF16)

    @pl.when(j >= 12)
    def _():
        o_ref[...] = jax.nn.sigmoid(acc).astype(BF16)


def _in_proj(h0, g1, w_in_bf, cs, sn, tp):
    n_rows = h0.shape[0]
    nt_b = tp // IP_TM
    return pl.pallas_call(
        _inproj_kernel,
        grid=(n_rows // IP_TM, IN_PROJ_W // IP_TN),
        in_specs=[
            pl.BlockSpec((IP_TM, D_MODEL), lambda i, j: (i, 0)),
            pl.BlockSpec((1, D_MODEL), lambda i, j: (0, 0)),
            pl.BlockSpec((D_MODEL, IP_TN), lambda i, j: (0, j)),
            pl.BlockSpec((IP_TM, RET_QK_HEAD), lambda i, j: (i % nt_b, 0)),
            pl.BlockSpec((IP_TM, RET_QK_HEAD), lambda i, j: (i % nt_b, 0)),
        ],
        out_specs=pl.BlockSpec((IP_TM, IP_TN), lambda i, j: (i, j)),
        out_shape=jax.ShapeDtypeStruct((n_rows, IN_PROJ_W), BF16),
        scratch_shapes=[pltpu.VMEM((IP_TM, D_MODEL), BF16)],
        compiler_params=_cparams(("arbitrary", "arbitrary")),
        name="in_proj",
    )(h0, g1, w_in_bf, cs, sn)


def _ret_kernel(lf_ref, lb_ref, q_ref, k_ref, v_ref, gr_ref, gn_ref, o_ref, ob_ref, s_ref, *, nch):
    h = pl.program_id(1)
    lf = lf_ref[h]
    lb = lb_ref[h]
    ri = lax.broadcasted_iota(I32, (CH, CH), 0).astype(F32)
    ci = lax.broadcasted_iota(I32, (CH, CH), 1).astype(F32)
    diff = ri - ci
    mask = jnp.exp(jnp.where(diff >= 0, lf * diff, -lb * diff))
    w_end = jnp.exp(lf * (CH - 1.0 - ri))
    w_start = jnp.exp(lb * ri)
    qw_f = jnp.exp(lf * (ri + 1.0))
    qw_b = jnp.exp(lb * (CH - ri))
    dec_f = jnp.exp(jnp.full((CH, RET_V_HEAD), lf * CH, F32))
    dec_b = jnp.exp(jnp.full((CH, RET_V_HEAD), lb * CH, F32))
    tn_dims = (((0,), (0,)), ((), ()))
    nt_dims = (((1,), (1,)), ((), ()))

    s_ref[...] = jnp.zeros_like(s_ref)

    def bwd(ii, carry):
        n = nch - 1 - ii
        r0 = pl.multiple_of(n * CH, CH)
        q = q_ref[0, pl.ds(r0, CH), :].astype(F32)
        k = k_ref[0, pl.ds(r0, CH), :].astype(F32)
        v = v_ref[0, pl.ds(r0, CH), :]
        s = s_ref[...]
        ob_ref[pl.ds(r0, CH), :] = jnp.dot((q * qw_b).astype(BF16), s.astype(BF16),
                                           preferred_element_type=F32)
        a = lax.dot_general((k * w_start).astype(BF16), v, tn_dims, preferred_element_type=F32)
        s_ref[...] = s * dec_b + a
        return carry

    lax.fori_loop(0, nch, bwd, 0)

    s_ref[...] = jnp.zeros_like(s_ref)
    gn = gn_ref[...]

    def fwd(n, carry):
        r0 = pl.multiple_of(n * CH, CH)
        qb = q_ref[0, pl.ds(r0, CH), :]
        kb = k_ref[0, pl.ds(r0, CH), :]
        v = v_ref[0, pl.ds(r0, CH), :]
        q = qb.astype(F32)
        k = kb.astype(F32)
        s = s_ref[...]
        scores = lax.dot_general(qb, kb, nt_dims, preferred_element_type=F32) * mask
        o = jnp.dot(scores.astype(BF16), v, preferred_element_type=F32)
        o = o + jnp.dot((q * qw_f).astype(BF16), s.astype(BF16), preferred_element_type=F32)
        o = o + ob_ref[pl.ds(r0, CH), :]
        a = lax.dot_general((k * w_end).astype(BF16), v, tn_dims, preferred_element_type=F32)
        s_ref[...] = s * dec_f + a
        y = o * lax.rsqrt(jnp.mean(o * o, axis=-1, keepdims=True) + RMS_EPS) * gn
        o_ref[0, pl.ds(r0, CH), :] = (y * gr_ref[0, pl.ds(r0, CH), :].astype(F32)).astype(BF16)
        return carry

    lax.fori_loop(0, nch, fwd, 0)


def _retention(proj3, lf, lb, gn, tp):
    b = proj3.shape[0]
    nch = tp // CH
    qk_blocks = (RET_HEADS * RET_QK_HEAD) // RET_QK_HEAD
    v_blk0 = (2 * RET_HEADS * RET_QK_HEAD) // RET_V_HEAD
    g_blk0 = v_blk0 + RET_HEADS
    smem = pl.BlockSpec(memory_space=pltpu.SMEM)
    return pl.pallas_call(
        functools.partial(_ret_kernel, nch=nch),
        grid=(b, RET_HEADS),
        in_specs=[
            smem, smem,
            pl.BlockSpec((1, tp, RET_QK_HEAD), lambda bi, h: (bi, 0, h)),
            pl.BlockSpec((1, tp, RET_QK_HEAD), lambda bi, h: (bi, 0, qk_blocks + h)),
            pl.BlockSpec((1, tp, RET_V_HEAD), lambda bi, h: (bi, 0, v_blk0 + h)),
            pl.BlockSpec((1, tp, RET_V_HEAD), lambda bi, h: (bi, 0, g_blk0 + h)),
            pl.BlockSpec((1, RET_V_HEAD), lambda bi, h: (0, h)),
        ],
        out_specs=pl.BlockSpec((1, tp, RET_V_HEAD), lambda bi, h: (bi, 0, h)),
        out_shape=jax.ShapeDtypeStruct((b, tp, RET_HEADS * RET_V_HEAD), BF16),
        scratch_shapes=[pltpu.VMEM((tp, RET_V_HEAD), F32), pltpu.VMEM((RET_QK_HEAD, RET_V_HEAD), F32)],
        compiler_params=_cparams(("arbitrary", "arbitrary"), VMEM_LIMIT_BIG),
        name="retention",
    )(lf, lb, proj3, proj3, proj3, proj3, gn)


HP_CW = 128


def _hyprep_kernel(u0_ref, u1_ref, u2_ref, w0_ref, w1_ref, w2_ref, b0_ref, b1_ref, b2_ref,
                   x0_ref, z_ref, *, nch, tz):
    rows = lax.broadcasted_iota(I32, (CH, HP_CW), 0)
    halo = 16

    def conv(u_ref, w_ref, b_ref, n, r0):
        cur = u_ref[0, pl.ds(r0, CH), :].astype(F32)
        rp = pl.multiple_of(jnp.maximum(r0 - halo, 0), halo)
        rn = pl.multiple_of(jnp.minimum(r0 + CH, (nch - 1) * CH), halo)
        prev = u_ref[0, pl.ds(rp, halo), :].astype(F32)[halo - 1:halo, :]
        nxt = u_ref[0, pl.ds(rn, halo), :].astype(F32)[0:1, :]
        prev = jnp.where(n > 0, prev, 0.0)
        nxt = jnp.where(n < nch - 1, nxt, 0.0)
        up = jnp.where(rows == 0, prev, pltpu.roll(cur, 1, axis=0))
        dn = jnp.where(rows == CH - 1, nxt, pltpu.roll(cur, CH - 1, axis=0))
        w = w_ref[...]
        return up * w[0:1, :] + cur * w[1:2, :] + dn * w[2:3, :] + b_ref[...]

    def body(n, carry):
        r0 = pl.multiple_of(n * CH, CH)
        x0 = conv(u0_ref, w0_ref, b0_ref, n, r0)
        x1 = conv(u1_ref, w1_ref, b1_ref, n, r0)
        vv = conv(u2_ref, w2_ref, b2_ref, n, r0)
        z = jnp.where(rows + r0 >= PAD, x1 * vv, 0.0)
        x0_ref[0, pl.ds(r0, CH), :] = x0.astype(BF16)
        z_ref[0, pl.ds(r0, CH), :] = z.astype(BF16)
        return carry

    lax.fori_loop(0, nch, body, 0)
    z_ref[0, nch * CH:tz, :] = jnp.zeros((tz - nch * CH, HP_CW), BF16)


def _hy_prep(proj3, conv_w, conv_b, tp):
    b = proj3.shape[0]
    nch = tp // CH
    tz = ZCH * CH
    ncb = D_MODEL // HP_CW
    u_blk0 = 3072 // HP_CW
    uspec = lambda s: pl.BlockSpec((1, tp, HP_CW), lambda bi, c: (bi, 0, u_blk0 + s * ncb + c))
    wspec = lambda s: pl.BlockSpec((3, HP_CW), lambda bi, c: (0, s * ncb + c))
    bspec = lambda s: pl.BlockSpec((1, HP_CW), lambda bi, c: (0, s * ncb + c))
    return pl.pallas_call(
        functools.partial(_hyprep_kernel, nch=nch, tz=tz),
        grid=(b, ncb),
        in_specs=[uspec(0), uspec(1), uspec(2), wspec(0), wspec(1), wspec(2), bspec(0), bspec(1), bspec(2)],
        out_specs=[pl.BlockSpec((1, tp, HP_CW), lambda bi, c: (bi, 0, c)),
                   pl.BlockSpec((1, tz, HP_CW), lambda bi, c: (bi, 0, c))],
        out_shape=[jax.ShapeDtypeStruct((b, tp, D_MODEL), BF16),
                   jax.ShapeDtypeStruct((b, tz, D_MODEL), BF16)],
        compiler_params=_cparams(("arbitrary", "arbitrary")),
        name="hy_prep",
    )(proj3, proj3, proj3, conv_w, conv_w, conv_w, conv_b, conv_b, conv_b)


FEAT_W = 128


FT_ROWS = 512
HALF = CH // 2


def _filter_kernel(feat_ref, w1_ref, b1_ref, w2_ref, b2_ref, w3_ref, b3_ref, fq_ref, w4_ref, dl_ref, o_ref,
                   *, groups):
    feat = feat_ref[...]
    fq = fq_ref[...]
    hdn = jnp.sin(fq * (jnp.dot(feat, w1_ref[...], precision=HIGHEST, preferred_element_type=F32) + b1_ref[...]))
    hdn = jnp.sin(fq * (jnp.dot(hdn, w2_ref[...], precision=HIGHEST, preferred_element_type=F32) + b2_ref[...]))
    hdn = jnp.sin(fq * (jnp.dot(hdn, w3_ref[...], precision=HIGHEST, preferred_element_type=F32) + b3_ref[...]))
    scale = jnp.exp(-feat[:, 0:1] * dl_ref[...]) * feat[:, HY_EMB_DIM:HY_EMB_DIM + 1]
    for d in range(2):
        rows = [slice(g * CH + d * HALF, g * CH + (d + 1) * HALF) for g in range(groups)]
        hd = jnp.concatenate([hdn[r] for r in rows], axis=0)
        filt = jnp.dot(hd, w4_ref[:, d * D_MODEL:(d + 1) * D_MODEL], precision=HIGHEST, preferred_element_type=F32)
        for g, r in enumerate(rows):
            o_ref[r, :] = filt[g * HALF:(g + 1) * HALF] * scale[r]


def _filters(feat, w1p, b1, w2, b2, w3, b3, fq, w4, dl, rows):
    n_rows = feat.shape[0]
    full = lambda a: pl.BlockSpec(a.shape, lambda i: (0,) * a.ndim)
    return pl.pallas_call(
        functools.partial(_filter_kernel, groups=rows // CH),
        grid=(n_rows // rows,),
        in_specs=[pl.BlockSpec((rows, FEAT_W), lambda i: (i, 0)),
                  full(w1p), full(b1), full(w2), full(b2), full(w3), full(b3), full(fq), full(w4), full(dl)],
        out_specs=pl.BlockSpec((rows, D_MODEL), lambda i: (i, 0)),
        out_shape=jax.ShapeDtypeStruct((n_rows, D_MODEL), F32),
        compiler_params=_cparams(("arbitrary",)),
        name="hy_filter",
    )(feat, w1p, b1, w2, b2, w3, b3, fq, w4, dl)


def _fft_a_data_kernel(m_ref, z_ref, o_ref):
    x = jnp.concatenate([z_ref[0], z_ref[1]], axis=0)
    o_ref[0] = jnp.dot(m_ref[0], x, preferred_element_type=F32).astype(BF16)


def _fft_a_filt_kernel(m_ref, g_ref, o_ref):
    o_ref[0] = jnp.dot(m_ref[0], g_ref[...].astype(BF16), preferred_element_type=F32).astype(BF16)


def _fft_b_filt_kernel(f_ref, yr_ref, yi_ref, o_ref):
    y = jnp.concatenate([yr_ref[...], yi_ref[...]], axis=0)
    o_ref[0] = jnp.dot(f_ref[...], y, preferred_element_type=F32) * (1.0 / NFFT)


def _fft_b_data_kernel(f_ref, fi_ref, yr_ref, yi_ref, g_ref, o_ref):
    y = jnp.concatenate([yr_ref[...], yi_ref[...]], axis=0)
    x = jnp.dot(f_ref[...], y, preferred_element_type=F32)
    xr, xi = x[:CH], x[CH:]
    gr, gi = g_ref[0, :CH], g_ref[0, CH:]
    p = jnp.concatenate([xr * gr - xi * gi, xr * gi + xi * gr], axis=0).astype(BF16)
    o_ref[0] = jnp.dot(fi_ref[...], p, preferred_element_type=F32).astype(BF16)


def _fft_a_inv_kernel(m_ref, ur_ref, ui_ref, o_ref):
    u = jnp.concatenate([ur_ref[...], ui_ref[...]], axis=0)
    y = jnp.dot(m_ref[0], u, preferred_element_type=F32).astype(BF16)
    o_ref[0] = y[:ZCH]
    o_ref[1] = y[ZCH:]


def _dft_tables():
    n2 = np.arange(CH)[:, None, None]
    k1 = np.arange(CH)[None, :, None]

    def theta(n1_count):
        n1 = np.arange(n1_count)[None, None, :]
        return 2.0 * np.pi * ((k1 * (CH * n1 + n2)) % NFFT) / NFFT

    th = theta(ZCH)
    c, s = np.cos(th), np.sin(th)
    m_a = np.concatenate([np.concatenate([c, s], axis=2), np.concatenate([-s, c], axis=2)], axis=1)
    m_ainv = np.transpose(m_a, (0, 2, 1))
    th = theta(CH)
    m_af = np.concatenate([np.cos(th), -np.sin(th)], axis=1)
    a = 2.0 * np.pi * ((np.arange(CH)[:, None] * np.arange(CH)[None, :]) % CH) / CH
    c, s = np.cos(a), np.sin(a)
    f2 = np.block([[c, s], [-s, c]])
    f2i = np.block([[c, -s], [s, c]])
    f = lambda t: jnp.asarray(t.astype(np.float32)).astype(BF16)
    return f(m_a), f(m_ainv), f(m_af), f(f2), f(f2i)


def _fft_conv(z, g):
    c = z.shape[-1]
    m_a, m_ainv, m_af, f2, f2i = _dft_tables()
    cp = _cparams(("arbitrary",))
    full2 = pl.BlockSpec((2 * CH, 2 * CH), lambda i: (0, 0))
    col_re = pl.BlockSpec((CH, c), lambda i: (0, i))
    col_im = pl.BlockSpec((CH, c), lambda i: (0, CH + i))
    blk = pl.BlockSpec((1, 2 * CH, c), lambda i: (i, 0, 0))
    spec_shape = jax.ShapeDtypeStruct((CH, 2 * CH, c), BF16)

    yg = pl.pallas_call(
        _fft_a_filt_kernel, grid=(CH,),
        in_specs=[pl.BlockSpec((1, 2 * CH, CH), lambda i: (i, 0, 0)), pl.BlockSpec((CH, c), lambda i: (i, 0))],
        out_specs=blk, out_shape=spec_shape, compiler_params=cp, name="fft_a_filt",
    )(m_af, g)
    yg2 = yg.reshape(CH, 2 * CH * c)
    gh = pl.pallas_call(
        _fft_b_filt_kernel, grid=(CH,),
        in_specs=[full2, col_re, col_im],
        out_specs=blk, out_shape=jax.ShapeDtypeStruct((CH, 2 * CH, c), F32),
        compiler_params=cp, name="fft_b_filt",
    )(f2, yg2, yg2)

    y = pl.pallas_call(
        _fft_a_data_kernel, grid=(CH,),
        in_specs=[pl.BlockSpec((1, 2 * CH, 2 * ZCH), lambda i: (i, 0, 0)),
                  pl.BlockSpec((2, ZCH, c), lambda i: (0, 0, i))],
        out_specs=blk, out_shape=spec_shape, compiler_params=cp, name="fft_a_data",
    )(m_a, z.reshape(2, ZCH, CH * c))
    y2 = y.reshape(CH, 2 * CH * c)
    u = pl.pallas_call(
        _fft_b_data_kernel, grid=(CH,),
        in_specs=[full2, full2, col_re, col_im, blk],
        out_specs=blk, out_shape=spec_shape, compiler_params=cp, name="fft_b_data",
    )(f2, f2i, y2, y2, gh)
    u2 = u.reshape(CH, 2 * CH * c)
    yc = pl.pallas_call(
        _fft_a_inv_kernel, grid=(CH,),
        in_specs=[pl.BlockSpec((1, 2 * ZCH, 2 * CH), lambda i: (i, 0, 0)), col_re, col_im],
        out_specs=pl.BlockSpec((2, ZCH, c), lambda i: (0, 0, i)),
        out_shape=jax.ShapeDtypeStruct((2, ZCH, CH * c), BF16),
        compiler_params=cp, name="fft_a_inv",
    )(m_ainv, u2, u2)
    return yc.reshape(2, ZCH * CH, c)


def _far_kernel(far_ref, zm_ref, zl_ref, o_ref):
    g_hi = far_ref[HALF + FAR:HALF + 2 * FAR, :]
    ef = far_ref[0:FAR, :] - g_hi
    g_lo = far_ref[FAR:2 * FAR, :]
    hb = far_ref[HALF:HALF + FAR, :]
    zm = zm_ref[0, PAD:CH, :].astype(F32)
    zl = zl_ref[0, PAD:CH, :].astype(F32)
    row = lambda a, i: a[i:i + 1, :]
    eb = [row(hb, 0) - row(g_hi, 0)] + [row(hb, m) - row(g_lo, FAR - m) for m in range(1, FAR)]
    for j in range(FAR):
        acc_f = row(ef, j) * row(zm, 0)
        for i in range(1, j + 1):
            acc_f = acc_f + row(ef, j - i) * row(zm, i)
        o_ref[0, 0, j:j + 1, :] = acc_f
        acc_b = eb[0] * row(zl, j)
        for i in range(j + 1, FAR):
            acc_b = acc_b + eb[i - j] * row(zl, i)
        o_ref[0, 1, j:j + 1, :] = acc_b


def _far_correction(h_far, z, tp):
    b, _, c = z.shape
    last = tp // CH - 1
    return pl.pallas_call(
        _far_kernel, grid=(b,),
        in_specs=[pl.BlockSpec((CH, c), lambda bi: (0, 0)),
                  pl.BlockSpec((1, CH, c), lambda bi: (bi, 0, 0)),
                  pl.BlockSpec((1, CH, c), lambda bi: (bi, last, 0))],
        out_specs=pl.BlockSpec((1, 2, FAR, c), lambda bi: (bi, 0, 0, 0)),
        out_shape=jax.ShapeDtypeStruct((b, 2, FAR, c), F32),
        compiler_params=_cparams(("arbitrary",)),
        name="hy_far",
    )(h_far, z, z)


MG_TM = 640
ROUTER_W = 128


def _merge_kernel(h_ref, og_ref, x0_ref, z_ref, yc_ref, ga_ref, gb_ref, corr_ref, skip_ref,
                  wr_ref, wh_ref, wo_ref, g2_ref, wrt_ref,
                  h1_ref, xn_ref, aff_ref, pre_ref, *, nt_b):
    i = pl.program_id(0)
    ib = i % nt_b
    pre_ref[...] = yc_ref[0].astype(F32) + z_ref[0].astype(F32) * skip_ref[...]

    @pl.when(ib == 0)
    def _():
        pre_ref[PAD:CH, :] += corr_ref[0, 1]

    @pl.when(ib == nt_b - 1)
    def _():
        pre_ref[MG_TM - FAR:MG_TM, :] += corr_ref[0, 0]

    pre = (x0_ref[0].astype(F32) * pre_ref[...]).astype(BF16)
    ya = jnp.dot(og_ref[0], wr_ref[...], preferred_element_type=F32)
    yb = jnp.dot(pre, wh_ref[...], preferred_element_type=F32)
    mixed = (ga_ref[0].astype(F32) * ya + gb_ref[0].astype(F32) * yb).astype(BF16)
    h1 = h_ref[0] + jnp.dot(mixed, wo_ref[...], preferred_element_type=F32)
    h1_ref[0] = h1
    xn = h1 * lax.rsqrt(jnp.mean(h1 * h1, axis=-1, keepdims=True) + RMS_EPS) * g2_ref[...]
    xn_ref[0] = xn
    logits = jnp.dot(xn, wrt_ref[...], precision=HIGHEST, preferred_element_type=F32)
    lane = lax.broadcasted_iota(I32, logits.shape, 1)
    logits = jnp.where(lane < N_EXPERTS, logits, -jnp.inf)
    m = jnp.max(logits, axis=-1, keepdims=True)
    e = jnp.exp(logits - m)
    aff = e / jnp.sum(e, axis=-1, keepdims=True)
    rows = lax.broadcasted_iota(I32, logits.shape, 0) + ib * MG_TM
    aff_ref[0] = jnp.where(rows >= PAD, aff, -1.0)


def _merge(h0, og, x0c, z, yc, proj3, corr, skip, w_ret, w_hy, w_o, g2, w_router_p, tp):
    b = h0.shape[0]
    nt_b = tp // MG_TM
    ga_blk0 = 6144 // D_MODEL
    row = lambda w: pl.BlockSpec((1, MG_TM, w), lambda i: (i // nt_b, i % nt_b, 0))
    full = lambda a: pl.BlockSpec(a.shape, lambda i: (0,) * a.ndim)
    return pl.pallas_call(
        functools.partial(_merge_kernel, nt_b=nt_b),
        grid=(b * nt_b,),
        in_specs=[row(D_MODEL), row(D_MODEL), row(D_MODEL), row(D_MODEL), row(D_MODEL),
                  pl.BlockSpec((1, MG_TM, D_MODEL), lambda i: (i // nt_b, i % nt_b, ga_blk0)),
                  pl.BlockSpec((1, MG_TM, D_MODEL), lambda i: (i // nt_b, i % nt_b, ga_blk0 + 1)),
                  pl.BlockSpec((1, 2, FAR, D_MODEL), lambda i: (i // nt_b, 0, 0, 0)),
                  full(skip), full(w_ret), full(w_hy), full(w_o), full(g2), full(w_router_p)],
        out_specs=[row(D_MODEL), row(D_MODEL), row(ROUTER_W)],
        out_shape=[jax.ShapeDtypeStruct((b, tp, D_MODEL), F32),
                   jax.ShapeDtypeStruct((b, tp, D_MODEL), F32),
                   jax.ShapeDtypeStruct((b, tp, ROUTER_W), F32)],
        scratch_shapes=[pltpu.VMEM((MG_TM, D_MODEL), F32)],
        compiler_params=_cparams(("arbitrary",), VMEM_LIMIT_BIG),
        name="merge",
    )(h0, og, x0c, z, yc, proj3, proj3, corr, skip, w_ret, w_hy, w_o, g2, w_router_p)


TK_W = 128


def _select_kernel(aff_ref, low_ref, slot_ref, offs_ref, *, cap, nch):
    def chunk(c):
        r0 = pl.multiple_of(c * CH, CH)
        return aff_ref[0, pl.ds(r0, CH), :]

    def count(pred):
        def body(c, acc):
            return acc + pred(chunk(c)).astype(I32)

        acc = lax.fori_loop(0, nch, body, jnp.zeros((CH, ROUTER_W), I32))
        return jnp.sum(acc, axis=0, keepdims=True)

    def search(it, bits):
        cand = bits | jnp.left_shift(1, 29 - it)
        cand_f = pltpu.bitcast(cand, F32)
        return jnp.where(count(lambda a: a >= cand_f) >= cap, cand, bits)

    thr = pltpu.bitcast(lax.fori_loop(0, 30, search, jnp.zeros((1, ROUTER_W), I32)), F32)
    need = (cap - count(lambda a: a > thr)).astype(F32)
    low = low_ref[...]

    def scan(c, carry):
        c_eq, c_sel = carry
        a = chunk(c)
        eq = a == thr
        eq_f = eq.astype(F32)
        eq_rank = jnp.dot(low, eq_f, preferred_element_type=F32) + c_eq
        sel = jnp.logical_or(a > thr, jnp.logical_and(eq, eq_rank < need))
        sel_f = sel.astype(F32)
        slot = jnp.dot(low, sel_f, preferred_element_type=F32) + c_sel
        r0 = pl.multiple_of(c * CH, CH)
        slot_ref[0, pl.ds(r0, CH), :] = jnp.where(sel, slot, -1.0).astype(I32)
        offs_ref[0, c] = c_sel.astype(I32)
        return (c_eq + jnp.sum(eq_f, axis=0, keepdims=True), c_sel + jnp.sum(sel_f, axis=0, keepdims=True))

    zero = jnp.zeros((1, ROUTER_W), F32)
    lax.fori_loop(0, nch, scan, (zero, zero))


def _extract_kernel(offs_ref, slot_ref, aff_ref, idx_ref, gate_ref, *, nch, nwin):
    idx_ref[...] = jnp.zeros_like(idx_ref)
    gate_ref[...] = jnp.zeros_like(gate_ref)
    lane = lax.broadcasted_iota(I32, (CH, TK_W), 1)
    trow = lax.broadcasted_iota(I32, (CH, TK_W), 0)

    def per_chunk(c, carry):
        r0 = pl.multiple_of(c * CH, CH)
        slots = slot_ref[0, pl.ds(r0, CH), :]
        affs = aff_ref[0, pl.ds(r0, CH), :]
        tpos = (trow + r0).astype(F32)
        for e in range(N_EXPERTS):
            col = jnp.broadcast_to(slots[:, e:e + 1], (CH, TK_W))
            gcol = jnp.broadcast_to(affs[:, e:e + 1], (CH, TK_W))
            w0 = offs_ref[0, 0, c * N_EXPERTS + e] // TK_W
            for dw in range(2):
                w = w0 + dw
                base = jnp.where(w < nwin, w * TK_W, -2 * TK_W)
                hit = col == lane + base
                row = e * nwin + jnp.minimum(w, nwin - 1)
                idx_ref[row] += jnp.sum(jnp.where(hit, tpos, 0.0), axis=0, keepdims=True)
                gate_ref[row] += jnp.sum(jnp.where(hit, gcol, 0.0), axis=0, keepdims=True)
        return carry

    lax.fori_loop(0, nch, per_chunk, 0)


def _topk(aff, cap, slots):
    b, tp, _ = aff.shape
    nch = tp // CH
    nwin = -(-slots // TK_W)
    low = jnp.asarray(np.tril(np.ones((CH, CH), np.float32), k=-1))
    slot, offs = pl.pallas_call(
        functools.partial(_select_kernel, cap=cap, nch=nch),
        grid=(b,),
        in_specs=[pl.BlockSpec((1, tp, ROUTER_W), lambda bi: (bi, 0, 0)),
                  pl.BlockSpec((CH, CH), lambda bi: (0, 0))],
        out_specs=[pl.BlockSpec((1, tp, ROUTER_W), lambda bi: (bi, 0, 0)),
                   pl.BlockSpec((1, nch, 1, ROUTER_W), lambda bi: (bi, 0, 0, 0))],
        out_shape=[jax.ShapeDtypeStruct((b, tp, ROUTER_W), I32),
                   jax.ShapeDtypeStruct((b, nch, 1, ROUTER_W), I32)],
        compiler_params=_cparams(("arbitrary",)),
        name="topk_select",
    )(aff, low)
    offs_s = offs[:, :, 0, :N_EXPERTS].reshape(b, 1, nch * N_EXPERTS)
    rows = pl.BlockSpec((None, N_EXPERTS * nwin, 1, TK_W), lambda bi: (bi, 0, 0, 0))
    out = jax.ShapeDtypeStruct((b, N_EXPERTS * nwin, 1, TK_W), F32)
    idx, gate = pl.pallas_call(
        functools.partial(_extract_kernel, nch=nch, nwin=nwin),
        grid=(b,),
        in_specs=[pl.BlockSpec((1, 1, nch * N_EXPERTS), lambda bi: (bi, 0, 0), memory_space=pltpu.SMEM),
                  pl.BlockSpec((1, tp, ROUTER_W), lambda bi: (bi, 0, 0)),
                  pl.BlockSpec((1, tp, ROUTER_W), lambda bi: (bi, 0, 0))],
        out_specs=[rows, rows],
        out_shape=[out, out],
        compiler_params=_cparams(("arbitrary",)),
        name="topk_extract",
    )(offs_s, slot, aff)
    idx = idx.reshape(b * N_EXPERTS, 1, nwin * TK_W)[:, :, :slots].astype(I32)
    gate = gate.reshape(b * N_EXPERTS, nwin * TK_W, 1)[:, :slots]
    return idx, gate


FF_TF = 512


MOE_UNROLL = 4


def _moe_kernel(idx_ref, idx_next_ref, xn_hbm, gate_ref, wg_ref, wu_ref, wd_ref, o_ref, xe32_ref, xe_ref, sem,
                *, slots, nf):
    i = pl.program_id(0)
    f = pl.program_id(1)
    buf = i % 2
    share = slots // nf

    def row_copy(ids_ref, s, b):
        return pltpu.make_async_copy(xn_hbm.at[pl.ds(ids_ref[0, 0, s], 1), :],
                                     xe32_ref.at[b, pl.ds(s, 1), :], sem.at[b])

    def for_rows(lo, n, fn):
        def body(k, carry):
            for u in range(MOE_UNROLL):
                fn(lo + k * MOE_UNROLL + u)
            return carry

        lax.fori_loop(0, n // MOE_UNROLL, body, 0)

    @pl.when(jnp.logical_and(i == 0, f == 0))
    def _():
        for_rows(0, slots, lambda s: row_copy(idx_ref, s, 0).start())

    @pl.when(f == 0)
    def _():
        for_rows(0, slots, lambda s: row_copy(idx_ref, s, buf).wait())
        xe_ref[...] = xe32_ref[buf].astype(BF16)

    @pl.when(i + 1 < pl.num_programs(0))
    def _():
        for_rows(f * share, share, lambda s: row_copy(idx_next_ref, s, 1 - buf).start())

    xe = xe_ref[...]
    gg = jnp.dot(xe, wg_ref[0].astype(BF16), preferred_element_type=F32)
    uu = jnp.dot(xe, wu_ref[0].astype(BF16), preferred_element_type=F32)
    hid = (gg * jax.nn.sigmoid(gg) * uu).astype(BF16)
    part = jnp.dot(hid, wd_ref[0].astype(BF16), preferred_element_type=F32)

    @pl.when(f == 0)
    def _():
        o_ref[0] = part

    @pl.when(f > 0)
    def _():
        o_ref[0] += part

    @pl.when(f == pl.num_programs(1) - 1)
    def _():
        o_ref[0] = o_ref[0] * gate_ref[0]


def _moe_ffn(idx, gate, xn_flat, w_gate, w_up, w_down, slots):
    be = idx.shape[0]
    nf = D_FF // FF_TF
    assert slots % (nf * MOE_UNROLL) == 0
    return pl.pallas_call(
        functools.partial(_moe_kernel, slots=slots, nf=nf),
        grid=(be, nf),
        in_specs=[pl.BlockSpec((1, 1, slots), lambda i, f: (i, 0, 0), memory_space=pltpu.SMEM),
                  pl.BlockSpec((1, 1, slots), lambda i, f: (jnp.minimum(i + 1, be - 1), 0, 0),
                               memory_space=pltpu.SMEM),
                  pl.BlockSpec(memory_space=pl.ANY),
                  pl.BlockSpec((1, slots, 1), lambda i, f: (i, 0, 0)),
                  pl.BlockSpec((1, D_MODEL, FF_TF), lambda i, f: (i % N_EXPERTS, 0, f)),
                  pl.BlockSpec((1, D_MODEL, FF_TF), lambda i, f: (i % N_EXPERTS, 0, f)),
                  pl.BlockSpec((1, FF_TF, D_MODEL), lambda i, f: (i % N_EXPERTS, f, 0))],
        out_specs=pl.BlockSpec((1, slots, D_MODEL), lambda i, f: (i, 0, 0)),
        out_shape=jax.ShapeDtypeStruct((be, slots, D_MODEL), F32),
        scratch_shapes=[pltpu.VMEM((2, slots, D_MODEL), F32), pltpu.VMEM((slots, D_MODEL), BF16),
                        pltpu.SemaphoreType.DMA((2,))],
        compiler_params=_cparams(("arbitrary", "arbitrary"), VMEM_LIMIT_BIG),
        name="moe_ffn",
    )(idx, idx, xn_flat, gate, w_gate, w_up, w_down)


CB_U = 6
CB_NB = 64


def _combine_kernel(idx_ref, h1_hbm, ye_ref, g_ref, o_hbm, acc_ref, sem, *, cap, tp):
    b = pl.program_id(0)
    e = pl.program_id(1)

    @pl.when(e == 0)
    def _():
        cp = pltpu.make_async_copy(h1_hbm.at[b], acc_ref, sem)
        cp.start()
        cp.wait()

    def rmw(g, carry):
        ts = [idx_ref[0, 0, g * CB_U + u] for u in range(CB_U)]
        vals = [acc_ref[ts[u]] + ye_ref[g * CB_U + u] for u in range(CB_U)]
        for u in range(CB_U):
            acc_ref[ts[u]] = vals[u]
        return carry

    lax.fori_loop(0, cap // CB_U, rmw, 0)

    @pl.when(e == pl.num_programs(1) - 1)
    def _():
        gamma = g_ref[...]

        def norm(r, carry):
            r0 = pl.multiple_of(r * CB_NB, CB_NB)
            x = acc_ref[pl.ds(r0, CB_NB)]
            ms = jnp.mean(x * x, axis=-1, keepdims=True)
            acc_ref[pl.ds(r0, CB_NB)] = x * lax.rsqrt(ms + RMS_EPS) * gamma
            return carry

        lax.fori_loop(CH // CB_NB, tp // CB_NB, norm, 0)
        cp = pltpu.make_async_copy(acc_ref.at[pl.ds(CH, tp - CH)], o_hbm.at[b], sem)
        cp.start()
        cp.wait()


def _combine(idx_local, h1, ye, gf, cap, slots):
    b, tp, _ = h1.shape
    assert cap % CB_U == 0
    return pl.pallas_call(
        functools.partial(_combine_kernel, cap=cap, tp=tp),
        grid=(b, N_EXPERTS),
        in_specs=[pl.BlockSpec((1, 1, slots), lambda bi, e: (bi * N_EXPERTS + e, 0, 0), memory_space=pltpu.SMEM),
                  pl.BlockSpec(memory_space=pl.ANY),
                  pl.BlockSpec((None, slots, 1, D_MODEL), lambda bi, e: (bi * N_EXPERTS + e, 0, 0, 0)),
                  pl.BlockSpec((1, 1, D_MODEL), lambda bi, e: (0, 0, 0))],
        out_specs=pl.BlockSpec(memory_space=pl.ANY),
        out_shape=jax.ShapeDtypeStruct((b, tp - CH, 1, D_MODEL), F32),
        scratch_shapes=[pltpu.VMEM((tp, 1, D_MODEL), F32), pltpu.SemaphoreType.DMA(())],
        compiler_params=_cparams(("arbitrary", "arbitrary"), VMEM_LIMIT_BIG),
        name="combine",
    )(idx_local, h1.reshape(b, tp, 1, D_MODEL), ye.reshape(ye.shape[0], slots, 1, D_MODEL),
      gf.reshape(1, 1, D_MODEL))


def _rope_tables(tp):
    half = RET_QK_HEAD // 2
    pos = jnp.arange(tp, dtype=F32) - float(PAD)
    inv = ROPE_BASE ** (-jnp.arange(half, dtype=F32) / half)
    ang = pos[:, None] * inv[None, :]
    cos, sin = jnp.cos(ang), jnp.sin(ang)
    return jnp.concatenate([cos, cos], axis=1), jnp.concatenate([-sin, sin], axis=1)


def _filter_features(t_len):
    half = NFFT // 2
    q = np.arange(NFFT)
    r = CH * (q % CH) + q // CH
    p_main = np.where(r < half, r, NFFT - r)
    valid_main = (r != half).astype(np.float32)
    m = np.arange(FAR)
    pad = np.zeros(HALF - 2 * FAR, np.int64)
    p_far = np.concatenate([half + m, half - FAR + m, pad, half + m, half - m, pad])
    valid_far = np.ones(CH, np.float32)
    valid_far[HALF + FAR] = 0.0
    p = jnp.asarray(np.concatenate([p_main, p_far]).astype(np.float32))
    valid = jnp.asarray(np.concatenate([valid_main, valid_far]))
    t_norm = p / (t_len - 1)
    bands = (HY_EMB_DIM - 1) // 2
    fr = jnp.linspace(1e-4, bands - 1, bands, dtype=F32)
    ang = (2.0 * math.pi * p / t_len)[:, None] * fr[None, :]
    feat = jnp.concatenate([t_norm[:, None], jnp.cos(ang), -jnp.sin(ang), valid[:, None]], axis=-1)
    return jnp.pad(feat, ((0, 0), (0, FEAT_W - feat.shape[1])))


def kernel(x, meta_tokens, norm1_g, w_in, ret_decay_fwd, ret_decay_bwd, ret_head_norm_g, w_ret_out,
           hy_conv_w, hy_conv_b, hy_filt_w1, hy_filt_b1, hy_filt_w2, hy_filt_b2, hy_filt_w3, hy_filt_b3,
           hy_filt_freq, hy_filt_w4, hy_skip, w_hy_out, w_o, norm2_g, w_router, w_exp_gate, w_exp_up,
           w_exp_down, final_norm_g):
    b, seq, d = x.shape
    t_len = seq + N_META
    tp = PAD + t_len
    assert d == D_MODEL and tp % IP_TM == 0 and tp % CH == 0 and t_len - NFFT // 2 == FAR
    cap = EC_CAPACITY * t_len // N_EXPERTS
    slots = -(-cap // 16) * 16
    l = 0

    meta = jnp.broadcast_to(meta_tokens[None].astype(x.dtype), (b, N_META, d))
    h0 = jnp.concatenate([jnp.zeros((b, PAD, d), x.dtype), meta, x], axis=1)

    cs, sn = _rope_tables(tp)
    proj = _in_proj(h0.reshape(b * tp, d), norm1_g[l][None], w_in[l].astype(BF16), cs, sn, tp)
    proj3 = proj.reshape(b, tp, IN_PROJ_W)

    lf = jax.nn.log_sigmoid(ret_decay_fwd[l].astype(F32))
    lb = jax.nn.log_sigmoid(ret_decay_bwd[l].astype(F32))
    og = _retention(proj3, lf, lb, ret_head_norm_g[l][None], tp)

    x0c, z = _hy_prep(proj3, hy_conv_w[l], hy_conv_b[l][None], tp)

    feat = _filter_features(t_len)
    w1p = jnp.pad(hy_filt_w1[l].astype(F32), ((0, FEAT_W - HY_EMB_DIM), (0, 0)))
    max_decay = math.log(HY_DECAY_TARGET) / HY_FAST_DECAY_PCT
    min_decay = math.log(HY_DECAY_TARGET) / HY_SLOW_DECAY_PCT
    dl = jnp.abs(jnp.linspace(min_decay, max_decay, D_MODEL, dtype=F32))[None]
    fargs = (w1p, hy_filt_b1[l][None].astype(F32), hy_filt_w2[l].astype(F32), hy_filt_b2[l][None].astype(F32),
             hy_filt_w3[l].astype(F32), hy_filt_b3[l][None].astype(F32), hy_filt_freq[l][None].astype(F32),
             hy_filt_w4[l].astype(F32), dl)
    g = _filters(feat[:NFFT], *fargs, FT_ROWS)
    h_far = _filters(feat[NFFT:], *fargs, CH)
    yc = _fft_conv(z, g)
    corr = _far_correction(h_far, z, tp)

    w_router_p = jnp.pad(w_router[l].astype(F32), ((0, 0), (0, ROUTER_W - N_EXPERTS)))
    h1, xn2, aff = _merge(h0, og, x0c, z, yc, proj3, corr, hy_skip[l][None].astype(F32),
                          w_ret_out[l].astype(BF16), w_hy_out[l].astype(BF16), w_o[l].astype(BF16),
                          norm2_g[l][None].astype(F32), w_router_p, tp)

    idx, gate = _topk(aff, cap, slots)
    live = (jnp.arange(slots) < cap)[None, None, :]
    idx_local = jnp.where(live, idx, PAD)
    idx_flat = idx_local + (jnp.arange(b * N_EXPERTS, dtype=I32) // N_EXPERTS * tp)[:, None, None]
    ye = _moe_ffn(idx_flat, gate, xn2.reshape(b * tp, d),
                  w_exp_gate[l], w_exp_up[l], w_exp_down[l], slots)
    out = _combine(idx_local, h1, ye, final_norm_g.astype(F32), cap, slots)
    return out.reshape(b, seq, d)
```

```python
import functools
import math

import numpy as np
import jax
import jax.numpy as jnp
from jax import lax
from jax.experimental import pallas as pl
from jax.experimental.pallas import tpu as pltpu

F32 = jnp.float32
BF16 = jnp.bfloat16
I32 = jnp.int32
HIGHEST = lax.Precision.HIGHEST

D_MODEL = 1024
N_META = 16
RET_HEADS = 4
RET_QK_HEAD = 128
RET_V_HEAD = 256
ROPE_BASE = 10000.0
HY_EMB_DIM = 33
HY_FILTER_ORDER = 64
HY_FAST_DECAY_PCT = 0.3
HY_SLOW_DECAY_PCT = 1.5
HY_DECAY_TARGET = 1e-2
N_EXPERTS = 16
EC_CAPACITY = 2
D_FF = 2 * D_MODEL
RMS_EPS = 1e-6
IN_PROJ_W = 8192

CH = 128
PAD = CH - N_META
NFFT = 16384
ZCH = 80
FAR = 16

VMEM_LIMIT_BIG = 56 * 1024 * 1024
VMEM_LIMIT_MID = 40 * 1024 * 1024


def _cparams(sem, vmem=VMEM_LIMIT_MID):
    return pltpu.CompilerParams(dimension_semantics=sem, vmem_limit_bytes=vmem)


IP_TM = 640
IP_TN = 1024
IP_CN = 256


def _inproj_kernel(x_ref, g_ref, w_ref, cs_ref, sn_ref, o_ref, xn_ref):
    j = pl.program_id(1)

    @pl.when(j == 0)
    def _():
        x = x_ref[...]
        ms = jnp.mean(x * x, axis=-1, keepdims=True)
        xn_ref[...] = (x * lax.rsqrt(ms + RMS_EPS) * g_ref[...]).astype(BF16)

    def run(epilogue):
        for c in range(IP_TN // IP_CN):
            cols = slice(c * IP_CN, (c + 1) * IP_CN)
            acc = jnp.dot(xn_ref[...], w_ref[:, cols], preferred_element_type=F32)
            epilogue(c, cols, acc)

    def rotary(c, cols, acc):
        scale = 1.0 if c < (IP_TN // IP_CN) // 2 else RET_QK_HEAD ** -0.5
        cs = cs_ref[...] * scale
        sn = sn_ref[...] * scale
        for hh in range(IP_CN // RET_QK_HEAD):
            xh = acc[:, hh * RET_QK_HEAD:(hh + 1) * RET_QK_HEAD]
            rot = xh * cs + pltpu.roll(xh, RET_QK_HEAD // 2, axis=1) * sn
            lo = c * IP_CN + hh * RET_QK_HEAD
            o_ref[:, lo:lo + RET_QK_HEAD] = rot.astype(BF16)

    def raw(c, cols, acc):
        o_ref[:, cols] = acc.astype(BF16)

    def swish(c, cols, acc):
        o_ref[:, cols] = (acc * jax.nn.sigmoid(acc)).astype(BF16)

    def sigm(c, cols, acc):
        o_ref[:, cols] = jax.nn.sigmoid(acc).astype(BF16)

    pl.when(j == 0)(lambda: run(rotary))
    pl.when(jnp.logical_or(j == 1, jnp.logical_and(j >= 3, j < 6)))(lambda: run(raw))
    pl.when(j == 2)(lambda: run(swish))
    pl.when(j >= 6)(lambda: run(sigm))


def _in_proj(h0, g1, w_in_bf, cs, sn, tp):
    n_rows = h0.shape[0]
    nt_b = tp // IP_TM
    return pl.pallas_call(
        _inproj_kernel,
        grid=(n_rows // IP_TM, IN_PROJ_W // IP_TN),
        in_specs=[
            pl.BlockSpec((IP_TM, D_MODEL), lambda i, j: (i, 0)),
            pl.BlockSpec((1, D_MODEL), lambda i, j: (0, 0)),
            pl.BlockSpec((D_MODEL, IP_TN), lambda i, j: (0, j)),
            pl.BlockSpec((IP_TM, RET_QK_HEAD), lambda i, j: (i % nt_b, 0)),
            pl.BlockSpec((IP_TM, RET_QK_HEAD), lambda i, j: (i % nt_b, 0)),
        ],
        out_specs=pl.BlockSpec((IP_TM, IP_TN), lambda i, j: (i, j)),
        out_shape=jax.ShapeDtypeStruct((n_rows, IN_PROJ_W), BF16),
        scratch_shapes=[pltpu.VMEM((IP_TM, D_MODEL), BF16)],
        compiler_params=_cparams(("arbitrary", "arbitrary")),
        name="in_proj",
    )(h0, g1, w_in_bf, cs, sn)


def _ret_kernel(lf_ref, lb_ref, q_ref, k_ref, v_ref, gr_ref, gn_ref, o_ref, ob_ref, s_ref, *, nch):
    h = pl.program_id(1)
    lf = lf_ref[h]
    lb = lb_ref[h]
    ri = lax.broadcasted_iota(I32, (CH, CH), 0).astype(F32)
    ci = lax.broadcasted_iota(I32, (CH, CH), 1).astype(F32)
    diff = ri - ci
    mask = jnp.exp(jnp.where(diff >= 0, lf * diff, -lb * diff))
    w_end = jnp.exp(lf * (CH - 1.0 - ri))
    w_start = jnp.exp(lb * ri)
    qw_f = jnp.exp(lf * (ri + 1.0))
    qw_b = jnp.exp(lb * (CH - ri))
    dec_f = jnp.exp(jnp.full((CH, RET_V_HEAD), lf * CH, F32))
    dec_b = jnp.exp(jnp.full((CH, RET_V_HEAD), lb * CH, F32))
    tn_dims = (((0,), (0,)), ((), ()))
    nt_dims = (((1,), (1,)), ((), ()))

    s_ref[...] = jnp.zeros_like(s_ref)

    def bwd(ii, carry):
        n = nch - 1 - ii
        r0 = pl.multiple_of(n * CH, CH)
        q = q_ref[0, pl.ds(r0, CH), :].astype(F32)
        k = k_ref[0, pl.ds(r0, CH), :].astype(F32)
        v = v_ref[0, pl.ds(r0, CH), :]
        s = s_ref[...]
        ob_ref[pl.ds(r0, CH), :] = jnp.dot((q * qw_b).astype(BF16), s.astype(BF16),
                                           preferred_element_type=F32)
        a = lax.dot_general((k * w_start).astype(BF16), v, tn_dims, preferred_element_type=F32)
        s_ref[...] = s * dec_b + a
        return carry

    lax.fori_loop(0, nch, bwd, 0)

    s_ref[...] = jnp.zeros_like(s_ref)
    gn = gn_ref[...]

    def fwd(n, carry):
        r0 = pl.multiple_of(n * CH, CH)
        qb = q_ref[0, pl.ds(r0, CH), :]
        kb = k_ref[0, pl.ds(r0, CH), :]
        v = v_ref[0, pl.ds(r0, CH), :]
        q = qb.astype(F32)
        k = kb.astype(F32)
        s = s_ref[...]
        scores = lax.dot_general(qb, kb, nt_dims, preferred_element_type=F32) * mask
        o = jnp.dot(scores.astype(BF16), v, preferred_element_type=F32)
        o = o + jnp.dot((q * qw_f).astype(BF16), s.astype(BF16), preferred_element_type=F32)
        o = o + ob_ref[pl.ds(r0, CH), :]
        a = lax.dot_general((k * w_end).astype(BF16), v, tn_dims, preferred_element_type=F32)
        s_ref[...] = s * dec_f + a
        y = o * lax.rsqrt(jnp.mean(o * o, axis=-1, keepdims=True) + RMS_EPS) * gn
        o_ref[0, pl.ds(r0, CH), :] = (y * gr_ref[0, pl.ds(r0, CH), :].astype(F32)).astype(BF16)
        return carry

    lax.fori_loop(0, nch, fwd, 0)


def _retention(proj3, lf, lb, gn, tp):
    b = proj3.shape[0]
    nch = tp // CH
    qk_blocks = (RET_HEADS * RET_QK_HEAD) // RET_QK_HEAD
    v_blk0 = (2 * RET_HEADS * RET_QK_HEAD) // RET_V_HEAD
    g_blk0 = v_blk0 + RET_HEADS
    smem = pl.BlockSpec(memory_space=pltpu.SMEM)
    return pl.pallas_call(
        functools.partial(_ret_kernel, nch=nch),
        grid=(b, RET_HEADS),
        in_specs=[
            smem, smem,
            pl.BlockSpec((1, tp, RET_QK_HEAD), lambda bi, h: (bi, 0, h)),
            pl.BlockSpec((1, tp, RET_QK_HEAD), lambda bi, h: (bi, 0, qk_blocks + h)),
            pl.BlockSpec((1, tp, RET_V_HEAD), lambda bi, h: (bi, 0, v_blk0 + h)),
            pl.BlockSpec((1, tp, RET_V_HEAD), lambda bi, h: (bi, 0, g_blk0 + h)),
            pl.BlockSpec((1, RET_V_HEAD), lambda bi, h: (0, h)),
        ],
        out_specs=pl.BlockSpec((1, tp, RET_V_HEAD), lambda bi, h: (bi, 0, h)),
        out_shape=jax.ShapeDtypeStruct((b, tp, RET_HEADS * RET_V_HEAD), BF16),
        scratch_shapes=[pltpu.VMEM((tp, RET_V_HEAD), F32), pltpu.VMEM((RET_QK_HEAD, RET_V_HEAD), F32)],
        compiler_params=_cparams(("arbitrary", "arbitrary"), VMEM_LIMIT_BIG),
        name="retention",
    )(lf, lb, proj3, proj3, proj3, proj3, gn)


HP_CW = 128


def _hyprep_kernel(u0_ref, u1_ref, u2_ref, w0_ref, w1_ref, w2_ref, b0_ref, b1_ref, b2_ref,
                   x0_ref, z_ref, *, nch, tz):
    rows = lax.broadcasted_iota(I32, (CH, HP_CW), 0)
    halo = 16

    def conv(u_ref, w_ref, b_ref, n, r0):
        cur = u_ref[0, pl.ds(r0, CH), :].astype(F32)
        rp = pl.multiple_of(jnp.maximum(r0 - halo, 0), halo)
        rn = pl.multiple_of(jnp.minimum(r0 + CH, (nch - 1) * CH), halo)
        prev = u_ref[0, pl.ds(rp, halo), :].astype(F32)[halo - 1:halo, :]
        nxt = u_ref[0, pl.ds(rn, halo), :].astype(F32)[0:1, :]
        prev = jnp.where(n > 0, prev, 0.0)
        nxt = jnp.where(n < nch - 1, nxt, 0.0)
        up = jnp.where(rows == 0, prev, pltpu.roll(cur, 1, axis=0))
        dn = jnp.where(rows == CH - 1, nxt, pltpu.roll(cur, CH - 1, axis=0))
        w = w_ref[...]
        return up * w[0:1, :] + cur * w[1:2, :] + dn * w[2:3, :] + b_ref[...]

    def body(n, carry):
        r0 = pl.multiple_of(n * CH, CH)
        x0 = conv(u0_ref, w0_ref, b0_ref, n, r0)
        x1 = conv(u1_ref, w1_ref, b1_ref, n, r0)
        vv = conv(u2_ref, w2_ref, b2_ref, n, r0)
        z = jnp.where(rows + r0 >= PAD, x1 * vv, 0.0)
        x0_ref[0, pl.ds(r0, CH), :] = x0.astype(BF16)
        z_ref[0, pl.ds(r0, CH), :] = z.astype(BF16)
        return carry

    lax.fori_loop(0, nch, body, 0)
    z_ref[0, nch * CH:tz, :] = jnp.zeros((tz - nch * CH, HP_CW), BF16)


def _hy_prep(proj3, conv_w, conv_b, tp):
    b = proj3.shape[0]
    nch = tp // CH
    tz = ZCH * CH
    ncb = D_MODEL // HP_CW
    u_blk0 = 3072 // HP_CW
    uspec = lambda s: pl.BlockSpec((1, tp, HP_CW), lambda bi, c: (bi, 0, u_blk0 + s * ncb + c))
    wspec = lambda s: pl.BlockSpec((3, HP_CW), lambda bi, c: (0, s * ncb + c))
    bspec = lambda s: pl.BlockSpec((1, HP_CW), lambda bi, c: (0, s * ncb + c))
    return pl.pallas_call(
        functools.partial(_hyprep_kernel, nch=nch, tz=tz),
        grid=(b, ncb),
        in_specs=[uspec(0), uspec(1), uspec(2), wspec(0), wspec(1), wspec(2), bspec(0), bspec(1), bspec(2)],
        out_specs=[pl.BlockSpec((1, tp, HP_CW), lambda bi, c: (bi, 0, c)),
                   pl.BlockSpec((1, tz, HP_CW), lambda bi, c: (bi, 0, c))],
        out_shape=[jax.ShapeDtypeStruct((b, tp, D_MODEL), BF16),
                   jax.ShapeDtypeStruct((b, tz, D_MODEL), BF16)],
        compiler_params=_cparams(("arbitrary", "arbitrary")),
        name="hy_prep",
    )(proj3, proj3, proj3, conv_w, conv_w, conv_w, conv_b, conv_b, conv_b)


FEAT_W = 128


FT_ROWS = 512
HALF = CH // 2


def _filter_kernel(feat_ref, w1_ref, b1_ref, w2_ref, b2_ref, w3_ref, b3_ref, fq_ref, w4_ref, dl_ref, o_ref,
                   *, groups):
    feat = feat_ref[...]
    fq = fq_ref[...]
    hdn = jnp.sin(fq * (jnp.dot(feat, w1_ref[...], precision=HIGHEST, preferred_element_type=F32) + b1_ref[...]))
    hdn = jnp.sin(fq * (jnp.dot(hdn, w2_ref[...], precision=HIGHEST, preferred_element_type=F32) + b2_ref[...]))
    hdn = jnp.sin(fq * (jnp.dot(hdn, w3_ref[...], precision=HIGHEST, preferred_element_type=F32) + b3_ref[...]))
    scale = jnp.exp(-feat[:, 0:1] * dl_ref[...]) * feat[:, HY_EMB_DIM:HY_EMB_DIM + 1]
    for d in range(2):
        rows = [slice(g * CH + d * HALF, g * CH + (d + 1) * HALF) for g in range(groups)]
        hd = jnp.concatenate([hdn[r] for r in rows], axis=0)
        filt = jnp.dot(hd, w4_ref[:, d * D_MODEL:(d + 1) * D_MODEL], precision=HIGHEST, preferred_element_type=F32)
        for g, r in enumerate(rows):
            o_ref[r, :] = filt[g * HALF:(g + 1) * HALF] * scale[r]


def _filters(feat, w1p, b1, w2, b2, w3, b3, fq, w4, dl, rows):
    n_rows = feat.shape[0]
    full = lambda a: pl.BlockSpec(a.shape, lambda i: (0,) * a.ndim)
    return pl.pallas_call(
        functools.partial(_filter_kernel, groups=rows // CH),
        grid=(n_rows // rows,),
        in_specs=[pl.BlockSpec((rows, FEAT_W), lambda i: (i, 0)),
                  full(w1p), full(b1), full(w2), full(b2), full(w3), full(b3), full(fq), full(w4), full(dl)],
        out_specs=pl.BlockSpec((rows, D_MODEL), lambda i: (i, 0)),
        out_shape=jax.ShapeDtypeStruct((n_rows, D_MODEL), F32),
        compiler_params=_cparams(("arbitrary",)),
        name="hy_filter",
    )(feat, w1p, b1, w2, b2, w3, b3, fq, w4, dl)


def _fft_a_data_kernel(m_ref, z_ref, o_ref):
    x = jnp.concatenate([z_ref[0], z_ref[1]], axis=0)
    o_ref[0] = jnp.dot(m_ref[0], x, preferred_element_type=F32).astype(BF16)


def _fft_a_filt_kernel(m_ref, g_ref, o_ref):
    o_ref[0] = jnp.dot(m_ref[0], g_ref[...].astype(BF16), preferred_element_type=F32).astype(BF16)


def _fft_b_filt_kernel(f_ref, yr_ref, yi_ref, o_ref):
    y = jnp.concatenate([yr_ref[...], yi_ref[...]], axis=0)
    o_ref[0] = jnp.dot(f_ref[...], y, preferred_element_type=F32) * (1.0 / NFFT)


def _fft_b_data_kernel(f_ref, fi_ref, yr_ref, yi_ref, g_ref, o_ref):
    y = jnp.concatenate([yr_ref[...], yi_ref[...]], axis=0)
    x = jnp.dot(f_ref[...], y, preferred_element_type=F32)
    xr, xi = x[:CH], x[CH:]
    gr, gi = g_ref[0, :CH], g_ref[0, CH:]
    p = jnp.concatenate([xr * gr - xi * gi, xr * gi + xi * gr], axis=0).astype(BF16)
    o_ref[0] = jnp.dot(fi_ref[...], p, preferred_element_type=F32).astype(BF16)


def _fft_a_inv_kernel(m_ref, ur_ref, ui_ref, o_ref):
    u = jnp.concatenate([ur_ref[...], ui_ref[...]], axis=0)
    y = jnp.dot(m_ref[0], u, preferred_element_type=F32).astype(BF16)
    o_ref[0] = y[:ZCH]
    o_ref[1] = y[ZCH:]


def _dft_tables():
    n2 = np.arange(CH)[:, None, None]
    k1 = np.arange(CH)[None, :, None]

    def theta(n1_count):
        n1 = np.arange(n1_count)[None, None, :]
        return 2.0 * np.pi * ((k1 * (CH * n1 + n2)) % NFFT) / NFFT

    th = theta(ZCH)
    c, s = np.cos(th), np.sin(th)
    m_a = np.concatenate([np.concatenate([c, s], axis=2), np.concatenate([-s, c], axis=2)], axis=1)
    m_ainv = np.transpose(m_a, (0, 2, 1))
    th = theta(CH)
    m_af = np.concatenate([np.cos(th), -np.sin(th)], axis=1)
    a = 2.0 * np.pi * ((np.arange(CH)[:, None] * np.arange(CH)[None, :]) % CH) / CH
    c, s = np.cos(a), np.sin(a)
    f2 = np.block([[c, s], [-s, c]])
    f2i = np.block([[c, -s], [s, c]])
    f = lambda t: jnp.asarray(t.astype(np.float32)).astype(BF16)
    return f(m_a), f(m_ainv), f(m_af), f(f2), f(f2i)


def _fft_conv(z, g):
    c = z.shape[-1]
    m_a, m_ainv, m_af, f2, f2i = _dft_tables()
    cp = _cparams(("arbitrary",))
    full2 = pl.BlockSpec((2 * CH, 2 * CH), lambda i: (0, 0))
    col_re = pl.BlockSpec((CH, c), lambda i: (0, i))
    col_im = pl.BlockSpec((CH, c), lambda i: (0, CH + i))
    blk = pl.BlockSpec((1, 2 * CH, c), lambda i: (i, 0, 0))
    spec_shape = jax.ShapeDtypeStruct((CH, 2 * CH, c), BF16)

    yg = pl.pallas_call(
        _fft_a_filt_kernel, grid=(CH,),
        in_specs=[pl.BlockSpec((1, 2 * CH, CH), lambda i: (i, 0, 0)), pl.BlockSpec((CH, c), lambda i: (i, 0))],
        out_specs=blk, out_shape=spec_shape, compiler_params=cp, name="fft_a_filt",
    )(m_af, g)
    yg2 = yg.reshape(CH, 2 * CH * c)
    gh = pl.pallas_call(
        _fft_b_filt_kernel, grid=(CH,),
        in_specs=[full2, col_re, col_im],
        out_specs=blk, out_shape=jax.ShapeDtypeStruct((CH, 2 * CH, c), F32),
        compiler_params=cp, name="fft_b_filt",
    )(f2, yg2, yg2)

    y = pl.pallas_call(
        _fft_a_data_kernel, grid=(CH,),
        in_specs=[pl.BlockSpec((1, 2 * CH, 2 * ZCH), lambda i: (i, 0, 0)),
                  pl.BlockSpec((2, ZCH, c), lambda i: (0, 0, i))],
        out_specs=blk, out_shape=spec_shape, compiler_params=cp, name="fft_a_data",
    )(m_a, z.reshape(2, ZCH, CH * c))
    y2 = y.reshape(CH, 2 * CH * c)
    u = pl.pallas_call(
        _fft_b_data_kernel, grid=(CH,),
        in_specs=[full2, full2, col_re, col_im, blk],
        out_specs=blk, out_shape=spec_shape, compiler_params=cp, name="fft_b_data",
    )(f2, f2i, y2, y2, gh)
    u2 = u.reshape(CH, 2 * CH * c)
    yc = pl.pallas_call(
        _fft_a_inv_kernel, grid=(CH,),
        in_specs=[pl.BlockSpec((1, 2 * ZCH, 2 * CH), lambda i: (i, 0, 0)), col_re, col_im],
        out_specs=pl.BlockSpec((2, ZCH, c), lambda i: (0, 0, i)),
        out_shape=jax.ShapeDtypeStruct((2, ZCH, CH * c), BF16),
        compiler_params=cp, name="fft_a_inv",
    )(m_ainv, u2, u2)
    return yc.reshape(2, ZCH * CH, c)


def _far_kernel(far_ref, zm_ref, zl_ref, o_ref):
    g_hi = far_ref[HALF + FAR:HALF + 2 * FAR, :]
    ef = far_ref[0:FAR, :] - g_hi
    g_lo = far_ref[FAR:2 * FAR, :]
    hb = far_ref[HALF:HALF + FAR, :]
    zm = zm_ref[0, PAD:CH, :].astype(F32)
    zl = zl_ref[0, PAD:CH, :].astype(F32)
    row = lambda a, i: a[i:i + 1, :]
    eb = [row(hb, 0) - row(g_hi, 0)] + [row(hb, m) - row(g_lo, FAR - m) for m in range(1, FAR)]
    for j in range(FAR):
        acc_f = row(ef, j) * row(zm, 0)
        for i in range(1, j + 1):
            acc_f = acc_f + row(ef, j - i) * row(zm, i)
        o_ref[0, 0, j:j + 1, :] = acc_f
        acc_b = eb[0] * row(zl, j)
        for i in range(j + 1, FAR):
            acc_b = acc_b + eb[i - j] * row(zl, i)
        o_ref[0, 1, j:j + 1, :] = acc_b


def _far_correction(h_far, z, tp):
    b, _, c = z.shape
    last = tp // CH - 1
    return pl.pallas_call(
        _far_kernel, grid=(b,),
        in_specs=[pl.BlockSpec((CH, c), lambda bi: (0, 0)),
                  pl.BlockSpec((1, CH, c), lambda bi: (bi, 0, 0)),
                  pl.BlockSpec((1, CH, c), lambda bi: (bi, last, 0))],
        out_specs=pl.BlockSpec((1, 2, FAR, c), lambda bi: (bi, 0, 0, 0)),
        out_shape=jax.ShapeDtypeStruct((b, 2, FAR, c), F32),
        compiler_params=_cparams(("arbitrary",)),
        name="hy_far",
    )(h_far, z, z)


MG_TM = 640
MG_CN = 256
ROUTER_W = 128


def _merge_kernel(h_ref, og_ref, x0_ref, z_ref, yc_ref, ga_ref, gb_ref, corr_ref, skip_ref,
                  wr_ref, wh_ref, wo_ref, g2_ref, wrt_ref,
                  h1_ref, xn_ref, aff_ref, pre_ref, mix_ref, *, nt_b):
    i = pl.program_id(0)
    ib = i % nt_b
    pre_ref[...] = yc_ref[0].astype(F32) + z_ref[0].astype(F32) * skip_ref[...]

    @pl.when(ib == 0)
    def _():
        pre_ref[PAD:CH, :] += corr_ref[0, 1]

    @pl.when(ib == nt_b - 1)
    def _():
        pre_ref[MG_TM - FAR:MG_TM, :] += corr_ref[0, 0]

    chunks = [slice(c * MG_CN, (c + 1) * MG_CN) for c in range(D_MODEL // MG_CN)]
    og = og_ref[0]
    pre = (x0_ref[0].astype(F32) * pre_ref[...]).astype(BF16)
    for cols in chunks:
        ya = jnp.dot(og, wr_ref[:, cols], preferred_element_type=F32)
        yb = jnp.dot(pre, wh_ref[:, cols], preferred_element_type=F32)
        mix_ref[:, cols] = (ga_ref[0, :, cols].astype(F32) * ya + gb_ref[0, :, cols].astype(F32) * yb).astype(BF16)
    mixed = mix_ref[...]
    ss = jnp.zeros((MG_TM, 1), F32)
    for cols in chunks:
        h1 = h_ref[0, :, cols] + jnp.dot(mixed, wo_ref[:, cols], preferred_element_type=F32)
        h1_ref[0, :, cols] = h1
        ss = ss + jnp.sum(h1 * h1, axis=-1, keepdims=True)
    rinv = lax.rsqrt(ss * (1.0 / D_MODEL) + RMS_EPS)
    xn = h1_ref[0] * rinv * g2_ref[...]
    xn_ref[0] = xn
    logits = jnp.dot(xn, wrt_ref[...], precision=HIGHEST, preferred_element_type=F32)
    lane = lax.broadcasted_iota(I32, logits.shape, 1)
    logits = jnp.where(lane < N_EXPERTS, logits, -jnp.inf)
    m = jnp.max(logits, axis=-1, keepdims=True)
    e = jnp.exp(logits - m)
    aff = e / jnp.sum(e, axis=-1, keepdims=True)
    rows = lax.broadcasted_iota(I32, logits.shape, 0) + ib * MG_TM
    aff_ref[0] = jnp.where(rows >= PAD, aff, -1.0)


def _merge(h0, og, x0c, z, yc, proj3, corr, skip, w_ret, w_hy, w_o, g2, w_router_p, tp):
    b = h0.shape[0]
    nt_b = tp // MG_TM
    ga_blk0 = 6144 // D_MODEL
    row = lambda w: pl.BlockSpec((1, MG_TM, w), lambda i: (i // nt_b, i % nt_b, 0))
    full = lambda a: pl.BlockSpec(a.shape, lambda i: (0,) * a.ndim)
    return pl.pallas_call(
        functools.partial(_merge_kernel, nt_b=nt_b),
        grid=(b * nt_b,),
        in_specs=[row(D_MODEL), row(D_MODEL), row(D_MODEL), row(D_MODEL), row(D_MODEL),
                  pl.BlockSpec((1, MG_TM, D_MODEL), lambda i: (i // nt_b, i % nt_b, ga_blk0)),
                  pl.BlockSpec((1, MG_TM, D_MODEL), lambda i: (i // nt_b, i % nt_b, ga_blk0 + 1)),
                  pl.BlockSpec((1, 2, FAR, D_MODEL), lambda i: (i // nt_b, 0, 0, 0)),
                  full(skip), full(w_ret), full(w_hy), full(w_o), full(g2), full(w_router_p)],
        out_specs=[row(D_MODEL), row(D_MODEL), row(ROUTER_W)],
        out_shape=[jax.ShapeDtypeStruct((b, tp, D_MODEL), F32),
                   jax.ShapeDtypeStruct((b, tp, D_MODEL), F32),
                   jax.ShapeDtypeStruct((b, tp, ROUTER_W), F32)],
        scratch_shapes=[pltpu.VMEM((MG_TM, D_MODEL), F32), pltpu.VMEM((MG_TM, D_MODEL), BF16)],
        compiler_params=_cparams(("arbitrary",), VMEM_LIMIT_BIG),
        name="merge",
    )(h0, og, x0c, z, yc, proj3, proj3, corr, skip, w_ret, w_hy, w_o, g2, w_router_p)


TK_W = 128


def _select_kernel(aff_ref, low_ref, slot_ref, offs_ref, *, cap, nch):
    def chunk(c):
        r0 = pl.multiple_of(c * CH, CH)
        return aff_ref[0, pl.ds(r0, CH), :]

    def count(pred):
        def body(c, acc):
            return acc + pred(chunk(c)).astype(I32)

        acc = lax.fori_loop(0, nch, body, jnp.zeros((CH, ROUTER_W), I32))
        return jnp.sum(acc, axis=0, keepdims=True)

    def search(it, bits):
        cand = bits | jnp.left_shift(1, 29 - it)
        cand_f = pltpu.bitcast(cand, F32)
        return jnp.where(count(lambda a: a >= cand_f) >= cap, cand, bits)

    thr = pltpu.bitcast(lax.fori_loop(0, 30, search, jnp.zeros((1, ROUTER_W), I32)), F32)
    need = (cap - count(lambda a: a > thr)).astype(F32)
    low = low_ref[...]

    def scan(c, carry):
        c_eq, c_sel = carry
        a = chunk(c)
        eq = a == thr
        eq_f = eq.astype(F32)
        eq_rank = jnp.dot(low, eq_f, preferred_element_type=F32) + c_eq
        sel = jnp.logical_or(a > thr, jnp.logical_and(eq, eq_rank < need))
        sel_f = sel.astype(F32)
        slot = jnp.dot(low, sel_f, preferred_element_type=F32) + c_sel
        r0 = pl.multiple_of(c * CH, CH)
        slot_ref[0, pl.ds(r0, CH), :] = jnp.where(sel, slot, -1.0).astype(I32)
        offs_ref[0, c] = c_sel.astype(I32)
        return (c_eq + jnp.sum(eq_f, axis=0, keepdims=True), c_sel + jnp.sum(sel_f, axis=0, keepdims=True))

    zero = jnp.zeros((1, ROUTER_W), F32)
    lax.fori_loop(0, nch, scan, (zero, zero))


def _extract_kernel(offs_ref, slot_ref, aff_ref, idx_ref, gate_ref, *, nch, nwin):
    idx_ref[...] = jnp.zeros_like(idx_ref)
    gate_ref[...] = jnp.zeros_like(gate_ref)
    lane = lax.broadcasted_iota(I32, (CH, TK_W), 1)
    trow = lax.broadcasted_iota(I32, (CH, TK_W), 0)

    def per_chunk(c, carry):
        r0 = pl.multiple_of(c * CH, CH)
        slots = slot_ref[0, pl.ds(r0, CH), :]
        affs = aff_ref[0, pl.ds(r0, CH), :]
        tpos = (trow + r0).astype(F32)
        for e in range(N_EXPERTS):
            col = jnp.broadcast_to(slots[:, e:e + 1], (CH, TK_W))
            gcol = jnp.broadcast_to(affs[:, e:e + 1], (CH, TK_W))
            w0 = offs_ref[0, 0, c * N_EXPERTS + e] // TK_W
            for dw in range(2):
                w = w0 + dw
                base = jnp.where(w < nwin, w * TK_W, -2 * TK_W)
                hit = col == lane + base
                row = e * nwin + jnp.minimum(w, nwin - 1)
                idx_ref[row] += jnp.sum(jnp.where(hit, tpos, 0.0), axis=0, keepdims=True)
                gate_ref[row] += jnp.sum(jnp.where(hit, gcol, 0.0), axis=0, keepdims=True)
        return carry

    lax.fori_loop(0, nch, per_chunk, 0)


def _topk(aff, cap, slots):
    b, tp, _ = aff.shape
    nch = tp // CH
    nwin = -(-slots // TK_W)
    low = jnp.asarray(np.tril(np.ones((CH, CH), np.float32), k=-1))
    slot, offs = pl.pallas_call(
        functools.partial(_select_kernel, cap=cap, nch=nch),
        grid=(b,),
        in_specs=[pl.BlockSpec((1, tp, ROUTER_W), lambda bi: (bi, 0, 0)),
                  pl.BlockSpec((CH, CH), lambda bi: (0, 0))],
        out_specs=[pl.BlockSpec((1, tp, ROUTER_W), lambda bi: (bi, 0, 0)),
                   pl.BlockSpec((1, nch, 1, ROUTER_W), lambda bi: (bi, 0, 0, 0))],
        out_shape=[jax.ShapeDtypeStruct((b, tp, ROUTER_W), I32),
                   jax.ShapeDtypeStruct((b, nch, 1, ROUTER_W), I32)],
        compiler_params=_cparams(("arbitrary",)),
        name="topk_select",
    )(aff, low)
    offs_s = offs[:, :, 0, :N_EXPERTS].reshape(b, 1, nch * N_EXPERTS)
    rows = pl.BlockSpec((None, N_EXPERTS * nwin, 1, TK_W), lambda bi: (bi, 0, 0, 0))
    out = jax.ShapeDtypeStruct((b, N_EXPERTS * nwin, 1, TK_W), F32)
    idx, gate = pl.pallas_call(
        functools.partial(_extract_kernel, nch=nch, nwin=nwin),
        grid=(b,),
        in_specs=[pl.BlockSpec((1, 1, nch * N_EXPERTS), lambda bi: (bi, 0, 0), memory_space=pltpu.SMEM),
                  pl.BlockSpec((1, tp, ROUTER_W), lambda bi: (bi, 0, 0)),
                  pl.BlockSpec((1, tp, ROUTER_W), lambda bi: (bi, 0, 0))],
        out_specs=[rows, rows],
        out_shape=[out, out],
        compiler_params=_cparams(("arbitrary",)),
        name="topk_extract",
    )(offs_s, slot, aff)
    idx = idx.reshape(b * N_EXPERTS, 1, nwin * TK_W)[:, :, :slots].astype(I32)
    gate = gate.reshape(b * N_EXPERTS, nwin * TK_W, 1)[:, :slots]
    return idx, gate


FF_TF = 1024
MOE_CN = 256


MOE_UNROLL = 4


def _moe_kernel(idx_ref, idx_next_ref, xn_hbm, gate_ref, wg_ref, wu_ref, wd_ref, o_ref, xe32_ref, xe_ref, hid_ref, sem,
                *, slots, nf):
    i = pl.program_id(0)
    f = pl.program_id(1)
    buf = i % 2
    share = slots // nf

    def row_copy(ids_ref, s, b):
        return pltpu.make_async_copy(xn_hbm.at[pl.ds(ids_ref[0, 0, s], 1), :],
                                     xe32_ref.at[b, pl.ds(s, 1), :], sem.at[b])

    def for_rows(lo, n, fn):
        def body(k, carry):
            for u in range(MOE_UNROLL):
                fn(lo + k * MOE_UNROLL + u)
            return carry

        lax.fori_loop(0, n // MOE_UNROLL, body, 0)

    @pl.when(jnp.logical_and(i == 0, f == 0))
    def _():
        for_rows(0, slots, lambda s: row_copy(idx_ref, s, 0).start())

    @pl.when(f == 0)
    def _():
        for_rows(0, slots, lambda s: row_copy(idx_ref, s, buf).wait())
        xe_ref[...] = xe32_ref[buf].astype(BF16)

    @pl.when(f == 0)
    def _():
        o_ref[...] = jnp.zeros_like(o_ref)

    n_up, n_down = FF_TF // MOE_CN, D_MODEL // MOE_CN
    per_chunk = share // (n_up + n_down)

    def prefetch(chunk):
        base = f * share + chunk * per_chunk
        for u in range(per_chunk):
            row_copy(idx_next_ref, base + u, 1 - buf).start()

    xe = xe_ref[...]
    for c in range(n_up):
        cols = slice(c * MOE_CN, (c + 1) * MOE_CN)
        gg = jnp.dot(xe, wg_ref[0, :, cols].astype(BF16), preferred_element_type=F32)
        uu = jnp.dot(xe, wu_ref[0, :, cols].astype(BF16), preferred_element_type=F32)
        hid_ref[:, cols] = (gg * jax.nn.sigmoid(gg) * uu).astype(BF16)
        prefetch(c)
    scale = jnp.where(f == pl.num_programs(1) - 1, gate_ref[0], 1.0)
    hid = hid_ref[...]
    for c in range(n_down):
        cols = slice(c * MOE_CN, (c + 1) * MOE_CN)
        part = jnp.dot(hid, wd_ref[0, :, cols].astype(BF16), preferred_element_type=F32)
        o_ref[0, :, cols] = (o_ref[0, :, cols] + part) * scale
        prefetch(n_up + c)

    @pl.when(jnp.logical_and(i == pl.num_programs(0) - 1, f == pl.num_programs(1) - 1))
    def _():
        for_rows(0, slots, lambda s: row_copy(idx_next_ref, s, 1 - buf).wait())


def _moe_ffn(idx, gate, xn_flat, w_gate, w_up, w_down, slots):
    be = idx.shape[0]
    nf = D_FF // FF_TF
    assert slots % MOE_UNROLL == 0 and slots % (nf * (FF_TF // MOE_CN + D_MODEL // MOE_CN)) == 0
    return pl.pallas_call(
        functools.partial(_moe_kernel, slots=slots, nf=nf),
        grid=(be, nf),
        in_specs=[pl.BlockSpec((1, 1, slots), lambda i, f: (i, 0, 0), memory_space=pltpu.SMEM),
                  pl.BlockSpec((1, 1, slots), lambda i, f: (jnp.minimum(i + 1, be - 1), 0, 0),
                               memory_space=pltpu.SMEM),
                  pl.BlockSpec(memory_space=pl.ANY),
                  pl.BlockSpec((1, slots, 1), lambda i, f: (i, 0, 0)),
                  pl.BlockSpec((1, D_MODEL, FF_TF), lambda i, f: (i % N_EXPERTS, 0, f)),
                  pl.BlockSpec((1, D_MODEL, FF_TF), lambda i, f: (i % N_EXPERTS, 0, f)),
                  pl.BlockSpec((1, FF_TF, D_MODEL), lambda i, f: (i % N_EXPERTS, f, 0))],
        out_specs=pl.BlockSpec((1, slots, D_MODEL), lambda i, f: (i, 0, 0)),
        out_shape=jax.ShapeDtypeStruct((be, slots, D_MODEL), F32),
        scratch_shapes=[pltpu.VMEM((2, slots, D_MODEL), F32), pltpu.VMEM((slots, D_MODEL), BF16),
                        pltpu.VMEM((slots, FF_TF), BF16),
                        pltpu.SemaphoreType.DMA((2,))],
        compiler_params=_cparams(("arbitrary", "arbitrary"), VMEM_LIMIT_BIG),
        name="moe_ffn",
    )(idx, idx, xn_flat, gate, w_gate, w_up, w_down)


CB_U = 6
CB_NB = 64


def _combine_kernel(idx_ref, h1_hbm, ye_ref, g_ref, o_hbm, acc_ref, sem, *, cap, tp):
    b = pl.program_id(0)
    e = pl.program_id(1)

    @pl.when(e == 0)
    def _():
        cp = pltpu.make_async_copy(h1_hbm.at[b], acc_ref, sem)
        cp.start()
        cp.wait()

    def rmw(g, carry):
        ts = [idx_ref[0, 0, g * CB_U + u] for u in range(CB_U)]
        vals = [acc_ref[ts[u]] + ye_ref[g * CB_U + u] for u in range(CB_U)]
        for u in range(CB_U):
            acc_ref[ts[u]] = vals[u]
        return carry

    lax.fori_loop(0, cap // CB_U, rmw, 0)

    @pl.when(e == pl.num_programs(1) - 1)
    def _():
        gamma = g_ref[...]

        def norm(r, carry):
            r0 = pl.multiple_of(r * CB_NB, CB_NB)
            x = acc_ref[pl.ds(r0, CB_NB)]
            ms = jnp.mean(x * x, axis=-1, keepdims=True)
            acc_ref[pl.ds(r0, CB_NB)] = x * lax.rsqrt(ms + RMS_EPS) * gamma
            return carry

        lax.fori_loop(CH // CB_NB, tp // CB_NB, norm, 0)
        cp = pltpu.make_async_copy(acc_ref.at[pl.ds(CH, tp - CH)], o_hbm.at[b], sem)
        cp.start()
        cp.wait()


def _combine(idx_local, h1, ye, gf, cap, slots):
    b, tp, _ = h1.shape
    assert cap % CB_U == 0
    return pl.pallas_call(
        functools.partial(_combine_kernel, cap=cap, tp=tp),
        grid=(b, N_EXPERTS),
        in_specs=[pl.BlockSpec((1, 1, slots), lambda bi, e: (bi * N_EXPERTS + e, 0, 0), memory_space=pltpu.SMEM),
                  pl.BlockSpec(memory_space=pl.ANY),
                  pl.BlockSpec((None, slots, 1, D_MODEL), lambda bi, e: (bi * N_EXPERTS + e, 0, 0, 0)),
                  pl.BlockSpec((1, 1, D_MODEL), lambda bi, e: (0, 0, 0))],
        out_specs=pl.BlockSpec(memory_space=pl.ANY),
        out_shape=jax.ShapeDtypeStruct((b, tp - CH, 1, D_MODEL), F32),
        scratch_shapes=[pltpu.VMEM((tp, 1, D_MODEL), F32), pltpu.SemaphoreType.DMA(())],
        compiler_params=_cparams(("arbitrary", "arbitrary"), VMEM_LIMIT_BIG),
        name="combine",
    )(idx_local, h1.reshape(b, tp, 1, D_MODEL), ye.reshape(ye.shape[0], slots, 1, D_MODEL),
      gf.reshape(1, 1, D_MODEL))


def _rope_tables(tp):
    half = RET_QK_HEAD // 2
    pos = jnp.arange(tp, dtype=F32) - float(PAD)
    inv = ROPE_BASE ** (-jnp.arange(half, dtype=F32) / half)
    ang = pos[:, None] * inv[None, :]
    cos, sin = jnp.cos(ang), jnp.sin(ang)
    return jnp.concatenate([cos, cos], axis=1), jnp.concatenate([-sin, sin], axis=1)


def _filter_features(t_len):
    half = NFFT // 2
    q = np.arange(NFFT)
    r = CH * (q % CH) + q // CH
    p_main = np.where(r < half, r, NFFT - r)
    valid_main = (r != half).astype(np.float32)
    m = np.arange(FAR)
    pad = np.zeros(HALF - 2 * FAR, np.int64)
    p_far = np.concatenate([half + m, half - FAR + m, pad, half + m, half - m, pad])
    valid_far = np.ones(CH, np.float32)
    valid_far[HALF + FAR] = 0.0
    p = jnp.asarray(np.concatenate([p_main, p_far]).astype(np.float32))
    valid = jnp.asarray(np.concatenate([valid_main, valid_far]))
    t_norm = p / (t_len - 1)
    bands = (HY_EMB_DIM - 1) // 2
    fr = jnp.linspace(1e-4, bands - 1, bands, dtype=F32)
    ang = (2.0 * math.pi * p / t_len)[:, None] * fr[None, :]
    feat = jnp.concatenate([t_norm[:, None], jnp.cos(ang), -jnp.sin(ang), valid[:, None]], axis=-1)
    return jnp.pad(feat, ((0, 0), (0, FEAT_W - feat.shape[1])))


def kernel(x, meta_tokens, norm1_g, w_in, ret_decay_fwd, ret_decay_bwd, ret_head_norm_g, w_ret_out,
           hy_conv_w, hy_conv_b, hy_filt_w1, hy_filt_b1, hy_filt_w2, hy_filt_b2, hy_filt_w3, hy_filt_b3,
           hy_filt_freq, hy_filt_w4, hy_skip, w_hy_out, w_o, norm2_g, w_router, w_exp_gate, w_exp_up,
           w_exp_down, final_norm_g):
    b, seq, d = x.shape
    t_len = seq + N_META
    tp = PAD + t_len
    assert d == D_MODEL and tp % IP_TM == 0 and tp % CH == 0 and t_len - NFFT // 2 == FAR
    cap = EC_CAPACITY * t_len // N_EXPERTS
    slots = -(-cap // 16) * 16
    l = 0

    meta = jnp.broadcast_to(meta_tokens[None].astype(x.dtype), (b, N_META, d))
    h0 = jnp.concatenate([jnp.zeros((b, PAD, d), x.dtype), meta, x], axis=1)

    cs, sn = _rope_tables(tp)
    proj = _in_proj(h0.reshape(b * tp, d), norm1_g[l][None], w_in[l].astype(BF16), cs, sn, tp)
    proj3 = proj.reshape(b, tp, IN_PROJ_W)

    lf = jax.nn.log_sigmoid(ret_decay_fwd[l].astype(F32))
    lb = jax.nn.log_sigmoid(ret_decay_bwd[l].astype(F32))
    og = _retention(proj3, lf, lb, ret_head_norm_g[l][None], tp)

    x0c, z = _hy_prep(proj3, hy_conv_w[l], hy_conv_b[l][None], tp)

    feat = _filter_features(t_len)
    w1p = jnp.pad(hy_filt_w1[l].astype(F32), ((0, FEAT_W - HY_EMB_DIM), (0, 0)))
    max_decay = math.log(HY_DECAY_TARGET) / HY_FAST_DECAY_PCT
    min_decay = math.log(HY_DECAY_TARGET) / HY_SLOW_DECAY_PCT
    dl = jnp.abs(jnp.linspace(min_decay, max_decay, D_MODEL, dtype=F32))[None]
    fargs = (w1p, hy_filt_b1[l][None].astype(F32), hy_filt_w2[l].astype(F32), hy_filt_b2[l][None].astype(F32),
             hy_filt_w3[l].astype(F32), hy_filt_b3[l][None].astype(F32), hy_filt_freq[l][None].astype(F32),
             hy_filt_w4[l].astype(F32), dl)
    g = _filters(feat[:NFFT], *fargs, FT_ROWS)
    h_far = _filters(feat[NFFT:], *fargs, CH)
    yc = _fft_conv(z, g)
    corr = _far_correction(h_far, z, tp)

    w_router_p = jnp.pad(w_router[l].astype(F32), ((0, 0), (0, ROUTER_W - N_EXPERTS)))
    h1, xn2, aff = _merge(h0, og, x0c, z, yc, proj3, corr, hy_skip[l][None].astype(F32),
                          w_ret_out[l].astype(BF16), w_hy_out[l].astype(BF16), w_o[l].astype(BF16),
                          norm2_g[l][None].astype(F32), w_router_p, tp)

    idx, gate = _topk(aff, cap, slots)
    live = (jnp.arange(slots) < cap)[None, None, :]
    idx_local = jnp.where(live, idx, PAD)
    idx_flat = idx_local + (jnp.arange(b * N_EXPERTS, dtype=I32) // N_EXPERTS * tp)[:, None, None]
    ye = _moe_ffn(idx_flat, gate, xn2.reshape(b * tp, d),
                  w_exp_gate[l], w_exp_up[l], w_exp_down[l], slots)
    out = _combine(idx_local, h1, ye, final_norm_g.astype(F32), cap, slots)
    return out.reshape(b, seq, d)
```

```python
import functools
import math

import numpy as np
import jax
import jax.numpy as jnp
from jax import lax
from jax.experimental import pallas as pl
from jax.experimental.pallas import tpu as pltpu

F32 = jnp.float32
BF16 = jnp.bfloat16
I32 = jnp.int32
HIGHEST = lax.Precision.HIGHEST

D_MODEL = 1024
N_META = 16
RET_HEADS = 4
RET_QK_HEAD = 128
RET_V_HEAD = 256
ROPE_BASE = 10000.0
HY_EMB_DIM = 33
HY_FILTER_ORDER = 64
HY_FAST_DECAY_PCT = 0.3
HY_SLOW_DECAY_PCT = 1.5
HY_DECAY_TARGET = 1e-2
N_EXPERTS = 16
EC_CAPACITY = 2
D_FF = 2 * D_MODEL
RMS_EPS = 1e-6
IN_PROJ_W = 8192

CH = 128
PAD = CH - N_META
NFFT = 16384
ZCH = 80
FAR = 16

VMEM_LIMIT_BIG = 56 * 1024 * 1024
VMEM_LIMIT_MID = 40 * 1024 * 1024


def _cparams(sem, vmem=VMEM_LIMIT_MID):
    return pltpu.CompilerParams(dimension_semantics=sem, vmem_limit_bytes=vmem)


IP_TM = 640
IP_TN = 1024
IP_CN = 256


def _inproj_kernel(x_ref, g_ref, w_ref, cs_ref, sn_ref, o_ref, xn_ref):
    j = pl.program_id(1)

    @pl.when(j == 0)
    def _():
        x = x_ref[...]
        ms = jnp.mean(x * x, axis=-1, keepdims=True)
        xn_ref[...] = (x * lax.rsqrt(ms + RMS_EPS) * g_ref[...]).astype(BF16)

    def run(epilogue):
        for c in range(IP_TN // IP_CN):
            cols = slice(c * IP_CN, (c + 1) * IP_CN)
            acc = jnp.dot(xn_ref[...], w_ref[:, cols], preferred_element_type=F32)
            epilogue(c, cols, acc)

    def rotary(c, cols, acc):
        scale = 1.0 if c < (IP_TN // IP_CN) // 2 else RET_QK_HEAD ** -0.5
        cs = cs_ref[...] * scale
        sn = sn_ref[...] * scale
        for hh in range(IP_CN // RET_QK_HEAD):
            xh = acc[:, hh * RET_QK_HEAD:(hh + 1) * RET_QK_HEAD]
            rot = xh * cs + pltpu.roll(xh, RET_QK_HEAD // 2, axis=1) * sn
            lo = c * IP_CN + hh * RET_QK_HEAD
            o_ref[:, lo:lo + RET_QK_HEAD] = rot.astype(BF16)

    def raw(c, cols, acc):
        o_ref[:, cols] = acc.astype(BF16)

    def swish(c, cols, acc):
        o_ref[:, cols] = (acc * jax.nn.sigmoid(acc)).astype(BF16)

    def sigm(c, cols, acc):
        o_ref[:, cols] = jax.nn.sigmoid(acc).astype(BF16)

    pl.when(j == 0)(lambda: run(rotary))
    pl.when(jnp.logical_or(j == 1, jnp.logical_and(j >= 3, j < 6)))(lambda: run(raw))
    pl.when(j == 2)(lambda: run(swish))
    pl.when(j >= 6)(lambda: run(sigm))


def _in_proj(h0, g1, w_in_bf, cs, sn, tp):
    n_rows = h0.shape[0]
    nt_b = tp // IP_TM
    return pl.pallas_call(
        _inproj_kernel,
        grid=(n_rows // IP_TM, IN_PROJ_W // IP_TN),
        in_specs=[
            pl.BlockSpec((IP_TM, D_MODEL), lambda i, j: (i, 0)),
            pl.BlockSpec((1, D_MODEL), lambda i, j: (0, 0)),
            pl.BlockSpec((D_MODEL, IP_TN), lambda i, j: (0, j)),
            pl.BlockSpec((IP_TM, RET_QK_HEAD), lambda i, j: (i % nt_b, 0)),
            pl.BlockSpec((IP_TM, RET_QK_HEAD), lambda i, j: (i % nt_b, 0)),
        ],
        out_specs=pl.BlockSpec((IP_TM, IP_TN), lambda i, j: (i, j)),
        out_shape=jax.ShapeDtypeStruct((n_rows, IN_PROJ_W), BF16),
        scratch_shapes=[pltpu.VMEM((IP_TM, D_MODEL), BF16)],
        compiler_params=_cparams(("arbitrary", "arbitrary")),
        name="in_proj",
    )(h0, g1, w_in_bf, cs, sn)


RET_UNROLL = 5


def _ret_kernel(lf_ref, lb_ref, q_ref, k_ref, v_ref, gr_ref, gn_ref, o_ref, ob_ref, s_ref, *, nch):
    h = pl.program_id(1)
    lf = lf_ref[h]
    lb = lb_ref[h]
    ri = lax.broadcasted_iota(I32, (CH, CH), 0).astype(F32)
    ci = lax.broadcasted_iota(I32, (CH, CH), 1).astype(F32)
    diff = ri - ci
    mask = jnp.exp(jnp.where(diff >= 0, lf * diff, -lb * diff))
    w_end = jnp.exp(lf * (CH - 1.0 - ri))
    w_start = jnp.exp(lb * ri)
    qw_f = jnp.exp(lf * (ri + 1.0))
    qw_b = jnp.exp(lb * (CH - ri))
    dec_f = jnp.exp(jnp.full((CH, RET_V_HEAD), lf * CH, F32))
    dec_b = jnp.exp(jnp.full((CH, RET_V_HEAD), lb * CH, F32))
    tn_dims = (((0,), (0,)), ((), ()))
    nt_dims = (((1,), (1,)), ((), ()))

    s_ref[...] = jnp.zeros_like(s_ref)

    def bwd(it, carry):
        s = s_ref[...]
        for u in range(RET_UNROLL):
            n = nch - 1 - (it * RET_UNROLL + u)
            r0 = pl.multiple_of(n * CH, CH)
            q = q_ref[0, pl.ds(r0, CH), :].astype(F32)
            k = k_ref[0, pl.ds(r0, CH), :].astype(F32)
            v = v_ref[0, pl.ds(r0, CH), :]
            ob_ref[pl.ds(r0, CH), :] = jnp.dot((q * qw_b).astype(BF16), s.astype(BF16),
                                               preferred_element_type=F32)
            a = lax.dot_general((k * w_start).astype(BF16), v, tn_dims, preferred_element_type=F32)
            s = s * dec_b + a
        s_ref[...] = s
        return carry

    lax.fori_loop(0, nch // RET_UNROLL, bwd, 0)

    s_ref[...] = jnp.zeros_like(s_ref)
    gn = gn_ref[...]

    def fwd(it, carry):
        s = s_ref[...]
        for u in range(RET_UNROLL):
            n = it * RET_UNROLL + u
            r0 = pl.multiple_of(n * CH, CH)
            qb = q_ref[0, pl.ds(r0, CH), :]
            kb = k_ref[0, pl.ds(r0, CH), :]
            v = v_ref[0, pl.ds(r0, CH), :]
            q = qb.astype(F32)
            k = kb.astype(F32)
            scores = lax.dot_general(qb, kb, nt_dims, preferred_element_type=F32) * mask
            o = jnp.dot(scores.astype(BF16), v, preferred_element_type=F32)
            o = o + jnp.dot((q * qw_f).astype(BF16), s.astype(BF16), preferred_element_type=F32)
            o = o + ob_ref[pl.ds(r0, CH), :]
            a = lax.dot_general((k * w_end).astype(BF16), v, tn_dims, preferred_element_type=F32)
            s = s * dec_f + a
            y = o * lax.rsqrt(jnp.mean(o * o, axis=-1, keepdims=True) + RMS_EPS) * gn
            o_ref[0, pl.ds(r0, CH), :] = (y * gr_ref[0, pl.ds(r0, CH), :].astype(F32)).astype(BF16)
        s_ref[...] = s
        return carry

    lax.fori_loop(0, nch // RET_UNROLL, fwd, 0)


def _retention(proj3, lf, lb, gn, tp):
    b = proj3.shape[0]
    nch = tp // CH
    qk_blocks = (RET_HEADS * RET_QK_HEAD) // RET_QK_HEAD
    v_blk0 = (2 * RET_HEADS * RET_QK_HEAD) // RET_V_HEAD
    g_blk0 = v_blk0 + RET_HEADS
    smem = pl.BlockSpec(memory_space=pltpu.SMEM)
    return pl.pallas_call(
        functools.partial(_ret_kernel, nch=nch),
        grid=(b, RET_HEADS),
        in_specs=[
            smem, smem,
            pl.BlockSpec((1, tp, RET_QK_HEAD), lambda bi, h: (bi, 0, h)),
            pl.BlockSpec((1, tp, RET_QK_HEAD), lambda bi, h: (bi, 0, qk_blocks + h)),
            pl.BlockSpec((1, tp, RET_V_HEAD), lambda bi, h: (bi, 0, v_blk0 + h)),
            pl.BlockSpec((1, tp, RET_V_HEAD), lambda bi, h: (bi, 0, g_blk0 + h)),
            pl.BlockSpec((1, RET_V_HEAD), lambda bi, h: (0, h)),
        ],
        out_specs=pl.BlockSpec((1, tp, RET_V_HEAD), lambda bi, h: (bi, 0, h)),
        out_shape=jax.ShapeDtypeStruct((b, tp, RET_HEADS * RET_V_HEAD), BF16),
        scratch_shapes=[pltpu.VMEM((tp, RET_V_HEAD), F32), pltpu.VMEM((RET_QK_HEAD, RET_V_HEAD), F32)],
        compiler_params=_cparams(("arbitrary", "arbitrary"), VMEM_LIMIT_BIG),
        name="retention",
    )(lf, lb, proj3, proj3, proj3, proj3, gn)


HP_CW = 128


def _hyprep_kernel(u0_ref, u1_ref, u2_ref, w0_ref, w1_ref, w2_ref, b0_ref, b1_ref, b2_ref,
                   x0_ref, z_ref, *, nch, tz):
    rows = lax.broadcasted_iota(I32, (CH, HP_CW), 0)
    halo = 16

    def conv(u_ref, w_ref, b_ref, n, r0):
        cur = u_ref[0, pl.ds(r0, CH), :].astype(F32)
        rp = pl.multiple_of(jnp.maximum(r0 - halo, 0), halo)
        rn = pl.multiple_of(jnp.minimum(r0 + CH, (nch - 1) * CH), halo)
        prev = u_ref[0, pl.ds(rp, halo), :].astype(F32)[halo - 1:halo, :]
        nxt = u_ref[0, pl.ds(rn, halo), :].astype(F32)[0:1, :]
        prev = jnp.where(n > 0, prev, 0.0)
        nxt = jnp.where(n < nch - 1, nxt, 0.0)
        up = jnp.where(rows == 0, prev, pltpu.roll(cur, 1, axis=0))
        dn = jnp.where(rows == CH - 1, nxt, pltpu.roll(cur, CH - 1, axis=0))
        w = w_ref[...]
        return up * w[0:1, :] + cur * w[1:2, :] + dn * w[2:3, :] + b_ref[...]

    def body(n, carry):
        r0 = pl.multiple_of(n * CH, CH)
        x0 = conv(u0_ref, w0_ref, b0_ref, n, r0)
        x1 = conv(u1_ref, w1_ref, b1_ref, n, r0)
        vv = conv(u2_ref, w2_ref, b2_ref, n, r0)
        z = jnp.where(rows + r0 >= PAD, x1 * vv, 0.0)
        x0_ref[0, pl.ds(r0, CH), :] = x0.astype(BF16)
        z_ref[0, pl.ds(r0, CH), :] = z.astype(BF16)
        return carry

    lax.fori_loop(0, nch, body, 0)
    z_ref[0, nch * CH:tz, :] = jnp.zeros((tz - nch * CH, HP_CW), BF16)


def _hy_prep(proj3, conv_w, conv_b, tp):
    b = proj3.shape[0]
    nch = tp // CH
    tz = ZCH * CH
    ncb = D_MODEL // HP_CW
    u_blk0 = 3072 // HP_CW
    uspec = lambda s: pl.BlockSpec((1, tp, HP_CW), lambda bi, c: (bi, 0, u_blk0 + s * ncb + c))
    wspec = lambda s: pl.BlockSpec((3, HP_CW), lambda bi, c: (0, s * ncb + c))
    bspec = lambda s: pl.BlockSpec((1, HP_CW), lambda bi, c: (0, s * ncb + c))
    return pl.pallas_call(
        functools.partial(_hyprep_kernel, nch=nch, tz=tz),
        grid=(b, ncb),
        in_specs=[uspec(0), uspec(1), uspec(2), wspec(0), wspec(1), wspec(2), bspec(0), bspec(1), bspec(2)],
        out_specs=[pl.BlockSpec((1, tp, HP_CW), lambda bi, c: (bi, 0, c)),
                   pl.BlockSpec((1, tz, HP_CW), lambda bi, c: (bi, 0, c))],
        out_shape=[jax.ShapeDtypeStruct((b, tp, D_MODEL), BF16),
                   jax.ShapeDtypeStruct((b, tz, D_MODEL), BF16)],
        compiler_params=_cparams(("arbitrary", "arbitrary")),
        name="hy_prep",
    )(proj3, proj3, proj3, conv_w, conv_w, conv_w, conv_b, conv_b, conv_b)


FEAT_W = 128


FT_ROWS = 512
HALF = CH // 2


def _filter_kernel(feat_ref, w1_ref, b1_ref, w2_ref, b2_ref, w3_ref, b3_ref, fq_ref, w4_ref, dl_ref, o_ref,
                   *, groups):
    feat = feat_ref[...]
    fq = fq_ref[...]
    hdn = jnp.sin(fq * (jnp.dot(feat, w1_ref[...], precision=HIGHEST, preferred_element_type=F32) + b1_ref[...]))
    hdn = jnp.sin(fq * (jnp.dot(hdn, w2_ref[...], precision=HIGHEST, preferred_element_type=F32) + b2_ref[...]))
    hdn = jnp.sin(fq * (jnp.dot(hdn, w3_ref[...], precision=HIGHEST, preferred_element_type=F32) + b3_ref[...]))
    scale = jnp.exp(-feat[:, 0:1] * dl_ref[...]) * feat[:, HY_EMB_DIM:HY_EMB_DIM + 1]
    for d in range(2):
        rows = [slice(g * CH + d * HALF, g * CH + (d + 1) * HALF) for g in range(groups)]
        hd = jnp.concatenate([hdn[r] for r in rows], axis=0)
        filt = jnp.dot(hd, w4_ref[:, d * D_MODEL:(d + 1) * D_MODEL], precision=HIGHEST, preferred_element_type=F32)
        for g, r in enumerate(rows):
            o_ref[r, :] = filt[g * HALF:(g + 1) * HALF] * scale[r]


def _filters(feat, w1p, b1, w2, b2, w3, b3, fq, w4, dl, rows):
    n_rows = feat.shape[0]
    full = lambda a: pl.BlockSpec(a.shape, lambda i: (0,) * a.ndim)
    return pl.pallas_call(
        functools.partial(_filter_kernel, groups=rows // CH),
        grid=(n_rows // rows,),
        in_specs=[pl.BlockSpec((rows, FEAT_W), lambda i: (i, 0)),
                  full(w1p), full(b1), full(w2), full(b2), full(w3), full(b3), full(fq), full(w4), full(dl)],
        out_specs=pl.BlockSpec((rows, D_MODEL), lambda i: (i, 0)),
        out_shape=jax.ShapeDtypeStruct((n_rows, D_MODEL), F32),
        compiler_params=_cparams(("arbitrary",)),
        name="hy_filter",
    )(feat, w1p, b1, w2, b2, w3, b3, fq, w4, dl)


def _fft_a_data_kernel(m_ref, z_ref, o_ref):
    x = jnp.concatenate([z_ref[0], z_ref[1]], axis=0)
    o_ref[0] = jnp.dot(m_ref[0], x, preferred_element_type=F32).astype(BF16)


def _fft_a_filt_kernel(m_ref, g_ref, o_ref):
    o_ref[0] = jnp.dot(m_ref[0], g_ref[...].astype(BF16), preferred_element_type=F32).astype(BF16)


def _fft_b_filt_kernel(f_ref, yr_ref, yi_ref, o_ref):
    y = jnp.concatenate([yr_ref[...], yi_ref[...]], axis=0)
    o_ref[0] = jnp.dot(f_ref[...], y, preferred_element_type=F32) * (1.0 / NFFT)


def _fft_b_data_kernel(f_ref, fi_ref, yr_ref, yi_ref, g_ref, o_ref):
    y = jnp.concatenate([yr_ref[...], yi_ref[...]], axis=0)
    x = jnp.dot(f_ref[...], y, preferred_element_type=F32)
    xr, xi = x[:CH], x[CH:]
    gr, gi = g_ref[0, :CH], g_ref[0, CH:]
    p = jnp.concatenate([xr * gr - xi * gi, xr * gi + xi * gr], axis=0).astype(BF16)
    o_ref[0] = jnp.dot(fi_ref[...], p, preferred_element_type=F32).astype(BF16)


def _fft_a_inv_kernel(m_ref, ur_ref, ui_ref, o_ref):
    u = jnp.concatenate([ur_ref[...], ui_ref[...]], axis=0)
    y = jnp.dot(m_ref[0], u, preferred_element_type=F32).astype(BF16)
    o_ref[0] = y[:ZCH]
    o_ref[1] = y[ZCH:]


def _dft_tables():
    n2 = np.arange(CH)[:, None, None]
    k1 = np.arange(CH)[None, :, None]

    def theta(n1_count):
        n1 = np.arange(n1_count)[None, None, :]
        return 2.0 * np.pi * ((k1 * (CH * n1 + n2)) % NFFT) / NFFT

    th = theta(ZCH)
    c, s = np.cos(th), np.sin(th)
    m_a = np.concatenate([np.concatenate([c, s], axis=2), np.concatenate([-s, c], axis=2)], axis=1)
    m_ainv = np.transpose(m_a, (0, 2, 1))
    th = theta(CH)
    m_af = np.concatenate([np.cos(th), -np.sin(th)], axis=1)
    a = 2.0 * np.pi * ((np.arange(CH)[:, None] * np.arange(CH)[None, :]) % CH) / CH
    c, s = np.cos(a), np.sin(a)
    f2 = np.block([[c, s], [-s, c]])
    f2i = np.block([[c, -s], [s, c]])
    f = lambda t: jnp.asarray(t.astype(np.float32)).astype(BF16)
    return f(m_a), f(m_ainv), f(m_af), f(f2), f(f2i)


def _fft_conv(z, g):
    c = z.shape[-1]
    m_a, m_ainv, m_af, f2, f2i = _dft_tables()
    cp = _cparams(("arbitrary",))
    full2 = pl.BlockSpec((2 * CH, 2 * CH), lambda i: (0, 0))
    col_re = pl.BlockSpec((CH, c), lambda i: (0, i))
    col_im = pl.BlockSpec((CH, c), lambda i: (0, CH + i))
    blk = pl.BlockSpec((1, 2 * CH, c), lambda i: (i, 0, 0))
    spec_shape = jax.ShapeDtypeStruct((CH, 2 * CH, c), BF16)

    yg = pl.pallas_call(
        _fft_a_filt_kernel, grid=(CH,),
        in_specs=[pl.BlockSpec((1, 2 * CH, CH), lambda i: (i, 0, 0)), pl.BlockSpec((CH, c), lambda i: (i, 0))],
        out_specs=blk, out_shape=spec_shape, compiler_params=cp, name="fft_a_filt",
    )(m_af, g)
    yg2 = yg.reshape(CH, 2 * CH * c)
    gh = pl.pallas_call(
        _fft_b_filt_kernel, grid=(CH,),
        in_specs=[full2, col_re, col_im],
        out_specs=blk, out_shape=jax.ShapeDtypeStruct((CH, 2 * CH, c), F32),
        compiler_params=cp, name="fft_b_filt",
    )(f2, yg2, yg2)

    y = pl.pallas_call(
        _fft_a_data_kernel, grid=(CH,),
        in_specs=[pl.BlockSpec((1, 2 * CH, 2 * ZCH), lambda i: (i, 0, 0)),
                  pl.BlockSpec((2, ZCH, c), lambda i: (0, 0, i))],
        out_specs=blk, out_shape=spec_shape, compiler_params=cp, name="fft_a_data",
    )(m_a, z.reshape(2, ZCH, CH * c))
    y2 = y.reshape(CH, 2 * CH * c)
    u = pl.pallas_call(
        _fft_b_data_kernel, grid=(CH,),
        in_specs=[full2, full2, col_re, col_im, blk],
        out_specs=blk, out_shape=spec_shape, compiler_params=cp, name="fft_b_data",
    )(f2, f2i, y2, y2, gh)
    u2 = u.reshape(CH, 2 * CH * c)
    yc = pl.pallas_call(
        _fft_a_inv_kernel, grid=(CH,),
        in_specs=[pl.BlockSpec((1, 2 * ZCH, 2 * CH), lambda i: (i, 0, 0)), col_re, col_im],
        out_specs=pl.BlockSpec((2, ZCH, c), lambda i: (0, 0, i)),
        out_shape=jax.ShapeDtypeStruct((2, ZCH, CH * c), BF16),
        compiler_params=cp, name="fft_a_inv",
    )(m_ainv, u2, u2)
    return yc.reshape(2, ZCH * CH, c)


def _far_kernel(far_ref, zm_ref, zl_ref, o_ref):
    g_hi = far_ref[HALF + FAR:HALF + 2 * FAR, :]
    ef = far_ref[0:FAR, :] - g_hi
    g_lo = far_ref[FAR:2 * FAR, :]
    hb = far_ref[HALF:HALF + FAR, :]
    zm = zm_ref[0, PAD:CH, :].astype(F32)
    zl = zl_ref[0, PAD:CH, :].astype(F32)
    row = lambda a, i: a[i:i + 1, :]
    eb = [row(hb, 0) - row(g_hi, 0)] + [row(hb, m) - row(g_lo, FAR - m) for m in range(1, FAR)]
    for j in range(FAR):
        acc_f = row(ef, j) * row(zm, 0)
        for i in range(1, j + 1):
            acc_f = acc_f + row(ef, j - i) * row(zm, i)
        o_ref[0, 0, j:j + 1, :] = acc_f
        acc_b = eb[0] * row(zl, j)
        for i in range(j + 1, FAR):
            acc_b = acc_b + eb[i - j] * row(zl, i)
        o_ref[0, 1, j:j + 1, :] = acc_b


def _far_correction(h_far, z, tp):
    b, _, c = z.shape
    last = tp // CH - 1
    return pl.pallas_call(
        _far_kernel, grid=(b,),
        in_specs=[pl.BlockSpec((CH, c), lambda bi: (0, 0)),
                  pl.BlockSpec((1, CH, c), lambda bi: (bi, 0, 0)),
                  pl.BlockSpec((1, CH, c), lambda bi: (bi, last, 0))],
        out_specs=pl.BlockSpec((1, 2, FAR, c), lambda bi: (bi, 0, 0, 0)),
        out_shape=jax.ShapeDtypeStruct((b, 2, FAR, c), F32),
        compiler_params=_cparams(("arbitrary",)),
        name="hy_far",
    )(h_far, z, z)


MG_TM = 640
MG_CN = 256
ROUTER_W = 128
ROW_SUB, ROW_LANE = 8, 128


def _merge_kernel(h_ref, og_ref, x0_ref, z_ref, yc_ref, ga_ref, gb_ref, corr_ref, skip_ref,
                  wr_ref, wh_ref, wo_ref, g2_ref, wrt_ref,
                  h1_ref, xn_ref, aff_ref, pre_ref, mix_ref, h1s_ref, *, nt_b):
    i = pl.program_id(0)
    ib = i % nt_b
    pre_ref[...] = yc_ref[0].astype(F32) + z_ref[0].astype(F32) * skip_ref[...]

    @pl.when(ib == 0)
    def _():
        pre_ref[PAD:CH, :] += corr_ref[0, 1]

    @pl.when(ib == nt_b - 1)
    def _():
        pre_ref[MG_TM - FAR:MG_TM, :] += corr_ref[0, 0]

    chunks = [slice(c * MG_CN, (c + 1) * MG_CN) for c in range(D_MODEL // MG_CN)]
    og = og_ref[0]
    pre = (x0_ref[0].astype(F32) * pre_ref[...]).astype(BF16)
    for cols in chunks:
        ya = jnp.dot(og, wr_ref[:, cols], preferred_element_type=F32)
        yb = jnp.dot(pre, wh_ref[:, cols], preferred_element_type=F32)
        mix_ref[:, cols] = (ga_ref[0, :, cols].astype(F32) * ya + gb_ref[0, :, cols].astype(F32) * yb).astype(BF16)
    mixed = mix_ref[...]
    ss = jnp.zeros((MG_TM, 1), F32)
    for cols in chunks:
        h1 = h_ref[0, :, cols] + jnp.dot(mixed, wo_ref[:, cols], preferred_element_type=F32)
        h1s_ref[:, cols] = h1
        ss = ss + jnp.sum(h1 * h1, axis=-1, keepdims=True)
    rinv = lax.rsqrt(ss * (1.0 / D_MODEL) + RMS_EPS)
    h1 = h1s_ref[...]
    h1_ref[0] = h1.reshape(MG_TM, ROW_SUB, ROW_LANE)
    xn = h1 * rinv * g2_ref[...]
    xn_ref[0] = xn
    xh = xn.astype(BF16)
    xl = (xn - xh.astype(F32)).astype(BF16)
    logits = (jnp.dot(xh, wrt_ref[0], preferred_element_type=F32)
              + jnp.dot(xl, wrt_ref[0], preferred_element_type=F32)
              + jnp.dot(xh, wrt_ref[1], preferred_element_type=F32))
    lane = lax.broadcasted_iota(I32, logits.shape, 1)
    logits = jnp.where(lane < N_EXPERTS, logits, -jnp.inf)
    m = jnp.max(logits, axis=-1, keepdims=True)
    e = jnp.exp(logits - m)
    aff = e / jnp.sum(e, axis=-1, keepdims=True)
    rows = lax.broadcasted_iota(I32, logits.shape, 0) + ib * MG_TM
    aff_ref[0] = jnp.where(rows >= PAD, aff, -1.0)


def _merge(h0, og, x0c, z, yc, proj3, corr, skip, w_ret, w_hy, w_o, g2, w_router_p, tp):
    b = h0.shape[0]
    nt_b = tp // MG_TM
    ga_blk0 = 6144 // D_MODEL
    row = lambda w: pl.BlockSpec((1, MG_TM, w), lambda i: (i // nt_b, i % nt_b, 0))
    full = lambda a: pl.BlockSpec(a.shape, lambda i: (0,) * a.ndim)
    return pl.pallas_call(
        functools.partial(_merge_kernel, nt_b=nt_b),
        grid=(b * nt_b,),
        in_specs=[row(D_MODEL), row(D_MODEL), row(D_MODEL), row(D_MODEL), row(D_MODEL),
                  pl.BlockSpec((1, MG_TM, D_MODEL), lambda i: (i // nt_b, i % nt_b, ga_blk0)),
                  pl.BlockSpec((1, MG_TM, D_MODEL), lambda i: (i // nt_b, i % nt_b, ga_blk0 + 1)),
                  pl.BlockSpec((1, 2, FAR, D_MODEL), lambda i: (i // nt_b, 0, 0, 0)),
                  full(skip), full(w_ret), full(w_hy), full(w_o), full(g2), full(w_router_p)],
        out_specs=[pl.BlockSpec((1, MG_TM, ROW_SUB, ROW_LANE), lambda i: (i // nt_b, i % nt_b, 0, 0)),
                   row(D_MODEL), row(ROUTER_W)],
        out_shape=[jax.ShapeDtypeStruct((b, tp, ROW_SUB, ROW_LANE), F32),
                   jax.ShapeDtypeStruct((b, tp, D_MODEL), F32),
                   jax.ShapeDtypeStruct((b, tp, ROUTER_W), F32)],
        scratch_shapes=[pltpu.VMEM((MG_TM, D_MODEL), F32), pltpu.VMEM((MG_TM, D_MODEL), BF16),
                        pltpu.VMEM((MG_TM, D_MODEL), F32)],
        compiler_params=_cparams(("arbitrary",), VMEM_LIMIT_BIG),
        name="merge",
    )(h0, og, x0c, z, yc, proj3, proj3, corr, skip, w_ret, w_hy, w_o, g2, w_router_p)


TK_W = 128


def _select_kernel(aff_ref, low_ref, slot_ref, offs_ref, *, cap, nch):
    def chunk(c):
        r0 = pl.multiple_of(c * CH, CH)
        return aff_ref[0, pl.ds(r0, CH), :]

    def count(pred):
        def body(c, acc):
            return acc + pred(chunk(c)).astype(I32)

        acc = lax.fori_loop(0, nch, body, jnp.zeros((CH, ROUTER_W), I32))
        return jnp.sum(acc, axis=0, keepdims=True)

    def search(it, bits):
        cand = bits | jnp.left_shift(1, 29 - it)
        cand_f = pltpu.bitcast(cand, F32)
        return jnp.where(count(lambda a: a >= cand_f) >= cap, cand, bits)

    thr = pltpu.bitcast(lax.fori_loop(0, 30, search, jnp.zeros((1, ROUTER_W), I32)), F32)
    need = (cap - count(lambda a: a > thr)).astype(F32)
    low = low_ref[...]

    def scan(c, carry):
        c_eq, c_sel = carry
        a = chunk(c)
        eq = a == thr
        eq_f = eq.astype(F32)
        eq_rank = jnp.dot(low, eq_f, preferred_element_type=F32) + c_eq
        sel = jnp.logical_or(a > thr, jnp.logical_and(eq, eq_rank < need))
        sel_f = sel.astype(F32)
        slot = jnp.dot(low, sel_f, preferred_element_type=F32) + c_sel
        r0 = pl.multiple_of(c * CH, CH)
        slot_ref[0, pl.ds(r0, CH), :] = jnp.where(sel, slot, -1.0).astype(I32)
        offs_ref[0, c] = c_sel.astype(I32)
        return (c_eq + jnp.sum(eq_f, axis=0, keepdims=True), c_sel + jnp.sum(sel_f, axis=0, keepdims=True))

    zero = jnp.zeros((1, ROUTER_W), F32)
    lax.fori_loop(0, nch, scan, (zero, zero))


def _extract_kernel(offs_ref, slot_ref, aff_ref, idx_ref, gate_ref, *, nch, nwin):
    idx_ref[...] = jnp.zeros_like(idx_ref)
    gate_ref[...] = jnp.zeros_like(gate_ref)
    lane = lax.broadcasted_iota(I32, (CH, TK_W), 1)
    trow = lax.broadcasted_iota(I32, (CH, TK_W), 0)

    def per_chunk(c, carry):
        r0 = pl.multiple_of(c * CH, CH)
        slots = slot_ref[0, pl.ds(r0, CH), :]
        affs = aff_ref[0, pl.ds(r0, CH), :]
        tpos = (trow + r0).astype(F32)
        for e in range(N_EXPERTS):
            col = jnp.broadcast_to(slots[:, e:e + 1], (CH, TK_W))
            gcol = jnp.broadcast_to(affs[:, e:e + 1], (CH, TK_W))
            w0 = offs_ref[0, 0, c * N_EXPERTS + e] // TK_W
            for dw in range(2):
                w = w0 + dw
                base = jnp.where(w < nwin, w * TK_W, -2 * TK_W)
                hit = col == lane + base
                row = e * nwin + jnp.minimum(w, nwin - 1)
                idx_ref[row] += jnp.sum(jnp.where(hit, tpos, 0.0), axis=0, keepdims=True)
                gate_ref[row] += jnp.sum(jnp.where(hit, gcol, 0.0), axis=0, keepdims=True)
        return carry

    lax.fori_loop(0, nch, per_chunk, 0)


def _topk(aff, cap, slots):
    b, tp, _ = aff.shape
    nch = tp // CH
    nwin = -(-slots // TK_W)
    low = jnp.asarray(np.tril(np.ones((CH, CH), np.float32), k=-1))
    slot, offs = pl.pallas_call(
        functools.partial(_select_kernel, cap=cap, nch=nch),
        grid=(b,),
        in_specs=[pl.BlockSpec((1, tp, ROUTER_W), lambda bi: (bi, 0, 0)),
                  pl.BlockSpec((CH, CH), lambda bi: (0, 0))],
        out_specs=[pl.BlockSpec((1, tp, ROUTER_W), lambda bi: (bi, 0, 0)),
                   pl.BlockSpec((1, nch, 1, ROUTER_W), lambda bi: (bi, 0, 0, 0))],
        out_shape=[jax.ShapeDtypeStruct((b, tp, ROUTER_W), I32),
                   jax.ShapeDtypeStruct((b, nch, 1, ROUTER_W), I32)],
        compiler_params=_cparams(("arbitrary",)),
        name="topk_select",
    )(aff, low)
    offs_s = offs[:, :, 0, :N_EXPERTS].reshape(b, 1, nch * N_EXPERTS)
    rows = pl.BlockSpec((None, N_EXPERTS * nwin, 1, TK_W), lambda bi: (bi, 0, 0, 0))
    out = jax.ShapeDtypeStruct((b, N_EXPERTS * nwin, 1, TK_W), F32)
    idx, gate = pl.pallas_call(
        functools.partial(_extract_kernel, nch=nch, nwin=nwin),
        grid=(b,),
        in_specs=[pl.BlockSpec((1, 1, nch * N_EXPERTS), lambda bi: (bi, 0, 0), memory_space=pltpu.SMEM),
                  pl.BlockSpec((1, tp, ROUTER_W), lambda bi: (bi, 0, 0)),
                  pl.BlockSpec((1, tp, ROUTER_W), lambda bi: (bi, 0, 0))],
        out_specs=[rows, rows],
        out_shape=[out, out],
        compiler_params=_cparams(("arbitrary",)),
        name="topk_extract",
    )(offs_s, slot, aff)
    idx = idx.reshape(b * N_EXPERTS, 1, nwin * TK_W)[:, :, :slots].astype(I32)
    gate = gate.reshape(b * N_EXPERTS, nwin * TK_W, 1)[:, :slots]
    return idx, gate


FF_TF = 1024
MOE_CN = 256


MOE_UNROLL = 4


def _moe_kernel(idx_ref, idx_next_ref, xn_hbm, gate_ref, wg_ref, wu_ref, wd_ref, o_ref,
                xe32_ref, xe_ref, hid_ref, acc_ref, sem,
                *, slots, nf):
    i = pl.program_id(0)
    f = pl.program_id(1)
    buf = i % 2
    share = slots // nf

    def row_copy(ids_ref, s, b):
        return pltpu.make_async_copy(xn_hbm.at[pl.ds(ids_ref[0, 0, s], 1), :],
                                     xe32_ref.at[b, pl.ds(s, 1), :], sem.at[b])

    def for_rows(lo, n, fn):
        def body(k, carry):
            for u in range(MOE_UNROLL):
                fn(lo + k * MOE_UNROLL + u)
            return carry

        lax.fori_loop(0, n // MOE_UNROLL, body, 0)

    @pl.when(jnp.logical_and(i == 0, f == 0))
    def _():
        for_rows(0, slots, lambda s: row_copy(idx_ref, s, 0).start())

    @pl.when(f == 0)
    def _():
        for_rows(0, slots, lambda s: row_copy(idx_ref, s, buf).wait())
        xe_ref[...] = xe32_ref[buf].astype(BF16)

    @pl.when(f == 0)
    def _():
        acc_ref[...] = jnp.zeros_like(acc_ref)

    n_up, n_down = FF_TF // MOE_CN, D_MODEL // MOE_CN
    per_chunk = share // (n_up + n_down)

    def prefetch(chunk):
        base = f * share + chunk * per_chunk
        for u in range(per_chunk):
            row_copy(idx_next_ref, base + u, 1 - buf).start()

    xe = xe_ref[...]
    for c in range(n_up):
        cols = slice(c * MOE_CN, (c + 1) * MOE_CN)
        gg = jnp.dot(xe, wg_ref[0, :, cols].astype(BF16), preferred_element_type=F32)
        uu = jnp.dot(xe, wu_ref[0, :, cols].astype(BF16), preferred_element_type=F32)
        hid_ref[:, cols] = (gg * jax.nn.sigmoid(gg) * uu).astype(BF16)
        prefetch(c)
    scale = jnp.where(f == pl.num_programs(1) - 1, gate_ref[0], 1.0)
    hid = hid_ref[...]
    for c in range(n_down):
        cols = slice(c * MOE_CN, (c + 1) * MOE_CN)
        part = jnp.dot(hid, wd_ref[0, :, cols].astype(BF16), preferred_element_type=F32)
        acc_ref[:, cols] = (acc_ref[:, cols] + part) * scale
        prefetch(n_up + c)

    @pl.when(f == pl.num_programs(1) - 1)
    def _():
        o_ref[0] = acc_ref[...].reshape(slots, ROW_SUB, ROW_LANE)

    @pl.when(jnp.logical_and(i == pl.num_programs(0) - 1, f == pl.num_programs(1) - 1))
    def _():
        for_rows(0, slots, lambda s: row_copy(idx_next_ref, s, 1 - buf).wait())


def _moe_ffn(idx, gate, xn_flat, w_gate, w_up, w_down, slots):
    be = idx.shape[0]
    nf = D_FF // FF_TF
    assert slots % MOE_UNROLL == 0 and slots % (nf * (FF_TF // MOE_CN + D_MODEL // MOE_CN)) == 0
    return pl.pallas_call(
        functools.partial(_moe_kernel, slots=slots, nf=nf),
        grid=(be, nf),
        in_specs=[pl.BlockSpec((1, 1, slots), lambda i, f: (i, 0, 0), memory_space=pltpu.SMEM),
                  pl.BlockSpec((1, 1, slots), lambda i, f: (jnp.minimum(i + 1, be - 1), 0, 0),
                               memory_space=pltpu.SMEM),
                  pl.BlockSpec(memory_space=pl.ANY),
                  pl.BlockSpec((1, slots, 1), lambda i, f: (i, 0, 0)),
                  pl.BlockSpec((1, D_MODEL, FF_TF), lambda i, f: (i % N_EXPERTS, 0, f)),
                  pl.BlockSpec((1, D_MODEL, FF_TF), lambda i, f: (i % N_EXPERTS, 0, f)),
                  pl.BlockSpec((1, FF_TF, D_MODEL), lambda i, f: (i % N_EXPERTS, f, 0))],
        out_specs=pl.BlockSpec((1, slots, ROW_SUB, ROW_LANE), lambda i, f: (i, 0, 0, 0)),
        out_shape=jax.ShapeDtypeStruct((be, slots, ROW_SUB, ROW_LANE), F32),
        scratch_shapes=[pltpu.VMEM((2, slots, D_MODEL), F32), pltpu.VMEM((slots, D_MODEL), BF16),
                        pltpu.VMEM((slots, FF_TF), BF16), pltpu.VMEM((slots, D_MODEL), F32),
                        pltpu.SemaphoreType.DMA((2,))],
        compiler_params=_cparams(("arbitrary", "arbitrary"), VMEM_LIMIT_BIG),
        name="moe_ffn",
    )(idx, idx, xn_flat, gate, w_gate, w_up, w_down)


CB_U = 6
CB_NB = 512


def _combine_kernel(idx_ref, h1_hbm, ye_ref, g_ref, o_hbm, acc_ref, stage_ref, sem, osem, *, cap, tp):
    b = pl.program_id(0)
    e = pl.program_id(1)

    @pl.when(e == 0)
    def _():
        cp = pltpu.make_async_copy(h1_hbm.at[b], acc_ref, sem)
        cp.start()
        cp.wait()

    def rmw(g, carry):
        ts = [idx_ref[0, 0, g * CB_U + u] for u in range(CB_U)]
        vals = [acc_ref[ts[u]] + ye_ref[g * CB_U + u] for u in range(CB_U)]
        for u in range(CB_U):
            acc_ref[ts[u]] = vals[u]
        return carry

    lax.fori_loop(0, cap // CB_U, rmw, 0)

    @pl.when(e == pl.num_programs(1) - 1)
    def _():
        gamma = g_ref[...]
        n_blocks = (tp - CH) // CB_NB

        def out_copy(k):
            return pltpu.make_async_copy(stage_ref.at[k % 2], o_hbm.at[b, pl.ds(k * CB_NB, CB_NB), :],
                                         osem.at[k % 2])

        for k in range(n_blocks):
            x = acc_ref[pl.ds(CH + k * CB_NB, CB_NB)]
            ms = jnp.sum(jnp.sum(x * x, axis=2, keepdims=True), axis=1, keepdims=True) * (1.0 / D_MODEL)
            y = x * lax.rsqrt(ms + RMS_EPS) * gamma
            if k >= 2:
                out_copy(k - 2).wait()
            stage_ref[k % 2] = y.reshape(CB_NB, D_MODEL)
            out_copy(k).start()
        for k in range(max(n_blocks - 2, 0), n_blocks):
            out_copy(k).wait()


def _combine(idx_local, h1, ye, gf, cap, slots):
    b, tp = h1.shape[:2]
    assert cap % CB_U == 0 and (tp - CH) % CB_NB == 0
    return pl.pallas_call(
        functools.partial(_combine_kernel, cap=cap, tp=tp),
        grid=(b, N_EXPERTS),
        in_specs=[pl.BlockSpec((1, 1, slots), lambda bi, e: (bi * N_EXPERTS + e, 0, 0), memory_space=pltpu.SMEM),
                  pl.BlockSpec(memory_space=pl.ANY),
                  pl.BlockSpec((None, slots, ROW_SUB, ROW_LANE), lambda bi, e: (bi * N_EXPERTS + e, 0, 0, 0)),
                  pl.BlockSpec((1, ROW_SUB, ROW_LANE), lambda bi, e: (0, 0, 0))],
        out_specs=pl.BlockSpec(memory_space=pl.ANY),
        out_shape=jax.ShapeDtypeStruct((b, tp - CH, D_MODEL), F32),
        scratch_shapes=[pltpu.VMEM((tp, ROW_SUB, ROW_LANE), F32), pltpu.VMEM((2, CB_NB, D_MODEL), F32),
                        pltpu.SemaphoreType.DMA(()), pltpu.SemaphoreType.DMA((2,))],
        compiler_params=_cparams(("arbitrary", "arbitrary"), VMEM_LIMIT_BIG),
        name="combine",
    )(idx_local, h1, ye, gf.reshape(1, ROW_SUB, ROW_LANE))


def _rope_tables(tp):
    half = RET_QK_HEAD // 2
    pos = jnp.arange(tp, dtype=F32) - float(PAD)
    inv = ROPE_BASE ** (-jnp.arange(half, dtype=F32) / half)
    ang = pos[:, None] * inv[None, :]
    cos, sin = jnp.cos(ang), jnp.sin(ang)
    return jnp.concatenate([cos, cos], axis=1), jnp.concatenate([-sin, sin], axis=1)


def _filter_features(t_len):
    half = NFFT // 2
    q = np.arange(NFFT)
    r = CH * (q % CH) + q // CH
    p_main = np.where(r < half, r, NFFT - r)
    valid_main = (r != half).astype(np.float32)
    m = np.arange(FAR)
    pad = np.zeros(HALF - 2 * FAR, np.int64)
    p_far = np.concatenate([half + m, half - FAR + m, pad, half + m, half - m, pad])
    valid_far = np.ones(CH, np.float32)
    valid_far[HALF + FAR] = 0.0
    p = jnp.asarray(np.concatenate([p_main, p_far]).astype(np.float32))
    valid = jnp.asarray(np.concatenate([valid_main, valid_far]))
    t_norm = p / (t_len - 1)
    bands = (HY_EMB_DIM - 1) // 2
    fr = jnp.linspace(1e-4, bands - 1, bands, dtype=F32)
    ang = (2.0 * math.pi * p / t_len)[:, None] * fr[None, :]
    feat = jnp.concatenate([t_norm[:, None], jnp.cos(ang), -jnp.sin(ang), valid[:, None]], axis=-1)
    return jnp.pad(feat, ((0, 0), (0, FEAT_W - feat.shape[1])))


def kernel(x, meta_tokens, norm1_g, w_in, ret_decay_fwd, ret_decay_bwd, ret_head_norm_g, w_ret_out,
           hy_conv_w, hy_conv_b, hy_filt_w1, hy_filt_b1, hy_filt_w2, hy_filt_b2, hy_filt_w3, hy_filt_b3,
           hy_filt_freq, hy_filt_w4, hy_skip, w_hy_out, w_o, norm2_g, w_router, w_exp_gate, w_exp_up,
           w_exp_down, final_norm_g):
    b, seq, d = x.shape
    t_len = seq + N_META
    tp = PAD + t_len
    assert d == D_MODEL and tp % IP_TM == 0 and tp % CH == 0 and t_len - NFFT // 2 == FAR
    cap = EC_CAPACITY * t_len // N_EXPERTS
    slots = -(-cap // 16) * 16
    l = 0

    meta = jnp.broadcast_to(meta_tokens[None].astype(x.dtype), (b, N_META, d))
    h0 = jnp.concatenate([jnp.zeros((b, PAD, d), x.dtype), meta, x], axis=1)

    cs, sn = _rope_tables(tp)
    proj = _in_proj(h0.reshape(b * tp, d), norm1_g[l][None], w_in[l].astype(BF16), cs, sn, tp)
    proj3 = proj.reshape(b, tp, IN_PROJ_W)

    lf = jax.nn.log_sigmoid(ret_decay_fwd[l].astype(F32))
    lb = jax.nn.log_sigmoid(ret_decay_bwd[l].astype(F32))
    og = _retention(proj3, lf, lb, ret_head_norm_g[l][None], tp)

    x0c, z = _hy_prep(proj3, hy_conv_w[l], hy_conv_b[l][None], tp)

    feat = _filter_features(t_len)
    w1p = jnp.pad(hy_filt_w1[l].astype(F32), ((0, FEAT_W - HY_EMB_DIM), (0, 0)))
    max_decay = math.log(HY_DECAY_TARGET) / HY_FAST_DECAY_PCT
    min_decay = math.log(HY_DECAY_TARGET) / HY_SLOW_DECAY_PCT
    dl = jnp.abs(jnp.linspace(min_decay, max_decay, D_MODEL, dtype=F32))[None]
    fargs = (w1p, hy_filt_b1[l][None].astype(F32), hy_filt_w2[l].astype(F32), hy_filt_b2[l][None].astype(F32),
             hy_filt_w3[l].astype(F32), hy_filt_b3[l][None].astype(F32), hy_filt_freq[l][None].astype(F32),
             hy_filt_w4[l].astype(F32), dl)
    g = _filters(feat[:NFFT], *fargs, FT_ROWS)
    h_far = _filters(feat[NFFT:], *fargs, CH)
    yc = _fft_conv(z, g)
    corr = _far_correction(h_far, z, tp)

    w_router_p = jnp.pad(w_router[l].astype(F32), ((0, 0), (0, ROUTER_W - N_EXPERTS)))
    w_router_hi = w_router_p.astype(BF16)
    w_router_p = jnp.stack([w_router_hi, (w_router_p - w_router_hi.astype(F32)).astype(BF16)])
    h1, xn2, aff = _merge(h0, og, x0c, z, yc, proj3, corr, hy_skip[l][None].astype(F32),
                          w_ret_out[l].astype(BF16), w_hy_out[l].astype(BF16), w_o[l].astype(BF16),
                          norm2_g[l][None].astype(F32), w_router_p, tp)

    idx, gate = _topk(aff, cap, slots)
    live = (jnp.arange(slots) < cap)[None, None, :]
    idx_local = jnp.where(live, idx, PAD)
    idx_flat = idx_local + (jnp.arange(b * N_EXPERTS, dtype=I32) // N_EXPERTS * tp)[:, None, None]
    ye = _moe_ffn(idx_flat, gate, xn2.reshape(b * tp, d),
                  w_exp_gate[l], w_exp_up[l], w_exp_down[l], slots)
    return _combine(idx_local, h1, ye, final_norm_g.astype(F32), cap, slots)
```

```python
import functools
import math

import numpy as np
import jax
import jax.numpy as jnp
from jax import lax
from jax.experimental import pallas as pl
from jax.experimental.pallas import tpu as pltpu

F32 = jnp.float32
BF16 = jnp.bfloat16
I32 = jnp.int32
HIGHEST = lax.Precision.HIGHEST

D_MODEL = 1024
N_META = 16
RET_HEADS = 4
RET_QK_HEAD = 128
RET_V_HEAD = 256
ROPE_BASE = 10000.0
HY_EMB_DIM = 33
HY_FILTER_ORDER = 64
HY_FAST_DECAY_PCT = 0.3
HY_SLOW_DECAY_PCT = 1.5
HY_DECAY_TARGET = 1e-2
N_EXPERTS = 16
EC_CAPACITY = 2
D_FF = 2 * D_MODEL
RMS_EPS = 1e-6
IN_PROJ_W = 8192

CH = 128
PAD = CH - N_META
NFFT = 16384
ZCH = 80
FAR = 16

VMEM_LIMIT_BIG = 56 * 1024 * 1024
VMEM_LIMIT_MID = 40 * 1024 * 1024


def _cparams(sem, vmem=VMEM_LIMIT_MID):
    return pltpu.CompilerParams(dimension_semantics=sem, vmem_limit_bytes=vmem)


IP_TM = 640
IP_TN = 1024
IP_CN = 256


def _inproj_kernel(x_ref, g_ref, w_ref, cs_ref, sn_ref, o_ref, xn_ref):
    j = pl.program_id(1)

    @pl.when(j == 0)
    def _():
        x = x_ref[...]
        ms = jnp.mean(x * x, axis=-1, keepdims=True)
        xn_ref[...] = (x * lax.rsqrt(ms + RMS_EPS) * g_ref[...]).astype(BF16)

    def run(epilogue):
        for c in range(IP_TN // IP_CN):
            cols = slice(c * IP_CN, (c + 1) * IP_CN)
            acc = jnp.dot(xn_ref[...], w_ref[:, cols], preferred_element_type=F32)
            epilogue(c, cols, acc)

    def rotary(c, cols, acc):
        scale = 1.0 if c < (IP_TN // IP_CN) // 2 else RET_QK_HEAD ** -0.5
        cs = cs_ref[...] * scale
        sn = sn_ref[...] * scale
        for hh in range(IP_CN // RET_QK_HEAD):
            xh = acc[:, hh * RET_QK_HEAD:(hh + 1) * RET_QK_HEAD]
            rot = xh * cs + pltpu.roll(xh, RET_QK_HEAD // 2, axis=1) * sn
            lo = c * IP_CN + hh * RET_QK_HEAD
            o_ref[:, lo:lo + RET_QK_HEAD] = rot.astype(BF16)

    def raw(c, cols, acc):
        o_ref[:, cols] = acc.astype(BF16)

    def swish(c, cols, acc):
        o_ref[:, cols] = (acc * jax.nn.sigmoid(acc)).astype(BF16)

    def sigm(c, cols, acc):
        o_ref[:, cols] = jax.nn.sigmoid(acc).astype(BF16)

    pl.when(j == 0)(lambda: run(rotary))
    pl.when(jnp.logical_or(j == 1, jnp.logical_and(j >= 3, j < 6)))(lambda: run(raw))
    pl.when(j == 2)(lambda: run(swish))
    pl.when(j >= 6)(lambda: run(sigm))


def _in_proj(h0, g1, w_in_bf, cs, sn, tp):
    n_rows = h0.shape[0]
    nt_b = tp // IP_TM
    return pl.pallas_call(
        _inproj_kernel,
        grid=(n_rows // IP_TM, IN_PROJ_W // IP_TN),
        in_specs=[
            pl.BlockSpec((IP_TM, D_MODEL), lambda i, j: (i, 0)),
            pl.BlockSpec((1, D_MODEL), lambda i, j: (0, 0)),
            pl.BlockSpec((D_MODEL, IP_TN), lambda i, j: (0, j)),
            pl.BlockSpec((IP_TM, RET_QK_HEAD), lambda i, j: (i % nt_b, 0)),
            pl.BlockSpec((IP_TM, RET_QK_HEAD), lambda i, j: (i % nt_b, 0)),
        ],
        out_specs=pl.BlockSpec((IP_TM, IP_TN), lambda i, j: (i, j)),
        out_shape=jax.ShapeDtypeStruct((n_rows, IN_PROJ_W), BF16),
        scratch_shapes=[pltpu.VMEM((IP_TM, D_MODEL), BF16)],
        compiler_params=_cparams(("arbitrary", "arbitrary")),
        name="in_proj",
    )(h0, g1, w_in_bf, cs, sn)


RET_UNROLL = 5


def _ret_kernel(lf_ref, lb_ref, q_ref, k_ref, v_ref, gr_ref, gn_ref, o_ref, ob_ref, s_ref, *, nch):
    h = pl.program_id(1)
    lf = lf_ref[h]
    lb = lb_ref[h]
    ri = lax.broadcasted_iota(I32, (CH, CH), 0).astype(F32)
    ci = lax.broadcasted_iota(I32, (CH, CH), 1).astype(F32)
    diff = ri - ci
    mask = jnp.exp(jnp.where(diff >= 0, lf * diff, -lb * diff))
    w_end = jnp.exp(lf * (CH - 1.0 - ri))
    w_start = jnp.exp(lb * ri)
    qw_f = jnp.exp(lf * (ri + 1.0))
    qw_b = jnp.exp(lb * (CH - ri))
    dec_f = jnp.exp(jnp.full((CH, RET_V_HEAD), lf * CH, F32))
    dec_b = jnp.exp(jnp.full((CH, RET_V_HEAD), lb * CH, F32))
    tn_dims = (((0,), (0,)), ((), ()))
    nt_dims = (((1,), (1,)), ((), ()))

    s_ref[...] = jnp.zeros_like(s_ref)

    def bwd(it, carry):
        s = s_ref[...]
        for u in range(RET_UNROLL):
            n = nch - 1 - (it * RET_UNROLL + u)
            r0 = pl.multiple_of(n * CH, CH)
            q = q_ref[0, pl.ds(r0, CH), :].astype(F32)
            k = k_ref[0, pl.ds(r0, CH), :].astype(F32)
            v = v_ref[0, pl.ds(r0, CH), :]
            ob_ref[pl.ds(r0, CH), :] = jnp.dot((q * qw_b).astype(BF16), s.astype(BF16),
                                               preferred_element_type=F32)
            a = lax.dot_general((k * w_start).astype(BF16), v, tn_dims, preferred_element_type=F32)
            s = s * dec_b + a
        s_ref[...] = s
        return carry

    lax.fori_loop(0, nch // RET_UNROLL, bwd, 0)

    s_ref[...] = jnp.zeros_like(s_ref)
    gn = gn_ref[...]

    def fwd(it, carry):
        s = s_ref[...]
        for u in range(RET_UNROLL):
            n = it * RET_UNROLL + u
            r0 = pl.multiple_of(n * CH, CH)
            qb = q_ref[0, pl.ds(r0, CH), :]
            kb = k_ref[0, pl.ds(r0, CH), :]
            v = v_ref[0, pl.ds(r0, CH), :]
            q = qb.astype(F32)
            k = kb.astype(F32)
            scores = lax.dot_general(qb, kb, nt_dims, preferred_element_type=F32) * mask
            o = jnp.dot(scores.astype(BF16), v, preferred_element_type=F32)
            o = o + jnp.dot((q * qw_f).astype(BF16), s.astype(BF16), preferred_element_type=F32)
            o = o + ob_ref[pl.ds(r0, CH), :]
            a = lax.dot_general((k * w_end).astype(BF16), v, tn_dims, preferred_element_type=F32)
            s = s * dec_f + a
            y = o * lax.rsqrt(jnp.mean(o * o, axis=-1, keepdims=True) + RMS_EPS) * gn
            o_ref[0, pl.ds(r0, CH), :] = (y * gr_ref[0, pl.ds(r0, CH), :].astype(F32)).astype(BF16)
        s_ref[...] = s
        return carry

    lax.fori_loop(0, nch // RET_UNROLL, fwd, 0)


def _retention(proj3, lf, lb, gn, tp):
    b = proj3.shape[0]
    nch = tp // CH
    qk_blocks = (RET_HEADS * RET_QK_HEAD) // RET_QK_HEAD
    v_blk0 = (2 * RET_HEADS * RET_QK_HEAD) // RET_V_HEAD
    g_blk0 = v_blk0 + RET_HEADS
    smem = pl.BlockSpec(memory_space=pltpu.SMEM)
    return pl.pallas_call(
        functools.partial(_ret_kernel, nch=nch),
        grid=(b, RET_HEADS),
        in_specs=[
            smem, smem,
            pl.BlockSpec((1, tp, RET_QK_HEAD), lambda bi, h: (bi, 0, h)),
            pl.BlockSpec((1, tp, RET_QK_HEAD), lambda bi, h: (bi, 0, qk_blocks + h)),
            pl.BlockSpec((1, tp, RET_V_HEAD), lambda bi, h: (bi, 0, v_blk0 + h)),
            pl.BlockSpec((1, tp, RET_V_HEAD), lambda bi, h: (bi, 0, g_blk0 + h)),
            pl.BlockSpec((1, RET_V_HEAD), lambda bi, h: (0, h)),
        ],
        out_specs=pl.BlockSpec((1, tp, RET_V_HEAD), lambda bi, h: (bi, 0, h)),
        out_shape=jax.ShapeDtypeStruct((b, tp, RET_HEADS * RET_V_HEAD), BF16),
        scratch_shapes=[pltpu.VMEM((tp, RET_V_HEAD), F32), pltpu.VMEM((RET_QK_HEAD, RET_V_HEAD), F32)],
        compiler_params=_cparams(("arbitrary", "arbitrary"), VMEM_LIMIT_BIG),
        name="retention",
    )(lf, lb, proj3, proj3, proj3, proj3, gn)


HP_CW = 128


def _hyprep_kernel(u0_ref, u1_ref, u2_ref, w0_ref, w1_ref, w2_ref, b0_ref, b1_ref, b2_ref,
                   x0_ref, z_ref, *, nch, tz):
    rows = lax.broadcasted_iota(I32, (CH, HP_CW), 0)
    halo = 16

    def conv(u_ref, w_ref, b_ref, n, r0):
        cur = u_ref[0, pl.ds(r0, CH), :].astype(F32)
        rp = pl.multiple_of(jnp.maximum(r0 - halo, 0), halo)
        rn = pl.multiple_of(jnp.minimum(r0 + CH, (nch - 1) * CH), halo)
        prev = u_ref[0, pl.ds(rp, halo), :].astype(F32)[halo - 1:halo, :]
        nxt = u_ref[0, pl.ds(rn, halo), :].astype(F32)[0:1, :]
        prev = jnp.where(n > 0, prev, 0.0)
        nxt = jnp.where(n < nch - 1, nxt, 0.0)
        up = jnp.where(rows == 0, prev, pltpu.roll(cur, 1, axis=0))
        dn = jnp.where(rows == CH - 1, nxt, pltpu.roll(cur, CH - 1, axis=0))
        w = w_ref[...]
        return up * w[0:1, :] + cur * w[1:2, :] + dn * w[2:3, :] + b_ref[...]

    def body(n, carry):
        r0 = pl.multiple_of(n * CH, CH)
        x0 = conv(u0_ref, w0_ref, b0_ref, n, r0)
        x1 = conv(u1_ref, w1_ref, b1_ref, n, r0)
        vv = conv(u2_ref, w2_ref, b2_ref, n, r0)
        z = jnp.where(rows + r0 >= PAD, x1 * vv, 0.0)
        x0_ref[0, pl.ds(r0, CH), :] = x0.astype(BF16)
        z_ref[0, pl.ds(r0, CH), :] = z.astype(BF16)
        return carry

    lax.fori_loop(0, nch, body, 0)
    z_ref[0, nch * CH:tz, :] = jnp.zeros((tz - nch * CH, HP_CW), BF16)


def _hy_prep(proj3, conv_w, conv_b, tp):
    b = proj3.shape[0]
    nch = tp // CH
    tz = ZCH * CH
    ncb = D_MODEL // HP_CW
    u_blk0 = 3072 // HP_CW
    uspec = lambda s: pl.BlockSpec((1, tp, HP_CW), lambda bi, c: (bi, 0, u_blk0 + s * ncb + c))
    wspec = lambda s: pl.BlockSpec((3, HP_CW), lambda bi, c: (0, s * ncb + c))
    bspec = lambda s: pl.BlockSpec((1, HP_CW), lambda bi, c: (0, s * ncb + c))
    return pl.pallas_call(
        functools.partial(_hyprep_kernel, nch=nch, tz=tz),
        grid=(b, ncb),
        in_specs=[uspec(0), uspec(1), uspec(2), wspec(0), wspec(1), wspec(2), bspec(0), bspec(1), bspec(2)],
        out_specs=[pl.BlockSpec((1, tp, HP_CW), lambda bi, c: (bi, 0, c)),
                   pl.BlockSpec((1, tz, HP_CW), lambda bi, c: (bi, 0, c))],
        out_shape=[jax.ShapeDtypeStruct((b, tp, D_MODEL), BF16),
                   jax.ShapeDtypeStruct((b, tz, D_MODEL), BF16)],
        compiler_params=_cparams(("arbitrary", "arbitrary")),
        name="hy_prep",
    )(proj3, proj3, proj3, conv_w, conv_w, conv_w, conv_b, conv_b, conv_b)


FEAT_W = 128


FT_ROWS = 512
HALF = CH // 2


def _filter_kernel(feat_ref, w1_ref, b1_ref, w2_ref, b2_ref, w3_ref, b3_ref, fq_ref, w4_ref, dl_ref, o_ref,
                   *, groups):
    feat = feat_ref[...]
    fq = fq_ref[...]
    hdn = jnp.sin(fq * (jnp.dot(feat, w1_ref[...], precision=HIGHEST, preferred_element_type=F32) + b1_ref[...]))
    hdn = jnp.sin(fq * (jnp.dot(hdn, w2_ref[...], precision=HIGHEST, preferred_element_type=F32) + b2_ref[...]))
    hdn = jnp.sin(fq * (jnp.dot(hdn, w3_ref[...], precision=HIGHEST, preferred_element_type=F32) + b3_ref[...]))
    scale = jnp.exp(-feat[:, 0:1] * dl_ref[...]) * feat[:, HY_EMB_DIM:HY_EMB_DIM + 1]
    for d in range(2):
        rows = [slice(g * CH + d * HALF, g * CH + (d + 1) * HALF) for g in range(groups)]
        hd = jnp.concatenate([hdn[r] for r in rows], axis=0)
        filt = jnp.dot(hd, w4_ref[:, d * D_MODEL:(d + 1) * D_MODEL], precision=HIGHEST, preferred_element_type=F32)
        for g, r in enumerate(rows):
            o_ref[r, :] = filt[g * HALF:(g + 1) * HALF] * scale[r]


def _filters(feat, w1p, b1, w2, b2, w3, b3, fq, w4, dl, rows):
    n_rows = feat.shape[0]
    full = lambda a: pl.BlockSpec(a.shape, lambda i: (0,) * a.ndim)
    return pl.pallas_call(
        functools.partial(_filter_kernel, groups=rows // CH),
        grid=(n_rows // rows,),
        in_specs=[pl.BlockSpec((rows, FEAT_W), lambda i: (i, 0)),
                  full(w1p), full(b1), full(w2), full(b2), full(w3), full(b3), full(fq), full(w4), full(dl)],
        out_specs=pl.BlockSpec((rows, D_MODEL), lambda i: (i, 0)),
        out_shape=jax.ShapeDtypeStruct((n_rows, D_MODEL), F32),
        compiler_params=_cparams(("arbitrary",)),
        name="hy_filter",
    )(feat, w1p, b1, w2, b2, w3, b3, fq, w4, dl)


FFT_G = 8


def _lanes(j, c):
    return slice(j * c, (j + 1) * c)


def _fft_a_data_kernel(m_ref, z_ref, o_ref):
    c = o_ref.shape[-1]
    for j in range(FFT_G):
        x = jnp.concatenate([z_ref[0, :, _lanes(j, c)], z_ref[1, :, _lanes(j, c)]], axis=0)
        o_ref[j] = jnp.dot(m_ref[j], x, preferred_element_type=F32).astype(BF16)


def _fft_a_filt_kernel(m_ref, g_ref, o_ref):
    for j in range(FFT_G):
        gj = g_ref[j * CH:(j + 1) * CH, :].astype(BF16)
        o_ref[j] = jnp.dot(m_ref[j], gj, preferred_element_type=F32).astype(BF16)


def _fft_b_filt_kernel(f_ref, yr_ref, yi_ref, o_ref):
    c = o_ref.shape[-1]
    for j in range(FFT_G):
        y = jnp.concatenate([yr_ref[:, _lanes(j, c)], yi_ref[:, _lanes(j, c)]], axis=0)
        o_ref[j] = jnp.dot(f_ref[...], y, preferred_element_type=F32) * (1.0 / NFFT)


def _fft_b_data_kernel(f_ref, fi_ref, yr_ref, yi_ref, g_ref, o_ref):
    c = o_ref.shape[-1]
    for j in range(FFT_G):
        y = jnp.concatenate([yr_ref[:, _lanes(j, c)], yi_ref[:, _lanes(j, c)]], axis=0)
        x = jnp.dot(f_ref[...], y, preferred_element_type=F32)
        xr, xi = x[:CH], x[CH:]
        gr, gi = g_ref[j, :CH], g_ref[j, CH:]
        p = jnp.concatenate([xr * gr - xi * gi, xr * gi + xi * gr], axis=0).astype(BF16)
        o_ref[j] = jnp.dot(fi_ref[...], p, preferred_element_type=F32).astype(BF16)


def _fft_a_inv_kernel(m_ref, ur_ref, ui_ref, o_ref):
    c = ur_ref.shape[-1] // FFT_G
    for j in range(FFT_G):
        u = jnp.concatenate([ur_ref[:, _lanes(j, c)], ui_ref[:, _lanes(j, c)]], axis=0)
        y = jnp.dot(m_ref[j], u, preferred_element_type=F32).astype(BF16)
        o_ref[0, :, _lanes(j, c)] = y[:ZCH]
        o_ref[1, :, _lanes(j, c)] = y[ZCH:]


def _dft_tables():
    n2 = np.arange(CH)[:, None, None]
    k1 = np.arange(CH)[None, :, None]

    def theta(n1_count):
        n1 = np.arange(n1_count)[None, None, :]
        return 2.0 * np.pi * ((k1 * (CH * n1 + n2)) % NFFT) / NFFT

    th = theta(ZCH)
    c, s = np.cos(th), np.sin(th)
    m_a = np.concatenate([np.concatenate([c, s], axis=2), np.concatenate([-s, c], axis=2)], axis=1)
    m_ainv = np.transpose(m_a, (0, 2, 1))
    th = theta(CH)
    m_af = np.concatenate([np.cos(th), -np.sin(th)], axis=1)
    a = 2.0 * np.pi * ((np.arange(CH)[:, None] * np.arange(CH)[None, :]) % CH) / CH
    c, s = np.cos(a), np.sin(a)
    f2 = np.block([[c, s], [-s, c]])
    f2i = np.block([[c, -s], [s, c]])
    f = lambda t: jnp.asarray(t.astype(np.float32)).astype(BF16)
    return f(m_a), f(m_ainv), f(m_af), f(f2), f(f2i)


def _fft_conv(z, g):
    c = z.shape[-1]
    m_a, m_ainv, m_af, f2, f2i = _dft_tables()
    cp = _cparams(("arbitrary",), VMEM_LIMIT_BIG)
    steps = CH // FFT_G
    gc = FFT_G * c
    full2 = pl.BlockSpec((2 * CH, 2 * CH), lambda i: (0, 0))
    col_re = pl.BlockSpec((CH, gc), lambda i: (0, i))
    col_im = pl.BlockSpec((CH, gc), lambda i: (0, steps + i))
    blk = pl.BlockSpec((FFT_G, 2 * CH, c), lambda i: (i, 0, 0))
    spec_shape = jax.ShapeDtypeStruct((CH, 2 * CH, c), BF16)

    yg = pl.pallas_call(
        _fft_a_filt_kernel, grid=(steps,),
        in_specs=[pl.BlockSpec((FFT_G, 2 * CH, CH), lambda i: (i, 0, 0)),
                  pl.BlockSpec((FFT_G * CH, c), lambda i: (i, 0))],
        out_specs=blk, out_shape=spec_shape, compiler_params=cp, name="fft_a_filt",
    )(m_af, g)
    yg2 = yg.reshape(CH, 2 * CH * c)
    gh = pl.pallas_call(
        _fft_b_filt_kernel, grid=(steps,),
        in_specs=[full2, col_re, col_im],
        out_specs=blk, out_shape=jax.ShapeDtypeStruct((CH, 2 * CH, c), F32),
        compiler_params=cp, name="fft_b_filt",
    )(f2, yg2, yg2)

    y = pl.pallas_call(
        _fft_a_data_kernel, grid=(steps,),
        in_specs=[pl.BlockSpec((FFT_G, 2 * CH, 2 * ZCH), lambda i: (i, 0, 0)),
                  pl.BlockSpec((2, ZCH, gc), lambda i: (0, 0, i))],
        out_specs=blk, out_shape=spec_shape, compiler_params=cp, name="fft_a_data",
    )(m_a, z.reshape(2, ZCH, CH * c))
    y2 = y.reshape(CH, 2 * CH * c)
    u = pl.pallas_call(
        _fft_b_data_kernel, grid=(steps,),
        in_specs=[full2, full2, col_re, col_im, blk],
        out_specs=blk, out_shape=spec_shape, compiler_params=cp, name="fft_b_data",
    )(f2, f2i, y2, y2, gh)
    u2 = u.reshape(CH, 2 * CH * c)
    yc = pl.pallas_call(
        _fft_a_inv_kernel, grid=(steps,),
        in_specs=[pl.BlockSpec((FFT_G, 2 * ZCH, 2 * CH), lambda i: (i, 0, 0)), col_re, col_im],
        out_specs=pl.BlockSpec((2, ZCH, gc), lambda i: (0, 0, i)),
        out_shape=jax.ShapeDtypeStruct((2, ZCH, CH * c), BF16),
        compiler_params=cp, name="fft_a_inv",
    )(m_ainv, u2, u2)
    return yc.reshape(2, ZCH * CH, c)


def _far_kernel(far_ref, zm_ref, zl_ref, o_ref):
    g_hi = far_ref[HALF + FAR:HALF + 2 * FAR, :]
    ef = far_ref[0:FAR, :] - g_hi
    g_lo = far_ref[FAR:2 * FAR, :]
    hb = far_ref[HALF:HALF + FAR, :]
    zm = zm_ref[0, PAD:CH, :].astype(F32)
    zl = zl_ref[0, PAD:CH, :].astype(F32)
    row = lambda a, i: a[i:i + 1, :]
    eb = [row(hb, 0) - row(g_hi, 0)] + [row(hb, m) - row(g_lo, FAR - m) for m in range(1, FAR)]
    for j in range(FAR):
        acc_f = row(ef, j) * row(zm, 0)
        for i in range(1, j + 1):
            acc_f = acc_f + row(ef, j - i) * row(zm, i)
        o_ref[0, 0, j:j + 1, :] = acc_f
        acc_b = eb[0] * row(zl, j)
        for i in range(j + 1, FAR):
            acc_b = acc_b + eb[i - j] * row(zl, i)
        o_ref[0, 1, j:j + 1, :] = acc_b


def _far_correction(h_far, z, tp):
    b, _, c = z.shape
    last = tp // CH - 1
    return pl.pallas_call(
        _far_kernel, grid=(b,),
        in_specs=[pl.BlockSpec((CH, c), lambda bi: (0, 0)),
                  pl.BlockSpec((1, CH, c), lambda bi: (bi, 0, 0)),
                  pl.BlockSpec((1, CH, c), lambda bi: (bi, last, 0))],
        out_specs=pl.BlockSpec((1, 2, FAR, c), lambda bi: (bi, 0, 0, 0)),
        out_shape=jax.ShapeDtypeStruct((b, 2, FAR, c), F32),
        compiler_params=_cparams(("arbitrary",)),
        name="hy_far",
    )(h_far, z, z)


MG_TM = 640
MG_CN = 256
ROUTER_W = 128
ROW_SUB, ROW_LANE = 8, 128


def _merge_kernel(h_ref, og_ref, x0_ref, z_ref, yc_ref, ga_ref, gb_ref, corr_ref, skip_ref,
                  wr_ref, wh_ref, wo_ref, g2_ref, wrt_ref,
                  h1_ref, xn_ref, aff_ref, pre_ref, mix_ref, h1s_ref, *, nt_b):
    i = pl.program_id(0)
    ib = i % nt_b
    pre_ref[...] = yc_ref[0].astype(F32) + z_ref[0].astype(F32) * skip_ref[...]

    @pl.when(ib == 0)
    def _():
        pre_ref[PAD:CH, :] += corr_ref[0, 1]

    @pl.when(ib == nt_b - 1)
    def _():
        pre_ref[MG_TM - FAR:MG_TM, :] += corr_ref[0, 0]

    chunks = [slice(c * MG_CN, (c + 1) * MG_CN) for c in range(D_MODEL // MG_CN)]
    og = og_ref[0]
    pre = (x0_ref[0].astype(F32) * pre_ref[...]).astype(BF16)
    for cols in chunks:
        ya = jnp.dot(og, wr_ref[:, cols], preferred_element_type=F32)
        yb = jnp.dot(pre, wh_ref[:, cols], preferred_element_type=F32)
        mix_ref[:, cols] = (ga_ref[0, :, cols].astype(F32) * ya + gb_ref[0, :, cols].astype(F32) * yb).astype(BF16)
    mixed = mix_ref[...]
    ss = jnp.zeros((MG_TM, 1), F32)
    for cols in chunks:
        h1 = h_ref[0, :, cols] + jnp.dot(mixed, wo_ref[:, cols], preferred_element_type=F32)
        h1s_ref[:, cols] = h1
        ss = ss + jnp.sum(h1 * h1, axis=-1, keepdims=True)
    rinv = lax.rsqrt(ss * (1.0 / D_MODEL) + RMS_EPS)
    h1 = h1s_ref[...]
    h1_ref[0] = h1.reshape(MG_TM, ROW_SUB, ROW_LANE)
    xn = h1 * rinv * g2_ref[...]
    xn_ref[0] = xn
    xh = xn.astype(BF16)
    xl = (xn - xh.astype(F32)).astype(BF16)
    logits = (jnp.dot(xh, wrt_ref[0], preferred_element_type=F32)
              + jnp.dot(xl, wrt_ref[0], preferred_element_type=F32)
              + jnp.dot(xh, wrt_ref[1], preferred_element_type=F32))
    lane = lax.broadcasted_iota(I32, logits.shape, 1)
    logits = jnp.where(lane < N_EXPERTS, logits, -jnp.inf)
    m = jnp.max(logits, axis=-1, keepdims=True)
    e = jnp.exp(logits - m)
    aff = e / jnp.sum(e, axis=-1, keepdims=True)
    rows = lax.broadcasted_iota(I32, logits.shape, 0) + ib * MG_TM
    aff_ref[0] = jnp.where(rows >= PAD, aff, -1.0)


def _merge(h0, og, x0c, z, yc, proj3, corr, skip, w_ret, w_hy, w_o, g2, w_router_p, tp):
    b = h0.shape[0]
    nt_b = tp // MG_TM
    ga_blk0 = 6144 // D_MODEL
    row = lambda w: pl.BlockSpec((1, MG_TM, w), lambda i: (i // nt_b, i % nt_b, 0))
    full = lambda a: pl.BlockSpec(a.shape, lambda i: (0,) * a.ndim)
    return pl.pallas_call(
        functools.partial(_merge_kernel, nt_b=nt_b),
        grid=(b * nt_b,),
        in_specs=[row(D_MODEL), row(D_MODEL), row(D_MODEL), row(D_MODEL), row(D_MODEL),
                  pl.BlockSpec((1, MG_TM, D_MODEL), lambda i: (i // nt_b, i % nt_b, ga_blk0)),
                  pl.BlockSpec((1, MG_TM, D_MODEL), lambda i: (i // nt_b, i % nt_b, ga_blk0 + 1)),
                  pl.BlockSpec((1, 2, FAR, D_MODEL), lambda i: (i // nt_b, 0, 0, 0)),
                  full(skip), full(w_ret), full(w_hy), full(w_o), full(g2), full(w_router_p)],
        out_specs=[pl.BlockSpec((1, MG_TM, ROW_SUB, ROW_LANE), lambda i: (i // nt_b, i % nt_b, 0, 0)),
                   row(D_MODEL), row(ROUTER_W)],
        out_shape=[jax.ShapeDtypeStruct((b, tp, ROW_SUB, ROW_LANE), F32),
                   jax.ShapeDtypeStruct((b, tp, D_MODEL), F32),
                   jax.ShapeDtypeStruct((b, tp, ROUTER_W), F32)],
        scratch_shapes=[pltpu.VMEM((MG_TM, D_MODEL), F32), pltpu.VMEM((MG_TM, D_MODEL), BF16),
                        pltpu.VMEM((MG_TM, D_MODEL), F32)],
        compiler_params=_cparams(("arbitrary",), VMEM_LIMIT_BIG),
        name="merge",
    )(h0, og, x0c, z, yc, proj3, proj3, corr, skip, w_ret, w_hy, w_o, g2, w_router_p)


TK_W = 128


def _select_kernel(aff_ref, low_ref, slot_ref, offs_ref, *, cap, nch):
    def chunk(c):
        r0 = pl.multiple_of(c * CH, CH)
        return aff_ref[0, pl.ds(r0, CH), :]

    def count(pred):
        def body(c, acc):
            return acc + pred(chunk(c)).astype(I32)

        acc = lax.fori_loop(0, nch, body, jnp.zeros((CH, ROUTER_W), I32))
        return jnp.sum(acc, axis=0, keepdims=True)

    def search(it, bits):
        cand = bits | jnp.left_shift(1, 29 - it)
        cand_f = pltpu.bitcast(cand, F32)
        return jnp.where(count(lambda a: a >= cand_f) >= cap, cand, bits)

    thr = pltpu.bitcast(lax.fori_loop(0, 30, search, jnp.zeros((1, ROUTER_W), I32)), F32)
    need = (cap - count(lambda a: a > thr)).astype(F32)
    low = low_ref[...]

    def scan(c, carry):
        c_eq, c_sel = carry
        a = chunk(c)
        eq = a == thr
        eq_f = eq.astype(F32)
        eq_rank = jnp.dot(low, eq_f, preferred_element_type=F32) + c_eq
        sel = jnp.logical_or(a > thr, jnp.logical_and(eq, eq_rank < need))
        sel_f = sel.astype(F32)
        slot = jnp.dot(low, sel_f, preferred_element_type=F32) + c_sel
        r0 = pl.multiple_of(c * CH, CH)
        slot_ref[0, pl.ds(r0, CH), :] = jnp.where(sel, slot, -1.0).astype(I32)
        offs_ref[0, c] = c_sel.astype(I32)
        return (c_eq + jnp.sum(eq_f, axis=0, keepdims=True), c_sel + jnp.sum(sel_f, axis=0, keepdims=True))

    zero = jnp.zeros((1, ROUTER_W), F32)
    lax.fori_loop(0, nch, scan, (zero, zero))


def _extract_kernel(offs_ref, slot_ref, aff_ref, idx_ref, gate_ref, *, nch, nwin):
    idx_ref[...] = jnp.zeros_like(idx_ref)
    gate_ref[...] = jnp.zeros_like(gate_ref)
    lane = lax.broadcasted_iota(I32, (CH, TK_W), 1)
    trow = lax.broadcasted_iota(I32, (CH, TK_W), 0)

    def per_chunk(c, carry):
        r0 = pl.multiple_of(c * CH, CH)
        slots = slot_ref[0, pl.ds(r0, CH), :]
        affs = aff_ref[0, pl.ds(r0, CH), :]
        tpos = (trow + r0).astype(F32)
        for e in range(N_EXPERTS):
            col = jnp.broadcast_to(slots[:, e:e + 1], (CH, TK_W))
            gcol = jnp.broadcast_to(affs[:, e:e + 1], (CH, TK_W))
            w0 = offs_ref[0, 0, c * N_EXPERTS + e] // TK_W
            for dw in range(2):
                w = w0 + dw
                base = jnp.where(w < nwin, w * TK_W, -2 * TK_W)
                hit = col == lane + base
                row = e * nwin + jnp.minimum(w, nwin - 1)
                idx_ref[row] += jnp.sum(jnp.where(hit, tpos, 0.0), axis=0, keepdims=True)
                gate_ref[row] += jnp.sum(jnp.where(hit, gcol, 0.0), axis=0, keepdims=True)
        return carry

    lax.fori_loop(0, nch, per_chunk, 0)


def _topk(aff, cap, slots):
    b, tp, _ = aff.shape
    nch = tp // CH
    nwin = -(-slots // TK_W)
    low = jnp.asarray(np.tril(np.ones((CH, CH), np.float32), k=-1))
    slot, offs = pl.pallas_call(
        functools.partial(_select_kernel, cap=cap, nch=nch),
        grid=(b,),
        in_specs=[pl.BlockSpec((1, tp, ROUTER_W), lambda bi: (bi, 0, 0)),
                  pl.BlockSpec((CH, CH), lambda bi: (0, 0))],
        out_specs=[pl.BlockSpec((1, tp, ROUTER_W), lambda bi: (bi, 0, 0)),
                   pl.BlockSpec((1, nch, 1, ROUTER_W), lambda bi: (bi, 0, 0, 0))],
        out_shape=[jax.ShapeDtypeStruct((b, tp, ROUTER_W), I32),
                   jax.ShapeDtypeStruct((b, nch, 1, ROUTER_W), I32)],
        compiler_params=_cparams(("arbitrary",)),
        name="topk_select",
    )(aff, low)
    offs_s = offs[:, :, 0, :N_EXPERTS].reshape(b, 1, nch * N_EXPERTS)
    rows = pl.BlockSpec((None, N_EXPERTS * nwin, 1, TK_W), lambda bi: (bi, 0, 0, 0))
    out = jax.ShapeDtypeStruct((b, N_EXPERTS * nwin, 1, TK_W), F32)
    idx, gate = pl.pallas_call(
        functools.partial(_extract_kernel, nch=nch, nwin=nwin),
        grid=(b,),
        in_specs=[pl.BlockSpec((1, 1, nch * N_EXPERTS), lambda bi: (bi, 0, 0), memory_space=pltpu.SMEM),
                  pl.BlockSpec((1, tp, ROUTER_W), lambda bi: (bi, 0, 0)),
                  pl.BlockSpec((1, tp, ROUTER_W), lambda bi: (bi, 0, 0))],
        out_specs=[rows, rows],
        out_shape=[out, out],
        compiler_params=_cparams(("arbitrary",)),
        name="topk_extract",
    )(offs_s, slot, aff)
    idx = idx.reshape(b * N_EXPERTS, 1, nwin * TK_W)[:, :, :slots].astype(I32)
    gate = gate.reshape(b * N_EXPERTS, nwin * TK_W, 1)[:, :slots]
    return idx, gate


FF_TF = 1024
MOE_CN = 256


MOE_UNROLL = 4


def _moe_kernel(idx_ref, idx_next_ref, xn_hbm, gate_ref, wg_ref, wu_ref, wd_ref, o_ref,
                xe32_ref, xe_ref, hid_ref, acc_ref, sem,
                *, slots, nf):
    i = pl.program_id(0)
    f = pl.program_id(1)
    buf = i % 2
    share = slots // nf

    def row_copy(ids_ref, s, b):
        return pltpu.make_async_copy(xn_hbm.at[pl.ds(ids_ref[0, 0, s], 1), :],
                                     xe32_ref.at[b, pl.ds(s, 1), :], sem.at[b])

    def for_rows(lo, n, fn):
        def body(k, carry):
            for u in range(MOE_UNROLL):
                fn(lo + k * MOE_UNROLL + u)
            return carry

        lax.fori_loop(0, n // MOE_UNROLL, body, 0)

    @pl.when(jnp.logical_and(i == 0, f == 0))
    def _():
        for_rows(0, slots, lambda s: row_copy(idx_ref, s, 0).start())

    @pl.when(f == 0)
    def _():
        for_rows(0, slots, lambda s: row_copy(idx_ref, s, buf).wait())
        xe_ref[...] = xe32_ref[buf].astype(BF16)

    @pl.when(f == 0)
    def _():
        acc_ref[...] = jnp.zeros_like(acc_ref)

    n_up, n_down = FF_TF // MOE_CN, D_MODEL // MOE_CN
    per_chunk = share // (n_up + n_down)

    def prefetch(chunk):
        base = f * share + chunk * per_chunk
        for u in range(per_chunk):
            row_copy(idx_next_ref, base + u, 1 - buf).start()

    xe = xe_ref[...]
    for c in range(n_up):
        cols = slice(c * MOE_CN, (c + 1) * MOE_CN)
        gg = jnp.dot(xe, wg_ref[0, :, cols].astype(BF16), preferred_element_type=F32)
        uu = jnp.dot(xe, wu_ref[0, :, cols].astype(BF16), preferred_element_type=F32)
        hid_ref[:, cols] = (gg * jax.nn.sigmoid(gg) * uu).astype(BF16)
        prefetch(c)
    scale = jnp.where(f == pl.num_programs(1) - 1, gate_ref[0], 1.0)
    hid = hid_ref[...]
    for c in range(n_down):
        cols = slice(c * MOE_CN, (c + 1) * MOE_CN)
        part = jnp.dot(hid, wd_ref[0, :, cols].astype(BF16), preferred_element_type=F32)
        acc_ref[:, cols] = (acc_ref[:, cols] + part) * scale
        prefetch(n_up + c)

    @pl.when(f == pl.num_programs(1) - 1)
    def _():
        o_ref[0] = acc_ref[...].reshape(slots, ROW_SUB, ROW_LANE)

    @pl.when(jnp.logical_and(i == pl.num_programs(0) - 1, f == pl.num_programs(1) - 1))
    def _():
        for_rows(0, slots, lambda s: row_copy(idx_next_ref, s, 1 - buf).wait())


def _moe_ffn(idx, gate, xn_flat, w_gate, w_up, w_down, slots):
    be = idx.shape[0]
    nf = D_FF // FF_TF
    assert slots % MOE_UNROLL == 0 and slots % (nf * (FF_TF // MOE_CN + D_MODEL // MOE_CN)) == 0
    return pl.pallas_call(
        functools.partial(_moe_kernel, slots=slots, nf=nf),
        grid=(be, nf),
        in_specs=[pl.BlockSpec((1, 1, slots), lambda i, f: (i, 0, 0), memory_space=pltpu.SMEM),
                  pl.BlockSpec((1, 1, slots), lambda i, f: (jnp.minimum(i + 1, be - 1), 0, 0),
                               memory_space=pltpu.SMEM),
                  pl.BlockSpec(memory_space=pl.ANY),
                  pl.BlockSpec((1, slots, 1), lambda i, f: (i, 0, 0)),
                  pl.BlockSpec((1, D_MODEL, FF_TF), lambda i, f: (i % N_EXPERTS, 0, f)),
                  pl.BlockSpec((1, D_MODEL, FF_TF), lambda i, f: (i % N_EXPERTS, 0, f)),
                  pl.BlockSpec((1, FF_TF, D_MODEL), lambda i, f: (i % N_EXPERTS, f, 0))],
        out_specs=pl.BlockSpec((1, slots, ROW_SUB, ROW_LANE), lambda i, f: (i, 0, 0, 0)),
        out_shape=jax.ShapeDtypeStruct((be, slots, ROW_SUB, ROW_LANE), F32),
        scratch_shapes=[pltpu.VMEM((2, slots, D_MODEL), F32), pltpu.VMEM((slots, D_MODEL), BF16),
                        pltpu.VMEM((slots, FF_TF), BF16), pltpu.VMEM((slots, D_MODEL), F32),
                        pltpu.SemaphoreType.DMA((2,))],
        compiler_params=_cparams(("arbitrary", "arbitrary"), VMEM_LIMIT_BIG),
        name="moe_ffn",
    )(idx, idx, xn_flat, gate, w_gate, w_up, w_down)


CB_U = 6
CB_NB = 512


def _combine_kernel(idx_ref, h1_hbm, ye_ref, g_ref, o_hbm, acc_ref, stage_ref, sem, osem, *, cap, tp):
    b = pl.program_id(0)
    e = pl.program_id(1)

    @pl.when(e == 0)
    def _():
        cp = pltpu.make_async_copy(h1_hbm.at[b], acc_ref, sem)
        cp.start()
        cp.wait()

    def rmw(g, carry):
        ts = [idx_ref[0, 0, g * CB_U + u] for u in range(CB_U)]
        vals = [acc_ref[ts[u]] + ye_ref[g * CB_U + u] for u in range(CB_U)]
        for u in range(CB_U):
            acc_ref[ts[u]] = vals[u]
        return carry

    lax.fori_loop(0, cap // CB_U, rmw, 0)

    @pl.when(e == pl.num_programs(1) - 1)
    def _():
        gamma = g_ref[...]
        n_blocks = (tp - CH) // CB_NB

        def out_copy(k):
            return pltpu.make_async_copy(stage_ref.at[k % 2], o_hbm.at[b, pl.ds(k * CB_NB, CB_NB), :],
                                         osem.at[k % 2])

        for k in range(n_blocks):
            x = acc_ref[pl.ds(CH + k * CB_NB, CB_NB)]
            ms = jnp.sum(jnp.sum(x * x, axis=2, keepdims=True), axis=1, keepdims=True) * (1.0 / D_MODEL)
            y = x * lax.rsqrt(ms + RMS_EPS) * gamma
            if k >= 2:
                out_copy(k - 2).wait()
            stage_ref[k % 2] = y.reshape(CB_NB, D_MODEL)
            out_copy(k).start()
        for k in range(max(n_blocks - 2, 0), n_blocks):
            out_copy(k).wait()


def _combine(idx_local, h1, ye, gf, cap, slots):
    b, tp = h1.shape[:2]
    assert cap % CB_U == 0 and (tp - CH) % CB_NB == 0
    return pl.pallas_call(
        functools.partial(_combine_kernel, cap=cap, tp=tp),
        grid=(b, N_EXPERTS),
        in_specs=[pl.BlockSpec((1, 1, slots), lambda bi, e: (bi * N_EXPERTS + e, 0, 0), memory_space=pltpu.SMEM),
                  pl.BlockSpec(memory_space=pl.ANY),
                  pl.BlockSpec((None, slots, ROW_SUB, ROW_LANE), lambda bi, e: (bi * N_EXPERTS + e, 0, 0, 0)),
                  pl.BlockSpec((1, ROW_SUB, ROW_LANE), lambda bi, e: (0, 0, 0))],
        out_specs=pl.BlockSpec(memory_space=pl.ANY),
        out_shape=jax.ShapeDtypeStruct((b, tp - CH, D_MODEL), F32),
        scratch_shapes=[pltpu.VMEM((tp, ROW_SUB, ROW_LANE), F32), pltpu.VMEM((2, CB_NB, D_MODEL), F32),
                        pltpu.SemaphoreType.DMA(()), pltpu.SemaphoreType.DMA((2,))],
        compiler_params=_cparams(("arbitrary", "arbitrary"), VMEM_LIMIT_BIG),
        name="combine",
    )(idx_local, h1, ye, gf.reshape(1, ROW_SUB, ROW_LANE))


def _rope_tables(tp):
    half = RET_QK_HEAD // 2
    pos = jnp.arange(tp, dtype=F32) - float(PAD)
    inv = ROPE_BASE ** (-jnp.arange(half, dtype=F32) / half)
    ang = pos[:, None] * inv[None, :]
    cos, sin = jnp.cos(ang), jnp.sin(ang)
    return jnp.concatenate([cos, cos], axis=1), jnp.concatenate([-sin, sin], axis=1)


def _filter_features(t_len):
    half = NFFT // 2
    q = np.arange(NFFT)
    r = CH * (q % CH) + q // CH
    p_main = np.where(r < half, r, NFFT - r)
    valid_main = (r != half).astype(np.float32)
    m = np.arange(FAR)
    pad = np.zeros(HALF - 2 * FAR, np.int64)
    p_far = np.concatenate([half + m, half - FAR + m, pad, half + m, half - m, pad])
    valid_far = np.ones(CH, np.float32)
    valid_far[HALF + FAR] = 0.0
    p = jnp.asarray(np.concatenate([p_main, p_far]).astype(np.float32))
    valid = jnp.asarray(np.concatenate([valid_main, valid_far]))
    t_norm = p / (t_len - 1)
    bands = (HY_EMB_DIM - 1) // 2
    fr = jnp.linspace(1e-4, bands - 1, bands, dtype=F32)
    ang = (2.0 * math.pi * p / t_len)[:, None] * fr[None, :]
    feat = jnp.concatenate([t_norm[:, None], jnp.cos(ang), -jnp.sin(ang), valid[:, None]], axis=-1)
    return jnp.pad(feat, ((0, 0), (0, FEAT_W - feat.shape[1])))


def kernel(x, meta_tokens, norm1_g, w_in, ret_decay_fwd, ret_decay_bwd, ret_head_norm_g, w_ret_out,
           hy_conv_w, hy_conv_b, hy_filt_w1, hy_filt_b1, hy_filt_w2, hy_filt_b2, hy_filt_w3, hy_filt_b3,
           hy_filt_freq, hy_filt_w4, hy_skip, w_hy_out, w_o, norm2_g, w_router, w_exp_gate, w_exp_up,
           w_exp_down, final_norm_g):
    b, seq, d = x.shape
    t_len = seq + N_META
    tp = PAD + t_len
    assert d == D_MODEL and tp % IP_TM == 0 and tp % CH == 0 and t_len - NFFT // 2 == FAR
    cap = EC_CAPACITY * t_len // N_EXPERTS
    slots = -(-cap // 16) * 16
    l = 0

    meta = jnp.broadcast_to(meta_tokens[None].astype(x.dtype), (b, N_META, d))
    h0 = jnp.concatenate([jnp.zeros((b, PAD, d), x.dtype), meta, x], axis=1)

    cs, sn = _rope_tables(tp)
    proj = _in_proj(h0.reshape(b * tp, d), norm1_g[l][None], w_in[l].astype(BF16), cs, sn, tp)
    proj3 = proj.reshape(b, tp, IN_PROJ_W)

    lf = jax.nn.log_sigmoid(ret_decay_fwd[l].astype(F32))
    lb = jax.nn.log_sigmoid(ret_decay_bwd[l].astype(F32))
    og = _retention(proj3, lf, lb, ret_head_norm_g[l][None], tp)

    x0c, z = _hy_prep(proj3, hy_conv_w[l], hy_conv_b[l][None], tp)

    feat = _filter_features(t_len)
    w1p = jnp.pad(hy_filt_w1[l].astype(F32), ((0, FEAT_W - HY_EMB_DIM), (0, 0)))
    max_decay = math.log(HY_DECAY_TARGET) / HY_FAST_DECAY_PCT
    min_decay = math.log(HY_DECAY_TARGET) / HY_SLOW_DECAY_PCT
    dl = jnp.abs(jnp.linspace(min_decay, max_decay, D_MODEL, dtype=F32))[None]
    fargs = (w1p, hy_filt_b1[l][None].astype(F32), hy_filt_w2[l].astype(F32), hy_filt_b2[l][None].astype(F32),
             hy_filt_w3[l].astype(F32), hy_filt_b3[l][None].astype(F32), hy_filt_freq[l][None].astype(F32),
             hy_filt_w4[l].astype(F32), dl)
    g = _filters(feat[:NFFT], *fargs, FT_ROWS)
    h_far = _filters(feat[NFFT:], *fargs, CH)
    yc = _fft_conv(z, g)
    corr = _far_correction(h_far, z, tp)

    w_router_p = jnp.pad(w_router[l].astype(F32), ((0, 0), (0, ROUTER_W - N_EXPERTS)))
    w_router_hi = w_router_p.astype(BF16)
    w_router_p = jnp.stack([w_router_hi, (w_router_p - w_router_hi.astype(F32)).astype(BF16)])
    h1, xn2, aff = _merge(h0, og, x0c, z, yc, proj3, corr, hy_skip[l][None].astype(F32),
                          w_ret_out[l].astype(BF16), w_hy_out[l].astype(BF16), w_o[l].astype(BF16),
                          norm2_g[l][None].astype(F32), w_router_p, tp)

    idx, gate = _topk(aff, cap, slots)
    live = (jnp.arange(slots) < cap)[None, None, :]
    idx_local = jnp.where(live, idx, PAD)
    idx_flat = idx_local + (jnp.arange(b * N_EXPERTS, dtype=I32) // N_EXPERTS * tp)[:, None, None]
    ye = _moe_ffn(idx_flat, gate, xn2.reshape(b * tp, d),
                  w_exp_gate[l], w_exp_up[l], w_exp_down[l], slots)
    return _combine(idx_local, h1, ye, final_norm_g.astype(F32), cap, slots)
```

```python
import functools
import math

import numpy as np
import jax
import jax.numpy as jnp
from jax import lax
from jax.experimental import pallas as pl
from jax.experimental.pallas import tpu as pltpu

F32 = jnp.float32
BF16 = jnp.bfloat16
I32 = jnp.int32

D_MODEL = 1024
N_META = 16
RET_HEADS = 4
RET_QK_HEAD = 128
RET_V_HEAD = 256
ROPE_BASE = 10000.0
HY_EMB_DIM = 33
HY_FILTER_ORDER = 64
HY_FAST_DECAY_PCT = 0.3
HY_SLOW_DECAY_PCT = 1.5
HY_DECAY_TARGET = 1e-2
N_EXPERTS = 16
EC_CAPACITY = 2
D_FF = 2 * D_MODEL
RMS_EPS = 1e-6
IN_PROJ_W = 8192

CH = 128
PAD = CH - N_META
NFFT = 16384
ZCH = 80
FAR = 16

VMEM_LIMIT_BIG = 56 * 1024 * 1024
VMEM_LIMIT_MID = 40 * 1024 * 1024


def _cparams(sem, vmem=VMEM_LIMIT_MID):
    return pltpu.CompilerParams(dimension_semantics=sem, vmem_limit_bytes=vmem)


def _split_hi_lo(w):
    w = w.astype(F32)
    hi = w.astype(BF16)
    return jnp.stack([hi, (w - hi.astype(F32)).astype(BF16)])


def _dot3(a, w_hi, w_lo):
    a_hi = a.astype(BF16)
    a_lo = (a - a_hi.astype(F32)).astype(BF16)
    return (jnp.dot(a_hi, w_hi, preferred_element_type=F32) + jnp.dot(a_lo, w_hi, preferred_element_type=F32)
            + jnp.dot(a_hi, w_lo, preferred_element_type=F32))


IP_TM = 1280
IP_TN = 1024
IP_CN = 256


def _inproj_kernel(x_ref, g_ref, w_ref, cs_ref, sn_ref, o_ref, xn_ref):
    j = pl.program_id(1)

    @pl.when(j == 0)
    def _():
        x = x_ref[...]
        ms = jnp.mean(x * x, axis=-1, keepdims=True)
        xn_ref[...] = (x * lax.rsqrt(ms + RMS_EPS) * g_ref[...]).astype(BF16)

    def run(epilogue):
        for c in range(IP_TN // IP_CN):
            cols = slice(c * IP_CN, (c + 1) * IP_CN)
            acc = jnp.dot(xn_ref[...], w_ref[:, cols], preferred_element_type=F32)
            epilogue(c, cols, acc)

    def rotary(c, cols, acc):
        scale = 1.0 if c < (IP_TN // IP_CN) // 2 else RET_QK_HEAD ** -0.5
        cs = cs_ref[...] * scale
        sn = sn_ref[...] * scale
        for hh in range(IP_CN // RET_QK_HEAD):
            xh = acc[:, hh * RET_QK_HEAD:(hh + 1) * RET_QK_HEAD]
            rot = xh * cs + pltpu.roll(xh, RET_QK_HEAD // 2, axis=1) * sn
            lo = c * IP_CN + hh * RET_QK_HEAD
            o_ref[:, lo:lo + RET_QK_HEAD] = rot.astype(BF16)

    def raw(c, cols, acc):
        o_ref[:, cols] = acc.astype(BF16)

    def swish(c, cols, acc):
        o_ref[:, cols] = (acc * jax.nn.sigmoid(acc)).astype(BF16)

    def sigm(c, cols, acc):
        o_ref[:, cols] = jax.nn.sigmoid(acc).astype(BF16)

    pl.when(j == 0)(lambda: run(rotary))
    pl.when(jnp.logical_or(j == 1, jnp.logical_and(j >= 3, j < 6)))(lambda: run(raw))
    pl.when(j == 2)(lambda: run(swish))
    pl.when(j >= 6)(lambda: run(sigm))


def _in_proj(h0, g1, w_in_bf, cs, sn, tp):
    n_rows = h0.shape[0]
    return pl.pallas_call(
        _inproj_kernel,
        grid=(n_rows // IP_TM, IN_PROJ_W // IP_TN),
        in_specs=[
            pl.BlockSpec((IP_TM, D_MODEL), lambda i, j: (i, 0)),
            pl.BlockSpec((1, D_MODEL), lambda i, j: (0, 0)),
            pl.BlockSpec((D_MODEL, IP_TN), lambda i, j: (0, j)),
            pl.BlockSpec((IP_TM, RET_QK_HEAD), lambda i, j: (i, 0)),
            pl.BlockSpec((IP_TM, RET_QK_HEAD), lambda i, j: (i, 0)),
        ],
        out_specs=pl.BlockSpec((IP_TM, IP_TN), lambda i, j: (i, j)),
        out_shape=jax.ShapeDtypeStruct((n_rows, IN_PROJ_W), BF16),
        scratch_shapes=[pltpu.VMEM((IP_TM, D_MODEL), BF16)],
        compiler_params=_cparams(("arbitrary", "arbitrary")),
        name="in_proj",
    )(h0, g1, w_in_bf, cs, sn)


RET_UNROLL = 5


def _ret_kernel(lf_ref, lb_ref, q_ref, k_ref, v_ref, gr_ref, gn_ref, o_ref, ob_ref, s_ref, *, nch):
    h = pl.program_id(1)
    lf = lf_ref[h]
    lb = lb_ref[h]
    ri = lax.broadcasted_iota(I32, (CH, CH), 0).astype(F32)
    ci = lax.broadcasted_iota(I32, (CH, CH), 1).astype(F32)
    diff = ri - ci
    mask = jnp.exp(jnp.where(diff >= 0, lf * diff, -lb * diff))
    w_end = jnp.exp(lf * (CH - 1.0 - ri))
    w_start = jnp.exp(lb * ri)
    qw_f = jnp.exp(lf * (ri + 1.0))
    qw_b = jnp.exp(lb * (CH - ri))
    dec_f = jnp.exp(jnp.full((CH, RET_V_HEAD), lf * CH, F32))
    dec_b = jnp.exp(jnp.full((CH, RET_V_HEAD), lb * CH, F32))
    tn_dims = (((0,), (0,)), ((), ()))
    nt_dims = (((1,), (1,)), ((), ()))

    s_ref[...] = jnp.zeros_like(s_ref)

    def bwd(it, carry):
        s = s_ref[...]
        for u in range(RET_UNROLL):
            n = nch - 1 - (it * RET_UNROLL + u)
            r0 = pl.multiple_of(n * CH, CH)
            q = q_ref[0, pl.ds(r0, CH), :].astype(F32)
            k = k_ref[0, pl.ds(r0, CH), :].astype(F32)
            v = v_ref[0, pl.ds(r0, CH), :]
            ob_ref[pl.ds(r0, CH), :] = jnp.dot((q * qw_b).astype(BF16), s.astype(BF16),
                                               preferred_element_type=F32)
            a = lax.dot_general((k * w_start).astype(BF16), v, tn_dims, preferred_element_type=F32)
            s = s * dec_b + a
        s_ref[...] = s
        return carry

    lax.fori_loop(0, nch // RET_UNROLL, bwd, 0)

    s_ref[...] = jnp.zeros_like(s_ref)
    gn = gn_ref[...]

    def fwd(it, carry):
        s = s_ref[...]
        for u in range(RET_UNROLL):
            n = it * RET_UNROLL + u
            r0 = pl.multiple_of(n * CH, CH)
            qb = q_ref[0, pl.ds(r0, CH), :]
            kb = k_ref[0, pl.ds(r0, CH), :]
            v = v_ref[0, pl.ds(r0, CH), :]
            q = qb.astype(F32)
            k = kb.astype(F32)
            scores = lax.dot_general(qb, kb, nt_dims, preferred_element_type=F32) * mask
            o = jnp.dot(scores.astype(BF16), v, preferred_element_type=F32)
            o = o + jnp.dot((q * qw_f).astype(BF16), s.astype(BF16), preferred_element_type=F32)
            o = o + ob_ref[pl.ds(r0, CH), :]
            a = lax.dot_general((k * w_end).astype(BF16), v, tn_dims, preferred_element_type=F32)
            s = s * dec_f + a
            y = o * lax.rsqrt(jnp.mean(o * o, axis=-1, keepdims=True) + RMS_EPS) * gn
            o_ref[0, pl.ds(r0, CH), :] = (y * gr_ref[0, pl.ds(r0, CH), :].astype(F32)).astype(BF16)
        s_ref[...] = s
        return carry

    lax.fori_loop(0, nch // RET_UNROLL, fwd, 0)


def _retention(proj3, lf, lb, gn, tp):
    b = proj3.shape[0]
    nch = tp // CH
    qk_blocks = (RET_HEADS * RET_QK_HEAD) // RET_QK_HEAD
    v_blk0 = (2 * RET_HEADS * RET_QK_HEAD) // RET_V_HEAD
    g_blk0 = v_blk0 + RET_HEADS
    smem = pl.BlockSpec(memory_space=pltpu.SMEM)
    return pl.pallas_call(
        functools.partial(_ret_kernel, nch=nch),
        grid=(b, RET_HEADS),
        in_specs=[
            smem, smem,
            pl.BlockSpec((1, tp, RET_QK_HEAD), lambda bi, h: (bi, 0, h)),
            pl.BlockSpec((1, tp, RET_QK_HEAD), lambda bi, h: (bi, 0, qk_blocks + h)),
            pl.BlockSpec((1, tp, RET_V_HEAD), lambda bi, h: (bi, 0, v_blk0 + h)),
            pl.BlockSpec((1, tp, RET_V_HEAD), lambda bi, h: (bi, 0, g_blk0 + h)),
            pl.BlockSpec((1, RET_V_HEAD), lambda bi, h: (0, h)),
        ],
        out_specs=pl.BlockSpec((1, tp, RET_V_HEAD), lambda bi, h: (bi, 0, h)),
        out_shape=jax.ShapeDtypeStruct((b, tp, RET_HEADS * RET_V_HEAD), BF16),
        scratch_shapes=[pltpu.VMEM((tp, RET_V_HEAD), F32), pltpu.VMEM((RET_QK_HEAD, RET_V_HEAD), F32)],
        compiler_params=_cparams(("arbitrary", "arbitrary"), VMEM_LIMIT_BIG),
        name="retention",
    )(lf, lb, proj3, proj3, proj3, proj3, gn)


HP_CW = 128


def _hyprep_kernel(u0_ref, u1_ref, u2_ref, w0_ref, w1_ref, w2_ref, b0_ref, b1_ref, b2_ref,
                   x0_ref, z_ref, *, nch, tz):
    rows = lax.broadcasted_iota(I32, (CH, HP_CW), 0)
    halo = 16

    def conv(u_ref, w_ref, b_ref, n, r0):
        cur = u_ref[0, pl.ds(r0, CH), :].astype(F32)
        rp = pl.multiple_of(jnp.maximum(r0 - halo, 0), halo)
        rn = pl.multiple_of(jnp.minimum(r0 + CH, (nch - 1) * CH), halo)
        prev = u_ref[0, pl.ds(rp, halo), :].astype(F32)[halo - 1:halo, :]
        nxt = u_ref[0, pl.ds(rn, halo), :].astype(F32)[0:1, :]
        prev = jnp.where(n > 0, prev, 0.0)
        nxt = jnp.where(n < nch - 1, nxt, 0.0)
        up = jnp.where(rows == 0, prev, pltpu.roll(cur, 1, axis=0))
        dn = jnp.where(rows == CH - 1, nxt, pltpu.roll(cur, CH - 1, axis=0))
        w = w_ref[...]
        return up * w[0:1, :] + cur * w[1:2, :] + dn * w[2:3, :] + b_ref[...]

    def body(n, carry):
        r0 = pl.multiple_of(n * CH, CH)
        x0 = conv(u0_ref, w0_ref, b0_ref, n, r0)
        x1 = conv(u1_ref, w1_ref, b1_ref, n, r0)
        vv = conv(u2_ref, w2_ref, b2_ref, n, r0)
        z = jnp.where(rows + r0 >= PAD, x1 * vv, 0.0)
        x0_ref[0, pl.ds(r0, CH), :] = x0.astype(BF16)
        z_ref[0, pl.ds(r0, CH), :] = z.astype(BF16)
        return carry

    lax.fori_loop(0, nch, body, 0)
    z_ref[0, nch * CH:tz, :] = jnp.zeros((tz - nch * CH, HP_CW), BF16)


def _hy_prep(proj3, conv_w, conv_b, tp):
    b = proj3.shape[0]
    nch = tp // CH
    tz = ZCH * CH
    ncb = D_MODEL // HP_CW
    u_blk0 = 3072 // HP_CW
    uspec = lambda s: pl.BlockSpec((1, tp, HP_CW), lambda bi, c: (bi, 0, u_blk0 + s * ncb + c))
    wspec = lambda s: pl.BlockSpec((3, HP_CW), lambda bi, c: (0, s * ncb + c))
    bspec = lambda s: pl.BlockSpec((1, HP_CW), lambda bi, c: (0, s * ncb + c))
    return pl.pallas_call(
        functools.partial(_hyprep_kernel, nch=nch, tz=tz),
        grid=(b, ncb),
        in_specs=[uspec(0), uspec(1), uspec(2), wspec(0), wspec(1), wspec(2), bspec(0), bspec(1), bspec(2)],
        out_specs=[pl.BlockSpec((1, tp, HP_CW), lambda bi, c: (bi, 0, c)),
                   pl.BlockSpec((1, tz, HP_CW), lambda bi, c: (bi, 0, c))],
        out_shape=[jax.ShapeDtypeStruct((b, tp, D_MODEL), BF16),
                   jax.ShapeDtypeStruct((b, tz, D_MODEL), BF16)],
        compiler_params=_cparams(("arbitrary", "arbitrary")),
        name="hy_prep",
    )(proj3, proj3, proj3, conv_w, conv_w, conv_w, conv_b, conv_b, conv_b)


FEAT_W = 128


FT_ROWS = 512
HALF = CH // 2


def _filter_kernel(feat_ref, w1_ref, b1_ref, w2_ref, b2_ref, w3_ref, b3_ref, fq_ref, w4_ref, dl_ref, o_ref,
                   *, groups):
    feat = feat_ref[...]
    fq = fq_ref[...]
    hdn = jnp.sin(fq * (_dot3(feat, w1_ref[0], w1_ref[1]) + b1_ref[...]))
    hdn = jnp.sin(fq * (_dot3(hdn, w2_ref[0], w2_ref[1]) + b2_ref[...]))
    hdn = jnp.sin(fq * (_dot3(hdn, w3_ref[0], w3_ref[1]) + b3_ref[...]))
    scale = jnp.exp(-feat[:, 0:1] * dl_ref[...]) * feat[:, HY_EMB_DIM:HY_EMB_DIM + 1]
    for d in range(2):
        rows = [slice(g * CH + d * HALF, g * CH + (d + 1) * HALF) for g in range(groups)]
        hd = jnp.concatenate([hdn[r] for r in rows], axis=0)
        cols = slice(d * D_MODEL, (d + 1) * D_MODEL)
        filt = _dot3(hd, w4_ref[0, :, cols], w4_ref[1, :, cols])
        for g, r in enumerate(rows):
            o_ref[r, :] = filt[g * HALF:(g + 1) * HALF] * scale[r]


def _filters(feat, w1p, b1, w2, b2, w3, b3, fq, w4, dl, rows):
    n_rows = feat.shape[0]
    full = lambda a: pl.BlockSpec(a.shape, lambda i: (0,) * a.ndim)
    return pl.pallas_call(
        functools.partial(_filter_kernel, groups=rows // CH),
        grid=(n_rows // rows,),
        in_specs=[pl.BlockSpec((rows, FEAT_W), lambda i: (i, 0)),
                  full(w1p), full(b1), full(w2), full(b2), full(w3), full(b3), full(fq), full(w4), full(dl)],
        out_specs=pl.BlockSpec((rows, D_MODEL), lambda i: (i, 0)),
        out_shape=jax.ShapeDtypeStruct((n_rows, D_MODEL), F32),
        compiler_params=_cparams(("arbitrary",)),
        name="hy_filter",
    )(feat, w1p, b1, w2, b2, w3, b3, fq, w4, dl)


FFT_G = 8


def _lanes(j, c):
    return slice(j * c, (j + 1) * c)


def _fft_a_data_kernel(m_ref, z_ref, o_ref):
    c = o_ref.shape[-1]
    for j in range(FFT_G):
        x = jnp.concatenate([z_ref[0, :, _lanes(j, c)], z_ref[1, :, _lanes(j, c)]], axis=0)
        o_ref[j] = jnp.dot(m_ref[j], x, preferred_element_type=F32).astype(BF16)


def _fft_a_filt_kernel(m_ref, g_ref, o_ref):
    for j in range(FFT_G):
        gj = g_ref[j * CH:(j + 1) * CH, :].astype(BF16)
        o_ref[j] = jnp.dot(m_ref[j], gj, preferred_element_type=F32).astype(BF16)


def _fft_b_filt_kernel(f_ref, yr_ref, yi_ref, o_ref):
    c = o_ref.shape[-1]
    for j in range(FFT_G):
        y = jnp.concatenate([yr_ref[:, _lanes(j, c)], yi_ref[:, _lanes(j, c)]], axis=0)
        o_ref[j] = jnp.dot(f_ref[...], y, preferred_element_type=F32) * (1.0 / NFFT)


def _fft_b_data_kernel(f_ref, fi_ref, yr_ref, yi_ref, g_ref, o_ref):
    c = o_ref.shape[-1]
    for j in range(FFT_G):
        y = jnp.concatenate([yr_ref[:, _lanes(j, c)], yi_ref[:, _lanes(j, c)]], axis=0)
        x = jnp.dot(f_ref[...], y, preferred_element_type=F32)
        xr, xi = x[:CH], x[CH:]
        gr, gi = g_ref[j, :CH], g_ref[j, CH:]
        p = jnp.concatenate([xr * gr - xi * gi, xr * gi + xi * gr], axis=0).astype(BF16)
        o_ref[j] = jnp.dot(fi_ref[...], p, preferred_element_type=F32).astype(BF16)


def _fft_a_inv_kernel(m_ref, ur_ref, ui_ref, o_ref):
    c = ur_ref.shape[-1] // FFT_G
    for j in range(FFT_G):
        u = jnp.concatenate([ur_ref[:, _lanes(j, c)], ui_ref[:, _lanes(j, c)]], axis=0)
        y = jnp.dot(m_ref[j], u, preferred_element_type=F32).astype(BF16)
        o_ref[0, :, _lanes(j, c)] = y[:ZCH]
        o_ref[1, :, _lanes(j, c)] = y[ZCH:]


def _dft_tables():
    n2 = np.arange(CH)[:, None, None]
    k1 = np.arange(CH)[None, :, None]

    def theta(n1_count):
        n1 = np.arange(n1_count)[None, None, :]
        return 2.0 * np.pi * ((k1 * (CH * n1 + n2)) % NFFT) / NFFT

    th = theta(ZCH)
    c, s = np.cos(th), np.sin(th)
    m_a = np.concatenate([np.concatenate([c, s], axis=2), np.concatenate([-s, c], axis=2)], axis=1)
    m_ainv = np.transpose(m_a, (0, 2, 1))
    th = theta(CH)
    m_af = np.concatenate([np.cos(th), -np.sin(th)], axis=1)
    a = 2.0 * np.pi * ((np.arange(CH)[:, None] * np.arange(CH)[None, :]) % CH) / CH
    c, s = np.cos(a), np.sin(a)
    f2 = np.block([[c, s], [-s, c]])
    f2i = np.block([[c, -s], [s, c]])
    f = lambda t: jnp.asarray(t.astype(np.float32)).astype(BF16)
    return f(m_a), f(m_ainv), f(m_af), f(f2), f(f2i)


def _fft_conv(z, g):
    c = z.shape[-1]
    m_a, m_ainv, m_af, f2, f2i = _dft_tables()
    cp = _cparams(("arbitrary",), VMEM_LIMIT_BIG)
    steps = CH // FFT_G
    gc = FFT_G * c
    full2 = pl.BlockSpec((2 * CH, 2 * CH), lambda i: (0, 0))
    col_re = pl.BlockSpec((CH, gc), lambda i: (0, i))
    col_im = pl.BlockSpec((CH, gc), lambda i: (0, steps + i))
    blk = pl.BlockSpec((FFT_G, 2 * CH, c), lambda i: (i, 0, 0))
    spec_shape = jax.ShapeDtypeStruct((CH, 2 * CH, c), BF16)

    yg = pl.pallas_call(
        _fft_a_filt_kernel, grid=(steps,),
        in_specs=[pl.BlockSpec((FFT_G, 2 * CH, CH), lambda i: (i, 0, 0)),
                  pl.BlockSpec((FFT_G * CH, c), lambda i: (i, 0))],
        out_specs=blk, out_shape=spec_shape, compiler_params=cp, name="fft_a_filt",
    )(m_af, g)
    yg2 = yg.reshape(CH, 2 * CH * c)
    gh = pl.pallas_call(
        _fft_b_filt_kernel, grid=(steps,),
        in_specs=[full2, col_re, col_im],
        out_specs=blk, out_shape=jax.ShapeDtypeStruct((CH, 2 * CH, c), F32),
        compiler_params=cp, name="fft_b_filt",
    )(f2, yg2, yg2)

    y = pl.pallas_call(
        _fft_a_data_kernel, grid=(steps,),
        in_specs=[pl.BlockSpec((FFT_G, 2 * CH, 2 * ZCH), lambda i: (i, 0, 0)),
                  pl.BlockSpec((2, ZCH, gc), lambda i: (0, 0, i))],
        out_specs=blk, out_shape=spec_shape, compiler_params=cp, name="fft_a_data",
    )(m_a, z.reshape(2, ZCH, CH * c))
    y2 = y.reshape(CH, 2 * CH * c)
    u = pl.pallas_call(
        _fft_b_data_kernel, grid=(steps,),
        in_specs=[full2, full2, col_re, col_im, blk],
        out_specs=blk, out_shape=spec_shape, compiler_params=cp, name="fft_b_data",
    )(f2, f2i, y2, y2, gh)
    u2 = u.reshape(CH, 2 * CH * c)
    yc = pl.pallas_call(
        _fft_a_inv_kernel, grid=(steps,),
        in_specs=[pl.BlockSpec((FFT_G, 2 * ZCH, 2 * CH), lambda i: (i, 0, 0)), col_re, col_im],
        out_specs=pl.BlockSpec((2, ZCH, gc), lambda i: (0, 0, i)),
        out_shape=jax.ShapeDtypeStruct((2, ZCH, CH * c), BF16),
        compiler_params=cp, name="fft_a_inv",
    )(m_ainv, u2, u2)
    return yc.reshape(2, ZCH * CH, c)


def _far_kernel(far_ref, zm_ref, zl_ref, o_ref):
    g_hi = far_ref[HALF + FAR:HALF + 2 * FAR, :]
    ef = far_ref[0:FAR, :] - g_hi
    g_lo = far_ref[FAR:2 * FAR, :]
    hb = far_ref[HALF:HALF + FAR, :]
    zm = zm_ref[0, PAD:CH, :].astype(F32)
    zl = zl_ref[0, PAD:CH, :].astype(F32)
    row = lambda a, i: a[i:i + 1, :]
    eb = [row(hb, 0) - row(g_hi, 0)] + [row(hb, m) - row(g_lo, FAR - m) for m in range(1, FAR)]
    for j in range(FAR):
        acc_f = row(ef, j) * row(zm, 0)
        for i in range(1, j + 1):
            acc_f = acc_f + row(ef, j - i) * row(zm, i)
        o_ref[0, 0, j:j + 1, :] = acc_f
        acc_b = eb[0] * row(zl, j)
        for i in range(j + 1, FAR):
            acc_b = acc_b + eb[i - j] * row(zl, i)
        o_ref[0, 1, j:j + 1, :] = acc_b


def _far_correction(h_far, z, tp):
    b, _, c = z.shape
    last = tp // CH - 1
    return pl.pallas_call(
        _far_kernel, grid=(b,),
        in_specs=[pl.BlockSpec((CH, c), lambda bi: (0, 0)),
                  pl.BlockSpec((1, CH, c), lambda bi: (bi, 0, 0)),
                  pl.BlockSpec((1, CH, c), lambda bi: (bi, last, 0))],
        out_specs=pl.BlockSpec((1, 2, FAR, c), lambda bi: (bi, 0, 0, 0)),
        out_shape=jax.ShapeDtypeStruct((b, 2, FAR, c), F32),
        compiler_params=_cparams(("arbitrary",)),
        name="hy_far",
    )(h_far, z, z)


MG_TM = 640
MG_CN = 256
ROUTER_W = 128
ROW_SUB, ROW_LANE = 8, 128


def _merge_kernel(h_ref, og_ref, x0_ref, z_ref, yc_ref, ga_ref, gb_ref, corr_ref, skip_ref,
                  wr_ref, wh_ref, wo_ref, g2_ref, wrt_ref,
                  h1_ref, xn_ref, aff_ref, pre_ref, mix_ref, h1s_ref, *, nt_b):
    i = pl.program_id(0)
    ib = i % nt_b
    pre_ref[...] = yc_ref[0].astype(F32) + z_ref[0].astype(F32) * skip_ref[...]

    @pl.when(ib == 0)
    def _():
        pre_ref[PAD:CH, :] += corr_ref[0, 1]

    @pl.when(ib == nt_b - 1)
    def _():
        pre_ref[MG_TM - FAR:MG_TM, :] += corr_ref[0, 0]

    chunks = [slice(c * MG_CN, (c + 1) * MG_CN) for c in range(D_MODEL // MG_CN)]
    og = og_ref[0]
    pre = (x0_ref[0].astype(F32) * pre_ref[...]).astype(BF16)
    for cols in chunks:
        ya = jnp.dot(og, wr_ref[:, cols], preferred_element_type=F32)
        yb = jnp.dot(pre, wh_ref[:, cols], preferred_element_type=F32)
        mix_ref[:, cols] = (ga_ref[0, :, cols].astype(F32) * ya + gb_ref[0, :, cols].astype(F32) * yb).astype(BF16)
    mixed = mix_ref[...]
    ss = jnp.zeros((MG_TM, 1), F32)
    for cols in chunks:
        h1 = h_ref[0, :, cols] + jnp.dot(mixed, wo_ref[:, cols], preferred_element_type=F32)
        h1s_ref[:, cols] = h1
        ss = ss + jnp.sum(h1 * h1, axis=-1, keepdims=True)
    rinv = lax.rsqrt(ss * (1.0 / D_MODEL) + RMS_EPS)
    h1 = h1s_ref[...]
    h1_ref[0] = h1.reshape(MG_TM, ROW_SUB, ROW_LANE)
    xn = h1 * rinv * g2_ref[...]
    xn_ref[0] = xn.reshape(MG_TM, ROW_SUB, ROW_LANE)
    logits = _dot3(xn, wrt_ref[0], wrt_ref[1])
    lane = lax.broadcasted_iota(I32, logits.shape, 1)
    logits = jnp.where(lane < N_EXPERTS, logits, -jnp.inf)
    m = jnp.max(logits, axis=-1, keepdims=True)
    e = jnp.exp(logits - m)
    aff = e / jnp.sum(e, axis=-1, keepdims=True)
    rows = lax.broadcasted_iota(I32, logits.shape, 0) + ib * MG_TM
    aff_ref[0] = jnp.where(rows >= PAD, aff, -1.0)


def _merge(h0, og, x0c, z, yc, proj3, corr, skip, w_ret, w_hy, w_o, g2, w_router_p, tp):
    b = h0.shape[0]
    nt_b = tp // MG_TM
    ga_blk0 = 6144 // D_MODEL
    row = lambda w: pl.BlockSpec((1, MG_TM, w), lambda i: (i // nt_b, i % nt_b, 0))
    full = lambda a: pl.BlockSpec(a.shape, lambda i: (0,) * a.ndim)
    return pl.pallas_call(
        functools.partial(_merge_kernel, nt_b=nt_b),
        grid=(b * nt_b,),
        in_specs=[row(D_MODEL), row(D_MODEL), row(D_MODEL), row(D_MODEL), row(D_MODEL),
                  pl.BlockSpec((1, MG_TM, D_MODEL), lambda i: (i // nt_b, i % nt_b, ga_blk0)),
                  pl.BlockSpec((1, MG_TM, D_MODEL), lambda i: (i // nt_b, i % nt_b, ga_blk0 + 1)),
                  pl.BlockSpec((1, 2, FAR, D_MODEL), lambda i: (i // nt_b, 0, 0, 0)),
                  full(skip), full(w_ret), full(w_hy), full(w_o), full(g2), full(w_router_p)],
        out_specs=[pl.BlockSpec((1, MG_TM, ROW_SUB, ROW_LANE), lambda i: (i // nt_b, i % nt_b, 0, 0)),
                   pl.BlockSpec((1, MG_TM, ROW_SUB, ROW_LANE), lambda i: (i // nt_b, i % nt_b, 0, 0)),
                   row(ROUTER_W)],
        out_shape=[jax.ShapeDtypeStruct((b, tp, ROW_SUB, ROW_LANE), F32),
                   jax.ShapeDtypeStruct((b, tp, ROW_SUB, ROW_LANE), F32),
                   jax.ShapeDtypeStruct((b, tp, ROUTER_W), F32)],
        scratch_shapes=[pltpu.VMEM((MG_TM, D_MODEL), F32), pltpu.VMEM((MG_TM, D_MODEL), BF16),
                        pltpu.VMEM((MG_TM, D_MODEL), F32)],
        compiler_params=_cparams(("arbitrary",), VMEM_LIMIT_BIG),
        name="merge",
    )(h0, og, x0c, z, yc, proj3, proj3, corr, skip, w_ret, w_hy, w_o, g2, w_router_p)


TK_W = 128


def _select_kernel(aff_ref, low_ref, slot_ref, offs_ref, *, cap, nch):
    def chunk(c):
        r0 = pl.multiple_of(c * CH, CH)
        return aff_ref[0, pl.ds(r0, CH), :]

    def count(pred):
        def body(c, acc):
            return acc + pred(chunk(c)).astype(I32)

        acc = lax.fori_loop(0, nch, body, jnp.zeros((CH, ROUTER_W), I32))
        return jnp.sum(acc, axis=0, keepdims=True)

    def search(it, bits):
        cand = bits | jnp.left_shift(1, 29 - it)
        cand_f = pltpu.bitcast(cand, F32)
        return jnp.where(count(lambda a: a >= cand_f) >= cap, cand, bits)

    thr = pltpu.bitcast(lax.fori_loop(0, 30, search, jnp.zeros((1, ROUTER_W), I32)), F32)
    need = (cap - count(lambda a: a > thr)).astype(F32)
    low = low_ref[...]

    def scan(c, carry):
        c_eq, c_sel = carry
        a = chunk(c)
        eq = a == thr
        eq_f = eq.astype(F32)
        eq_rank = jnp.dot(low, eq_f, preferred_element_type=F32) + c_eq
        sel = jnp.logical_or(a > thr, jnp.logical_and(eq, eq_rank < need))
        sel_f = sel.astype(F32)
        slot = jnp.dot(low, sel_f, preferred_element_type=F32) + c_sel
        r0 = pl.multiple_of(c * CH, CH)
        slot_ref[0, pl.ds(r0, CH), :] = jnp.where(sel, slot, -1.0).astype(I32)
        offs_ref[0, c] = c_sel.astype(I32)
        return (c_eq + jnp.sum(eq_f, axis=0, keepdims=True), c_sel + jnp.sum(sel_f, axis=0, keepdims=True))

    zero = jnp.zeros((1, ROUTER_W), F32)
    lax.fori_loop(0, nch, scan, (zero, zero))


def _extract_kernel(offs_ref, slot_ref, aff_ref, idx_ref, gate_ref, *, nch, nwin):
    idx_ref[...] = jnp.zeros_like(idx_ref)
    gate_ref[...] = jnp.zeros_like(gate_ref)
    lane = lax.broadcasted_iota(I32, (CH, TK_W), 1)
    trow = lax.broadcasted_iota(I32, (CH, TK_W), 0)

    def per_chunk(c, carry):
        r0 = pl.multiple_of(c * CH, CH)
        slots = slot_ref[0, pl.ds(r0, CH), :]
        affs = aff_ref[0, pl.ds(r0, CH), :]
        tpos = (trow + r0).astype(F32)
        for e in range(N_EXPERTS):
            col = jnp.broadcast_to(slots[:, e:e + 1], (CH, TK_W))
            gcol = jnp.broadcast_to(affs[:, e:e + 1], (CH, TK_W))
            w0 = offs_ref[0, 0, c * N_EXPERTS + e] // TK_W
            for dw in range(2):
                w = w0 + dw
                base = jnp.where(w < nwin, w * TK_W, -2 * TK_W)
                hit = col == lane + base
                row = e * nwin + jnp.minimum(w, nwin - 1)
                idx_ref[row] += jnp.sum(jnp.where(hit, tpos, 0.0), axis=0, keepdims=True)
                gate_ref[row] += jnp.sum(jnp.where(hit, gcol, 0.0), axis=0, keepdims=True)
        return carry

    lax.fori_loop(0, nch, per_chunk, 0)


def _topk(aff, cap, slots):
    b, tp, _ = aff.shape
    nch = tp // CH
    nwin = -(-slots // TK_W)
    low = jnp.asarray(np.tril(np.ones((CH, CH), np.float32), k=-1))
    slot, offs = pl.pallas_call(
        functools.partial(_select_kernel, cap=cap, nch=nch),
        grid=(b,),
        in_specs=[pl.BlockSpec((1, tp, ROUTER_W), lambda bi: (bi, 0, 0)),
                  pl.BlockSpec((CH, CH), lambda bi: (0, 0))],
        out_specs=[pl.BlockSpec((1, tp, ROUTER_W), lambda bi: (bi, 0, 0)),
                   pl.BlockSpec((1, nch, 1, ROUTER_W), lambda bi: (bi, 0, 0, 0))],
        out_shape=[jax.ShapeDtypeStruct((b, tp, ROUTER_W), I32),
                   jax.ShapeDtypeStruct((b, nch, 1, ROUTER_W), I32)],
        compiler_params=_cparams(("arbitrary",)),
        name="topk_select",
    )(aff, low)
    offs_s = offs[:, :, 0, :N_EXPERTS].reshape(b, 1, nch * N_EXPERTS)
    rows = pl.BlockSpec((None, N_EXPERTS * nwin, 1, TK_W), lambda bi: (bi, 0, 0, 0))
    out = jax.ShapeDtypeStruct((b, N_EXPERTS * nwin, 1, TK_W), F32)
    idx, gate = pl.pallas_call(
        functools.partial(_extract_kernel, nch=nch, nwin=nwin),
        grid=(b,),
        in_specs=[pl.BlockSpec((1, 1, nch * N_EXPERTS), lambda bi: (bi, 0, 0), memory_space=pltpu.SMEM),
                  pl.BlockSpec((1, tp, ROUTER_W), lambda bi: (bi, 0, 0)),
                  pl.BlockSpec((1, tp, ROUTER_W), lambda bi: (bi, 0, 0))],
        out_specs=[rows, rows],
        out_shape=[out, out],
        compiler_params=_cparams(("arbitrary",)),
        name="topk_extract",
    )(offs_s, slot, aff)
    idx = idx.reshape(b * N_EXPERTS, 1, nwin * TK_W)[:, :, :slots].astype(I32)
    gate = gate.reshape(b * N_EXPERTS, nwin * TK_W, 1)[:, :slots]
    return idx, gate


FF_TF = 1024
MOE_CN = 256


MOE_UNROLL = 4


def _moe_kernel(idx_ref, idx_next_ref, xn_hbm, gate_ref, wg_ref, wu_ref, wd_ref, o_ref,
                xe32_ref, xe_ref, hid_ref, acc_ref, sem,
                *, slots, nf):
    i = pl.program_id(0)
    f = pl.program_id(1)
    buf = i % 2
    share = slots // nf

    def row_copy(ids_ref, s, b):
        return pltpu.make_async_copy(xn_hbm.at[ids_ref[0, 0, s]], xe32_ref.at[b, s], sem.at[b])

    def for_rows(lo, n, fn):
        def body(k, carry):
            for u in range(MOE_UNROLL):
                fn(lo + k * MOE_UNROLL + u)
            return carry

        lax.fori_loop(0, n // MOE_UNROLL, body, 0)

    @pl.when(jnp.logical_and(i == 0, f == 0))
    def _():
        for_rows(0, slots, lambda s: row_copy(idx_ref, s, 0).start())

    @pl.when(f == 0)
    def _():
        for_rows(0, slots, lambda s: row_copy(idx_ref, s, buf).wait())
        xe_ref[...] = xe32_ref[buf].reshape(slots, D_MODEL).astype(BF16)

    @pl.when(f == 0)
    def _():
        acc_ref[...] = jnp.zeros_like(acc_ref)

    n_up, n_down = FF_TF // MOE_CN, D_MODEL // MOE_CN
    per_chunk = share // (n_up + n_down)

    def prefetch(chunk):
        base = f * share + chunk * per_chunk
        for u in range(per_chunk):
            row_copy(idx_next_ref, base + u, 1 - buf).start()

    xe = xe_ref[...]
    for c in range(n_up):
        cols = slice(c * MOE_CN, (c + 1) * MOE_CN)
        gg = jnp.dot(xe, wg_ref[0, :, cols].astype(BF16), preferred_element_type=F32)
        uu = jnp.dot(xe, wu_ref[0, :, cols].astype(BF16), preferred_element_type=F32)
        hid_ref[:, cols] = (gg * jax.nn.sigmoid(gg) * uu).astype(BF16)
        prefetch(c)
    scale = jnp.where(f == pl.num_programs(1) - 1, gate_ref[0], 1.0)
    hid = hid_ref[...]
    for c in range(n_down):
        cols = slice(c * MOE_CN, (c + 1) * MOE_CN)
        part = jnp.dot(hid, wd_ref[0, :, cols].astype(BF16), preferred_element_type=F32)
        acc_ref[:, cols] = (acc_ref[:, cols] + part) * scale
        prefetch(n_up + c)

    @pl.when(f == pl.num_programs(1) - 1)
    def _():
        o_ref[0] = acc_ref[...].reshape(slots, ROW_SUB, ROW_LANE)

    @pl.when(jnp.logical_and(i == pl.num_programs(0) - 1, f == pl.num_programs(1) - 1))
    def _():
        for_rows(0, slots, lambda s: row_copy(idx_next_ref, s, 1 - buf).wait())


def _moe_ffn(idx, gate, xn_flat, w_gate, w_up, w_down, slots):
    be = idx.shape[0]
    nf = D_FF // FF_TF
    assert slots % MOE_UNROLL == 0 and slots % (nf * (FF_TF // MOE_CN + D_MODEL // MOE_CN)) == 0
    return pl.pallas_call(
        functools.partial(_moe_kernel, slots=slots, nf=nf),
        grid=(be, nf),
        in_specs=[pl.BlockSpec((1, 1, slots), lambda i, f: (i, 0, 0), memory_space=pltpu.SMEM),
                  pl.BlockSpec((1, 1, slots), lambda i, f: (jnp.minimum(i + 1, be - 1), 0, 0),
                               memory_space=pltpu.SMEM),
                  pl.BlockSpec(memory_space=pl.ANY),
                  pl.BlockSpec((1, slots, 1), lambda i, f: (i, 0, 0)),
                  pl.BlockSpec((1, D_MODEL, FF_TF), lambda i, f: (i % N_EXPERTS, 0, f)),
                  pl.BlockSpec((1, D_MODEL, FF_TF), lambda i, f: (i % N_EXPERTS, 0, f)),
                  pl.BlockSpec((1, FF_TF, D_MODEL), lambda i, f: (i % N_EXPERTS, f, 0))],
        out_specs=pl.BlockSpec((1, slots, ROW_SUB, ROW_LANE), lambda i, f: (i, 0, 0, 0)),
        out_shape=jax.ShapeDtypeStruct((be, slots, ROW_SUB, ROW_LANE), F32),
        scratch_shapes=[pltpu.VMEM((2, slots, ROW_SUB, ROW_LANE), F32), pltpu.VMEM((slots, D_MODEL), BF16),
                        pltpu.VMEM((slots, FF_TF), BF16), pltpu.VMEM((slots, D_MODEL), F32),
                        pltpu.SemaphoreType.DMA((2,))],
        compiler_params=_cparams(("arbitrary", "arbitrary"), VMEM_LIMIT_BIG),
        name="moe_ffn",
    )(idx, idx, xn_flat, gate, w_gate, w_up, w_down)


CB_U = 6
CB_NB = 512


def _combine_kernel(idx_ref, h1_hbm, ye_ref, g_ref, o_hbm, acc_ref, stage_ref, sem, osem, *, cap, tp):
    b = pl.program_id(0)
    e = pl.program_id(1)

    @pl.when(e == 0)
    def _():
        cp = pltpu.make_async_copy(h1_hbm.at[b], acc_ref, sem)
        cp.start()
        cp.wait()

    def rmw(g, carry):
        ts = [idx_ref[0, 0, g * CB_U + u] for u in range(CB_U)]
        vals = [acc_ref[ts[u]] + ye_ref[g * CB_U + u] for u in range(CB_U)]
        for u in range(CB_U):
            acc_ref[ts[u]] = vals[u]
        return carry

    lax.fori_loop(0, cap // CB_U, rmw, 0)

    @pl.when(e == pl.num_programs(1) - 1)
    def _():
        gamma = g_ref[...]
        n_blocks = (tp - CH) // CB_NB

        def out_copy(k):
            return pltpu.make_async_copy(stage_ref.at[k % 2], o_hbm.at[b, pl.ds(k * CB_NB, CB_NB), :],
                                         osem.at[k % 2])

        for k in range(n_blocks):
            x = acc_ref[pl.ds(CH + k * CB_NB, CB_NB)]
            ms = jnp.sum(jnp.sum(x * x, axis=2, keepdims=True), axis=1, keepdims=True) * (1.0 / D_MODEL)
            y = x * lax.rsqrt(ms + RMS_EPS) * gamma
            if k >= 2:
                out_copy(k - 2).wait()
            stage_ref[k % 2] = y.reshape(CB_NB, D_MODEL)
            out_copy(k).start()
        for k in range(max(n_blocks - 2, 0), n_blocks):
            out_copy(k).wait()


def _combine(idx_local, h1, ye, gf, cap, slots):
    b, tp = h1.shape[:2]
    assert cap % CB_U == 0 and (tp - CH) % CB_NB == 0
    return pl.pallas_call(
        functools.partial(_combine_kernel, cap=cap, tp=tp),
        grid=(b, N_EXPERTS),
        in_specs=[pl.BlockSpec((1, 1, slots), lambda bi, e: (bi * N_EXPERTS + e, 0, 0), memory_space=pltpu.SMEM),
                  pl.BlockSpec(memory_space=pl.ANY),
                  pl.BlockSpec((None, slots, ROW_SUB, ROW_LANE), lambda bi, e: (bi * N_EXPERTS + e, 0, 0, 0)),
                  pl.BlockSpec((1, ROW_SUB, ROW_LANE), lambda bi, e: (0, 0, 0))],
        out_specs=pl.BlockSpec(memory_space=pl.ANY),
        out_shape=jax.ShapeDtypeStruct((b, tp - CH, D_MODEL), F32),
        scratch_shapes=[pltpu.VMEM((tp, ROW_SUB, ROW_LANE), F32), pltpu.VMEM((2, CB_NB, D_MODEL), F32),
                        pltpu.SemaphoreType.DMA(()), pltpu.SemaphoreType.DMA((2,))],
        compiler_params=_cparams(("arbitrary", "arbitrary"), VMEM_LIMIT_BIG),
        name="combine",
    )(idx_local, h1, ye, gf.reshape(1, ROW_SUB, ROW_LANE))


def _rope_tables(tp):
    half = RET_QK_HEAD // 2
    pos = jnp.arange(tp, dtype=F32) - float(PAD)
    inv = ROPE_BASE ** (-jnp.arange(half, dtype=F32) / half)
    ang = pos[:, None] * inv[None, :]
    cos, sin = jnp.cos(ang), jnp.sin(ang)
    return jnp.concatenate([cos, cos], axis=1), jnp.concatenate([-sin, sin], axis=1)


def _filter_features(t_len):
    half = NFFT // 2
    q = np.arange(NFFT)
    r = CH * (q % CH) + q // CH
    p_main = np.where(r < half, r, NFFT - r)
    valid_main = (r != half).astype(np.float32)
    m = np.arange(FAR)
    pad = np.zeros(HALF - 2 * FAR, np.int64)
    p_far = np.concatenate([half + m, half - FAR + m, pad, half + m, half - m, pad])
    valid_far = np.ones(CH, np.float32)
    valid_far[HALF + FAR] = 0.0
    p = jnp.asarray(np.concatenate([p_main, p_far]).astype(np.float32))
    valid = jnp.asarray(np.concatenate([valid_main, valid_far]))
    t_norm = p / (t_len - 1)
    bands = (HY_EMB_DIM - 1) // 2
    fr = jnp.linspace(1e-4, bands - 1, bands, dtype=F32)
    ang = (2.0 * math.pi * p / t_len)[:, None] * fr[None, :]
    feat = jnp.concatenate([t_norm[:, None], jnp.cos(ang), -jnp.sin(ang), valid[:, None]], axis=-1)
    return jnp.pad(feat, ((0, 0), (0, FEAT_W - feat.shape[1])))


def kernel(x, meta_tokens, norm1_g, w_in, ret_decay_fwd, ret_decay_bwd, ret_head_norm_g, w_ret_out,
           hy_conv_w, hy_conv_b, hy_filt_w1, hy_filt_b1, hy_filt_w2, hy_filt_b2, hy_filt_w3, hy_filt_b3,
           hy_filt_freq, hy_filt_w4, hy_skip, w_hy_out, w_o, norm2_g, w_router, w_exp_gate, w_exp_up,
           w_exp_down, final_norm_g):
    b, seq, d = x.shape
    t_len = seq + N_META
    tp = PAD + t_len
    assert d == D_MODEL and (b * tp) % IP_TM == 0 and tp % MG_TM == 0 and tp % CH == 0 and t_len - NFFT // 2 == FAR
    cap = EC_CAPACITY * t_len // N_EXPERTS
    slots = -(-cap // 16) * 16
    l = 0

    meta = jnp.broadcast_to(meta_tokens[None].astype(x.dtype), (b, N_META, d))
    h0 = jnp.concatenate([jnp.zeros((b, PAD, d), x.dtype), meta, x], axis=1)

    cs, sn = (jnp.tile(t, (b, 1)) for t in _rope_tables(tp))
    proj = _in_proj(h0.reshape(b * tp, d), norm1_g[l][None], w_in[l].astype(BF16), cs, sn, tp)
    proj3 = proj.reshape(b, tp, IN_PROJ_W)

    lf = jax.nn.log_sigmoid(ret_decay_fwd[l].astype(F32))
    lb = jax.nn.log_sigmoid(ret_decay_bwd[l].astype(F32))
    og = _retention(proj3, lf, lb, ret_head_norm_g[l][None], tp)

    x0c, z = _hy_prep(proj3, hy_conv_w[l], hy_conv_b[l][None], tp)

    feat = _filter_features(t_len)
    w1p = jnp.pad(hy_filt_w1[l].astype(F32), ((0, FEAT_W - HY_EMB_DIM), (0, 0)))
    max_decay = math.log(HY_DECAY_TARGET) / HY_FAST_DECAY_PCT
    min_decay = math.log(HY_DECAY_TARGET) / HY_SLOW_DECAY_PCT
    dl = jnp.abs(jnp.linspace(min_decay, max_decay, D_MODEL, dtype=F32))[None]
    fargs = (_split_hi_lo(w1p), hy_filt_b1[l][None].astype(F32), _split_hi_lo(hy_filt_w2[l]),
             hy_filt_b2[l][None].astype(F32), _split_hi_lo(hy_filt_w3[l]), hy_filt_b3[l][None].astype(F32),
             hy_filt_freq[l][None].astype(F32), _split_hi_lo(hy_filt_w4[l]), dl)
    g = _filters(feat[:NFFT], *fargs, FT_ROWS)
    h_far = _filters(feat[NFFT:], *fargs, CH)
    yc = _fft_conv(z, g)
    corr = _far_correction(h_far, z, tp)

    w_router_p = jnp.pad(w_router[l].astype(F32), ((0, 0), (0, ROUTER_W - N_EXPERTS)))
    w_router_p = _split_hi_lo(w_router_p)
    h1, xn2, aff = _merge(h0, og, x0c, z, yc, proj3, corr, hy_skip[l][None].astype(F32),
                          w_ret_out[l].astype(BF16), w_hy_out[l].astype(BF16), w_o[l].astype(BF16),
                          norm2_g[l][None].astype(F32), w_router_p, tp)

    idx, gate = _topk(aff, cap, slots)
    live = (jnp.arange(slots) < cap)[None, None, :]
    idx_local = jnp.where(live, idx, PAD)
    idx_flat = idx_local + (jnp.arange(b * N_EXPERTS, dtype=I32) // N_EXPERTS * tp)[:, None, None]
    ye = _moe_ffn(idx_flat, gate, xn2.reshape(b * tp, ROW_SUB, ROW_LANE),
                  w_exp_gate[l], w_exp_up[l], w_exp_down[l], slots)
    return _combine(idx_local, h1, ye, final_norm_g.astype(F32), cap, slots)
```

```python
import functools
import math

import numpy as np
import jax
import jax.numpy as jnp
from jax import lax
from jax.experimental import pallas as pl
from jax.experimental.pallas import tpu as pltpu

F32 = jnp.float32
BF16 = jnp.bfloat16
I32 = jnp.int32

D_MODEL = 1024
N_META = 16
RET_HEADS = 4
RET_QK_HEAD = 128
RET_V_HEAD = 256
ROPE_BASE = 10000.0
HY_EMB_DIM = 33
HY_FILTER_ORDER = 64
HY_FAST_DECAY_PCT = 0.3
HY_SLOW_DECAY_PCT = 1.5
HY_DECAY_TARGET = 1e-2
N_EXPERTS = 16
EC_CAPACITY = 2
D_FF = 2 * D_MODEL
RMS_EPS = 1e-6
IN_PROJ_W = 8192

CH = 128
PAD = CH - N_META
NFFT = 16384
ZCH = 80
FAR = 16

VMEM_LIMIT_BIG = 56 * 1024 * 1024
VMEM_LIMIT_MID = 40 * 1024 * 1024


def _cparams(sem, vmem=VMEM_LIMIT_MID):
    return pltpu.CompilerParams(dimension_semantics=sem, vmem_limit_bytes=vmem)


def _split_hi_lo(w):
    w = w.astype(F32)
    hi = w.astype(BF16)
    return jnp.stack([hi, (w - hi.astype(F32)).astype(BF16)])


def _dot3(a, w_hi, w_lo):
    a_hi = a.astype(BF16)
    a_lo = (a - a_hi.astype(F32)).astype(BF16)
    return (jnp.dot(a_hi, w_hi, preferred_element_type=F32) + jnp.dot(a_lo, w_hi, preferred_element_type=F32)
            + jnp.dot(a_hi, w_lo, preferred_element_type=F32))


IP_TM = 1280
IP_TN = 1024
IP_CN = 256


def _inproj_kernel(x_ref, g_ref, w_ref, cs_ref, sn_ref, o_ref, xn_ref):
    j = pl.program_id(1)

    @pl.when(j == 0)
    def _():
        x = x_ref[...]
        ms = jnp.mean(x * x, axis=-1, keepdims=True)
        xn_ref[...] = (x * lax.rsqrt(ms + RMS_EPS) * g_ref[...]).astype(BF16)

    def run(epilogue):
        for c in range(IP_TN // IP_CN):
            cols = slice(c * IP_CN, (c + 1) * IP_CN)
            acc = jnp.dot(xn_ref[...], w_ref[:, cols], preferred_element_type=F32)
            epilogue(c, cols, acc)

    def rotary(c, cols, acc):
        scale = 1.0 if c < (IP_TN // IP_CN) // 2 else RET_QK_HEAD ** -0.5
        cs = cs_ref[...] * scale
        sn = sn_ref[...] * scale
        for hh in range(IP_CN // RET_QK_HEAD):
            xh = acc[:, hh * RET_QK_HEAD:(hh + 1) * RET_QK_HEAD]
            rot = xh * cs + pltpu.roll(xh, RET_QK_HEAD // 2, axis=1) * sn
            lo = c * IP_CN + hh * RET_QK_HEAD
            o_ref[:, lo:lo + RET_QK_HEAD] = rot.astype(BF16)

    def raw(c, cols, acc):
        o_ref[:, cols] = acc.astype(BF16)

    def swish(c, cols, acc):
        o_ref[:, cols] = (acc * jax.nn.sigmoid(acc)).astype(BF16)

    def sigm(c, cols, acc):
        o_ref[:, cols] = jax.nn.sigmoid(acc).astype(BF16)

    pl.when(j == 0)(lambda: run(rotary))
    pl.when(jnp.logical_or(j == 1, jnp.logical_and(j >= 3, j < 6)))(lambda: run(raw))
    pl.when(j == 2)(lambda: run(swish))
    pl.when(j >= 6)(lambda: run(sigm))


def _in_proj(h0, g1, w_in_bf, cs, sn, tp):
    n_rows = h0.shape[0]
    return pl.pallas_call(
        _inproj_kernel,
        grid=(n_rows // IP_TM, IN_PROJ_W // IP_TN),
        in_specs=[
            pl.BlockSpec((IP_TM, D_MODEL), lambda i, j: (i, 0)),
            pl.BlockSpec((1, D_MODEL), lambda i, j: (0, 0)),
            pl.BlockSpec((D_MODEL, IP_TN), lambda i, j: (0, j)),
            pl.BlockSpec((IP_TM, RET_QK_HEAD), lambda i, j: (i, 0)),
            pl.BlockSpec((IP_TM, RET_QK_HEAD), lambda i, j: (i, 0)),
        ],
        out_specs=pl.BlockSpec((IP_TM, IP_TN), lambda i, j: (i, j)),
        out_shape=jax.ShapeDtypeStruct((n_rows, IN_PROJ_W), BF16),
        scratch_shapes=[pltpu.VMEM((IP_TM, D_MODEL), BF16)],
        compiler_params=_cparams(("arbitrary", "arbitrary")),
        name="in_proj",
    )(h0, g1, w_in_bf, cs, sn)


RET_UNROLL = 5


def _ret_kernel(lf_ref, lb_ref, q_ref, k_ref, v_ref, gr_ref, gn_ref, o_ref, ob_ref, s_ref, *, nch):
    h = pl.program_id(1)
    lf = lf_ref[h]
    lb = lb_ref[h]
    ri = lax.broadcasted_iota(I32, (CH, CH), 0).astype(F32)
    ci = lax.broadcasted_iota(I32, (CH, CH), 1).astype(F32)
    diff = ri - ci
    mask = jnp.exp(jnp.where(diff >= 0, lf * diff, -lb * diff))
    w_end = jnp.exp(lf * (CH - 1.0 - ri))
    w_start = jnp.exp(lb * ri)
    qw_f = jnp.exp(lf * (ri + 1.0))
    qw_b = jnp.exp(lb * (CH - ri))
    dec_f = jnp.exp(jnp.full((CH, RET_V_HEAD), lf * CH, F32))
    dec_b = jnp.exp(jnp.full((CH, RET_V_HEAD), lb * CH, F32))
    tn_dims = (((0,), (0,)), ((), ()))
    nt_dims = (((1,), (1,)), ((), ()))

    s_ref[...] = jnp.zeros_like(s_ref)

    def bwd(it, carry):
        s = s_ref[...]
        for u in range(RET_UNROLL):
            n = nch - 1 - (it * RET_UNROLL + u)
            r0 = pl.multiple_of(n * CH, CH)
            q = q_ref[0, pl.ds(r0, CH), :].astype(F32)
            k = k_ref[0, pl.ds(r0, CH), :].astype(F32)
            v = v_ref[0, pl.ds(r0, CH), :]
            ob_ref[pl.ds(r0, CH), :] = jnp.dot((q * qw_b).astype(BF16), s.astype(BF16),
                                               preferred_element_type=F32)
            a = lax.dot_general((k * w_start).astype(BF16), v, tn_dims, preferred_element_type=F32)
            s = s * dec_b + a
        s_ref[...] = s
        return carry

    lax.fori_loop(0, nch // RET_UNROLL, bwd, 0)

    s_ref[...] = jnp.zeros_like(s_ref)
    gn = gn_ref[...]

    def fwd(it, carry):
        s = s_ref[...]
        for u in range(RET_UNROLL):
            n = it * RET_UNROLL + u
            r0 = pl.multiple_of(n * CH, CH)
            qb = q_ref[0, pl.ds(r0, CH), :]
            kb = k_ref[0, pl.ds(r0, CH), :]
            v = v_ref[0, pl.ds(r0, CH), :]
            q = qb.astype(F32)
            k = kb.astype(F32)
            scores = lax.dot_general(qb, kb, nt_dims, preferred_element_type=F32) * mask
            o = jnp.dot(scores.astype(BF16), v, preferred_element_type=F32)
            o = o + jnp.dot((q * qw_f).astype(BF16), s.astype(BF16), preferred_element_type=F32)
            o = o + ob_ref[pl.ds(r0, CH), :]
            a = lax.dot_general((k * w_end).astype(BF16), v, tn_dims, preferred_element_type=F32)
            s = s * dec_f + a
            y = o * lax.rsqrt(jnp.mean(o * o, axis=-1, keepdims=True) + RMS_EPS) * gn
            o_ref[0, pl.ds(r0, CH), :] = (y * gr_ref[0, pl.ds(r0, CH), :].astype(F32)).astype(BF16)
        s_ref[...] = s
        return carry

    lax.fori_loop(0, nch // RET_UNROLL, fwd, 0)


def _retention(proj3, lf, lb, gn, tp):
    b = proj3.shape[0]
    nch = tp // CH
    qk_blocks = (RET_HEADS * RET_QK_HEAD) // RET_QK_HEAD
    v_blk0 = (2 * RET_HEADS * RET_QK_HEAD) // RET_V_HEAD
    g_blk0 = v_blk0 + RET_HEADS
    smem = pl.BlockSpec(memory_space=pltpu.SMEM)
    return pl.pallas_call(
        functools.partial(_ret_kernel, nch=nch),
        grid=(b, RET_HEADS),
        in_specs=[
            smem, smem,
            pl.BlockSpec((1, tp, RET_QK_HEAD), lambda bi, h: (bi, 0, h)),
            pl.BlockSpec((1, tp, RET_QK_HEAD), lambda bi, h: (bi, 0, qk_blocks + h)),
            pl.BlockSpec((1, tp, RET_V_HEAD), lambda bi, h: (bi, 0, v_blk0 + h)),
            pl.BlockSpec((1, tp, RET_V_HEAD), lambda bi, h: (bi, 0, g_blk0 + h)),
            pl.BlockSpec((1, RET_V_HEAD), lambda bi, h: (0, h)),
        ],
        out_specs=pl.BlockSpec((1, tp, RET_V_HEAD), lambda bi, h: (bi, 0, h)),
        out_shape=jax.ShapeDtypeStruct((b, tp, RET_HEADS * RET_V_HEAD), BF16),
        scratch_shapes=[pltpu.VMEM((tp, RET_V_HEAD), F32), pltpu.VMEM((RET_QK_HEAD, RET_V_HEAD), F32)],
        compiler_params=_cparams(("arbitrary", "arbitrary"), VMEM_LIMIT_BIG),
        name="retention",
    )(lf, lb, proj3, proj3, proj3, proj3, gn)


HP_CW = 128


def _hyprep_kernel(u0_ref, u1_ref, u2_ref, w0_ref, w1_ref, w2_ref, b0_ref, b1_ref, b2_ref,
                   x0_ref, z_ref, *, nch, tz):
    rows = lax.broadcasted_iota(I32, (CH, HP_CW), 0)
    halo = 16

    def conv(u_ref, w_ref, b_ref, n, r0):
        cur = u_ref[0, pl.ds(r0, CH), :].astype(F32)
        rp = pl.multiple_of(jnp.maximum(r0 - halo, 0), halo)
        rn = pl.multiple_of(jnp.minimum(r0 + CH, (nch - 1) * CH), halo)
        prev = u_ref[0, pl.ds(rp, halo), :].astype(F32)[halo - 1:halo, :]
        nxt = u_ref[0, pl.ds(rn, halo), :].astype(F32)[0:1, :]
        prev = jnp.where(n > 0, prev, 0.0)
        nxt = jnp.where(n < nch - 1, nxt, 0.0)
        up = jnp.where(rows == 0, prev, pltpu.roll(cur, 1, axis=0))
        dn = jnp.where(rows == CH - 1, nxt, pltpu.roll(cur, CH - 1, axis=0))
        w = w_ref[...]
        return up * w[0:1, :] + cur * w[1:2, :] + dn * w[2:3, :] + b_ref[...]

    def body(n, carry):
        r0 = pl.multiple_of(n * CH, CH)
        x0 = conv(u0_ref, w0_ref, b0_ref, n, r0)
        x1 = conv(u1_ref, w1_ref, b1_ref, n, r0)
        vv = conv(u2_ref, w2_ref, b2_ref, n, r0)
        z = jnp.where(rows + r0 >= PAD, x1 * vv, 0.0)
        x0_ref[0, pl.ds(r0, CH), :] = x0.astype(BF16)
        z_ref[0, pl.ds(r0, CH), :] = z.astype(BF16)
        return carry

    lax.fori_loop(0, nch, body, 0)
    z_ref[0, nch * CH:tz, :] = jnp.zeros((tz - nch * CH, HP_CW), BF16)


def _hy_prep(proj3, conv_w, conv_b, tp):
    b = proj3.shape[0]
    nch = tp // CH
    tz = ZCH * CH
    ncb = D_MODEL // HP_CW
    u_blk0 = 3072 // HP_CW
    uspec = lambda s: pl.BlockSpec((1, tp, HP_CW), lambda bi, c: (bi, 0, u_blk0 + s * ncb + c))
    wspec = lambda s: pl.BlockSpec((3, HP_CW), lambda bi, c: (0, s * ncb + c))
    bspec = lambda s: pl.BlockSpec((1, HP_CW), lambda bi, c: (0, s * ncb + c))
    return pl.pallas_call(
        functools.partial(_hyprep_kernel, nch=nch, tz=tz),
        grid=(b, ncb),
        in_specs=[uspec(0), uspec(1), uspec(2), wspec(0), wspec(1), wspec(2), bspec(0), bspec(1), bspec(2)],
        out_specs=[pl.BlockSpec((1, tp, HP_CW), lambda bi, c: (bi, 0, c)),
                   pl.BlockSpec((1, tz, HP_CW), lambda bi, c: (bi, 0, c))],
        out_shape=[jax.ShapeDtypeStruct((b, tp, D_MODEL), BF16),
                   jax.ShapeDtypeStruct((b, tz, D_MODEL), BF16)],
        compiler_params=_cparams(("arbitrary", "arbitrary")),
        name="hy_prep",
    )(proj3, proj3, proj3, conv_w, conv_w, conv_w, conv_b, conv_b, conv_b)


FEAT_W = 128


FT_ROWS = 512
HALF = CH // 2


def _filter_kernel(feat_ref, w1_ref, b1_ref, w2_ref, b2_ref, w3_ref, b3_ref, fq_ref, w4_ref, dl_ref, o_ref,
                   *, groups):
    feat = feat_ref[...]
    fq = fq_ref[...]
    hdn = jnp.sin(fq * (_dot3(feat, w1_ref[0], w1_ref[1]) + b1_ref[...]))
    hdn = jnp.sin(fq * (_dot3(hdn, w2_ref[0], w2_ref[1]) + b2_ref[...]))
    hdn = jnp.sin(fq * (_dot3(hdn, w3_ref[0], w3_ref[1]) + b3_ref[...]))
    scale = jnp.exp(-feat[:, 0:1] * dl_ref[...]) * feat[:, HY_EMB_DIM:HY_EMB_DIM + 1]
    for d in range(2):
        rows = [slice(g * CH + d * HALF, g * CH + (d + 1) * HALF) for g in range(groups)]
        hd = jnp.concatenate([hdn[r] for r in rows], axis=0)
        cols = slice(d * D_MODEL, (d + 1) * D_MODEL)
        filt = _dot3(hd, w4_ref[0, :, cols], w4_ref[1, :, cols])
        for g, r in enumerate(rows):
            o_ref[r, :] = filt[g * HALF:(g + 1) * HALF] * scale[r]


def _filters(feat, w1p, b1, w2, b2, w3, b3, fq, w4, dl, rows):
    n_rows = feat.shape[0]
    full = lambda a: pl.BlockSpec(a.shape, lambda i: (0,) * a.ndim)
    return pl.pallas_call(
        functools.partial(_filter_kernel, groups=rows // CH),
        grid=(n_rows // rows,),
        in_specs=[pl.BlockSpec((rows, FEAT_W), lambda i: (i, 0)),
                  full(w1p), full(b1), full(w2), full(b2), full(w3), full(b3), full(fq), full(w4), full(dl)],
        out_specs=pl.BlockSpec((rows, D_MODEL), lambda i: (i, 0)),
        out_shape=jax.ShapeDtypeStruct((n_rows, D_MODEL), F32),
        compiler_params=_cparams(("arbitrary",)),
        name="hy_filter",
    )(feat, w1p, b1, w2, b2, w3, b3, fq, w4, dl)


FFT_G = 8


def _lanes(j, c):
    return slice(j * c, (j + 1) * c)


def _fft_a_data_kernel(m_ref, z_ref, o_ref):
    c = o_ref.shape[-1]
    for j in range(FFT_G):
        x = jnp.concatenate([z_ref[0, :, _lanes(j, c)], z_ref[1, :, _lanes(j, c)]], axis=0)
        o_ref[j] = jnp.dot(m_ref[j], x, preferred_element_type=F32).astype(BF16)


def _fft_a_filt_kernel(m_ref, g_ref, o_ref):
    for j in range(FFT_G):
        gj = g_ref[j * CH:(j + 1) * CH, :].astype(BF16)
        o_ref[j] = jnp.dot(m_ref[j], gj, preferred_element_type=F32).astype(BF16)


def _fft_b_kernel(f_ref, fi_ref, yr_ref, yi_ref, gr_ref, gi_ref, o_ref):
    c = o_ref.shape[-1]
    for j in range(FFT_G):
        y = jnp.concatenate([yr_ref[:, _lanes(j, c)], yi_ref[:, _lanes(j, c)]], axis=0)
        yg = jnp.concatenate([gr_ref[:, _lanes(j, c)], gi_ref[:, _lanes(j, c)]], axis=0)
        x = jnp.dot(f_ref[...], y, preferred_element_type=F32)
        g = jnp.dot(f_ref[...], yg, preferred_element_type=F32) * (1.0 / NFFT)
        xr, xi = x[:CH], x[CH:]
        gr, gi = g[:CH], g[CH:]
        p = jnp.concatenate([xr * gr - xi * gi, xr * gi + xi * gr], axis=0).astype(BF16)
        o_ref[j] = jnp.dot(fi_ref[...], p, preferred_element_type=F32).astype(BF16)


def _fft_a_inv_kernel(m_ref, ur_ref, ui_ref, o_ref):
    c = ur_ref.shape[-1] // FFT_G
    for j in range(FFT_G):
        u = jnp.concatenate([ur_ref[:, _lanes(j, c)], ui_ref[:, _lanes(j, c)]], axis=0)
        y = jnp.dot(m_ref[j], u, preferred_element_type=F32).astype(BF16)
        o_ref[0, :, _lanes(j, c)] = y[:ZCH]
        o_ref[1, :, _lanes(j, c)] = y[ZCH:]


def _dft_tables():
    n2 = np.arange(CH)[:, None, None]
    k1 = np.arange(CH)[None, :, None]

    def theta(n1_count):
        n1 = np.arange(n1_count)[None, None, :]
        return 2.0 * np.pi * ((k1 * (CH * n1 + n2)) % NFFT) / NFFT

    th = theta(ZCH)
    c, s = np.cos(th), np.sin(th)
    m_a = np.concatenate([np.concatenate([c, s], axis=2), np.concatenate([-s, c], axis=2)], axis=1)
    m_ainv = np.transpose(m_a, (0, 2, 1))
    th = theta(CH)
    m_af = np.concatenate([np.cos(th), -np.sin(th)], axis=1)
    a = 2.0 * np.pi * ((np.arange(CH)[:, None] * np.arange(CH)[None, :]) % CH) / CH
    c, s = np.cos(a), np.sin(a)
    f2 = np.block([[c, s], [-s, c]])
    f2i = np.block([[c, -s], [s, c]])
    f = lambda t: jnp.asarray(t.astype(np.float32)).astype(BF16)
    return f(m_a), f(m_ainv), f(m_af), f(f2), f(f2i)


def _fft_conv(z, g):
    c = z.shape[-1]
    m_a, m_ainv, m_af, f2, f2i = _dft_tables()
    cp = _cparams(("arbitrary",), VMEM_LIMIT_BIG)
    steps = CH // FFT_G
    gc = FFT_G * c
    full2 = pl.BlockSpec((2 * CH, 2 * CH), lambda i: (0, 0))
    col_re = pl.BlockSpec((CH, gc), lambda i: (0, i))
    col_im = pl.BlockSpec((CH, gc), lambda i: (0, steps + i))
    blk = pl.BlockSpec((FFT_G, 2 * CH, c), lambda i: (i, 0, 0))
    spec_shape = jax.ShapeDtypeStruct((CH, 2 * CH, c), BF16)

    yg = pl.pallas_call(
        _fft_a_filt_kernel, grid=(steps,),
        in_specs=[pl.BlockSpec((FFT_G, 2 * CH, CH), lambda i: (i, 0, 0)),
                  pl.BlockSpec((FFT_G * CH, c), lambda i: (i, 0))],
        out_specs=blk, out_shape=spec_shape, compiler_params=cp, name="fft_a_filt",
    )(m_af, g)
    yg2 = yg.reshape(CH, 2 * CH * c)

    y = pl.pallas_call(
        _fft_a_data_kernel, grid=(steps,),
        in_specs=[pl.BlockSpec((FFT_G, 2 * CH, 2 * ZCH), lambda i: (i, 0, 0)),
                  pl.BlockSpec((2, ZCH, gc), lambda i: (0, 0, i))],
        out_specs=blk, out_shape=spec_shape, compiler_params=cp, name="fft_a_data",
    )(m_a, z.reshape(2, ZCH, CH * c))
    y2 = y.reshape(CH, 2 * CH * c)
    u = pl.pallas_call(
        _fft_b_kernel, grid=(steps,),
        in_specs=[full2, full2, col_re, col_im, col_re, col_im],
        out_specs=blk, out_shape=spec_shape, compiler_params=cp, name="fft_b",
    )(f2, f2i, y2, y2, yg2, yg2)
    u2 = u.reshape(CH, 2 * CH * c)
    yc = pl.pallas_call(
        _fft_a_inv_kernel, grid=(steps,),
        in_specs=[pl.BlockSpec((FFT_G, 2 * ZCH, 2 * CH), lambda i: (i, 0, 0)), col_re, col_im],
        out_specs=pl.BlockSpec((2, ZCH, gc), lambda i: (0, 0, i)),
        out_shape=jax.ShapeDtypeStruct((2, ZCH, CH * c), BF16),
        compiler_params=cp, name="fft_a_inv",
    )(m_ainv, u2, u2)
    return yc.reshape(2, ZCH * CH, c)


def _far_kernel(far_ref, zm_ref, zl_ref, o_ref):
    g_hi = far_ref[HALF + FAR:HALF + 2 * FAR, :]
    ef = far_ref[0:FAR, :] - g_hi
    g_lo = far_ref[FAR:2 * FAR, :]
    hb = far_ref[HALF:HALF + FAR, :]
    zm = zm_ref[0, PAD:CH, :].astype(F32)
    zl = zl_ref[0, PAD:CH, :].astype(F32)
    row = lambda a, i: a[i:i + 1, :]
    eb = [row(hb, 0) - row(g_hi, 0)] + [row(hb, m) - row(g_lo, FAR - m) for m in range(1, FAR)]
    for j in range(FAR):
        acc_f = row(ef, j) * row(zm, 0)
        for i in range(1, j + 1):
            acc_f = acc_f + row(ef, j - i) * row(zm, i)
        o_ref[0, 0, j:j + 1, :] = acc_f
        acc_b = eb[0] * row(zl, j)
        for i in range(j + 1, FAR):
            acc_b = acc_b + eb[i - j] * row(zl, i)
        o_ref[0, 1, j:j + 1, :] = acc_b


def _far_correction(h_far, z, tp):
    b, _, c = z.shape
    last = tp // CH - 1
    return pl.pallas_call(
        _far_kernel, grid=(b,),
        in_specs=[pl.BlockSpec((CH, c), lambda bi: (0, 0)),
                  pl.BlockSpec((1, CH, c), lambda bi: (bi, 0, 0)),
                  pl.BlockSpec((1, CH, c), lambda bi: (bi, last, 0))],
        out_specs=pl.BlockSpec((1, 2, FAR, c), lambda bi: (bi, 0, 0, 0)),
        out_shape=jax.ShapeDtypeStruct((b, 2, FAR, c), F32),
        compiler_params=_cparams(("arbitrary",)),
        name="hy_far",
    )(h_far, z, z)


MG_TM = 640
MG_CN = 256
ROUTER_W = 128
ROW_SUB, ROW_LANE = 8, 128


def _merge_kernel(h_ref, og_ref, x0_ref, z_ref, yc_ref, ga_ref, gb_ref, corr_ref, skip_ref,
                  wr_ref, wh_ref, wo_ref, g2_ref, wrt_ref,
                  h1_ref, xn_ref, aff_ref, pre_ref, mix_ref, h1s_ref, *, nt_b):
    i = pl.program_id(0)
    ib = i % nt_b
    pre_ref[...] = yc_ref[0].astype(F32) + z_ref[0].astype(F32) * skip_ref[...]

    @pl.when(ib == 0)
    def _():
        pre_ref[PAD:CH, :] += corr_ref[0, 1]

    @pl.when(ib == nt_b - 1)
    def _():
        pre_ref[MG_TM - FAR:MG_TM, :] += corr_ref[0, 0]

    chunks = [slice(c * MG_CN, (c + 1) * MG_CN) for c in range(D_MODEL // MG_CN)]
    og = og_ref[0]
    pre = (x0_ref[0].astype(F32) * pre_ref[...]).astype(BF16)
    for cols in chunks:
        ya = jnp.dot(og, wr_ref[:, cols], preferred_element_type=F32)
        yb = jnp.dot(pre, wh_ref[:, cols], preferred_element_type=F32)
        mix_ref[:, cols] = (ga_ref[0, :, cols].astype(F32) * ya + gb_ref[0, :, cols].astype(F32) * yb).astype(BF16)
    mixed = mix_ref[...]
    ss = jnp.zeros((MG_TM, 1), F32)
    for cols in chunks:
        h1 = h_ref[0, :, cols] + jnp.dot(mixed, wo_ref[:, cols], preferred_element_type=F32)
        h1s_ref[:, cols] = h1
        ss = ss + jnp.sum(h1 * h1, axis=-1, keepdims=True)
    rinv = lax.rsqrt(ss * (1.0 / D_MODEL) + RMS_EPS)
    h1 = h1s_ref[...]
    h1_ref[0] = h1.reshape(MG_TM, ROW_SUB, ROW_LANE)
    xn = h1 * rinv * g2_ref[...]
    xn_ref[0] = xn.reshape(MG_TM, ROW_SUB, ROW_LANE)
    logits = _dot3(xn, wrt_ref[0], wrt_ref[1])
    lane = lax.broadcasted_iota(I32, logits.shape, 1)
    logits = jnp.where(lane < N_EXPERTS, logits, -jnp.inf)
    m = jnp.max(logits, axis=-1, keepdims=True)
    e = jnp.exp(logits - m)
    aff = e / jnp.sum(e, axis=-1, keepdims=True)
    rows = lax.broadcasted_iota(I32, logits.shape, 0) + ib * MG_TM
    aff_ref[0] = jnp.where(rows >= PAD, aff, -1.0)


def _merge(h0, og, x0c, z, yc, proj3, corr, skip, w_ret, w_hy, w_o, g2, w_router_p, tp):
    b = h0.shape[0]
    nt_b = tp // MG_TM
    ga_blk0 = 6144 // D_MODEL
    row = lambda w: pl.BlockSpec((1, MG_TM, w), lambda i: (i // nt_b, i % nt_b, 0))
    full = lambda a: pl.BlockSpec(a.shape, lambda i: (0,) * a.ndim)
    return pl.pallas_call(
        functools.partial(_merge_kernel, nt_b=nt_b),
        grid=(b * nt_b,),
        in_specs=[row(D_MODEL), row(D_MODEL), row(D_MODEL), row(D_MODEL), row(D_MODEL),
                  pl.BlockSpec((1, MG_TM, D_MODEL), lambda i: (i // nt_b, i % nt_b, ga_blk0)),
                  pl.BlockSpec((1, MG_TM, D_MODEL), lambda i: (i // nt_b, i % nt_b, ga_blk0 + 1)),
                  pl.BlockSpec((1, 2, FAR, D_MODEL), lambda i: (i // nt_b, 0, 0, 0)),
                  full(skip), full(w_ret), full(w_hy), full(w_o), full(g2), full(w_router_p)],
        out_specs=[pl.BlockSpec((1, MG_TM, ROW_SUB, ROW_LANE), lambda i: (i // nt_b, i % nt_b, 0, 0)),
                   pl.BlockSpec((1, MG_TM, ROW_SUB, ROW_LANE), lambda i: (i // nt_b, i % nt_b, 0, 0)),
                   row(ROUTER_W)],
        out_shape=[jax.ShapeDtypeStruct((b, tp, ROW_SUB, ROW_LANE), F32),
                   jax.ShapeDtypeStruct((b, tp, ROW_SUB, ROW_LANE), F32),
                   jax.ShapeDtypeStruct((b, tp, ROUTER_W), F32)],
        scratch_shapes=[pltpu.VMEM((MG_TM, D_MODEL), F32), pltpu.VMEM((MG_TM, D_MODEL), BF16),
                        pltpu.VMEM((MG_TM, D_MODEL), F32)],
        compiler_params=_cparams(("arbitrary",), VMEM_LIMIT_BIG),
        name="merge",
    )(h0, og, x0c, z, yc, proj3, proj3, corr, skip, w_ret, w_hy, w_o, g2, w_router_p)


TK_W = 128


TK_UNROLL = 5


def _select_kernel(aff_ref, affp_ref, low_ref, slot_ref, offs_ref, *, cap, nch):
    def chunk(c):
        r0 = pl.multiple_of(c * CH, CH)
        return aff_ref[0, pl.ds(r0, CH), :]

    def count(pred):
        cnt = jnp.sum(pred(affp_ref[0]).astype(I32), axis=0, keepdims=True)
        for shift in (N_EXPERTS, 2 * N_EXPERTS, 4 * N_EXPERTS):
            cnt = cnt + pltpu.roll(cnt, shift, axis=1)
        return cnt

    def search(it, bits):
        cand = bits | jnp.left_shift(1, 29 - it)
        cand_f = pltpu.bitcast(cand, F32)
        return jnp.where(count(lambda a: a >= cand_f) >= cap, cand, bits)

    thr = pltpu.bitcast(lax.fori_loop(0, 30, search, jnp.zeros((1, ROUTER_W), I32)), F32)
    need = (cap - count(lambda a: a > thr)).astype(F32)
    low = low_ref[...]

    def scan(it, carry):
        c_eq, c_sel = carry
        for u in range(TK_UNROLL):
            c = it * TK_UNROLL + u
            a = chunk(c)
            eq = a == thr
            eq_f = eq.astype(F32)
            eq_rank = jnp.dot(low, eq_f, preferred_element_type=F32) + c_eq
            sel = jnp.logical_or(a > thr, jnp.logical_and(eq, eq_rank < need))
            sel_f = sel.astype(F32)
            slot = jnp.dot(low, sel_f, preferred_element_type=F32) + c_sel
            r0 = pl.multiple_of(c * CH, CH)
            slot_ref[0, pl.ds(r0, CH), :] = jnp.where(sel, slot, -1.0).astype(I32)
            offs_ref[0, c] = c_sel.astype(I32)
            c_eq = c_eq + jnp.sum(eq_f, axis=0, keepdims=True)
            c_sel = c_sel + jnp.sum(sel_f, axis=0, keepdims=True)
        return c_eq, c_sel

    zero = jnp.zeros((1, ROUTER_W), F32)
    lax.fori_loop(0, nch // TK_UNROLL, scan, (zero, zero))


def _extract_kernel(offs_ref, slot_ref, aff_ref, idx_ref, gate_ref, *, nch, nwin):
    idx_ref[...] = jnp.zeros_like(idx_ref)
    gate_ref[...] = jnp.zeros_like(gate_ref)
    lane = lax.broadcasted_iota(I32, (CH, TK_W), 1)
    trow = lax.broadcasted_iota(I32, (CH, TK_W), 0)

    def per_chunk(c, carry):
        r0 = pl.multiple_of(c * CH, CH)
        slots = slot_ref[0, pl.ds(r0, CH), :]
        affs = aff_ref[0, pl.ds(r0, CH), :]
        tpos = (trow + r0).astype(F32)
        for e in range(N_EXPERTS):
            col = jnp.broadcast_to(slots[:, e:e + 1], (CH, TK_W))
            gcol = jnp.broadcast_to(affs[:, e:e + 1], (CH, TK_W))
            w0 = offs_ref[0, 0, c * N_EXPERTS + e] // TK_W
            for dw in range(2):
                w = w0 + dw
                base = jnp.where(w < nwin, w * TK_W, -2 * TK_W)
                hit = col == lane + base
                row = e * nwin + jnp.minimum(w, nwin - 1)
                idx_ref[row] += jnp.sum(jnp.where(hit, tpos, 0.0), axis=0, keepdims=True)
                gate_ref[row] += jnp.sum(jnp.where(hit, gcol, 0.0), axis=0, keepdims=True)
        return carry

    lax.fori_loop(0, nch, per_chunk, 0)


def _topk(aff, cap, slots):
    b, tp, _ = aff.shape
    nch = tp // CH
    nwin = -(-slots // TK_W)
    low = jnp.asarray(np.tril(np.ones((CH, CH), np.float32), k=-1))
    pack = ROUTER_W // N_EXPERTS
    assert nch % TK_UNROLL == 0 and tp % pack == 0
    aff_packed = aff[:, :, :N_EXPERTS].reshape(b, tp // pack, ROUTER_W)
    slot, offs = pl.pallas_call(
        functools.partial(_select_kernel, cap=cap, nch=nch),
        grid=(b,),
        in_specs=[pl.BlockSpec((1, tp, ROUTER_W), lambda bi: (bi, 0, 0)),
                  pl.BlockSpec((1, tp // pack, ROUTER_W), lambda bi: (bi, 0, 0)),
                  pl.BlockSpec((CH, CH), lambda bi: (0, 0))],
        out_specs=[pl.BlockSpec((1, tp, ROUTER_W), lambda bi: (bi, 0, 0)),
                   pl.BlockSpec((1, nch, 1, ROUTER_W), lambda bi: (bi, 0, 0, 0))],
        out_shape=[jax.ShapeDtypeStruct((b, tp, ROUTER_W), I32),
                   jax.ShapeDtypeStruct((b, nch, 1, ROUTER_W), I32)],
        compiler_params=_cparams(("arbitrary",)),
        name="topk_select",
    )(aff, aff_packed, low)
    offs_s = offs[:, :, 0, :N_EXPERTS].reshape(b, 1, nch * N_EXPERTS)
    rows = pl.BlockSpec((None, N_EXPERTS * nwin, 1, TK_W), lambda bi: (bi, 0, 0, 0))
    out = jax.ShapeDtypeStruct((b, N_EXPERTS * nwin, 1, TK_W), F32)
    idx, gate = pl.pallas_call(
        functools.partial(_extract_kernel, nch=nch, nwin=nwin),
        grid=(b,),
        in_specs=[pl.BlockSpec((1, 1, nch * N_EXPERTS), lambda bi: (bi, 0, 0), memory_space=pltpu.SMEM),
                  pl.BlockSpec((1, tp, ROUTER_W), lambda bi: (bi, 0, 0)),
                  pl.BlockSpec((1, tp, ROUTER_W), lambda bi: (bi, 0, 0))],
        out_specs=[rows, rows],
        out_shape=[out, out],
        compiler_params=_cparams(("arbitrary",)),
        name="topk_extract",
    )(offs_s, slot, aff)
    idx = idx.reshape(b * N_EXPERTS, 1, nwin * TK_W)[:, :, :slots].astype(I32)
    gate = gate.reshape(b * N_EXPERTS, nwin * TK_W, 1)[:, :slots]
    return idx, gate


FF_TF = 1024
MOE_CN = 256


MOE_UNROLL = 4


def _moe_kernel(idx_ref, idx_next_ref, xn_hbm, gate_ref, wg_ref, wu_ref, wd_ref, o_ref,
                xe32_ref, xe_ref, hid_ref, acc_ref, sem,
                *, slots, nf):
    i = pl.program_id(0)
    f = pl.program_id(1)
    buf = i % 2
    share = slots // nf

    def row_copy(ids_ref, s, b):
        return pltpu.make_async_copy(xn_hbm.at[ids_ref[0, 0, s]], xe32_ref.at[b, s], sem.at[b])

    def for_rows(lo, n, fn):
        def body(k, carry):
            for u in range(MOE_UNROLL):
                fn(lo + k * MOE_UNROLL + u)
            return carry

        lax.fori_loop(0, n // MOE_UNROLL, body, 0)

    @pl.when(jnp.logical_and(i == 0, f == 0))
    def _():
        for_rows(0, slots, lambda s: row_copy(idx_ref, s, 0).start())

    @pl.when(f == 0)
    def _():
        for_rows(0, slots, lambda s: row_copy(idx_ref, s, buf).wait())
        xe_ref[...] = xe32_ref[buf].reshape(slots, D_MODEL).astype(BF16)

    @pl.when(f == 0)
    def _():
        acc_ref[...] = jnp.zeros_like(acc_ref)

    n_up, n_down = FF_TF // MOE_CN, D_MODEL // MOE_CN
    per_chunk = share // (n_up + n_down)

    def prefetch(chunk):
        base = f * share + chunk * per_chunk
        for u in range(per_chunk):
            row_copy(idx_next_ref, base + u, 1 - buf).start()

    xe = xe_ref[...]
    for c in range(n_up):
        cols = slice(c * MOE_CN, (c + 1) * MOE_CN)
        gg = jnp.dot(xe, wg_ref[0, :, cols].astype(BF16), preferred_element_type=F32)
        uu = jnp.dot(xe, wu_ref[0, :, cols].astype(BF16), preferred_element_type=F32)
        hid_ref[:, cols] = (gg * jax.nn.sigmoid(gg) * uu).astype(BF16)
        prefetch(c)
    scale = jnp.where(f == pl.num_programs(1) - 1, gate_ref[0], 1.0)
    hid = hid_ref[...]
    for c in range(n_down):
        cols = slice(c * MOE_CN, (c + 1) * MOE_CN)
        part = jnp.dot(hid, wd_ref[0, :, cols].astype(BF16), preferred_element_type=F32)
        acc_ref[:, cols] = (acc_ref[:, cols] + part) * scale
        prefetch(n_up + c)

    @pl.when(f == pl.num_programs(1) - 1)
    def _():
        o_ref[0] = acc_ref[...].reshape(slots, ROW_SUB, ROW_LANE)

    @pl.when(jnp.logical_and(i == pl.num_programs(0) - 1, f == pl.num_programs(1) - 1))
    def _():
        for_rows(0, slots, lambda s: row_copy(idx_next_ref, s, 1 - buf).wait())


def _moe_ffn(idx, gate, xn_flat, w_gate, w_up, w_down, slots):
    be = idx.shape[0]
    nf = D_FF // FF_TF
    assert slots % MOE_UNROLL == 0 and slots % (nf * (FF_TF // MOE_CN + D_MODEL // MOE_CN)) == 0
    return pl.pallas_call(
        functools.partial(_moe_kernel, slots=slots, nf=nf),
        grid=(be, nf),
        in_specs=[pl.BlockSpec((1, 1, slots), lambda i, f: (i, 0, 0), memory_space=pltpu.SMEM),
                  pl.BlockSpec((1, 1, slots), lambda i, f: (jnp.minimum(i + 1, be - 1), 0, 0),
                               memory_space=pltpu.SMEM),
                  pl.BlockSpec(memory_space=pl.ANY),
                  pl.BlockSpec((1, slots, 1), lambda i, f: (i, 0, 0)),
                  pl.BlockSpec((1, D_MODEL, FF_TF), lambda i, f: (i % N_EXPERTS, 0, f)),
                  pl.BlockSpec((1, D_MODEL, FF_TF), lambda i, f: (i % N_EXPERTS, 0, f)),
                  pl.BlockSpec((1, FF_TF, D_MODEL), lambda i, f: (i % N_EXPERTS, f, 0))],
        out_specs=pl.BlockSpec((1, slots, ROW_SUB, ROW_LANE), lambda i, f: (i, 0, 0, 0)),
        out_shape=jax.ShapeDtypeStruct((be, slots, ROW_SUB, ROW_LANE), F32),
        scratch_shapes=[pltpu.VMEM((2, slots, ROW_SUB, ROW_LANE), F32), pltpu.VMEM((slots, D_MODEL), BF16),
                        pltpu.VMEM((slots, FF_TF), BF16), pltpu.VMEM((slots, D_MODEL), F32),
                        pltpu.SemaphoreType.DMA((2,))],
        compiler_params=_cparams(("arbitrary", "arbitrary"), VMEM_LIMIT_BIG),
        name="moe_ffn",
    )(idx, idx, xn_flat, gate, w_gate, w_up, w_down)


CB_U = 6
CB_NB = 512


def _combine_kernel(idx_ref, h1_hbm, ye_ref, g_ref, o_hbm, acc_ref, stage_ref, sem, osem, *, cap, tp):
    b = pl.program_id(0)
    e = pl.program_id(1)

    @pl.when(e == 0)
    def _():
        cp = pltpu.make_async_copy(h1_hbm.at[b], acc_ref, sem)
        cp.start()
        cp.wait()

    def rmw(g, carry):
        ts = [idx_ref[0, 0, g * CB_U + u] for u in range(CB_U)]
        vals = [acc_ref[ts[u]] + ye_ref[g * CB_U + u] for u in range(CB_U)]
        for u in range(CB_U):
            acc_ref[ts[u]] = vals[u]
        return carry

    lax.fori_loop(0, cap // CB_U, rmw, 0)

    @pl.when(e == pl.num_programs(1) - 1)
    def _():
        gamma = g_ref[...]
        n_blocks = (tp - CH) // CB_NB

        def out_copy(k):
            return pltpu.make_async_copy(stage_ref.at[k % 2], o_hbm.at[b, pl.ds(k * CB_NB, CB_NB), :],
                                         osem.at[k % 2])

        for k in range(n_blocks):
            x = acc_ref[pl.ds(CH + k * CB_NB, CB_NB)]
            ms = jnp.sum(jnp.sum(x * x, axis=2, keepdims=True), axis=1, keepdims=True) * (1.0 / D_MODEL)
            y = x * lax.rsqrt(ms + RMS_EPS) * gamma
            if k >= 2:
                out_copy(k - 2).wait()
            stage_ref[k % 2] = y.reshape(CB_NB, D_MODEL)
            out_copy(k).start()
        for k in range(max(n_blocks - 2, 0), n_blocks):
            out_copy(k).wait()


def _combine(idx_local, h1, ye, gf, cap, slots):
    b, tp = h1.shape[:2]
    assert cap % CB_U == 0 and (tp - CH) % CB_NB == 0
    return pl.pallas_call(
        functools.partial(_combine_kernel, cap=cap, tp=tp),
        grid=(b, N_EXPERTS),
        in_specs=[pl.BlockSpec((1, 1, slots), lambda bi, e: (bi * N_EXPERTS + e, 0, 0), memory_space=pltpu.SMEM),
                  pl.BlockSpec(memory_space=pl.ANY),
                  pl.BlockSpec((None, slots, ROW_SUB, ROW_LANE), lambda bi, e: (bi * N_EXPERTS + e, 0, 0, 0)),
                  pl.BlockSpec((1, ROW_SUB, ROW_LANE), lambda bi, e: (0, 0, 0))],
        out_specs=pl.BlockSpec(memory_space=pl.ANY),
        out_shape=jax.ShapeDtypeStruct((b, tp - CH, D_MODEL), F32),
        scratch_shapes=[pltpu.VMEM((tp, ROW_SUB, ROW_LANE), F32), pltpu.VMEM((2, CB_NB, D_MODEL), F32),
                        pltpu.SemaphoreType.DMA(()), pltpu.SemaphoreType.DMA((2,))],
        compiler_params=_cparams(("arbitrary", "arbitrary"), VMEM_LIMIT_BIG),
        name="combine",
    )(idx_local, h1, ye, gf.reshape(1, ROW_SUB, ROW_LANE))


def _rope_tables(tp):
    half = RET_QK_HEAD // 2
    pos = jnp.arange(tp, dtype=F32) - float(PAD)
    inv = ROPE_BASE ** (-jnp.arange(half, dtype=F32) / half)
    ang = pos[:, None] * inv[None, :]
    cos, sin = jnp.cos(ang), jnp.sin(ang)
    return jnp.concatenate([cos, cos], axis=1), jnp.concatenate([-sin, sin], axis=1)


def _filter_features(t_len):
    half = NFFT // 2
    q = np.arange(NFFT)
    r = CH * (q % CH) + q // CH
    p_main = np.where(r < half, r, NFFT - r)
    valid_main = (r != half).astype(np.float32)
    m = np.arange(FAR)
    pad = np.zeros(HALF - 2 * FAR, np.int64)
    p_far = np.concatenate([half + m, half - FAR + m, pad, half + m, half - m, pad])
    valid_far = np.ones(CH, np.float32)
    valid_far[HALF + FAR] = 0.0
    p = jnp.asarray(np.concatenate([p_main, p_far]).astype(np.float32))
    valid = jnp.asarray(np.concatenate([valid_main, valid_far]))
    t_norm = p / (t_len - 1)
    bands = (HY_EMB_DIM - 1) // 2
    fr = jnp.linspace(1e-4, bands - 1, bands, dtype=F32)
    ang = (2.0 * math.pi * p / t_len)[:, None] * fr[None, :]
    feat = jnp.concatenate([t_norm[:, None], jnp.cos(ang), -jnp.sin(ang), valid[:, None]], axis=-1)
    return jnp.pad(feat, ((0, 0), (0, FEAT_W - feat.shape[1])))


def kernel(x, meta_tokens, norm1_g, w_in, ret_decay_fwd, ret_decay_bwd, ret_head_norm_g, w_ret_out,
           hy_conv_w, hy_conv_b, hy_filt_w1, hy_filt_b1, hy_filt_w2, hy_filt_b2, hy_filt_w3, hy_filt_b3,
           hy_filt_freq, hy_filt_w4, hy_skip, w_hy_out, w_o, norm2_g, w_router, w_exp_gate, w_exp_up,
           w_exp_down, final_norm_g):
    b, seq, d = x.shape
    t_len = seq + N_META
    tp = PAD + t_len
    assert d == D_MODEL and (b * tp) % IP_TM == 0 and tp % MG_TM == 0 and tp % CH == 0 and t_len - NFFT // 2 == FAR
    cap = EC_CAPACITY * t_len // N_EXPERTS
    slots = -(-cap // 16) * 16
    l = 0

    meta = jnp.broadcast_to(meta_tokens[None].astype(x.dtype), (b, N_META, d))
    h0 = jnp.concatenate([jnp.zeros((b, PAD, d), x.dtype), meta, x], axis=1)

    cs, sn = (jnp.tile(t, (b, 1)) for t in _rope_tables(tp))
    proj = _in_proj(h0.reshape(b * tp, d), norm1_g[l][None], w_in[l].astype(BF16), cs, sn, tp)
    proj3 = proj.reshape(b, tp, IN_PROJ_W)

    lf = jax.nn.log_sigmoid(ret_decay_fwd[l].astype(F32))
    lb = jax.nn.log_sigmoid(ret_decay_bwd[l].astype(F32))
    og = _retention(proj3, lf, lb, ret_head_norm_g[l][None], tp)

    x0c, z = _hy_prep(proj3, hy_conv_w[l], hy_conv_b[l][None], tp)

    feat = _filter_features(t_len)
    w1p = jnp.pad(hy_filt_w1[l].astype(F32), ((0, FEAT_W - HY_EMB_DIM), (0, 0)))
    max_decay = math.log(HY_DECAY_TARGET) / HY_FAST_DECAY_PCT
    min_decay = math.log(HY_DECAY_TARGET) / HY_SLOW_DECAY_PCT
    dl = jnp.abs(jnp.linspace(min_decay, max_decay, D_MODEL, dtype=F32))[None]
    fargs = (_split_hi_lo(w1p), hy_filt_b1[l][None].astype(F32), _split_hi_lo(hy_filt_w2[l]),
             hy_filt_b2[l][None].astype(F32), _split_hi_lo(hy_filt_w3[l]), hy_filt_b3[l][None].astype(F32),
             hy_filt_freq[l][None].astype(F32), _split_hi_lo(hy_filt_w4[l]), dl)
    g = _filters(feat[:NFFT], *fargs, FT_ROWS)
    h_far = _filters(feat[NFFT:], *fargs, CH)
    yc = _fft_conv(z, g)
    corr = _far_correction(h_far, z, tp)

    w_router_p = jnp.pad(w_router[l].astype(F32), ((0, 0), (0, ROUTER_W - N_EXPERTS)))
    w_router_p = _split_hi_lo(w_router_p)
    h1, xn2, aff = _merge(h0, og, x0c, z, yc, proj3, corr, hy_skip[l][None].astype(F32),
                          w_ret_out[l].astype(BF16), w_hy_out[l].astype(BF16), w_o[l].astype(BF16),
                          norm2_g[l][None].astype(F32), w_router_p, tp)

    idx, gate = _topk(aff, cap, slots)
    live = (jnp.arange(slots) < cap)[None, None, :]
    idx_local = jnp.where(live, idx, PAD)
    idx_flat = idx_local + (jnp.arange(b * N_EXPERTS, dtype=I32) // N_EXPERTS * tp)[:, None, None]
    ye = _moe_ffn(idx_flat, gate, xn2.reshape(b * tp, ROW_SUB, ROW_LANE),
                  w_exp_gate[l], w_exp_up[l], w_exp_down[l], slots)
    return _combine(idx_local, h1, ye, final_norm_g.astype(F32), cap, slots)
```

```python
import functools
import math

import numpy as np
import jax
import jax.numpy as jnp
from jax import lax
from jax.experimental import pallas as pl
from jax.experimental.pallas import tpu as pltpu

F32 = jnp.float32
BF16 = jnp.bfloat16
I32 = jnp.int32

D_MODEL = 1024
N_META = 16
RET_HEADS = 4
RET_QK_HEAD = 128
RET_V_HEAD = 256
ROPE_BASE = 10000.0
HY_EMB_DIM = 33
HY_FILTER_ORDER = 64
HY_FAST_DECAY_PCT = 0.3
HY_SLOW_DECAY_PCT = 1.5
HY_DECAY_TARGET = 1e-2
N_EXPERTS = 16
EC_CAPACITY = 2
D_FF = 2 * D_MODEL
RMS_EPS = 1e-6
IN_PROJ_W = 8192

CH = 128
PAD = CH - N_META
NFFT = 16384
ZCH = 80
FAR = 16

VMEM_LIMIT_BIG = 56 * 1024 * 1024
VMEM_LIMIT_MID = 40 * 1024 * 1024


def _cparams(sem, vmem=VMEM_LIMIT_MID):
    return pltpu.CompilerParams(dimension_semantics=sem, vmem_limit_bytes=vmem)


def _split_hi_lo(w):
    w = w.astype(F32)
    hi = w.astype(BF16)
    return jnp.stack([hi, (w - hi.astype(F32)).astype(BF16)])


def _dot3(a, w_hi, w_lo):
    a_hi = a.astype(BF16)
    a_lo = (a - a_hi.astype(F32)).astype(BF16)
    return (jnp.dot(a_hi, w_hi, preferred_element_type=F32) + jnp.dot(a_lo, w_hi, preferred_element_type=F32)
            + jnp.dot(a_hi, w_lo, preferred_element_type=F32))


IP_TM = 1280
IP_TN = 1024
IP_CN = 256


def _inproj_kernel(x_ref, g_ref, w_ref, cs_ref, sn_ref, o_ref, xn_ref):
    j = pl.program_id(1)

    @pl.when(j == 0)
    def _():
        x = x_ref[...]
        ms = jnp.mean(x * x, axis=-1, keepdims=True)
        xn_ref[...] = (x * lax.rsqrt(ms + RMS_EPS) * g_ref[...]).astype(BF16)

    def run(epilogue):
        for c in range(IP_TN // IP_CN):
            cols = slice(c * IP_CN, (c + 1) * IP_CN)
            acc = jnp.dot(xn_ref[...], w_ref[:, cols], preferred_element_type=F32)
            epilogue(c, cols, acc)

    def rotary(c, cols, acc):
        scale = 1.0 if c < (IP_TN // IP_CN) // 2 else RET_QK_HEAD ** -0.5
        cs = cs_ref[...] * scale
        sn = sn_ref[...] * scale
        for hh in range(IP_CN // RET_QK_HEAD):
            xh = acc[:, hh * RET_QK_HEAD:(hh + 1) * RET_QK_HEAD]
            rot = xh * cs + pltpu.roll(xh, RET_QK_HEAD // 2, axis=1) * sn
            lo = c * IP_CN + hh * RET_QK_HEAD
            o_ref[:, lo:lo + RET_QK_HEAD] = rot.astype(BF16)

    def raw(c, cols, acc):
        o_ref[:, cols] = acc.astype(BF16)

    def swish(c, cols, acc):
        o_ref[:, cols] = (acc * jax.nn.sigmoid(acc)).astype(BF16)

    def sigm(c, cols, acc):
        o_ref[:, cols] = jax.nn.sigmoid(acc).astype(BF16)

    pl.when(j == 0)(lambda: run(rotary))
    pl.when(jnp.logical_or(j == 1, jnp.logical_and(j >= 3, j < 6)))(lambda: run(raw))
    pl.when(j == 2)(lambda: run(swish))
    pl.when(j >= 6)(lambda: run(sigm))


def _in_proj(h0, g1, w_in_bf, cs, sn, tp):
    n_rows = h0.shape[0]
    return pl.pallas_call(
        _inproj_kernel,
        grid=(n_rows // IP_TM, IN_PROJ_W // IP_TN),
        in_specs=[
            pl.BlockSpec((IP_TM, D_MODEL), lambda i, j: (i, 0)),
            pl.BlockSpec((1, D_MODEL), lambda i, j: (0, 0)),
            pl.BlockSpec((D_MODEL, IP_TN), lambda i, j: (0, j)),
            pl.BlockSpec((IP_TM, RET_QK_HEAD), lambda i, j: (i, 0)),
            pl.BlockSpec((IP_TM, RET_QK_HEAD), lambda i, j: (i, 0)),
        ],
        out_specs=pl.BlockSpec((IP_TM, IP_TN), lambda i, j: (i, j)),
        out_shape=jax.ShapeDtypeStruct((n_rows, IN_PROJ_W), BF16),
        scratch_shapes=[pltpu.VMEM((IP_TM, D_MODEL), BF16)],
        compiler_params=_cparams(("arbitrary", "arbitrary")),
        name="in_proj",
    )(h0, g1, w_in_bf, cs, sn)


RET_UNROLL = 5


def _ret_kernel(lf_ref, lb_ref, q_ref, k_ref, v_ref, gr_ref, gn_ref, o_ref, ob_ref, s_ref, *, nch):
    h = pl.program_id(1)
    lf = lf_ref[h]
    lb = lb_ref[h]
    ri = lax.broadcasted_iota(I32, (CH, CH), 0).astype(F32)
    ci = lax.broadcasted_iota(I32, (CH, CH), 1).astype(F32)
    diff = ri - ci
    mask = jnp.exp(jnp.where(diff >= 0, lf * diff, -lb * diff))
    w_end = jnp.exp(lf * (CH - 1.0 - ri))
    w_start = jnp.exp(lb * ri)
    qw_f = jnp.exp(lf * (ri + 1.0))
    qw_b = jnp.exp(lb * (CH - ri))
    dec_f = jnp.exp(jnp.full((CH, RET_V_HEAD), lf * CH, F32))
    dec_b = jnp.exp(jnp.full((CH, RET_V_HEAD), lb * CH, F32))
    tn_dims = (((0,), (0,)), ((), ()))
    nt_dims = (((1,), (1,)), ((), ()))

    s_ref[...] = jnp.zeros_like(s_ref)

    def bwd(it, carry):
        s = s_ref[...]
        for u in range(RET_UNROLL):
            n = nch - 1 - (it * RET_UNROLL + u)
            r0 = pl.multiple_of(n * CH, CH)
            q = q_ref[0, pl.ds(r0, CH), :].astype(F32)
            k = k_ref[0, pl.ds(r0, CH), :].astype(F32)
            v = v_ref[0, pl.ds(r0, CH), :]
            ob_ref[pl.ds(r0, CH), :] = jnp.dot((q * qw_b).astype(BF16), s.astype(BF16),
                                               preferred_element_type=F32)
            a = lax.dot_general((k * w_start).astype(BF16), v, tn_dims, preferred_element_type=F32)
            s = s * dec_b + a
        s_ref[...] = s
        return carry

    lax.fori_loop(0, nch // RET_UNROLL, bwd, 0)

    s_ref[...] = jnp.zeros_like(s_ref)
    gn = gn_ref[...]

    def fwd(it, carry):
        s = s_ref[...]
        for u in range(RET_UNROLL):
            n = it * RET_UNROLL + u
            r0 = pl.multiple_of(n * CH, CH)
            qb = q_ref[0, pl.ds(r0, CH), :]
            kb = k_ref[0, pl.ds(r0, CH), :]
            v = v_ref[0, pl.ds(r0, CH), :]
            q = qb.astype(F32)
            k = kb.astype(F32)
            scores = lax.dot_general(qb, kb, nt_dims, preferred_element_type=F32) * mask
            o = jnp.dot(scores.astype(BF16), v, preferred_element_type=F32)
            o = o + jnp.dot((q * qw_f).astype(BF16), s.astype(BF16), preferred_element_type=F32)
            o = o + ob_ref[pl.ds(r0, CH), :]
            a = lax.dot_general((k * w_end).astype(BF16), v, tn_dims, preferred_element_type=F32)
            s = s * dec_f + a
            y = o * lax.rsqrt(jnp.mean(o * o, axis=-1, keepdims=True) + RMS_EPS) * gn
            o_ref[0, pl.ds(r0, CH), :] = (y * gr_ref[0, pl.ds(r0, CH), :].astype(F32)).astype(BF16)
        s_ref[...] = s
        return carry

    lax.fori_loop(0, nch // RET_UNROLL, fwd, 0)


def _retention(proj3, lf, lb, gn, tp):
    b = proj3.shape[0]
    nch = tp // CH
    qk_blocks = (RET_HEADS * RET_QK_HEAD) // RET_QK_HEAD
    v_blk0 = (2 * RET_HEADS * RET_QK_HEAD) // RET_V_HEAD
    g_blk0 = v_blk0 + RET_HEADS
    smem = pl.BlockSpec(memory_space=pltpu.SMEM)
    return pl.pallas_call(
        functools.partial(_ret_kernel, nch=nch),
        grid=(b, RET_HEADS),
        in_specs=[
            smem, smem,
            pl.BlockSpec((1, tp, RET_QK_HEAD), lambda bi, h: (bi, 0, h)),
            pl.BlockSpec((1, tp, RET_QK_HEAD), lambda bi, h: (bi, 0, qk_blocks + h)),
            pl.BlockSpec((1, tp, RET_V_HEAD), lambda bi, h: (bi, 0, v_blk0 + h)),
            pl.BlockSpec((1, tp, RET_V_HEAD), lambda bi, h: (bi, 0, g_blk0 + h)),
            pl.BlockSpec((1, RET_V_HEAD), lambda bi, h: (0, h)),
        ],
        out_specs=pl.BlockSpec((1, tp, RET_V_HEAD), lambda bi, h: (bi, 0, h)),
        out_shape=jax.ShapeDtypeStruct((b, tp, RET_HEADS * RET_V_HEAD), BF16),
        scratch_shapes=[pltpu.VMEM((tp, RET_V_HEAD), F32), pltpu.VMEM((RET_QK_HEAD, RET_V_HEAD), F32)],
        compiler_params=_cparams(("arbitrary", "arbitrary"), VMEM_LIMIT_BIG),
        name="retention",
    )(lf, lb, proj3, proj3, proj3, proj3, gn)


HP_CW = 128


def _hyprep_kernel(u0_ref, u1_ref, u2_ref, w0_ref, w1_ref, w2_ref, b0_ref, b1_ref, b2_ref,
                   x0_ref, z_ref, zp_ref, *, nch):
    rows = lax.broadcasted_iota(I32, (CH, HP_CW), 0)
    halo = 16
    zp_ref[...] = jnp.zeros_like(zp_ref)

    def conv(u_ref, w_ref, b_ref, n, r0):
        cur = u_ref[0, pl.ds(r0, CH), :].astype(F32)
        rp = pl.multiple_of(jnp.maximum(r0 - halo, 0), halo)
        rn = pl.multiple_of(jnp.minimum(r0 + CH, (nch - 1) * CH), halo)
        prev = u_ref[0, pl.ds(rp, halo), :].astype(F32)[halo - 1:halo, :]
        nxt = u_ref[0, pl.ds(rn, halo), :].astype(F32)[0:1, :]
        prev = jnp.where(n > 0, prev, 0.0)
        nxt = jnp.where(n < nch - 1, nxt, 0.0)
        up = jnp.where(rows == 0, prev, pltpu.roll(cur, 1, axis=0))
        dn = jnp.where(rows == CH - 1, nxt, pltpu.roll(cur, CH - 1, axis=0))
        w = w_ref[...]
        return up * w[0:1, :] + cur * w[1:2, :] + dn * w[2:3, :] + b_ref[...]

    def body(n, carry):
        r0 = pl.multiple_of(n * CH, CH)
        x0 = conv(u0_ref, w0_ref, b0_ref, n, r0)
        x1 = conv(u1_ref, w1_ref, b1_ref, n, r0)
        vv = conv(u2_ref, w2_ref, b2_ref, n, r0)
        z = jnp.where(rows + r0 >= PAD, x1 * vv, 0.0)
        x0_ref[0, pl.ds(r0, CH), :] = x0.astype(BF16)
        z_ref[0, pl.ds(r0, CH), :] = z.astype(BF16)
        zp_ref[0, pl.ds(n, CH, stride=ZCH), :] = z
        return carry

    lax.fori_loop(0, nch, body, 0)


def _hy_prep(proj3, conv_w, conv_b, tp):
    b = proj3.shape[0]
    nch = tp // CH
    ncb = D_MODEL // HP_CW
    u_blk0 = 3072 // HP_CW
    uspec = lambda s: pl.BlockSpec((1, tp, HP_CW), lambda bi, c: (bi, 0, u_blk0 + s * ncb + c))
    wspec = lambda s: pl.BlockSpec((3, HP_CW), lambda bi, c: (0, s * ncb + c))
    bspec = lambda s: pl.BlockSpec((1, HP_CW), lambda bi, c: (0, s * ncb + c))
    return pl.pallas_call(
        functools.partial(_hyprep_kernel, nch=nch),
        grid=(b, ncb),
        in_specs=[uspec(0), uspec(1), uspec(2), wspec(0), wspec(1), wspec(2), bspec(0), bspec(1), bspec(2)],
        out_specs=[pl.BlockSpec((1, tp, HP_CW), lambda bi, c: (bi, 0, c)),
                   pl.BlockSpec((1, tp, HP_CW), lambda bi, c: (bi, 0, c)),
                   pl.BlockSpec((1, CH * ZCH, HP_CW), lambda bi, c: (bi, 0, c))],
        out_shape=[jax.ShapeDtypeStruct((b, tp, D_MODEL), BF16),
                   jax.ShapeDtypeStruct((b, tp, D_MODEL), BF16),
                   jax.ShapeDtypeStruct((b, CH * ZCH, D_MODEL), F32)],
        compiler_params=_cparams(("arbitrary", "arbitrary")),
        name="hy_prep",
    )(proj3, proj3, proj3, conv_w, conv_w, conv_w, conv_b, conv_b, conv_b)


FEAT_W = 128


FT_ROWS = 512
HALF = CH // 2


def _filter_kernel(feat_ref, w1_ref, b1_ref, w2_ref, b2_ref, w3_ref, b3_ref, fq_ref, w4_ref, dl_ref, o_ref,
                   *, groups):
    feat = feat_ref[...]
    fq = fq_ref[...]
    hdn = jnp.sin(fq * (_dot3(feat, w1_ref[0], w1_ref[1]) + b1_ref[...]))
    hdn = jnp.sin(fq * (_dot3(hdn, w2_ref[0], w2_ref[1]) + b2_ref[...]))
    hdn = jnp.sin(fq * (_dot3(hdn, w3_ref[0], w3_ref[1]) + b3_ref[...]))
    scale = jnp.exp(-feat[:, 0:1] * dl_ref[...]) * feat[:, HY_EMB_DIM:HY_EMB_DIM + 1]
    for d in range(2):
        rows = [slice(g * CH + d * HALF, g * CH + (d + 1) * HALF) for g in range(groups)]
        hd = jnp.concatenate([hdn[r] for r in rows], axis=0)
        cols = slice(d * D_MODEL, (d + 1) * D_MODEL)
        filt = _dot3(hd, w4_ref[0, :, cols], w4_ref[1, :, cols])
        for g, r in enumerate(rows):
            o_ref[r, :] = filt[g * HALF:(g + 1) * HALF] * scale[r]


def _filters(feat, w1p, b1, w2, b2, w3, b3, fq, w4, dl, rows):
    n_rows = feat.shape[0]
    full = lambda a: pl.BlockSpec(a.shape, lambda i: (0,) * a.ndim)
    return pl.pallas_call(
        functools.partial(_filter_kernel, groups=rows // CH),
        grid=(n_rows // rows,),
        in_specs=[pl.BlockSpec((rows, FEAT_W), lambda i: (i, 0)),
                  full(w1p), full(b1), full(w2), full(b2), full(w3), full(b3), full(fq), full(w4), full(dl)],
        out_specs=pl.BlockSpec((rows, D_MODEL), lambda i: (i, 0)),
        out_shape=jax.ShapeDtypeStruct((n_rows, D_MODEL), F32),
        compiler_params=_cparams(("arbitrary",)),
        name="hy_filter",
    )(feat, w1p, b1, w2, b2, w3, b3, fq, w4, dl)


FFT_G = 8


def _lanes(j, c):
    return slice(j * c, (j + 1) * c)


def _fft_a_data_kernel(m_ref, z_ref, o_ref):
    for j in range(FFT_G):
        rows = slice(j * ZCH, (j + 1) * ZCH)
        x = jnp.concatenate([z_ref[0, rows, :], z_ref[1, rows, :]], axis=0).astype(BF16)
        o_ref[j] = jnp.dot(m_ref[j], x, preferred_element_type=F32).astype(BF16)


def _fft_a_filt_kernel(m_ref, g_ref, o_ref):
    for j in range(FFT_G):
        gj = g_ref[j * CH:(j + 1) * CH, :].astype(BF16)
        o_ref[j] = jnp.dot(m_ref[j], gj, preferred_element_type=F32).astype(BF16)


def _fft_b_kernel(f_ref, fi_ref, yr_ref, yi_ref, gr_ref, gi_ref, o_ref):
    c = o_ref.shape[-1]
    for j in range(FFT_G):
        y = jnp.concatenate([yr_ref[:, _lanes(j, c)], yi_ref[:, _lanes(j, c)]], axis=0)
        yg = jnp.concatenate([gr_ref[:, _lanes(j, c)], gi_ref[:, _lanes(j, c)]], axis=0)
        x = jnp.dot(f_ref[...], y, preferred_element_type=F32)
        g = jnp.dot(f_ref[...], yg, preferred_element_type=F32) * (1.0 / NFFT)
        xr, xi = x[:CH], x[CH:]
        gr, gi = g[:CH], g[CH:]
        p = jnp.concatenate([xr * gr - xi * gi, xr * gi + xi * gr], axis=0).astype(BF16)
        o_ref[j] = jnp.dot(fi_ref[...], p, preferred_element_type=F32).astype(BF16)


def _fft_a_inv_kernel(m_ref, ur_ref, ui_ref, o_ref):
    c = ur_ref.shape[-1] // FFT_G
    for j in range(FFT_G):
        u = jnp.concatenate([ur_ref[:, _lanes(j, c)], ui_ref[:, _lanes(j, c)]], axis=0)
        y = jnp.dot(m_ref[j], u, preferred_element_type=F32).astype(BF16)
        o_ref[0, :, _lanes(j, c)] = y[:ZCH]
        o_ref[1, :, _lanes(j, c)] = y[ZCH:]


def _dft_tables():
    n2 = np.arange(CH)[:, None, None]
    k1 = np.arange(CH)[None, :, None]

    def theta(n1_count):
        n1 = np.arange(n1_count)[None, None, :]
        return 2.0 * np.pi * ((k1 * (CH * n1 + n2)) % NFFT) / NFFT

    th = theta(ZCH)
    c, s = np.cos(th), np.sin(th)
    m_a = np.concatenate([np.concatenate([c, s], axis=2), np.concatenate([-s, c], axis=2)], axis=1)
    m_ainv = np.transpose(m_a, (0, 2, 1))
    th = theta(CH)
    m_af = np.concatenate([np.cos(th), -np.sin(th)], axis=1)
    a = 2.0 * np.pi * ((np.arange(CH)[:, None] * np.arange(CH)[None, :]) % CH) / CH
    c, s = np.cos(a), np.sin(a)
    f2 = np.block([[c, s], [-s, c]])
    f2i = np.block([[c, -s], [s, c]])
    f = lambda t: jnp.asarray(t.astype(np.float32)).astype(BF16)
    return f(m_a), f(m_ainv), f(m_af), f(f2), f(f2i)


def _fft_conv(zp, g):
    c = zp.shape[-1]
    m_a, m_ainv, m_af, f2, f2i = _dft_tables()
    cp = _cparams(("arbitrary",), VMEM_LIMIT_BIG)
    steps = CH // FFT_G
    gc = FFT_G * c
    full2 = pl.BlockSpec((2 * CH, 2 * CH), lambda i: (0, 0))
    col_re = pl.BlockSpec((CH, gc), lambda i: (0, i))
    col_im = pl.BlockSpec((CH, gc), lambda i: (0, steps + i))
    blk = pl.BlockSpec((FFT_G, 2 * CH, c), lambda i: (i, 0, 0))
    spec_shape = jax.ShapeDtypeStruct((CH, 2 * CH, c), BF16)

    yg = pl.pallas_call(
        _fft_a_filt_kernel, grid=(steps,),
        in_specs=[pl.BlockSpec((FFT_G, 2 * CH, CH), lambda i: (i, 0, 0)),
                  pl.BlockSpec((FFT_G * CH, c), lambda i: (i, 0))],
        out_specs=blk, out_shape=spec_shape, compiler_params=cp, name="fft_a_filt",
    )(m_af, g)
    yg2 = yg.reshape(CH, 2 * CH * c)

    y = pl.pallas_call(
        _fft_a_data_kernel, grid=(steps,),
        in_specs=[pl.BlockSpec((FFT_G, 2 * CH, 2 * ZCH), lambda i: (i, 0, 0)),
                  pl.BlockSpec((2, FFT_G * ZCH, c), lambda i: (0, i, 0))],
        out_specs=blk, out_shape=spec_shape, compiler_params=cp, name="fft_a_data",
    )(m_a, zp)
    y2 = y.reshape(CH, 2 * CH * c)
    u = pl.pallas_call(
        _fft_b_kernel, grid=(steps,),
        in_specs=[full2, full2, col_re, col_im, col_re, col_im],
        out_specs=blk, out_shape=spec_shape, compiler_params=cp, name="fft_b",
    )(f2, f2i, y2, y2, yg2, yg2)
    u2 = u.reshape(CH, 2 * CH * c)
    yc = pl.pallas_call(
        _fft_a_inv_kernel, grid=(steps,),
        in_specs=[pl.BlockSpec((FFT_G, 2 * ZCH, 2 * CH), lambda i: (i, 0, 0)), col_re, col_im],
        out_specs=pl.BlockSpec((2, ZCH, gc), lambda i: (0, 0, i)),
        out_shape=jax.ShapeDtypeStruct((2, ZCH, CH * c), BF16),
        compiler_params=cp, name="fft_a_inv",
    )(m_ainv, u2, u2)
    return yc.reshape(2, ZCH * CH, c)


def _far_kernel(far_ref, zm_ref, zl_ref, o_ref):
    g_hi = far_ref[HALF + FAR:HALF + 2 * FAR, :]
    ef = far_ref[0:FAR, :] - g_hi
    g_lo = far_ref[FAR:2 * FAR, :]
    hb = far_ref[HALF:HALF + FAR, :]
    zm = zm_ref[0, PAD:CH, :].astype(F32)
    zl = zl_ref[0, PAD:CH, :].astype(F32)
    row = lambda a, i: a[i:i + 1, :]
    eb = [row(hb, 0) - row(g_hi, 0)] + [row(hb, m) - row(g_lo, FAR - m) for m in range(1, FAR)]
    for j in range(FAR):
        acc_f = row(ef, j) * row(zm, 0)
        for i in range(1, j + 1):
            acc_f = acc_f + row(ef, j - i) * row(zm, i)
        o_ref[0, 0, j:j + 1, :] = acc_f
        acc_b = eb[0] * row(zl, j)
        for i in range(j + 1, FAR):
            acc_b = acc_b + eb[i - j] * row(zl, i)
        o_ref[0, 1, j:j + 1, :] = acc_b


def _far_correction(h_far, z, tp):
    b, _, c = z.shape
    last = tp // CH - 1
    return pl.pallas_call(
        _far_kernel, grid=(b,),
        in_specs=[pl.BlockSpec((CH, c), lambda bi: (0, 0)),
                  pl.BlockSpec((1, CH, c), lambda bi: (bi, 0, 0)),
                  pl.BlockSpec((1, CH, c), lambda bi: (bi, last, 0))],
        out_specs=pl.BlockSpec((1, 2, FAR, c), lambda bi: (bi, 0, 0, 0)),
        out_shape=jax.ShapeDtypeStruct((b, 2, FAR, c), F32),
        compiler_params=_cparams(("arbitrary",)),
        name="hy_far",
    )(h_far, z, z)


MG_TM = 640
MG_CN = 256
ROUTER_W = 128
AFF_PACK = ROUTER_W // N_EXPERTS
ROW_SUB, ROW_LANE = 8, 128


def _merge_kernel(h_ref, og_ref, x0_ref, z_ref, yc_ref, ga_ref, gb_ref, corr_ref, skip_ref,
                  wr_ref, wh_ref, wo_ref, g2_ref, wrt_ref,
                  h1_ref, xn_ref, aff_ref, affp_ref, pre_ref, mix_ref, h1s_ref, *, nt_b):
    i = pl.program_id(0)
    ib = i % nt_b
    pre_ref[...] = yc_ref[0].astype(F32) + z_ref[0].astype(F32) * skip_ref[...]

    @pl.when(ib == 0)
    def _():
        pre_ref[PAD:CH, :] += corr_ref[0, 1]

    @pl.when(ib == nt_b - 1)
    def _():
        pre_ref[MG_TM - FAR:MG_TM, :] += corr_ref[0, 0]

    chunks = [slice(c * MG_CN, (c + 1) * MG_CN) for c in range(D_MODEL // MG_CN)]
    og = og_ref[0]
    pre = (x0_ref[0].astype(F32) * pre_ref[...]).astype(BF16)
    for cols in chunks:
        ya = jnp.dot(og, wr_ref[:, cols], preferred_element_type=F32)
        yb = jnp.dot(pre, wh_ref[:, cols], preferred_element_type=F32)
        mix_ref[:, cols] = (ga_ref[0, :, cols].astype(F32) * ya + gb_ref[0, :, cols].astype(F32) * yb).astype(BF16)
    mixed = mix_ref[...]
    ss = jnp.zeros((MG_TM, 1), F32)
    for cols in chunks:
        h1 = h_ref[0, :, cols] + jnp.dot(mixed, wo_ref[:, cols], preferred_element_type=F32)
        h1s_ref[:, cols] = h1
        ss = ss + jnp.sum(h1 * h1, axis=-1, keepdims=True)
    rinv = lax.rsqrt(ss * (1.0 / D_MODEL) + RMS_EPS)
    h1 = h1s_ref[...]
    h1_ref[0] = h1.reshape(MG_TM, ROW_SUB, ROW_LANE)
    xn = h1 * rinv * g2_ref[...]
    xn_ref[0] = xn.reshape(MG_TM, ROW_SUB, ROW_LANE)
    logits = _dot3(xn, wrt_ref[0], wrt_ref[1])
    lane = lax.broadcasted_iota(I32, logits.shape, 1)
    logits = jnp.where(lane < N_EXPERTS, logits, -jnp.inf)
    m = jnp.max(logits, axis=-1, keepdims=True)
    e = jnp.exp(logits - m)
    aff = e / jnp.sum(e, axis=-1, keepdims=True)
    rows = lax.broadcasted_iota(I32, logits.shape, 0) + ib * MG_TM
    aff_ref[0] = jnp.where(rows >= PAD, aff, -1.0)
    lane_grp = lax.broadcasted_iota(I32, (MG_TM // AFF_PACK, ROUTER_W), 1) // N_EXPERTS
    packed = jnp.zeros((MG_TM // AFF_PACK, ROUTER_W), F32)
    for g in range(AFF_PACK):
        blk = aff_ref[0, pl.ds(g, MG_TM // AFF_PACK, stride=AFF_PACK), :]
        if g:
            blk = pltpu.roll(blk, N_EXPERTS * g, axis=1)
        packed = jnp.where(lane_grp == g, blk, packed)
    affp_ref[0] = packed


def _merge(h0, og, x0c, z, yc, proj3, corr, skip, w_ret, w_hy, w_o, g2, w_router_p, tp):
    b = h0.shape[0]
    nt_b = tp // MG_TM
    ga_blk0 = 6144 // D_MODEL
    row = lambda w: pl.BlockSpec((1, MG_TM, w), lambda i: (i // nt_b, i % nt_b, 0))
    full = lambda a: pl.BlockSpec(a.shape, lambda i: (0,) * a.ndim)
    return pl.pallas_call(
        functools.partial(_merge_kernel, nt_b=nt_b),
        grid=(b * nt_b,),
        in_specs=[row(D_MODEL), row(D_MODEL), row(D_MODEL), row(D_MODEL), row(D_MODEL),
                  pl.BlockSpec((1, MG_TM, D_MODEL), lambda i: (i // nt_b, i % nt_b, ga_blk0)),
                  pl.BlockSpec((1, MG_TM, D_MODEL), lambda i: (i // nt_b, i % nt_b, ga_blk0 + 1)),
                  pl.BlockSpec((1, 2, FAR, D_MODEL), lambda i: (i // nt_b, 0, 0, 0)),
                  full(skip), full(w_ret), full(w_hy), full(w_o), full(g2), full(w_router_p)],
        out_specs=[pl.BlockSpec((1, MG_TM, ROW_SUB, ROW_LANE), lambda i: (i // nt_b, i % nt_b, 0, 0)),
                   pl.BlockSpec((1, MG_TM, ROW_SUB, ROW_LANE), lambda i: (i // nt_b, i % nt_b, 0, 0)),
                   row(ROUTER_W),
                   pl.BlockSpec((1, MG_TM // AFF_PACK, ROUTER_W), lambda i: (i // nt_b, i % nt_b, 0))],
        out_shape=[jax.ShapeDtypeStruct((b, tp, ROW_SUB, ROW_LANE), F32),
                   jax.ShapeDtypeStruct((b, tp, ROW_SUB, ROW_LANE), F32),
                   jax.ShapeDtypeStruct((b, tp, ROUTER_W), F32),
                   jax.ShapeDtypeStruct((b, tp // AFF_PACK, ROUTER_W), F32)],
        scratch_shapes=[pltpu.VMEM((MG_TM, D_MODEL), F32), pltpu.VMEM((MG_TM, D_MODEL), BF16),
                        pltpu.VMEM((MG_TM, D_MODEL), F32)],
        compiler_params=_cparams(("arbitrary",), VMEM_LIMIT_BIG),
        name="merge",
    )(h0, og, x0c, z, yc, proj3, proj3, corr, skip, w_ret, w_hy, w_o, g2, w_router_p)


TK_W = 128


TK_UNROLL = 5


def _select_kernel(aff_ref, affp_ref, low_ref, slot_ref, offs_ref, *, cap, nch):
    def chunk(c):
        r0 = pl.multiple_of(c * CH, CH)
        return aff_ref[0, pl.ds(r0, CH), :]

    def count(pred):
        cnt = jnp.sum(pred(affp_ref[0]).astype(I32), axis=0, keepdims=True)
        for shift in (N_EXPERTS, 2 * N_EXPERTS, 4 * N_EXPERTS):
            cnt = cnt + pltpu.roll(cnt, shift, axis=1)
        return cnt

    def search(it, bits):
        cand = bits | jnp.left_shift(1, 29 - it)
        cand_f = pltpu.bitcast(cand, F32)
        return jnp.where(count(lambda a: a >= cand_f) >= cap, cand, bits)

    thr = pltpu.bitcast(lax.fori_loop(0, 30, search, jnp.zeros((1, ROUTER_W), I32)), F32)
    need = (cap - count(lambda a: a > thr)).astype(F32)
    low = low_ref[...]

    def scan(it, carry):
        c_eq, c_sel = carry
        for u in range(TK_UNROLL):
            c = it * TK_UNROLL + u
            a = chunk(c)
            eq = a == thr
            eq_f = eq.astype(F32)
            eq_rank = jnp.dot(low, eq_f, preferred_element_type=F32) + c_eq
            sel = jnp.logical_or(a > thr, jnp.logical_and(eq, eq_rank < need))
            sel_f = sel.astype(F32)
            slot = jnp.dot(low, sel_f, preferred_element_type=F32) + c_sel
            r0 = pl.multiple_of(c * CH, CH)
            slot_ref[0, pl.ds(r0, CH), :] = jnp.where(sel, slot, -1.0).astype(I32)
            offs_ref[0, c] = c_sel.astype(I32)
            c_eq = c_eq + jnp.sum(eq_f, axis=0, keepdims=True)
            c_sel = c_sel + jnp.sum(sel_f, axis=0, keepdims=True)
        return c_eq, c_sel

    zero = jnp.zeros((1, ROUTER_W), F32)
    lax.fori_loop(0, nch // TK_UNROLL, scan, (zero, zero))


def _extract_kernel(offs_ref, slot_ref, aff_ref, idx_ref, gate_ref, *, nch, nwin):
    idx_ref[...] = jnp.zeros_like(idx_ref)
    gate_ref[...] = jnp.zeros_like(gate_ref)
    lane = lax.broadcasted_iota(I32, (CH, TK_W), 1)
    trow = lax.broadcasted_iota(I32, (CH, TK_W), 0)

    def per_chunk(c, carry):
        r0 = pl.multiple_of(c * CH, CH)
        slots = slot_ref[0, pl.ds(r0, CH), :]
        affs = aff_ref[0, pl.ds(r0, CH), :]
        tpos = (trow + r0).astype(F32)
        for e in range(N_EXPERTS):
            col = jnp.broadcast_to(slots[:, e:e + 1], (CH, TK_W))
            gcol = jnp.broadcast_to(affs[:, e:e + 1], (CH, TK_W))
            w0 = offs_ref[0, 0, c * N_EXPERTS + e] // TK_W
            for dw in range(2):
                w = w0 + dw
                base = jnp.where(w < nwin, w * TK_W, -2 * TK_W)
                hit = col == lane + base
                row = e * nwin + jnp.minimum(w, nwin - 1)
                idx_ref[row] += jnp.sum(jnp.where(hit, tpos, 0.0), axis=0, keepdims=True)
                gate_ref[row] += jnp.sum(jnp.where(hit, gcol, 0.0), axis=0, keepdims=True)
        return carry

    lax.fori_loop(0, nch, per_chunk, 0)


def _topk(aff, aff_packed, cap, slots):
    b, tp, _ = aff.shape
    nch = tp // CH
    nwin = -(-slots // TK_W)
    low = jnp.asarray(np.tril(np.ones((CH, CH), np.float32), k=-1))
    pack = AFF_PACK
    assert nch % TK_UNROLL == 0 and tp % pack == 0
    slot, offs = pl.pallas_call(
        functools.partial(_select_kernel, cap=cap, nch=nch),
        grid=(b,),
        in_specs=[pl.BlockSpec((1, tp, ROUTER_W), lambda bi: (bi, 0, 0)),
                  pl.BlockSpec((1, tp // pack, ROUTER_W), lambda bi: (bi, 0, 0)),
                  pl.BlockSpec((CH, CH), lambda bi: (0, 0))],
        out_specs=[pl.BlockSpec((1, tp, ROUTER_W), lambda bi: (bi, 0, 0)),
                   pl.BlockSpec((1, nch, 1, ROUTER_W), lambda bi: (bi, 0, 0, 0))],
        out_shape=[jax.ShapeDtypeStruct((b, tp, ROUTER_W), I32),
                   jax.ShapeDtypeStruct((b, nch, 1, ROUTER_W), I32)],
        compiler_params=_cparams(("arbitrary",)),
        name="topk_select",
    )(aff, aff_packed, low)
    offs_s = offs[:, :, 0, :N_EXPERTS].reshape(b, 1, nch * N_EXPERTS)
    rows = pl.BlockSpec((None, N_EXPERTS * nwin, 1, TK_W), lambda bi: (bi, 0, 0, 0))
    out = jax.ShapeDtypeStruct((b, N_EXPERTS * nwin, 1, TK_W), F32)
    idx, gate = pl.pallas_call(
        functools.partial(_extract_kernel, nch=nch, nwin=nwin),
        grid=(b,),
        in_specs=[pl.BlockSpec((1, 1, nch * N_EXPERTS), lambda bi: (bi, 0, 0), memory_space=pltpu.SMEM),
                  pl.BlockSpec((1, tp, ROUTER_W), lambda bi: (bi, 0, 0)),
                  pl.BlockSpec((1, tp, ROUTER_W), lambda bi: (bi, 0, 0))],
        out_specs=[rows, rows],
        out_shape=[out, out],
        compiler_params=_cparams(("arbitrary",)),
        name="topk_extract",
    )(offs_s, slot, aff)
    idx = idx.reshape(b * N_EXPERTS, 1, nwin * TK_W)[:, :, :slots].astype(I32)
    gate = gate.reshape(b * N_EXPERTS, nwin * TK_W, 1)[:, :slots]
    return idx, gate


FF_TF = 1024
MOE_CN = 256


MOE_UNROLL = 4


def _moe_kernel(idx_ref, idx_next_ref, xn_hbm, gate_ref, wg_ref, wu_ref, wd_ref, o_ref,
                xe32_ref, xe_ref, hid_ref, acc_ref, sem,
                *, slots, nf):
    i = pl.program_id(0)
    f = pl.program_id(1)
    buf = i % 2
    share = slots // nf

    def row_copy(ids_ref, s, b):
        return pltpu.make_async_copy(xn_hbm.at[ids_ref[0, 0, s]], xe32_ref.at[b, s], sem.at[b])

    def for_rows(lo, n, fn):
        def body(k, carry):
            for u in range(MOE_UNROLL):
                fn(lo + k * MOE_UNROLL + u)
            return carry

        lax.fori_loop(0, n // MOE_UNROLL, body, 0)

    @pl.when(jnp.logical_and(i == 0, f == 0))
    def _():
        for_rows(0, slots, lambda s: row_copy(idx_ref, s, 0).start())

    @pl.when(f == 0)
    def _():
        for_rows(0, slots, lambda s: row_copy(idx_ref, s, buf).wait())
        xe_ref[...] = xe32_ref[buf].reshape(slots, D_MODEL).astype(BF16)

    @pl.when(f == 0)
    def _():
        acc_ref[...] = jnp.zeros_like(acc_ref)

    n_up, n_down = FF_TF // MOE_CN, D_MODEL // MOE_CN
    per_chunk = share // (n_up + n_down)

    def prefetch(chunk):
        base = f * share + chunk * per_chunk
        for u in range(per_chunk):
            row_copy(idx_next_ref, base + u, 1 - buf).start()

    xe = xe_ref[...]
    for c in range(n_up):
        cols = slice(c * MOE_CN, (c + 1) * MOE_CN)
        gg = jnp.dot(xe, wg_ref[0, :, cols].astype(BF16), preferred_element_type=F32)
        uu = jnp.dot(xe, wu_ref[0, :, cols].astype(BF16), preferred_element_type=F32)
        hid_ref[:, cols] = (gg * jax.nn.sigmoid(gg) * uu).astype(BF16)
        prefetch(c)
    scale = jnp.where(f == pl.num_programs(1) - 1, gate_ref[0], 1.0)
    hid = hid_ref[...]
    for c in range(n_down):
        cols = slice(c * MOE_CN, (c + 1) * MOE_CN)
        part = jnp.dot(hid, wd_ref[0, :, cols].astype(BF16), preferred_element_type=F32)
        acc_ref[:, cols] = (acc_ref[:, cols] + part) * scale
        prefetch(n_up + c)

    @pl.when(f == pl.num_programs(1) - 1)
    def _():
        o_ref[0] = acc_ref[...].reshape(slots, ROW_SUB, ROW_LANE)

    @pl.when(jnp.logical_and(i == pl.num_programs(0) - 1, f == pl.num_programs(1) - 1))
    def _():
        for_rows(0, slots, lambda s: row_copy(idx_next_ref, s, 1 - buf).wait())


def _moe_ffn(idx, gate, xn_flat, w_gate, w_up, w_down, slots):
    be = idx.shape[0]
    nf = D_FF // FF_TF
    assert slots % MOE_UNROLL == 0 and slots % (nf * (FF_TF // MOE_CN + D_MODEL // MOE_CN)) == 0
    return pl.pallas_call(
        functools.partial(_moe_kernel, slots=slots, nf=nf),
        grid=(be, nf),
        in_specs=[pl.BlockSpec((1, 1, slots), lambda i, f: (i, 0, 0), memory_space=pltpu.SMEM),
                  pl.BlockSpec((1, 1, slots), lambda i, f: (jnp.minimum(i + 1, be - 1), 0, 0),
                               memory_space=pltpu.SMEM),
                  pl.BlockSpec(memory_space=pl.ANY),
                  pl.BlockSpec((1, slots, 1), lambda i, f: (i, 0, 0)),
                  pl.BlockSpec((1, D_MODEL, FF_TF), lambda i, f: (i % N_EXPERTS, 0, f)),
                  pl.BlockSpec((1, D_MODEL, FF_TF), lambda i, f: (i % N_EXPERTS, 0, f)),
                  pl.BlockSpec((1, FF_TF, D_MODEL), lambda i, f: (i % N_EXPERTS, f, 0))],
        out_specs=pl.BlockSpec((1, slots, ROW_SUB, ROW_LANE), lambda i, f: (i, 0, 0, 0)),
        out_shape=jax.ShapeDtypeStruct((be, slots, ROW_SUB, ROW_LANE), F32),
        scratch_shapes=[pltpu.VMEM((2, slots, ROW_SUB, ROW_LANE), F32), pltpu.VMEM((slots, D_MODEL), BF16),
                        pltpu.VMEM((slots, FF_TF), BF16), pltpu.VMEM((slots, D_MODEL), F32),
                        pltpu.SemaphoreType.DMA((2,))],
        compiler_params=_cparams(("arbitrary", "arbitrary"), VMEM_LIMIT_BIG),
        name="moe_ffn",
    )(idx, idx, xn_flat, gate, w_gate, w_up, w_down)


CB_U = 6
CB_NB = 512


def _combine_kernel(idx_ref, h1_hbm, ye_ref, g_ref, o_hbm, acc_ref, stage_ref, sem, osem, *, cap, tp):
    b = pl.program_id(0)
    e = pl.program_id(1)

    @pl.when(e == 0)
    def _():
        cp = pltpu.make_async_copy(h1_hbm.at[b], acc_ref, sem)
        cp.start()
        cp.wait()

    def rmw(g, carry):
        ts = [idx_ref[0, 0, g * CB_U + u] for u in range(CB_U)]
        vals = [acc_ref[ts[u]] + ye_ref[g * CB_U + u] for u in range(CB_U)]
        for u in range(CB_U):
            acc_ref[ts[u]] = vals[u]
        return carry

    lax.fori_loop(0, cap // CB_U, rmw, 0)

    @pl.when(e == pl.num_programs(1) - 1)
    def _():
        gamma = g_ref[...]
        n_blocks = (tp - CH) // CB_NB

        def out_copy(k):
            return pltpu.make_async_copy(stage_ref.at[k % 2], o_hbm.at[b, pl.ds(k * CB_NB, CB_NB), :],
                                         osem.at[k % 2])

        for k in range(n_blocks):
            x = acc_ref[pl.ds(CH + k * CB_NB, CB_NB)]
            ms = jnp.sum(jnp.sum(x * x, axis=2, keepdims=True), axis=1, keepdims=True) * (1.0 / D_MODEL)
            y = x * lax.rsqrt(ms + RMS_EPS) * gamma
            if k >= 2:
                out_copy(k - 2).wait()
            stage_ref[k % 2] = y.reshape(CB_NB, D_MODEL)
            out_copy(k).start()
        for k in range(max(n_blocks - 2, 0), n_blocks):
            out_copy(k).wait()


def _combine(idx_local, h1, ye, gf, cap, slots):
    b, tp = h1.shape[:2]
    assert cap % CB_U == 0 and (tp - CH) % CB_NB == 0
    return pl.pallas_call(
        functools.partial(_combine_kernel, cap=cap, tp=tp),
        grid=(b, N_EXPERTS),
        in_specs=[pl.BlockSpec((1, 1, slots), lambda bi, e: (bi * N_EXPERTS + e, 0, 0), memory_space=pltpu.SMEM),
                  pl.BlockSpec(memory_space=pl.ANY),
                  pl.BlockSpec((None, slots, ROW_SUB, ROW_LANE), lambda bi, e: (bi * N_EXPERTS + e, 0, 0, 0)),
                  pl.BlockSpec((1, ROW_SUB, ROW_LANE), lambda bi, e: (0, 0, 0))],
        out_specs=pl.BlockSpec(memory_space=pl.ANY),
        out_shape=jax.ShapeDtypeStruct((b, tp - CH, D_MODEL), F32),
        scratch_shapes=[pltpu.VMEM((tp, ROW_SUB, ROW_LANE), F32), pltpu.VMEM((2, CB_NB, D_MODEL), F32),
                        pltpu.SemaphoreType.DMA(()), pltpu.SemaphoreType.DMA((2,))],
        compiler_params=_cparams(("arbitrary", "arbitrary"), VMEM_LIMIT_BIG),
        name="combine",
    )(idx_local, h1, ye, gf.reshape(1, ROW_SUB, ROW_LANE))


def _rope_tables(tp):
    half = RET_QK_HEAD // 2
    pos = jnp.arange(tp, dtype=F32) - float(PAD)
    inv = ROPE_BASE ** (-jnp.arange(half, dtype=F32) / half)
    ang = pos[:, None] * inv[None, :]
    cos, sin = jnp.cos(ang), jnp.sin(ang)
    return jnp.concatenate([cos, cos], axis=1), jnp.concatenate([-sin, sin], axis=1)


def _filter_features(t_len):
    half = NFFT // 2
    q = np.arange(NFFT)
    r = CH * (q % CH) + q // CH
    p_main = np.where(r < half, r, NFFT - r)
    valid_main = (r != half).astype(np.float32)
    m = np.arange(FAR)
    pad = np.zeros(HALF - 2 * FAR, np.int64)
    p_far = np.concatenate([half + m, half - FAR + m, pad, half + m, half - m, pad])
    valid_far = np.ones(CH, np.float32)
    valid_far[HALF + FAR] = 0.0
    p = jnp.asarray(np.concatenate([p_main, p_far]).astype(np.float32))
    valid = jnp.asarray(np.concatenate([valid_main, valid_far]))
    t_norm = p / (t_len - 1)
    bands = (HY_EMB_DIM - 1) // 2
    fr = jnp.linspace(1e-4, bands - 1, bands, dtype=F32)
    ang = (2.0 * math.pi * p / t_len)[:, None] * fr[None, :]
    feat = jnp.concatenate([t_norm[:, None], jnp.cos(ang), -jnp.sin(ang), valid[:, None]], axis=-1)
    return jnp.pad(feat, ((0, 0), (0, FEAT_W - feat.shape[1])))


def kernel(x, meta_tokens, norm1_g, w_in, ret_decay_fwd, ret_decay_bwd, ret_head_norm_g, w_ret_out,
           hy_conv_w, hy_conv_b, hy_filt_w1, hy_filt_b1, hy_filt_w2, hy_filt_b2, hy_filt_w3, hy_filt_b3,
           hy_filt_freq, hy_filt_w4, hy_skip, w_hy_out, w_o, norm2_g, w_router, w_exp_gate, w_exp_up,
           w_exp_down, final_norm_g):
    b, seq, d = x.shape
    t_len = seq + N_META
    tp = PAD + t_len
    assert d == D_MODEL and (b * tp) % IP_TM == 0 and tp % MG_TM == 0 and tp % CH == 0 and t_len - NFFT // 2 == FAR
    cap = EC_CAPACITY * t_len // N_EXPERTS
    slots = -(-cap // 16) * 16
    l = 0

    meta = jnp.broadcast_to(meta_tokens[None].astype(x.dtype), (b, N_META, d))
    h0 = jnp.concatenate([jnp.zeros((b, PAD, d), x.dtype), meta, x], axis=1)

    cs, sn = (jnp.tile(t, (b, 1)) for t in _rope_tables(tp))
    proj = _in_proj(h0.reshape(b * tp, d), norm1_g[l][None], w_in[l].astype(BF16), cs, sn, tp)
    proj3 = proj.reshape(b, tp, IN_PROJ_W)

    lf = jax.nn.log_sigmoid(ret_decay_fwd[l].astype(F32))
    lb = jax.nn.log_sigmoid(ret_decay_bwd[l].astype(F32))
    og = _retention(proj3, lf, lb, ret_head_norm_g[l][None], tp)

    x0c, z, zp = _hy_prep(proj3, hy_conv_w[l], hy_conv_b[l][None], tp)

    feat = _filter_features(t_len)
    w1p = jnp.pad(hy_filt_w1[l].astype(F32), ((0, FEAT_W - HY_EMB_DIM), (0, 0)))
    max_decay = math.log(HY_DECAY_TARGET) / HY_FAST_DECAY_PCT
    min_decay = math.log(HY_DECAY_TARGET) / HY_SLOW_DECAY_PCT
    dl = jnp.abs(jnp.linspace(min_decay, max_decay, D_MODEL, dtype=F32))[None]
    fargs = (_split_hi_lo(w1p), hy_filt_b1[l][None].astype(F32), _split_hi_lo(hy_filt_w2[l]),
             hy_filt_b2[l][None].astype(F32), _split_hi_lo(hy_filt_w3[l]), hy_filt_b3[l][None].astype(F32),
             hy_filt_freq[l][None].astype(F32), _split_hi_lo(hy_filt_w4[l]), dl)
    g = _filters(feat[:NFFT], *fargs, FT_ROWS)
    h_far = _filters(feat[NFFT:], *fargs, CH)
    yc = _fft_conv(zp, g)
    corr = _far_correction(h_far, z, tp)

    w_router_p = jnp.pad(w_router[l].astype(F32), ((0, 0), (0, ROUTER_W - N_EXPERTS)))
    w_router_p = _split_hi_lo(w_router_p)
    h1, xn2, aff, aff_packed = _merge(h0, og, x0c, z, yc, proj3, corr, hy_skip[l][None].astype(F32),
                          w_ret_out[l].astype(BF16), w_hy_out[l].astype(BF16), w_o[l].astype(BF16),
                          norm2_g[l][None].astype(F32), w_router_p, tp)

    idx, gate = _topk(aff, aff_packed, cap, slots)
    live = (jnp.arange(slots) < cap)[None, None, :]
    idx_local = jnp.where(live, idx, PAD)
    idx_flat = idx_local + (jnp.arange(b * N_EXPERTS, dtype=I32) // N_EXPERTS * tp)[:, None, None]
    ye = _moe_ffn(idx_flat, gate, xn2.reshape(b * tp, ROW_SUB, ROW_LANE),
                  w_exp_gate[l], w_exp_up[l], w_exp_down[l], slots)
    return _combine(idx_local, h1, ye, final_norm_g.astype(F32), cap, slots)
```

```python
import functools
import math

import numpy as np
import jax
import jax.numpy as jnp
from jax import lax
from jax.experimental import pallas as pl
from jax.experimental.pallas import tpu as pltpu

F32 = jnp.float32
BF16 = jnp.bfloat16
I32 = jnp.int32

D_MODEL = 1024
N_META = 16
RET_HEADS = 4
RET_QK_HEAD = 128
RET_V_HEAD = 256
ROPE_BASE = 10000.0
HY_EMB_DIM = 33
HY_FILTER_ORDER = 64
HY_FAST_DECAY_PCT = 0.3
HY_SLOW_DECAY_PCT = 1.5
HY_DECAY_TARGET = 1e-2
N_EXPERTS = 16
EC_CAPACITY = 2
D_FF = 2 * D_MODEL
RMS_EPS = 1e-6
IN_PROJ_W = 8192

CH = 128
PAD = CH - N_META
NFFT = 16384
ZCH = 80
FAR = 16

VMEM_LIMIT_BIG = 56 * 1024 * 1024
VMEM_LIMIT_MID = 40 * 1024 * 1024


def _cparams(sem, vmem=VMEM_LIMIT_MID):
    return pltpu.CompilerParams(dimension_semantics=sem, vmem_limit_bytes=vmem)


def _split_hi_lo(w):
    w = w.astype(F32)
    hi = w.astype(BF16)
    return jnp.stack([hi, (w - hi.astype(F32)).astype(BF16)])


def _dot3(a, w_hi, w_lo):
    a_hi = a.astype(BF16)
    a_lo = (a - a_hi.astype(F32)).astype(BF16)
    return (jnp.dot(a_hi, w_hi, preferred_element_type=F32) + jnp.dot(a_lo, w_hi, preferred_element_type=F32)
            + jnp.dot(a_hi, w_lo, preferred_element_type=F32))


IP_TM = 1280
IP_TN = 1024
IP_CN = 256


def _inproj_kernel(x_ref, g_ref, w_ref, cs_ref, sn_ref, o_ref, xn_ref):
    j = pl.program_id(1)

    @pl.when(j == 0)
    def _():
        x = x_ref[...]
        ms = jnp.mean(x * x, axis=-1, keepdims=True)
        xn_ref[...] = (x * lax.rsqrt(ms + RMS_EPS) * g_ref[...]).astype(BF16)

    def run(epilogue):
        for c in range(IP_TN // IP_CN):
            cols = slice(c * IP_CN, (c + 1) * IP_CN)
            acc = jnp.dot(xn_ref[...], w_ref[:, cols].astype(BF16), preferred_element_type=F32)
            epilogue(c, cols, acc)

    def rotary(c, cols, acc):
        scale = 1.0 if c < (IP_TN // IP_CN) // 2 else RET_QK_HEAD ** -0.5
        cs = cs_ref[...] * scale
        sn = sn_ref[...] * scale
        for hh in range(IP_CN // RET_QK_HEAD):
            xh = acc[:, hh * RET_QK_HEAD:(hh + 1) * RET_QK_HEAD]
            rot = xh * cs + pltpu.roll(xh, RET_QK_HEAD // 2, axis=1) * sn
            lo = c * IP_CN + hh * RET_QK_HEAD
            o_ref[:, lo:lo + RET_QK_HEAD] = rot.astype(BF16)

    def raw(c, cols, acc):
        o_ref[:, cols] = acc.astype(BF16)

    def swish(c, cols, acc):
        o_ref[:, cols] = (acc * jax.nn.sigmoid(acc)).astype(BF16)

    def sigm(c, cols, acc):
        o_ref[:, cols] = jax.nn.sigmoid(acc).astype(BF16)

    pl.when(j == 0)(lambda: run(rotary))
    pl.when(jnp.logical_or(j == 1, jnp.logical_and(j >= 3, j < 6)))(lambda: run(raw))
    pl.when(j == 2)(lambda: run(swish))
    pl.when(j >= 6)(lambda: run(sigm))


def _in_proj(h0, g1, w_in_bf, cs, sn, tp):
    n_rows = h0.shape[0]
    return pl.pallas_call(
        _inproj_kernel,
        grid=(n_rows // IP_TM, IN_PROJ_W // IP_TN),
        in_specs=[
            pl.BlockSpec((IP_TM, D_MODEL), lambda i, j: (i, 0)),
            pl.BlockSpec((1, D_MODEL), lambda i, j: (0, 0)),
            pl.BlockSpec((D_MODEL, IP_TN), lambda i, j: (0, j)),
            pl.BlockSpec((IP_TM, RET_QK_HEAD), lambda i, j: (i, 0)),
            pl.BlockSpec((IP_TM, RET_QK_HEAD), lambda i, j: (i, 0)),
        ],
        out_specs=pl.BlockSpec((IP_TM, IP_TN), lambda i, j: (i, j)),
        out_shape=jax.ShapeDtypeStruct((n_rows, IN_PROJ_W), BF16),
        scratch_shapes=[pltpu.VMEM((IP_TM, D_MODEL), BF16)],
        compiler_params=_cparams(("arbitrary", "arbitrary")),
        name="in_proj",
    )(h0, g1, w_in_bf, cs, sn)


RET_UNROLL = 5


def _ret_kernel(lf_ref, lb_ref, q_ref, k_ref, v_ref, gr_ref, gn_ref, o_ref, ob_ref, s_ref, *, nch):
    h = pl.program_id(1)
    lf = lf_ref[h]
    lb = lb_ref[h]
    ri = lax.broadcasted_iota(I32, (CH, CH), 0).astype(F32)
    ci = lax.broadcasted_iota(I32, (CH, CH), 1).astype(F32)
    diff = ri - ci
    mask = jnp.exp(jnp.where(diff >= 0, lf * diff, -lb * diff))
    w_end = jnp.exp(lf * (CH - 1.0 - ri))
    w_start = jnp.exp(lb * ri)
    qw_f = jnp.exp(lf * (ri + 1.0))
    qw_b = jnp.exp(lb * (CH - ri))
    dec_f = jnp.exp(jnp.full((CH, RET_V_HEAD), lf * CH, F32))
    dec_b = jnp.exp(jnp.full((CH, RET_V_HEAD), lb * CH, F32))
    tn_dims = (((0,), (0,)), ((), ()))
    nt_dims = (((1,), (1,)), ((), ()))

    s_ref[...] = jnp.zeros_like(s_ref)

    def bwd(it, carry):
        s = s_ref[...]
        for u in range(RET_UNROLL):
            n = nch - 1 - (it * RET_UNROLL + u)
            r0 = pl.multiple_of(n * CH, CH)
            q = q_ref[0, pl.ds(r0, CH), :].astype(F32)
            k = k_ref[0, pl.ds(r0, CH), :].astype(F32)
            v = v_ref[0, pl.ds(r0, CH), :]
            ob_ref[pl.ds(r0, CH), :] = jnp.dot((q * qw_b).astype(BF16), s.astype(BF16),
                                               preferred_element_type=F32)
            a = lax.dot_general((k * w_start).astype(BF16), v, tn_dims, preferred_element_type=F32)
            s = s * dec_b + a
        s_ref[...] = s
        return carry

    lax.fori_loop(0, nch // RET_UNROLL, bwd, 0)

    s_ref[...] = jnp.zeros_like(s_ref)
    gn = gn_ref[...]

    def fwd(it, carry):
        s = s_ref[...]
        for u in range(RET_UNROLL):
            n = it * RET_UNROLL + u
            r0 = pl.multiple_of(n * CH, CH)
            qb = q_ref[0, pl.ds(r0, CH), :]
            kb = k_ref[0, pl.ds(r0, CH), :]
            v = v_ref[0, pl.ds(r0, CH), :]
            q = qb.astype(F32)
            k = kb.astype(F32)
            scores = lax.dot_general(qb, kb, nt_dims, preferred_element_type=F32) * mask
            o = jnp.dot(scores.astype(BF16), v, preferred_element_type=F32)
            o = o + jnp.dot((q * qw_f).astype(BF16), s.astype(BF16), preferred_element_type=F32)
            o = o + ob_ref[pl.ds(r0, CH), :]
            a = lax.dot_general((k * w_end).astype(BF16), v, tn_dims, preferred_element_type=F32)
            s = s * dec_f + a
            y = o * lax.rsqrt(jnp.mean(o * o, axis=-1, keepdims=True) + RMS_EPS) * gn
            o_ref[0, pl.ds(r0, CH), :] = (y * gr_ref[0, pl.ds(r0, CH), :].astype(F32)).astype(BF16)
        s_ref[...] = s
        return carry

    lax.fori_loop(0, nch // RET_UNROLL, fwd, 0)


def _retention(proj3, lf, lb, gn, tp):
    b = proj3.shape[0]
    nch = tp // CH
    qk_blocks = (RET_HEADS * RET_QK_HEAD) // RET_QK_HEAD
    v_blk0 = (2 * RET_HEADS * RET_QK_HEAD) // RET_V_HEAD
    g_blk0 = v_blk0 + RET_HEADS
    smem = pl.BlockSpec(memory_space=pltpu.SMEM)
    return pl.pallas_call(
        functools.partial(_ret_kernel, nch=nch),
        grid=(b, RET_HEADS),
        in_specs=[
            smem, smem,
            pl.BlockSpec((1, tp, RET_QK_HEAD), lambda bi, h: (bi, 0, h)),
            pl.BlockSpec((1, tp, RET_QK_HEAD), lambda bi, h: (bi, 0, qk_blocks + h)),
            pl.BlockSpec((1, tp, RET_V_HEAD), lambda bi, h: (bi, 0, v_blk0 + h)),
            pl.BlockSpec((1, tp, RET_V_HEAD), lambda bi, h: (bi, 0, g_blk0 + h)),
            pl.BlockSpec((1, RET_V_HEAD), lambda bi, h: (0, h)),
        ],
        out_specs=pl.BlockSpec((1, tp, RET_V_HEAD), lambda bi, h: (bi, 0, h)),
        out_shape=jax.ShapeDtypeStruct((b, tp, RET_HEADS * RET_V_HEAD), BF16),
        scratch_shapes=[pltpu.VMEM((tp, RET_V_HEAD), F32), pltpu.VMEM((RET_QK_HEAD, RET_V_HEAD), F32)],
        compiler_params=_cparams(("arbitrary", "arbitrary"), VMEM_LIMIT_BIG),
        name="retention",
    )(lf, lb, proj3, proj3, proj3, proj3, gn)


HP_CW = 128


def _hyprep_kernel(u0_ref, u1_ref, u2_ref, w0_ref, w1_ref, w2_ref, b0_ref, b1_ref, b2_ref,
                   x0_ref, z_ref, zp_ref, *, nch):
    rows = lax.broadcasted_iota(I32, (CH, HP_CW), 0)
    halo = 16
    zp_ref[...] = jnp.zeros_like(zp_ref)

    def conv(u_ref, w_ref, b_ref, n, r0):
        cur = u_ref[0, pl.ds(r0, CH), :].astype(F32)
        rp = pl.multiple_of(jnp.maximum(r0 - halo, 0), halo)
        rn = pl.multiple_of(jnp.minimum(r0 + CH, (nch - 1) * CH), halo)
        prev = u_ref[0, pl.ds(rp, halo), :].astype(F32)[halo - 1:halo, :]
        nxt = u_ref[0, pl.ds(rn, halo), :].astype(F32)[0:1, :]
        prev = jnp.where(n > 0, prev, 0.0)
        nxt = jnp.where(n < nch - 1, nxt, 0.0)
        up = jnp.where(rows == 0, prev, pltpu.roll(cur, 1, axis=0))
        dn = jnp.where(rows == CH - 1, nxt, pltpu.roll(cur, CH - 1, axis=0))
        w = w_ref[...]
        return up * w[0:1, :] + cur * w[1:2, :] + dn * w[2:3, :] + b_ref[...]

    def body(n, carry):
        r0 = pl.multiple_of(n * CH, CH)
        x0 = conv(u0_ref, w0_ref, b0_ref, n, r0)
        x1 = conv(u1_ref, w1_ref, b1_ref, n, r0)
        vv = conv(u2_ref, w2_ref, b2_ref, n, r0)
        z = jnp.where(rows + r0 >= PAD, x1 * vv, 0.0)
        x0_ref[0, pl.ds(r0, CH), :] = x0.astype(BF16)
        z_ref[0, pl.ds(r0, CH), :] = z.astype(BF16)
        zp_ref[0, pl.ds(n, CH, stride=ZCH), :] = z
        return carry

    lax.fori_loop(0, nch, body, 0)


def _hy_prep(proj3, conv_w, conv_b, tp):
    b = proj3.shape[0]
    nch = tp // CH
    ncb = D_MODEL // HP_CW
    u_blk0 = 3072 // HP_CW
    uspec = lambda s: pl.BlockSpec((1, tp, HP_CW), lambda bi, c: (bi, 0, u_blk0 + s * ncb + c))
    wspec = lambda s: pl.BlockSpec((3, HP_CW), lambda bi, c: (0, s * ncb + c))
    bspec = lambda s: pl.BlockSpec((1, HP_CW), lambda bi, c: (0, s * ncb + c))
    return pl.pallas_call(
        functools.partial(_hyprep_kernel, nch=nch),
        grid=(b, ncb),
        in_specs=[uspec(0), uspec(1), uspec(2), wspec(0), wspec(1), wspec(2), bspec(0), bspec(1), bspec(2)],
        out_specs=[pl.BlockSpec((1, tp, HP_CW), lambda bi, c: (bi, 0, c)),
                   pl.BlockSpec((1, tp, HP_CW), lambda bi, c: (bi, 0, c)),
                   pl.BlockSpec((1, CH * ZCH, HP_CW), lambda bi, c: (bi, 0, c))],
        out_shape=[jax.ShapeDtypeStruct((b, tp, D_MODEL), BF16),
                   jax.ShapeDtypeStruct((b, tp, D_MODEL), BF16),
                   jax.ShapeDtypeStruct((b, CH * ZCH, D_MODEL), F32)],
        compiler_params=_cparams(("arbitrary", "arbitrary")),
        name="hy_prep",
    )(proj3, proj3, proj3, conv_w, conv_w, conv_w, conv_b, conv_b, conv_b)


FEAT_W = 128


FT_ROWS = 512
HALF = CH // 2


def _filter_kernel(feat_ref, w1_ref, b1_ref, w2_ref, b2_ref, w3_ref, b3_ref, fq_ref, w4_ref, dl_ref, o_ref,
                   *, groups):
    feat = feat_ref[...]
    fq = fq_ref[...]
    hdn = jnp.sin(fq * (_dot3(feat, w1_ref[0], w1_ref[1]) + b1_ref[...]))
    hdn = jnp.sin(fq * (_dot3(hdn, w2_ref[0], w2_ref[1]) + b2_ref[...]))
    hdn = jnp.sin(fq * (_dot3(hdn, w3_ref[0], w3_ref[1]) + b3_ref[...]))
    scale = jnp.exp(-feat[:, 0:1] * dl_ref[...]) * feat[:, HY_EMB_DIM:HY_EMB_DIM + 1]
    for d in range(2):
        rows = [slice(g * CH + d * HALF, g * CH + (d + 1) * HALF) for g in range(groups)]
        hd = jnp.concatenate([hdn[r] for r in rows], axis=0)
        cols = slice(d * D_MODEL, (d + 1) * D_MODEL)
        filt = _dot3(hd, w4_ref[0, :, cols], w4_ref[1, :, cols])
        for g, r in enumerate(rows):
            o_ref[r, :] = filt[g * HALF:(g + 1) * HALF] * scale[r]


def _filters(feat, w1p, b1, w2, b2, w3, b3, fq, w4, dl, rows):
    n_rows = feat.shape[0]
    full = lambda a: pl.BlockSpec(a.shape, lambda i: (0,) * a.ndim)
    return pl.pallas_call(
        functools.partial(_filter_kernel, groups=rows // CH),
        grid=(n_rows // rows,),
        in_specs=[pl.BlockSpec((rows, FEAT_W), lambda i: (i, 0)),
                  full(w1p), full(b1), full(w2), full(b2), full(w3), full(b3), full(fq), full(w4), full(dl)],
        out_specs=pl.BlockSpec((rows, D_MODEL), lambda i: (i, 0)),
        out_shape=jax.ShapeDtypeStruct((n_rows, D_MODEL), F32),
        compiler_params=_cparams(("arbitrary",)),
        name="hy_filter",
    )(feat, w1p, b1, w2, b2, w3, b3, fq, w4, dl)


FFT_G = 8


def _lanes(j, c):
    return slice(j * c, (j + 1) * c)


def _fft_a_data_kernel(m_ref, z_ref, o_ref):
    for j in range(FFT_G):
        rows = slice(j * ZCH, (j + 1) * ZCH)
        x = jnp.concatenate([z_ref[0, rows, :], z_ref[1, rows, :]], axis=0).astype(BF16)
        o_ref[j] = jnp.dot(m_ref[j], x, preferred_element_type=F32).astype(BF16)


def _fft_a_filt_kernel(m_ref, g_ref, o_ref):
    for j in range(FFT_G):
        gj = g_ref[j * CH:(j + 1) * CH, :].astype(BF16)
        o_ref[j] = jnp.dot(m_ref[j], gj, preferred_element_type=F32).astype(BF16)


def _fft_b_kernel(f_ref, fi_ref, yr_ref, yi_ref, gr_ref, gi_ref, o_ref):
    c = o_ref.shape[-1]
    for j in range(FFT_G):
        y = jnp.concatenate([yr_ref[:, _lanes(j, c)], yi_ref[:, _lanes(j, c)]], axis=0)
        yg = jnp.concatenate([gr_ref[:, _lanes(j, c)], gi_ref[:, _lanes(j, c)]], axis=0)
        x = jnp.dot(f_ref[...], y, preferred_element_type=F32)
        g = jnp.dot(f_ref[...], yg, preferred_element_type=F32) * (1.0 / NFFT)
        xr, xi = x[:CH], x[CH:]
        gr, gi = g[:CH], g[CH:]
        p = jnp.concatenate([xr * gr - xi * gi, xr * gi + xi * gr], axis=0).astype(BF16)
        o_ref[j] = jnp.dot(fi_ref[...], p, preferred_element_type=F32).astype(BF16)


def _fft_a_inv_kernel(m_ref, ur_ref, ui_ref, o_ref):
    c = ur_ref.shape[-1] // FFT_G
    for j in range(FFT_G):
        u = jnp.concatenate([ur_ref[:, _lanes(j, c)], ui_ref[:, _lanes(j, c)]], axis=0)
        y = jnp.dot(m_ref[j], u, preferred_element_type=F32).astype(BF16)
        o_ref[0, :, _lanes(j, c)] = y[:ZCH]
        o_ref[1, :, _lanes(j, c)] = y[ZCH:]


def _dft_tables():
    n2 = np.arange(CH)[:, None, None]
    k1 = np.arange(CH)[None, :, None]

    def theta(n1_count):
        n1 = np.arange(n1_count)[None, None, :]
        return 2.0 * np.pi * ((k1 * (CH * n1 + n2)) % NFFT) / NFFT

    th = theta(ZCH)
    c, s = np.cos(th), np.sin(th)
    m_a = np.concatenate([np.concatenate([c, s], axis=2), np.concatenate([-s, c], axis=2)], axis=1)
    m_ainv = np.transpose(m_a, (0, 2, 1))
    th = theta(CH)
    m_af = np.concatenate([np.cos(th), -np.sin(th)], axis=1)
    a = 2.0 * np.pi * ((np.arange(CH)[:, None] * np.arange(CH)[None, :]) % CH) / CH
    c, s = np.cos(a), np.sin(a)
    f2 = np.block([[c, s], [-s, c]])
    f2i = np.block([[c, -s], [s, c]])
    f = lambda t: jnp.asarray(t.astype(np.float32)).astype(BF16)
    return f(m_a), f(m_ainv), f(m_af), f(f2), f(f2i)


def _fft_conv(zp, g):
    c = zp.shape[-1]
    m_a, m_ainv, m_af, f2, f2i = _dft_tables()
    cp = _cparams(("arbitrary",), VMEM_LIMIT_BIG)
    steps = CH // FFT_G
    gc = FFT_G * c
    full2 = pl.BlockSpec((2 * CH, 2 * CH), lambda i: (0, 0))
    col_re = pl.BlockSpec((CH, gc), lambda i: (0, i))
    col_im = pl.BlockSpec((CH, gc), lambda i: (0, steps + i))
    blk = pl.BlockSpec((FFT_G, 2 * CH, c), lambda i: (i, 0, 0))
    spec_shape = jax.ShapeDtypeStruct((CH, 2 * CH, c), BF16)

    yg = pl.pallas_call(
        _fft_a_filt_kernel, grid=(steps,),
        in_specs=[pl.BlockSpec((FFT_G, 2 * CH, CH), lambda i: (i, 0, 0)),
                  pl.BlockSpec((FFT_G * CH, c), lambda i: (i, 0))],
        out_specs=blk, out_shape=spec_shape, compiler_params=cp, name="fft_a_filt",
    )(m_af, g)
    yg2 = yg.reshape(CH, 2 * CH * c)

    y = pl.pallas_call(
        _fft_a_data_kernel, grid=(steps,),
        in_specs=[pl.BlockSpec((FFT_G, 2 * CH, 2 * ZCH), lambda i: (i, 0, 0)),
                  pl.BlockSpec((2, FFT_G * ZCH, c), lambda i: (0, i, 0))],
        out_specs=blk, out_shape=spec_shape, compiler_params=cp, name="fft_a_data",
    )(m_a, zp)
    y2 = y.reshape(CH, 2 * CH * c)
    u = pl.pallas_call(
        _fft_b_kernel, grid=(steps,),
        in_specs=[full2, full2, col_re, col_im, col_re, col_im],
        out_specs=blk, out_shape=spec_shape, compiler_params=cp, name="fft_b",
    )(f2, f2i, y2, y2, yg2, yg2)
    u2 = u.reshape(CH, 2 * CH * c)
    yc = pl.pallas_call(
        _fft_a_inv_kernel, grid=(steps,),
        in_specs=[pl.BlockSpec((FFT_G, 2 * ZCH, 2 * CH), lambda i: (i, 0, 0)), col_re, col_im],
        out_specs=pl.BlockSpec((2, ZCH, gc), lambda i: (0, 0, i)),
        out_shape=jax.ShapeDtypeStruct((2, ZCH, CH * c), BF16),
        compiler_params=cp, name="fft_a_inv",
    )(m_ainv, u2, u2)
    return yc.reshape(2, ZCH * CH, c)


def _far_kernel(far_ref, zm_ref, zl_ref, o_ref):
    g_hi = far_ref[HALF + FAR:HALF + 2 * FAR, :]
    ef = far_ref[0:FAR, :] - g_hi
    g_lo = far_ref[FAR:2 * FAR, :]
    hb = far_ref[HALF:HALF + FAR, :]
    zm = zm_ref[0, PAD:CH, :].astype(F32)
    zl = zl_ref[0, PAD:CH, :].astype(F32)
    row = lambda a, i: a[i:i + 1, :]
    eb = [row(hb, 0) - row(g_hi, 0)] + [row(hb, m) - row(g_lo, FAR - m) for m in range(1, FAR)]
    for j in range(FAR):
        acc_f = row(ef, j) * row(zm, 0)
        for i in range(1, j + 1):
            acc_f = acc_f + row(ef, j - i) * row(zm, i)
        o_ref[0, 0, j:j + 1, :] = acc_f
        acc_b = eb[0] * row(zl, j)
        for i in range(j + 1, FAR):
            acc_b = acc_b + eb[i - j] * row(zl, i)
        o_ref[0, 1, j:j + 1, :] = acc_b


def _far_correction(h_far, z, tp):
    b, _, c = z.shape
    last = tp // CH - 1
    return pl.pallas_call(
        _far_kernel, grid=(b,),
        in_specs=[pl.BlockSpec((CH, c), lambda bi: (0, 0)),
                  pl.BlockSpec((1, CH, c), lambda bi: (bi, 0, 0)),
                  pl.BlockSpec((1, CH, c), lambda bi: (bi, last, 0))],
        out_specs=pl.BlockSpec((1, 2, FAR, c), lambda bi: (bi, 0, 0, 0)),
        out_shape=jax.ShapeDtypeStruct((b, 2, FAR, c), F32),
        compiler_params=_cparams(("arbitrary",)),
        name="hy_far",
    )(h_far, z, z)


MG_TM = 640
MG_CN = 256
ROUTER_W = 128
AFF_PACK = ROUTER_W // N_EXPERTS
ROW_SUB, ROW_LANE = 8, 128


def _merge_kernel(h_ref, og_ref, x0_ref, z_ref, yc_ref, ga_ref, gb_ref, corr_ref, skip_ref,
                  wr_ref, wh_ref, wo_ref, g2_ref, wrt_ref,
                  h1_ref, xn_ref, aff_ref, affp_ref, pre_ref, mix_ref, h1s_ref, *, nt_b):
    i = pl.program_id(0)
    ib = i % nt_b
    pre_ref[...] = yc_ref[0].astype(F32) + z_ref[0].astype(F32) * skip_ref[...]

    @pl.when(ib == 0)
    def _():
        pre_ref[PAD:CH, :] += corr_ref[0, 1]

    @pl.when(ib == nt_b - 1)
    def _():
        pre_ref[MG_TM - FAR:MG_TM, :] += corr_ref[0, 0]

    chunks = [slice(c * MG_CN, (c + 1) * MG_CN) for c in range(D_MODEL // MG_CN)]
    og = og_ref[0]
    pre = (x0_ref[0].astype(F32) * pre_ref[...]).astype(BF16)
    for cols in chunks:
        ya = jnp.dot(og, wr_ref[:, cols], preferred_element_type=F32)
        yb = jnp.dot(pre, wh_ref[:, cols], preferred_element_type=F32)
        mix_ref[:, cols] = (ga_ref[0, :, cols].astype(F32) * ya + gb_ref[0, :, cols].astype(F32) * yb).astype(BF16)
    mixed = mix_ref[...]
    ss = jnp.zeros((MG_TM, 1), F32)
    for cols in chunks:
        h1 = h_ref[0, :, cols] + jnp.dot(mixed, wo_ref[:, cols], preferred_element_type=F32)
        h1s_ref[:, cols] = h1
        ss = ss + jnp.sum(h1 * h1, axis=-1, keepdims=True)
    rinv = lax.rsqrt(ss * (1.0 / D_MODEL) + RMS_EPS)
    h1 = h1s_ref[...]
    h1_ref[0] = h1.reshape(MG_TM, ROW_SUB, ROW_LANE)
    xn = h1 * rinv * g2_ref[...]
    xn_ref[0] = xn.reshape(MG_TM, ROW_SUB, ROW_LANE)
    logits = _dot3(xn, wrt_ref[0], wrt_ref[1])
    lane = lax.broadcasted_iota(I32, logits.shape, 1)
    logits = jnp.where(lane < N_EXPERTS, logits, -jnp.inf)
    m = jnp.max(logits, axis=-1, keepdims=True)
    e = jnp.exp(logits - m)
    aff = e / jnp.sum(e, axis=-1, keepdims=True)
    rows = lax.broadcasted_iota(I32, logits.shape, 0) + ib * MG_TM
    aff_ref[0] = jnp.where(rows >= PAD, aff, -1.0)
    lane_grp = lax.broadcasted_iota(I32, (MG_TM // AFF_PACK, ROUTER_W), 1) // N_EXPERTS
    packed = jnp.zeros((MG_TM // AFF_PACK, ROUTER_W), F32)
    for g in range(AFF_PACK):
        blk = aff_ref[0, pl.ds(g, MG_TM // AFF_PACK, stride=AFF_PACK), :]
        if g:
            blk = pltpu.roll(blk, N_EXPERTS * g, axis=1)
        packed = jnp.where(lane_grp == g, blk, packed)
    affp_ref[0] = packed


def _merge(h0, og, x0c, z, yc, proj3, corr, skip, w_ret, w_hy, w_o, g2, w_router_p, tp):
    b = h0.shape[0]
    nt_b = tp // MG_TM
    ga_blk0 = 6144 // D_MODEL
    row = lambda w: pl.BlockSpec((1, MG_TM, w), lambda i: (i // nt_b, i % nt_b, 0))
    full = lambda a: pl.BlockSpec(a.shape, lambda i: (0,) * a.ndim)
    return pl.pallas_call(
        functools.partial(_merge_kernel, nt_b=nt_b),
        grid=(b * nt_b,),
        in_specs=[row(D_MODEL), row(D_MODEL), row(D_MODEL), row(D_MODEL), row(D_MODEL),
                  pl.BlockSpec((1, MG_TM, D_MODEL), lambda i: (i // nt_b, i % nt_b, ga_blk0)),
                  pl.BlockSpec((1, MG_TM, D_MODEL), lambda i: (i // nt_b, i % nt_b, ga_blk0 + 1)),
                  pl.BlockSpec((1, 2, FAR, D_MODEL), lambda i: (i // nt_b, 0, 0, 0)),
                  full(skip), full(w_ret), full(w_hy), full(w_o), full(g2), full(w_router_p)],
        out_specs=[pl.BlockSpec((1, MG_TM, ROW_SUB, ROW_LANE), lambda i: (i // nt_b, i % nt_b, 0, 0)),
                   pl.BlockSpec((1, MG_TM, ROW_SUB, ROW_LANE), lambda i: (i // nt_b, i % nt_b, 0, 0)),
                   row(ROUTER_W),
                   pl.BlockSpec((1, MG_TM // AFF_PACK, ROUTER_W), lambda i: (i // nt_b, i % nt_b, 0))],
        out_shape=[jax.ShapeDtypeStruct((b, tp, ROW_SUB, ROW_LANE), F32),
                   jax.ShapeDtypeStruct((b, tp, ROW_SUB, ROW_LANE), F32),
                   jax.ShapeDtypeStruct((b, tp, ROUTER_W), F32),
                   jax.ShapeDtypeStruct((b, tp // AFF_PACK, ROUTER_W), F32)],
        scratch_shapes=[pltpu.VMEM((MG_TM, D_MODEL), F32), pltpu.VMEM((MG_TM, D_MODEL), BF16),
                        pltpu.VMEM((MG_TM, D_MODEL), F32)],
        compiler_params=_cparams(("arbitrary",), VMEM_LIMIT_BIG),
        name="merge",
    )(h0, og, x0c, z, yc, proj3, proj3, corr, skip, w_ret, w_hy, w_o, g2, w_router_p)


TK_W = 128


TK_UNROLL = 5


def _select_kernel(aff_ref, affp_ref, low_ref, slot_ref, offs_ref, *, cap, nch):
    def chunk(c):
        r0 = pl.multiple_of(c * CH, CH)
        return aff_ref[0, pl.ds(r0, CH), :]

    def count(pred):
        cnt = jnp.sum(pred(affp_ref[0]).astype(I32), axis=0, keepdims=True)
        for shift in (N_EXPERTS, 2 * N_EXPERTS, 4 * N_EXPERTS):
            cnt = cnt + pltpu.roll(cnt, shift, axis=1)
        return cnt

    def search(it, bits):
        cand = bits | jnp.left_shift(1, 29 - it)
        cand_f = pltpu.bitcast(cand, F32)
        return jnp.where(count(lambda a: a >= cand_f) >= cap, cand, bits)

    thr = pltpu.bitcast(lax.fori_loop(0, 30, search, jnp.zeros((1, ROUTER_W), I32)), F32)
    need = (cap - count(lambda a: a > thr)).astype(F32)
    low = low_ref[...]

    def scan(it, carry):
        c_eq, c_sel = carry
        for u in range(TK_UNROLL):
            c = it * TK_UNROLL + u
            a = chunk(c)
            eq = a == thr
            eq_f = eq.astype(F32)
            eq_rank = jnp.dot(low, eq_f, preferred_element_type=F32) + c_eq
            sel = jnp.logical_or(a > thr, jnp.logical_and(eq, eq_rank < need))
            sel_f = sel.astype(F32)
            slot = jnp.dot(low, sel_f, preferred_element_type=F32) + c_sel
            r0 = pl.multiple_of(c * CH, CH)
            slot_ref[0, pl.ds(r0, CH), :] = jnp.where(sel, slot, -1.0).astype(I32)
            offs_ref[0, c] = c_sel.astype(I32)
            c_eq = c_eq + jnp.sum(eq_f, axis=0, keepdims=True)
            c_sel = c_sel + jnp.sum(sel_f, axis=0, keepdims=True)
        return c_eq, c_sel

    zero = jnp.zeros((1, ROUTER_W), F32)
    lax.fori_loop(0, nch // TK_UNROLL, scan, (zero, zero))


def _extract_kernel(offs_ref, slot_ref, aff_ref, idx_ref, gate_ref, *, nch, nwin):
    idx_ref[...] = jnp.zeros_like(idx_ref)
    gate_ref[...] = jnp.zeros_like(gate_ref)
    lane = lax.broadcasted_iota(I32, (CH, TK_W), 1)
    trow = lax.broadcasted_iota(I32, (CH, TK_W), 0)

    def per_chunk(c, carry):
        r0 = pl.multiple_of(c * CH, CH)
        slots = slot_ref[0, pl.ds(r0, CH), :]
        affs = aff_ref[0, pl.ds(r0, CH), :]
        tpos = (trow + r0).astype(F32)
        for e in range(N_EXPERTS):
            col = jnp.broadcast_to(slots[:, e:e + 1], (CH, TK_W))
            gcol = jnp.broadcast_to(affs[:, e:e + 1], (CH, TK_W))
            w0 = offs_ref[0, 0, c * N_EXPERTS + e] // TK_W
            for dw in range(2):
                w = w0 + dw
                base = jnp.where(w < nwin, w * TK_W, -2 * TK_W)
                hit = col == lane + base
                row = e * nwin + jnp.minimum(w, nwin - 1)
                idx_ref[row] += jnp.sum(jnp.where(hit, tpos, 0.0), axis=0, keepdims=True)
                gate_ref[row] += jnp.sum(jnp.where(hit, gcol, 0.0), axis=0, keepdims=True)
        return carry

    lax.fori_loop(0, nch, per_chunk, 0)


def _topk(aff, aff_packed, cap, slots):
    b, tp, _ = aff.shape
    nch = tp // CH
    nwin = -(-slots // TK_W)
    low = jnp.asarray(np.tril(np.ones((CH, CH), np.float32), k=-1))
    pack = AFF_PACK
    assert nch % TK_UNROLL == 0 and tp % pack == 0
    slot, offs = pl.pallas_call(
        functools.partial(_select_kernel, cap=cap, nch=nch),
        grid=(b,),
        in_specs=[pl.BlockSpec((1, tp, ROUTER_W), lambda bi: (bi, 0, 0)),
                  pl.BlockSpec((1, tp // pack, ROUTER_W), lambda bi: (bi, 0, 0)),
                  pl.BlockSpec((CH, CH), lambda bi: (0, 0))],
        out_specs=[pl.BlockSpec((1, tp, ROUTER_W), lambda bi: (bi, 0, 0)),
                   pl.BlockSpec((1, nch, 1, ROUTER_W), lambda bi: (bi, 0, 0, 0))],
        out_shape=[jax.ShapeDtypeStruct((b, tp, ROUTER_W), I32),
                   jax.ShapeDtypeStruct((b, nch, 1, ROUTER_W), I32)],
        compiler_params=_cparams(("arbitrary",)),
        name="topk_select",
    )(aff, aff_packed, low)
    offs_s = offs[:, :, 0, :N_EXPERTS].reshape(b, 1, nch * N_EXPERTS)
    rows = pl.BlockSpec((None, N_EXPERTS * nwin, 1, TK_W), lambda bi: (bi, 0, 0, 0))
    out = jax.ShapeDtypeStruct((b, N_EXPERTS * nwin, 1, TK_W), F32)
    idx, gate = pl.pallas_call(
        functools.partial(_extract_kernel, nch=nch, nwin=nwin),
        grid=(b,),
        in_specs=[pl.BlockSpec((1, 1, nch * N_EXPERTS), lambda bi: (bi, 0, 0), memory_space=pltpu.SMEM),
                  pl.BlockSpec((1, tp, ROUTER_W), lambda bi: (bi, 0, 0)),
                  pl.BlockSpec((1, tp, ROUTER_W), lambda bi: (bi, 0, 0))],
        out_specs=[rows, rows],
        out_shape=[out, out],
        compiler_params=_cparams(("arbitrary",)),
        name="topk_extract",
    )(offs_s, slot, aff)
    idx = idx.reshape(b * N_EXPERTS, 1, nwin * TK_W)[:, :, :slots].astype(I32)
    gate = gate.reshape(b * N_EXPERTS, 1, nwin * TK_W)[:, :, :slots]
    return idx, gate


FF_TF = 1024
MOE_CN = 256


MOE_UNROLL = 4


def _moe_kernel(idx_ref, idx_next_ref, xn_hbm, wg_ref, wu_ref, wd_ref, o_ref,
                xe32_ref, xe_ref, hid_ref, acc_ref, sem,
                *, slots, nf):
    i = pl.program_id(0)
    f = pl.program_id(1)
    buf = i % 2
    share = slots // nf

    def row_copy(ids_ref, s, b):
        return pltpu.make_async_copy(xn_hbm.at[ids_ref[0, 0, s]], xe32_ref.at[b, s], sem.at[b])

    def for_rows(lo, n, fn):
        def body(k, carry):
            for u in range(MOE_UNROLL):
                fn(lo + k * MOE_UNROLL + u)
            return carry

        lax.fori_loop(0, n // MOE_UNROLL, body, 0)

    @pl.when(jnp.logical_and(i == 0, f == 0))
    def _():
        for_rows(0, slots, lambda s: row_copy(idx_ref, s, 0).start())

    @pl.when(f == 0)
    def _():
        for_rows(0, slots, lambda s: row_copy(idx_ref, s, buf).wait())
        xe_ref[...] = xe32_ref[buf].reshape(slots, D_MODEL).astype(BF16)

    @pl.when(f == 0)
    def _():
        acc_ref[...] = jnp.zeros_like(acc_ref)

    n_up, n_down = FF_TF // MOE_CN, D_MODEL // MOE_CN
    per_chunk = share // (n_up + n_down)

    def prefetch(chunk):
        base = f * share + chunk * per_chunk
        for u in range(per_chunk):
            row_copy(idx_next_ref, base + u, 1 - buf).start()

    xe = xe_ref[...]
    for c in range(n_up):
        cols = slice(c * MOE_CN, (c + 1) * MOE_CN)
        gg = jnp.dot(xe, wg_ref[0, :, cols].astype(BF16), preferred_element_type=F32)
        uu = jnp.dot(xe, wu_ref[0, :, cols].astype(BF16), preferred_element_type=F32)
        hid_ref[:, cols] = (gg * jax.nn.sigmoid(gg) * uu).astype(BF16)
        prefetch(c)
    hid = hid_ref[...]
    for c in range(n_down):
        cols = slice(c * MOE_CN, (c + 1) * MOE_CN)
        part = jnp.dot(hid, wd_ref[0, :, cols].astype(BF16), preferred_element_type=F32)
        acc_ref[:, cols] = acc_ref[:, cols] + part
        prefetch(n_up + c)

    @pl.when(f == pl.num_programs(1) - 1)
    def _():
        o_ref[0] = acc_ref[...].reshape(slots, ROW_SUB, ROW_LANE)

    @pl.when(jnp.logical_and(i == pl.num_programs(0) - 1, f == pl.num_programs(1) - 1))
    def _():
        for_rows(0, slots, lambda s: row_copy(idx_next_ref, s, 1 - buf).wait())


def _moe_ffn(idx, xn_flat, w_gate, w_up, w_down, slots):
    be = idx.shape[0]
    nf = D_FF // FF_TF
    assert slots % MOE_UNROLL == 0 and slots % (nf * (FF_TF // MOE_CN + D_MODEL // MOE_CN)) == 0
    return pl.pallas_call(
        functools.partial(_moe_kernel, slots=slots, nf=nf),
        grid=(be, nf),
        in_specs=[pl.BlockSpec((1, 1, slots), lambda i, f: (i, 0, 0), memory_space=pltpu.SMEM),
                  pl.BlockSpec((1, 1, slots), lambda i, f: (jnp.minimum(i + 1, be - 1), 0, 0),
                               memory_space=pltpu.SMEM),
                  pl.BlockSpec(memory_space=pl.ANY),
                  pl.BlockSpec((1, D_MODEL, FF_TF), lambda i, f: (i % N_EXPERTS, 0, f)),
                  pl.BlockSpec((1, D_MODEL, FF_TF), lambda i, f: (i % N_EXPERTS, 0, f)),
                  pl.BlockSpec((1, FF_TF, D_MODEL), lambda i, f: (i % N_EXPERTS, f, 0))],
        out_specs=pl.BlockSpec((1, slots, ROW_SUB, ROW_LANE), lambda i, f: (i, 0, 0, 0)),
        out_shape=jax.ShapeDtypeStruct((be, slots, ROW_SUB, ROW_LANE), F32),
        scratch_shapes=[pltpu.VMEM((2, slots, ROW_SUB, ROW_LANE), F32), pltpu.VMEM((slots, D_MODEL), BF16),
                        pltpu.VMEM((slots, FF_TF), BF16), pltpu.VMEM((slots, D_MODEL), F32),
                        pltpu.SemaphoreType.DMA((2,))],
        compiler_params=_cparams(("arbitrary", "arbitrary"), VMEM_LIMIT_BIG),
        name="moe_ffn",
    )(idx, idx, xn_flat, w_gate, w_up, w_down)


CB_U = 6
CB_NB = 512


def _combine_kernel(idx_ref, gate_ref, h1_hbm, ye_ref, g_ref, o_hbm, acc_ref, stage_ref, sem, osem, *, cap, tp):
    b = pl.program_id(0)
    e = pl.program_id(1)

    @pl.when(e == 0)
    def _():
        cp = pltpu.make_async_copy(h1_hbm.at[b], acc_ref, sem)
        cp.start()
        cp.wait()

    def rmw(g, carry):
        ts = [idx_ref[0, 0, g * CB_U + u] for u in range(CB_U)]
        vals = [acc_ref[ts[u]] + gate_ref[0, 0, g * CB_U + u] * ye_ref[g * CB_U + u] for u in range(CB_U)]
        for u in range(CB_U):
            acc_ref[ts[u]] = vals[u]
        return carry

    lax.fori_loop(0, cap // CB_U, rmw, 0)

    @pl.when(e == pl.num_programs(1) - 1)
    def _():
        gamma = g_ref[...]
        n_blocks = (tp - CH) // CB_NB

        def out_copy(k):
            return pltpu.make_async_copy(stage_ref.at[k % 2], o_hbm.at[b, pl.ds(k * CB_NB, CB_NB), :],
                                         osem.at[k % 2])

        for k in range(n_blocks):
            x = acc_ref[pl.ds(CH + k * CB_NB, CB_NB)]
            ms = jnp.sum(jnp.sum(x * x, axis=2, keepdims=True), axis=1, keepdims=True) * (1.0 / D_MODEL)
            y = x * lax.rsqrt(ms + RMS_EPS) * gamma
            if k >= 2:
                out_copy(k - 2).wait()
            stage_ref[k % 2] = y.reshape(CB_NB, D_MODEL)
            out_copy(k).start()
        for k in range(max(n_blocks - 2, 0), n_blocks):
            out_copy(k).wait()


def _combine(idx_local, gate, h1, ye, gf, cap, slots):
    b, tp = h1.shape[:2]
    assert cap % CB_U == 0 and (tp - CH) % CB_NB == 0
    return pl.pallas_call(
        functools.partial(_combine_kernel, cap=cap, tp=tp),
        grid=(b, N_EXPERTS),
        in_specs=[pl.BlockSpec((1, 1, slots), lambda bi, e: (bi * N_EXPERTS + e, 0, 0), memory_space=pltpu.SMEM),
                  pl.BlockSpec((1, 1, slots), lambda bi, e: (bi * N_EXPERTS + e, 0, 0), memory_space=pltpu.SMEM),
                  pl.BlockSpec(memory_space=pl.ANY),
                  pl.BlockSpec((None, slots, ROW_SUB, ROW_LANE), lambda bi, e: (bi * N_EXPERTS + e, 0, 0, 0)),
                  pl.BlockSpec((1, ROW_SUB, ROW_LANE), lambda bi, e: (0, 0, 0))],
        out_specs=pl.BlockSpec(memory_space=pl.ANY),
        out_shape=jax.ShapeDtypeStruct((b, tp - CH, D_MODEL), F32),
        scratch_shapes=[pltpu.VMEM((tp, ROW_SUB, ROW_LANE), F32), pltpu.VMEM((2, CB_NB, D_MODEL), F32),
                        pltpu.SemaphoreType.DMA(()), pltpu.SemaphoreType.DMA((2,))],
        compiler_params=_cparams(("arbitrary", "arbitrary"), VMEM_LIMIT_BIG),
        name="combine",
    )(idx_local, gate, h1, ye, gf.reshape(1, ROW_SUB, ROW_LANE))


def _rope_tables(tp, b):
    half = RET_QK_HEAD // 2
    pos = np.arange(tp, dtype=np.float64) - PAD
    inv = ROPE_BASE ** (-np.arange(half, dtype=np.float64) / half)
    ang = pos[:, None] * inv[None, :]
    cos, sin = np.cos(ang), np.sin(ang)
    tile = lambda t: jnp.asarray(np.tile(t, (b, 1)).astype(np.float32))
    return tile(np.concatenate([cos, cos], axis=1)), tile(np.concatenate([-sin, sin], axis=1))


def _filter_features(t_len):
    half = NFFT // 2
    q = np.arange(NFFT)
    r = CH * (q % CH) + q // CH
    p_main = np.where(r < half, r, NFFT - r)
    valid_main = (r != half).astype(np.float32)
    m = np.arange(FAR)
    pad = np.zeros(HALF - 2 * FAR, np.int64)
    p_far = np.concatenate([half + m, half - FAR + m, pad, half + m, half - m, pad])
    valid_far = np.ones(CH, np.float32)
    valid_far[HALF + FAR] = 0.0
    p = np.concatenate([p_main, p_far]).astype(np.float64)
    valid = np.concatenate([valid_main, valid_far]).astype(np.float64)
    t_norm = p / (t_len - 1)
    bands = (HY_EMB_DIM - 1) // 2
    fr = np.linspace(1e-4, bands - 1, bands)
    ang = (2.0 * math.pi * p / t_len)[:, None] * fr[None, :]
    feat = np.concatenate([t_norm[:, None], np.cos(ang), -np.sin(ang), valid[:, None]], axis=-1)
    feat = np.pad(feat, ((0, 0), (0, FEAT_W - feat.shape[1])))
    return jnp.asarray(feat.astype(np.float32))


def kernel(x, meta_tokens, norm1_g, w_in, ret_decay_fwd, ret_decay_bwd, ret_head_norm_g, w_ret_out,
           hy_conv_w, hy_conv_b, hy_filt_w1, hy_filt_b1, hy_filt_w2, hy_filt_b2, hy_filt_w3, hy_filt_b3,
           hy_filt_freq, hy_filt_w4, hy_skip, w_hy_out, w_o, norm2_g, w_router, w_exp_gate, w_exp_up,
           w_exp_down, final_norm_g):
    b, seq, d = x.shape
    t_len = seq + N_META
    tp = PAD + t_len
    assert d == D_MODEL and (b * tp) % IP_TM == 0 and tp % MG_TM == 0 and tp % CH == 0 and t_len - NFFT // 2 == FAR
    cap = EC_CAPACITY * t_len // N_EXPERTS
    slots = -(-cap // 16) * 16
    l = 0

    meta = jnp.broadcast_to(meta_tokens[None].astype(x.dtype), (b, N_META, d))
    h0 = jnp.concatenate([jnp.zeros((b, PAD, d), x.dtype), meta, x], axis=1)

    cs, sn = _rope_tables(tp, b)
    proj = _in_proj(h0.reshape(b * tp, d), norm1_g[l][None], w_in[l], cs, sn, tp)
    proj3 = proj.reshape(b, tp, IN_PROJ_W)

    lf = jax.nn.log_sigmoid(ret_decay_fwd[l].astype(F32))
    lb = jax.nn.log_sigmoid(ret_decay_bwd[l].astype(F32))
    og = _retention(proj3, lf, lb, ret_head_norm_g[l][None], tp)

    x0c, z, zp = _hy_prep(proj3, hy_conv_w[l], hy_conv_b[l][None], tp)

    feat = _filter_features(t_len)
    w1p = jnp.pad(hy_filt_w1[l].astype(F32), ((0, FEAT_W - HY_EMB_DIM), (0, 0)))
    max_decay = math.log(HY_DECAY_TARGET) / HY_FAST_DECAY_PCT
    min_decay = math.log(HY_DECAY_TARGET) / HY_SLOW_DECAY_PCT
    dl = jnp.abs(jnp.linspace(min_decay, max_decay, D_MODEL, dtype=F32))[None]
    fargs = (_split_hi_lo(w1p), hy_filt_b1[l][None].astype(F32), _split_hi_lo(hy_filt_w2[l]),
             hy_filt_b2[l][None].astype(F32), _split_hi_lo(hy_filt_w3[l]), hy_filt_b3[l][None].astype(F32),
             hy_filt_freq[l][None].astype(F32), _split_hi_lo(hy_filt_w4[l]), dl)
    g = _filters(feat[:NFFT], *fargs, FT_ROWS)
    h_far = _filters(feat[NFFT:], *fargs, CH)
    yc = _fft_conv(zp, g)
    corr = _far_correction(h_far, z, tp)

    w_router_p = jnp.pad(w_router[l].astype(F32), ((0, 0), (0, ROUTER_W - N_EXPERTS)))
    w_router_p = _split_hi_lo(w_router_p)
    h1, xn2, aff, aff_packed = _merge(h0, og, x0c, z, yc, proj3, corr, hy_skip[l][None].astype(F32),
                          w_ret_out[l].astype(BF16), w_hy_out[l].astype(BF16), w_o[l].astype(BF16),
                          norm2_g[l][None].astype(F32), w_router_p, tp)

    idx, gate = _topk(aff, aff_packed, cap, slots)
    live = (jnp.arange(slots) < cap)[None, None, :]
    idx_local = jnp.where(live, idx, PAD)
    idx_flat = idx_local + (jnp.arange(b * N_EXPERTS, dtype=I32) // N_EXPERTS * tp)[:, None, None]
    ye = _moe_ffn(idx_flat, xn2.reshape(b * tp, ROW_SUB, ROW_LANE),
                  w_exp_gate[l], w_exp_up[l], w_exp_down[l], slots)
    return _combine(idx_local, gate, h1, ye, final_norm_g.astype(F32), cap, slots)
```

```python
import functools
import math

import numpy as np
import jax
import jax.numpy as jnp
from jax import lax
from jax.experimental import pallas as pl
from jax.experimental.pallas import tpu as pltpu

F32 = jnp.float32
BF16 = jnp.bfloat16
I32 = jnp.int32

D_MODEL = 1024
N_META = 16
RET_HEADS = 4
RET_QK_HEAD = 128
RET_V_HEAD = 256
ROPE_BASE = 10000.0
HY_EMB_DIM = 33
HY_FILTER_ORDER = 64
HY_FAST_DECAY_PCT = 0.3
HY_SLOW_DECAY_PCT = 1.5
HY_DECAY_TARGET = 1e-2
N_EXPERTS = 16
EC_CAPACITY = 2
D_FF = 2 * D_MODEL
RMS_EPS = 1e-6
IN_PROJ_W = 8192

CH = 128
PAD = CH - N_META
NFFT = 16384
ZCH = 80
FAR = 16

VMEM_LIMIT_BIG = 56 * 1024 * 1024
VMEM_LIMIT_MID = 40 * 1024 * 1024


def _cparams(sem, vmem=VMEM_LIMIT_MID):
    return pltpu.CompilerParams(dimension_semantics=sem, vmem_limit_bytes=vmem)


def _split_hi_lo(w):
    w = w.astype(F32)
    hi = w.astype(BF16)
    return jnp.stack([hi, (w - hi.astype(F32)).astype(BF16)])


def _dot3(a, w_hi, w_lo):
    a_hi = a.astype(BF16)
    a_lo = (a - a_hi.astype(F32)).astype(BF16)
    return (jnp.dot(a_hi, w_hi, preferred_element_type=F32) + jnp.dot(a_lo, w_hi, preferred_element_type=F32)
            + jnp.dot(a_hi, w_lo, preferred_element_type=F32))


IP_TM = 1280
IP_TN = 1024
IP_CN = 256


def _inproj_kernel(x_ref, g_ref, w_ref, cs_ref, sn_ref, o_ref, xn_ref):
    j = pl.program_id(1)

    @pl.when(j == 0)
    def _():
        x = x_ref[...]
        ms = jnp.mean(x * x, axis=-1, keepdims=True)
        xn_ref[...] = (x * lax.rsqrt(ms + RMS_EPS) * g_ref[...]).astype(BF16)

    def run(epilogue):
        for c in range(IP_TN // IP_CN):
            cols = slice(c * IP_CN, (c + 1) * IP_CN)
            acc = jnp.dot(xn_ref[...], w_ref[:, cols].astype(BF16), preferred_element_type=F32)
            epilogue(c, cols, acc)

    def rotary(c, cols, acc):
        scale = 1.0 if c < (IP_TN // IP_CN) // 2 else RET_QK_HEAD ** -0.5
        cs = cs_ref[...] * scale
        sn = sn_ref[...] * scale
        for hh in range(IP_CN // RET_QK_HEAD):
            xh = acc[:, hh * RET_QK_HEAD:(hh + 1) * RET_QK_HEAD]
            rot = xh * cs + pltpu.roll(xh, RET_QK_HEAD // 2, axis=1) * sn
            lo = c * IP_CN + hh * RET_QK_HEAD
            o_ref[:, lo:lo + RET_QK_HEAD] = rot.astype(BF16)

    def raw(c, cols, acc):
        o_ref[:, cols] = acc.astype(BF16)

    def swish(c, cols, acc):
        o_ref[:, cols] = (acc * jax.nn.sigmoid(acc)).astype(BF16)

    def sigm(c, cols, acc):
        o_ref[:, cols] = jax.nn.sigmoid(acc).astype(BF16)

    pl.when(j == 0)(lambda: run(rotary))
    pl.when(jnp.logical_or(j == 1, jnp.logical_and(j >= 3, j < 6)))(lambda: run(raw))
    pl.when(j == 2)(lambda: run(swish))
    pl.when(j >= 6)(lambda: run(sigm))


def _in_proj(h0, g1, w_in_bf, cs, sn, tp):
    n_rows = h0.shape[0]
    return pl.pallas_call(
        _inproj_kernel,
        grid=(n_rows // IP_TM, IN_PROJ_W // IP_TN),
        in_specs=[
            pl.BlockSpec((IP_TM, D_MODEL), lambda i, j: (i, 0)),
            pl.BlockSpec((1, D_MODEL), lambda i, j: (0, 0)),
            pl.BlockSpec((D_MODEL, IP_TN), lambda i, j: (0, j)),
            pl.BlockSpec((IP_TM, RET_QK_HEAD), lambda i, j: (i, 0)),
            pl.BlockSpec((IP_TM, RET_QK_HEAD), lambda i, j: (i, 0)),
        ],
        out_specs=pl.BlockSpec((IP_TM, IP_TN), lambda i, j: (i, j)),
        out_shape=jax.ShapeDtypeStruct((n_rows, IN_PROJ_W), BF16),
        scratch_shapes=[pltpu.VMEM((IP_TM, D_MODEL), BF16)],
        compiler_params=_cparams(("arbitrary", "arbitrary")),
        name="in_proj",
    )(h0, g1, w_in_bf, cs, sn)


RET_UNROLL = 13


def _ret_kernel(lf_ref, lb_ref, q_ref, k_ref, v_ref, gr_ref, gn_ref, o_ref, ob_ref, s_ref, *, nch):
    h = pl.program_id(1)
    lf = lf_ref[h]
    lb = lb_ref[h]
    ri = lax.broadcasted_iota(I32, (CH, CH), 0).astype(F32)
    ci = lax.broadcasted_iota(I32, (CH, CH), 1).astype(F32)
    diff = ri - ci
    mask = jnp.exp(jnp.where(diff >= 0, lf * diff, -lb * diff))
    w_end = jnp.exp(lf * (CH - 1.0 - ri))
    w_start = jnp.exp(lb * ri)
    qw_f = jnp.exp(lf * (ri + 1.0))
    qw_b = jnp.exp(lb * (CH - ri))
    dec_f = jnp.exp(jnp.full((CH, RET_V_HEAD), lf * CH, F32))
    dec_b = jnp.exp(jnp.full((CH, RET_V_HEAD), lb * CH, F32))
    tn_dims = (((0,), (0,)), ((), ()))
    nt_dims = (((1,), (1,)), ((), ()))

    s_ref[...] = jnp.zeros_like(s_ref)

    def bwd(it, carry):
        s = s_ref[...]
        for u in range(RET_UNROLL):
            n = nch - 1 - (it * RET_UNROLL + u)
            r0 = pl.multiple_of(n * CH, CH)
            q = q_ref[0, pl.ds(r0, CH), :].astype(F32)
            k = k_ref[0, pl.ds(r0, CH), :].astype(F32)
            v = v_ref[0, pl.ds(r0, CH), :]
            ob_ref[pl.ds(r0, CH), :] = jnp.dot((q * qw_b).astype(BF16), s.astype(BF16),
                                               preferred_element_type=F32)
            a = lax.dot_general((k * w_start).astype(BF16), v, tn_dims, preferred_element_type=F32)
            s = s * dec_b + a
        s_ref[...] = s
        return carry

    lax.fori_loop(0, nch // RET_UNROLL, bwd, 0)

    s_ref[...] = jnp.zeros_like(s_ref)
    gn = gn_ref[...]

    def fwd(it, carry):
        s = s_ref[...]
        for u in range(RET_UNROLL):
            n = it * RET_UNROLL + u
            r0 = pl.multiple_of(n * CH, CH)
            qb = q_ref[0, pl.ds(r0, CH), :]
            kb = k_ref[0, pl.ds(r0, CH), :]
            v = v_ref[0, pl.ds(r0, CH), :]
            q = qb.astype(F32)
            k = kb.astype(F32)
            scores = lax.dot_general(qb, kb, nt_dims, preferred_element_type=F32) * mask
            o = jnp.dot(scores.astype(BF16), v, preferred_element_type=F32)
            o = o + jnp.dot((q * qw_f).astype(BF16), s.astype(BF16), preferred_element_type=F32)
            o = o + ob_ref[pl.ds(r0, CH), :]
            a = lax.dot_general((k * w_end).astype(BF16), v, tn_dims, preferred_element_type=F32)
            s = s * dec_f + a
            y = o * lax.rsqrt(jnp.mean(o * o, axis=-1, keepdims=True) + RMS_EPS) * gn
            o_ref[0, pl.ds(r0, CH), :] = (y * gr_ref[0, pl.ds(r0, CH), :].astype(F32)).astype(BF16)
        s_ref[...] = s
        return carry

    lax.fori_loop(0, nch // RET_UNROLL, fwd, 0)


def _retention(proj3, lf, lb, gn, tp):
    b = proj3.shape[0]
    nch = tp // CH
    qk_blocks = (RET_HEADS * RET_QK_HEAD) // RET_QK_HEAD
    v_blk0 = (2 * RET_HEADS * RET_QK_HEAD) // RET_V_HEAD
    g_blk0 = v_blk0 + RET_HEADS
    smem = pl.BlockSpec(memory_space=pltpu.SMEM)
    return pl.pallas_call(
        functools.partial(_ret_kernel, nch=nch),
        grid=(b, RET_HEADS),
        in_specs=[
            smem, smem,
            pl.BlockSpec((1, tp, RET_QK_HEAD), lambda bi, h: (bi, 0, h)),
            pl.BlockSpec((1, tp, RET_QK_HEAD), lambda bi, h: (bi, 0, qk_blocks + h)),
            pl.BlockSpec((1, tp, RET_V_HEAD), lambda bi, h: (bi, 0, v_blk0 + h)),
            pl.BlockSpec((1, tp, RET_V_HEAD), lambda bi, h: (bi, 0, g_blk0 + h)),
            pl.BlockSpec((1, RET_V_HEAD), lambda bi, h: (0, h)),
        ],
        out_specs=pl.BlockSpec((1, tp, RET_V_HEAD), lambda bi, h: (bi, 0, h)),
        out_shape=jax.ShapeDtypeStruct((b, tp, RET_HEADS * RET_V_HEAD), BF16),
        scratch_shapes=[pltpu.VMEM((tp, RET_V_HEAD), F32), pltpu.VMEM((RET_QK_HEAD, RET_V_HEAD), F32)],
        compiler_params=_cparams(("arbitrary", "arbitrary"), VMEM_LIMIT_BIG),
        name="retention",
    )(lf, lb, proj3, proj3, proj3, proj3, gn)


HP_CW = 128


def _hyprep_kernel(u0_ref, u1_ref, u2_ref, w0_ref, w1_ref, w2_ref, b0_ref, b1_ref, b2_ref,
                   x0_ref, z_ref, zp_ref, *, nch):
    rows = lax.broadcasted_iota(I32, (CH, HP_CW), 0)
    halo = 16
    zp_ref[...] = jnp.zeros_like(zp_ref)

    def conv(u_ref, w_ref, b_ref, n, r0):
        cur = u_ref[0, pl.ds(r0, CH), :].astype(F32)
        rp = pl.multiple_of(jnp.maximum(r0 - halo, 0), halo)
        rn = pl.multiple_of(jnp.minimum(r0 + CH, (nch - 1) * CH), halo)
        prev = u_ref[0, pl.ds(rp, halo), :].astype(F32)[halo - 1:halo, :]
        nxt = u_ref[0, pl.ds(rn, halo), :].astype(F32)[0:1, :]
        prev = jnp.where(n > 0, prev, 0.0)
        nxt = jnp.where(n < nch - 1, nxt, 0.0)
        up = jnp.where(rows == 0, prev, pltpu.roll(cur, 1, axis=0))
        dn = jnp.where(rows == CH - 1, nxt, pltpu.roll(cur, CH - 1, axis=0))
        w = w_ref[...]
        return up * w[0:1, :] + cur * w[1:2, :] + dn * w[2:3, :] + b_ref[...]

    def body(n, carry):
        r0 = pl.multiple_of(n * CH, CH)
        x0 = conv(u0_ref, w0_ref, b0_ref, n, r0)
        x1 = conv(u1_ref, w1_ref, b1_ref, n, r0)
        vv = conv(u2_ref, w2_ref, b2_ref, n, r0)
        z = jnp.where(rows + r0 >= PAD, x1 * vv, 0.0)
        x0_ref[0, pl.ds(r0, CH), :] = x0.astype(BF16)
        z_ref[0, pl.ds(r0, CH), :] = z.astype(BF16)
        zp_ref[0, pl.ds(n, CH, stride=ZCH), :] = z
        return carry

    lax.fori_loop(0, nch, body, 0)


def _hy_prep(proj3, conv_w, conv_b, tp):
    b = proj3.shape[0]
    nch = tp // CH
    ncb = D_MODEL // HP_CW
    u_blk0 = 3072 // HP_CW
    uspec = lambda s: pl.BlockSpec((1, tp, HP_CW), lambda bi, c: (bi, 0, u_blk0 + s * ncb + c))
    wspec = lambda s: pl.BlockSpec((3, HP_CW), lambda bi, c: (0, s * ncb + c))
    bspec = lambda s: pl.BlockSpec((1, HP_CW), lambda bi, c: (0, s * ncb + c))
    return pl.pallas_call(
        functools.partial(_hyprep_kernel, nch=nch),
        grid=(b, ncb),
        in_specs=[uspec(0), uspec(1), uspec(2), wspec(0), wspec(1), wspec(2), bspec(0), bspec(1), bspec(2)],
        out_specs=[pl.BlockSpec((1, tp, HP_CW), lambda bi, c: (bi, 0, c)),
                   pl.BlockSpec((1, tp, HP_CW), lambda bi, c: (bi, 0, c)),
                   pl.BlockSpec((1, CH * ZCH, HP_CW), lambda bi, c: (bi, 0, c))],
        out_shape=[jax.ShapeDtypeStruct((b, tp, D_MODEL), BF16),
                   jax.ShapeDtypeStruct((b, tp, D_MODEL), BF16),
                   jax.ShapeDtypeStruct((b, CH * ZCH, D_MODEL), F32)],
        compiler_params=_cparams(("arbitrary", "arbitrary")),
        name="hy_prep",
    )(proj3, proj3, proj3, conv_w, conv_w, conv_w, conv_b, conv_b, conv_b)


FEAT_W = 128


FT_ROWS = 512
HALF = CH // 2


def _filter_kernel(feat_ref, w1_ref, b1_ref, w2_ref, b2_ref, w3_ref, b3_ref, fq_ref, w4_ref, dl_ref, o_ref,
                   *, groups):
    feat = feat_ref[...]
    fq = fq_ref[...]
    hdn = jnp.sin(fq * (_dot3(feat, w1_ref[0], w1_ref[1]) + b1_ref[...]))
    hdn = jnp.sin(fq * (_dot3(hdn, w2_ref[0], w2_ref[1]) + b2_ref[...]))
    hdn = jnp.sin(fq * (_dot3(hdn, w3_ref[0], w3_ref[1]) + b3_ref[...]))
    filt = _dot3(hdn, w4_ref[0], w4_ref[1])
    dl = dl_ref[...]
    for d in range(2):
        f0 = d * FEAT_W
        scale = jnp.exp(-feat[:, f0:f0 + 1] * dl) * feat[:, f0 + HY_EMB_DIM:f0 + HY_EMB_DIM + 1]
        fd = filt[:, d * D_MODEL:(d + 1) * D_MODEL] * scale
        for g in range(groups):
            o_ref[g * CH + d * HALF:g * CH + (d + 1) * HALF, :] = fd[g * HALF:(g + 1) * HALF]


def _block_diag2(w):
    z = jnp.zeros_like(w)
    return jnp.concatenate([jnp.concatenate([w, z], axis=1), jnp.concatenate([z, w], axis=1)], axis=0)


def _filters(feat_pairs, w1p, b1, w2, b2, w3, b3, fq, w4, dl, rows):
    n_rows = 2 * feat_pairs.shape[0]
    twice = lambda v: jnp.concatenate([v, v], axis=1)
    c = w4.shape[1] // 2
    w4bd = jnp.concatenate([jnp.concatenate([w4[:, :c], jnp.zeros_like(w4[:, c:])], axis=1),
                            jnp.concatenate([jnp.zeros_like(w4[:, :c]), w4[:, c:]], axis=1)], axis=0)
    args = (_split_hi_lo(_block_diag2(w1p)), twice(b1), _split_hi_lo(_block_diag2(w2)), twice(b2),
            _split_hi_lo(_block_diag2(w3)), twice(b3), twice(fq), _split_hi_lo(w4bd), dl)
    full = lambda a: pl.BlockSpec(a.shape, lambda i: (0,) * a.ndim)
    return pl.pallas_call(
        functools.partial(_filter_kernel, groups=rows // CH),
        grid=(n_rows // rows,),
        in_specs=[pl.BlockSpec((rows // 2, 2 * FEAT_W), lambda i: (i, 0))] + [full(a) for a in args],
        out_specs=pl.BlockSpec((rows, D_MODEL), lambda i: (i, 0)),
        out_shape=jax.ShapeDtypeStruct((n_rows, D_MODEL), F32),
        compiler_params=_cparams(("arbitrary",)),
        name="hy_filter",
    )(feat_pairs, *args)


FFT_G = 8


def _lanes(j, c):
    return slice(j * c, (j + 1) * c)


def _fft_a_data_kernel(m_ref, z_ref, o_ref):
    for j in range(FFT_G):
        rows = slice(j * ZCH, (j + 1) * ZCH)
        x = jnp.concatenate([z_ref[0, rows, :], z_ref[1, rows, :]], axis=0).astype(BF16)
        o_ref[j] = jnp.dot(m_ref[j], x, preferred_element_type=F32).astype(BF16)


def _fft_a_filt_kernel(m_ref, g_ref, o_ref):
    for j in range(FFT_G):
        gj = g_ref[j * CH:(j + 1) * CH, :].astype(BF16)
        o_ref[j] = jnp.dot(m_ref[j], gj, preferred_element_type=F32).astype(BF16)


def _fft_b_kernel(f_ref, fi_ref, yr_ref, yi_ref, gr_ref, gi_ref, o_ref):
    c = o_ref.shape[-1]
    for j in range(FFT_G):
        y = jnp.concatenate([yr_ref[:, _lanes(j, c)], yi_ref[:, _lanes(j, c)]], axis=0)
        yg = jnp.concatenate([gr_ref[:, _lanes(j, c)], gi_ref[:, _lanes(j, c)]], axis=0)
        x = jnp.dot(f_ref[...], y, preferred_element_type=F32)
        g = jnp.dot(f_ref[...], yg, preferred_element_type=F32) * (1.0 / NFFT)
        xr, xi = x[:CH], x[CH:]
        gr, gi = g[:CH], g[CH:]
        p = jnp.concatenate([xr * gr - xi * gi, xr * gi + xi * gr], axis=0).astype(BF16)
        o_ref[j] = jnp.dot(fi_ref[...], p, preferred_element_type=F32).astype(BF16)


def _fft_a_inv_kernel(m_ref, ur_ref, ui_ref, o_ref):
    c = ur_ref.shape[-1] // FFT_G
    for j in range(FFT_G):
        u = jnp.concatenate([ur_ref[:, _lanes(j, c)], ui_ref[:, _lanes(j, c)]], axis=0)
        y = jnp.dot(m_ref[j], u, preferred_element_type=F32).astype(BF16)
        o_ref[0, :, _lanes(j, c)] = y[:ZCH]
        o_ref[1, :, _lanes(j, c)] = y[ZCH:]


def _dft_tables():
    n2 = np.arange(CH)[:, None, None]
    k1 = np.arange(CH)[None, :, None]

    def theta(n1_count):
        n1 = np.arange(n1_count)[None, None, :]
        return 2.0 * np.pi * ((k1 * (CH * n1 + n2)) % NFFT) / NFFT

    th = theta(ZCH)
    c, s = np.cos(th), np.sin(th)
    m_a = np.concatenate([np.concatenate([c, s], axis=2), np.concatenate([-s, c], axis=2)], axis=1)
    m_ainv = np.transpose(m_a, (0, 2, 1))
    th = theta(CH)
    m_af = np.concatenate([np.cos(th), -np.sin(th)], axis=1)
    a = 2.0 * np.pi * ((np.arange(CH)[:, None] * np.arange(CH)[None, :]) % CH) / CH
    c, s = np.cos(a), np.sin(a)
    f2 = np.block([[c, s], [-s, c]])
    f2i = np.block([[c, -s], [s, c]])
    f = lambda t: jnp.asarray(t.astype(np.float32)).astype(BF16)
    return f(m_a), f(m_ainv), f(m_af), f(f2), f(f2i)


def _fft_conv(zp, g):
    c = zp.shape[-1]
    m_a, m_ainv, m_af, f2, f2i = _dft_tables()
    cp = _cparams(("arbitrary",), VMEM_LIMIT_BIG)
    steps = CH // FFT_G
    gc = FFT_G * c
    full2 = pl.BlockSpec((2 * CH, 2 * CH), lambda i: (0, 0))
    col_re = pl.BlockSpec((CH, gc), lambda i: (0, i))
    col_im = pl.BlockSpec((CH, gc), lambda i: (0, steps + i))
    blk = pl.BlockSpec((FFT_G, 2 * CH, c), lambda i: (i, 0, 0))
    spec_shape = jax.ShapeDtypeStruct((CH, 2 * CH, c), BF16)

    yg = pl.pallas_call(
        _fft_a_filt_kernel, grid=(steps,),
        in_specs=[pl.BlockSpec((FFT_G, 2 * CH, CH), lambda i: (i, 0, 0)),
                  pl.BlockSpec((FFT_G * CH, c), lambda i: (i, 0))],
        out_specs=blk, out_shape=spec_shape, compiler_params=cp, name="fft_a_filt",
    )(m_af, g)
    yg2 = yg.reshape(CH, 2 * CH * c)

    y = pl.pallas_call(
        _fft_a_data_kernel, grid=(steps,),
        in_specs=[pl.BlockSpec((FFT_G, 2 * CH, 2 * ZCH), lambda i: (i, 0, 0)),
                  pl.BlockSpec((2, FFT_G * ZCH, c), lambda i: (0, i, 0))],
        out_specs=blk, out_shape=spec_shape, compiler_params=cp, name="fft_a_data",
    )(m_a, zp)
    y2 = y.reshape(CH, 2 * CH * c)
    u = pl.pallas_call(
        _fft_b_kernel, grid=(steps,),
        in_specs=[full2, full2, col_re, col_im, col_re, col_im],
        out_specs=blk, out_shape=spec_shape, compiler_params=cp, name="fft_b",
    )(f2, f2i, y2, y2, yg2, yg2)
    u2 = u.reshape(CH, 2 * CH * c)
    yc = pl.pallas_call(
        _fft_a_inv_kernel, grid=(steps,),
        in_specs=[pl.BlockSpec((FFT_G, 2 * ZCH, 2 * CH), lambda i: (i, 0, 0)), col_re, col_im],
        out_specs=pl.BlockSpec((2, ZCH, gc), lambda i: (0, 0, i)),
        out_shape=jax.ShapeDtypeStruct((2, ZCH, CH * c), BF16),
        compiler_params=cp, name="fft_a_inv",
    )(m_ainv, u2, u2)
    return yc.reshape(2, ZCH * CH, c)


def _far_kernel(far_ref, zm_ref, zl_ref, o_ref):
    g_hi = far_ref[HALF + FAR:HALF + 2 * FAR, :]
    ef = far_ref[0:FAR, :] - g_hi
    g_lo = far_ref[FAR:2 * FAR, :]
    hb = far_ref[HALF:HALF + FAR, :]
    zm = zm_ref[0, PAD:CH, :].astype(F32)
    zl = zl_ref[0, PAD:CH, :].astype(F32)
    row = lambda a, i: a[i:i + 1, :]
    eb = [row(hb, 0) - row(g_hi, 0)] + [row(hb, m) - row(g_lo, FAR - m) for m in range(1, FAR)]
    for j in range(FAR):
        acc_f = row(ef, j) * row(zm, 0)
        for i in range(1, j + 1):
            acc_f = acc_f + row(ef, j - i) * row(zm, i)
        o_ref[0, 0, j:j + 1, :] = acc_f
        acc_b = eb[0] * row(zl, j)
        for i in range(j + 1, FAR):
            acc_b = acc_b + eb[i - j] * row(zl, i)
        o_ref[0, 1, j:j + 1, :] = acc_b


def _far_correction(h_far, z, tp):
    b, _, c = z.shape
    last = tp // CH - 1
    return pl.pallas_call(
        _far_kernel, grid=(b,),
        in_specs=[pl.BlockSpec((CH, c), lambda bi: (0, 0)),
                  pl.BlockSpec((1, CH, c), lambda bi: (bi, 0, 0)),
                  pl.BlockSpec((1, CH, c), lambda bi: (bi, last, 0))],
        out_specs=pl.BlockSpec((1, 2, FAR, c), lambda bi: (bi, 0, 0, 0)),
        out_shape=jax.ShapeDtypeStruct((b, 2, FAR, c), F32),
        compiler_params=_cparams(("arbitrary",)),
        name="hy_far",
    )(h_far, z, z)


MG_TM = 640
MG_CN = 256
ROUTER_W = 128
AFF_PACK = ROUTER_W // N_EXPERTS
ROW_SUB, ROW_LANE = 8, 128


def _merge_kernel(h_ref, og_ref, x0_ref, z_ref, yc_ref, ga_ref, gb_ref, corr_ref, skip_ref,
                  wr_ref, wh_ref, wo_ref, g2_ref, wrt_ref,
                  h1_ref, xn_ref, aff_ref, affp_ref, pre_ref, mix_ref, h1s_ref, *, nt_b):
    i = pl.program_id(0)
    ib = i % nt_b
    pre_ref[...] = yc_ref[0].astype(F32) + z_ref[0].astype(F32) * skip_ref[...]

    @pl.when(ib == 0)
    def _():
        pre_ref[PAD:CH, :] += corr_ref[0, 1]

    @pl.when(ib == nt_b - 1)
    def _():
        pre_ref[MG_TM - FAR:MG_TM, :] += corr_ref[0, 0]

    chunks = [slice(c * MG_CN, (c + 1) * MG_CN) for c in range(D_MODEL // MG_CN)]
    og = og_ref[0]
    pre = (x0_ref[0].astype(F32) * pre_ref[...]).astype(BF16)
    for cols in chunks:
        ya = jnp.dot(og, wr_ref[:, cols], preferred_element_type=F32)
        yb = jnp.dot(pre, wh_ref[:, cols], preferred_element_type=F32)
        mix_ref[:, cols] = (ga_ref[0, :, cols].astype(F32) * ya + gb_ref[0, :, cols].astype(F32) * yb).astype(BF16)
    mixed = mix_ref[...]
    ss = jnp.zeros((MG_TM, 1), F32)
    for cols in chunks:
        h1 = h_ref[0, :, cols] + jnp.dot(mixed, wo_ref[:, cols], preferred_element_type=F32)
        h1s_ref[:, cols] = h1
        ss = ss + jnp.sum(h1 * h1, axis=-1, keepdims=True)
    rinv = lax.rsqrt(ss * (1.0 / D_MODEL) + RMS_EPS)
    h1 = h1s_ref[...]
    h1_ref[0] = h1.reshape(MG_TM, ROW_SUB, ROW_LANE)
    xn = h1 * rinv * g2_ref[...]
    xn_ref[0] = xn.reshape(MG_TM, ROW_SUB, ROW_LANE)
    logits = _dot3(xn, wrt_ref[0], wrt_ref[1])
    lane = lax.broadcasted_iota(I32, logits.shape, 1)
    logits = jnp.where(lane < N_EXPERTS, logits, -jnp.inf)
    m = jnp.max(logits, axis=-1, keepdims=True)
    e = jnp.exp(logits - m)
    aff = e / jnp.sum(e, axis=-1, keepdims=True)
    rows = lax.broadcasted_iota(I32, logits.shape, 0) + ib * MG_TM
    aff_ref[0] = jnp.where(rows >= PAD, aff, -1.0)
    lane_grp = lax.broadcasted_iota(I32, (MG_TM // AFF_PACK, ROUTER_W), 1) // N_EXPERTS
    packed = jnp.zeros((MG_TM // AFF_PACK, ROUTER_W), F32)
    for g in range(AFF_PACK):
        blk = aff_ref[0, pl.ds(g, MG_TM // AFF_PACK, stride=AFF_PACK), :]
        if g:
            blk = pltpu.roll(blk, N_EXPERTS * g, axis=1)
        packed = jnp.where(lane_grp == g, blk, packed)
    affp_ref[0] = packed


def _merge(h0, og, x0c, z, yc, proj3, corr, skip, w_ret, w_hy, w_o, g2, w_router_p, tp):
    b = h0.shape[0]
    nt_b = tp // MG_TM
    ga_blk0 = 6144 // D_MODEL
    row = lambda w: pl.BlockSpec((1, MG_TM, w), lambda i: (i // nt_b, i % nt_b, 0))
    full = lambda a: pl.BlockSpec(a.shape, lambda i: (0,) * a.ndim)
    return pl.pallas_call(
        functools.partial(_merge_kernel, nt_b=nt_b),
        grid=(b * nt_b,),
        in_specs=[row(D_MODEL), row(D_MODEL), row(D_MODEL), row(D_MODEL), row(D_MODEL),
                  pl.BlockSpec((1, MG_TM, D_MODEL), lambda i: (i // nt_b, i % nt_b, ga_blk0)),
                  pl.BlockSpec((1, MG_TM, D_MODEL), lambda i: (i // nt_b, i % nt_b, ga_blk0 + 1)),
                  pl.BlockSpec((1, 2, FAR, D_MODEL), lambda i: (i // nt_b, 0, 0, 0)),
                  full(skip), full(w_ret), full(w_hy), full(w_o), full(g2), full(w_router_p)],
        out_specs=[pl.BlockSpec((1, MG_TM, ROW_SUB, ROW_LANE), lambda i: (i // nt_b, i % nt_b, 0, 0)),
                   pl.BlockSpec((1, MG_TM, ROW_SUB, ROW_LANE), lambda i: (i // nt_b, i % nt_b, 0, 0)),
                   row(ROUTER_W),
                   pl.BlockSpec((1, MG_TM // AFF_PACK, ROUTER_W), lambda i: (i // nt_b, i % nt_b, 0))],
        out_shape=[jax.ShapeDtypeStruct((b, tp, ROW_SUB, ROW_LANE), F32),
                   jax.ShapeDtypeStruct((b, tp, ROW_SUB, ROW_LANE), F32),
                   jax.ShapeDtypeStruct((b, tp, ROUTER_W), F32),
                   jax.ShapeDtypeStruct((b, tp // AFF_PACK, ROUTER_W), F32)],
        scratch_shapes=[pltpu.VMEM((MG_TM, D_MODEL), F32), pltpu.VMEM((MG_TM, D_MODEL), BF16),
                        pltpu.VMEM((MG_TM, D_MODEL), F32)],
        compiler_params=_cparams(("arbitrary",), VMEM_LIMIT_BIG),
        name="merge",
    )(h0, og, x0c, z, yc, proj3, proj3, corr, skip, w_ret, w_hy, w_o, g2, w_router_p)


TK_W = 128


TK_UNROLL = 5


def _select_kernel(aff_ref, affp_ref, low_ref, slot_ref, offs_ref, *, cap, nch):
    def chunk(c):
        r0 = pl.multiple_of(c * CH, CH)
        return aff_ref[0, pl.ds(r0, CH), :]

    def count(pred):
        cnt = jnp.sum(pred(affp_ref[0]).astype(I32), axis=0, keepdims=True)
        for shift in (N_EXPERTS, 2 * N_EXPERTS, 4 * N_EXPERTS):
            cnt = cnt + pltpu.roll(cnt, shift, axis=1)
        return cnt

    def search(it, bits):
        cand = bits | jnp.left_shift(1, 29 - it)
        cand_f = pltpu.bitcast(cand, F32)
        return jnp.where(count(lambda a: a >= cand_f) >= cap, cand, bits)

    thr = pltpu.bitcast(lax.fori_loop(0, 30, search, jnp.zeros((1, ROUTER_W), I32)), F32)
    need = (cap - count(lambda a: a > thr)).astype(F32)
    low = low_ref[...]

    def scan(it, carry):
        c_eq, c_sel = carry
        for u in range(TK_UNROLL):
            c = it * TK_UNROLL + u
            a = chunk(c)
            eq = a == thr
            eq_f = eq.astype(F32)
            eq_rank = jnp.dot(low, eq_f, preferred_element_type=F32) + c_eq
            sel = jnp.logical_or(a > thr, jnp.logical_and(eq, eq_rank < need))
            sel_f = sel.astype(F32)
            slot = jnp.dot(low, sel_f, preferred_element_type=F32) + c_sel
            r0 = pl.multiple_of(c * CH, CH)
            slot_ref[0, pl.ds(r0, CH), :] = jnp.where(sel, slot, -1.0).astype(I32)
            offs_ref[0, c] = c_sel.astype(I32)
            c_eq = c_eq + jnp.sum(eq_f, axis=0, keepdims=True)
            c_sel = c_sel + jnp.sum(sel_f, axis=0, keepdims=True)
        return c_eq, c_sel

    zero = jnp.zeros((1, ROUTER_W), F32)
    lax.fori_loop(0, nch // TK_UNROLL, scan, (zero, zero))


def _extract_kernel(offs_ref, slot_ref, aff_ref, idx_ref, gate_ref, *, nch, nwin):
    idx_ref[...] = jnp.zeros_like(idx_ref)
    gate_ref[...] = jnp.zeros_like(gate_ref)
    lane = lax.broadcasted_iota(I32, (CH, TK_W), 1)
    trow = lax.broadcasted_iota(I32, (CH, TK_W), 0)

    def per_chunk(c, carry):
        r0 = pl.multiple_of(c * CH, CH)
        slots = slot_ref[0, pl.ds(r0, CH), :]
        affs = aff_ref[0, pl.ds(r0, CH), :]
        tpos = (trow + r0).astype(F32)
        for e in range(N_EXPERTS):
            col = jnp.broadcast_to(slots[:, e:e + 1], (CH, TK_W))
            gcol = jnp.broadcast_to(affs[:, e:e + 1], (CH, TK_W))
            w0 = offs_ref[0, 0, c * N_EXPERTS + e] // TK_W
            for dw in range(2):
                w = w0 + dw
                base = jnp.where(w < nwin, w * TK_W, -2 * TK_W)
                hit = col == lane + base
                row = e * nwin + jnp.minimum(w, nwin - 1)
                idx_ref[row] += jnp.sum(jnp.where(hit, tpos, 0.0), axis=0, keepdims=True)
                gate_ref[row] += jnp.sum(jnp.where(hit, gcol, 0.0), axis=0, keepdims=True)
        return carry

    lax.fori_loop(0, nch, per_chunk, 0)


def _topk(aff, aff_packed, cap, slots):
    b, tp, _ = aff.shape
    nch = tp // CH
    nwin = -(-slots // TK_W)
    low = jnp.asarray(np.tril(np.ones((CH, CH), np.float32), k=-1))
    pack = AFF_PACK
    assert nch % TK_UNROLL == 0 and tp % pack == 0
    slot, offs = pl.pallas_call(
        functools.partial(_select_kernel, cap=cap, nch=nch),
        grid=(b,),
        in_specs=[pl.BlockSpec((1, tp, ROUTER_W), lambda bi: (bi, 0, 0)),
                  pl.BlockSpec((1, tp // pack, ROUTER_W), lambda bi: (bi, 0, 0)),
                  pl.BlockSpec((CH, CH), lambda bi: (0, 0))],
        out_specs=[pl.BlockSpec((1, tp, ROUTER_W), lambda bi: (bi, 0, 0)),
                   pl.BlockSpec((1, nch, 1, ROUTER_W), lambda bi: (bi, 0, 0, 0))],
        out_shape=[jax.ShapeDtypeStruct((b, tp, ROUTER_W), I32),
                   jax.ShapeDtypeStruct((b, nch, 1, ROUTER_W), I32)],
        compiler_params=_cparams(("arbitrary",)),
        name="topk_select",
    )(aff, aff_packed, low)
    offs_s = offs[:, :, 0, :N_EXPERTS].reshape(b, 1, nch * N_EXPERTS)
    rows = pl.BlockSpec((None, N_EXPERTS * nwin, 1, TK_W), lambda bi: (bi, 0, 0, 0))
    out = jax.ShapeDtypeStruct((b, N_EXPERTS * nwin, 1, TK_W), F32)
    idx, gate = pl.pallas_call(
        functools.partial(_extract_kernel, nch=nch, nwin=nwin),
        grid=(b,),
        in_specs=[pl.BlockSpec((1, 1, nch * N_EXPERTS), lambda bi: (bi, 0, 0), memory_space=pltpu.SMEM),
                  pl.BlockSpec((1, tp, ROUTER_W), lambda bi: (bi, 0, 0)),
                  pl.BlockSpec((1, tp, ROUTER_W), lambda bi: (bi, 0, 0))],
        out_specs=[rows, rows],
        out_shape=[out, out],
        compiler_params=_cparams(("arbitrary",)),
        name="topk_extract",
    )(offs_s, slot, aff)
    idx = idx.reshape(b * N_EXPERTS, 1, nwin * TK_W)[:, :, :slots].astype(I32)
    gate = gate.reshape(b * N_EXPERTS, 1, nwin * TK_W)[:, :, :slots]
    return idx, gate


FF_TF = 1024
MOE_CN = 256


MOE_UNROLL = 4


def _moe_kernel(idx_ref, idx_next_ref, xn_hbm, wg_ref, wu_ref, wd_ref, o_ref,
                xe32_ref, xe_ref, hid_ref, acc_ref, sem,
                *, slots, nf):
    i = pl.program_id(0)
    f = pl.program_id(1)
    buf = i % 2
    share = slots // nf

    def row_copy(ids_ref, s, b):
        return pltpu.make_async_copy(xn_hbm.at[ids_ref[0, 0, s]], xe32_ref.at[b, s], sem.at[b])

    def for_rows(lo, n, fn):
        def body(k, carry):
            for u in range(MOE_UNROLL):
                fn(lo + k * MOE_UNROLL + u)
            return carry

        lax.fori_loop(0, n // MOE_UNROLL, body, 0)

    @pl.when(jnp.logical_and(i == 0, f == 0))
    def _():
        for_rows(0, slots, lambda s: row_copy(idx_ref, s, 0).start())

    @pl.when(f == 0)
    def _():
        for_rows(0, slots, lambda s: row_copy(idx_ref, s, buf).wait())
        xe_ref[...] = xe32_ref[buf].reshape(slots, D_MODEL).astype(BF16)

    @pl.when(f == 0)
    def _():
        acc_ref[...] = jnp.zeros_like(acc_ref)

    n_up, n_down = FF_TF // MOE_CN, D_MODEL // MOE_CN
    per_chunk = share // (n_up + n_down)

    def prefetch(chunk):
        base = f * share + chunk * per_chunk
        for u in range(per_chunk):
            row_copy(idx_next_ref, base + u, 1 - buf).start()

    xe = xe_ref[...]
    for c in range(n_up):
        cols = slice(c * MOE_CN, (c + 1) * MOE_CN)
        gg = jnp.dot(xe, wg_ref[0, :, cols].astype(BF16), preferred_element_type=F32)
        uu = jnp.dot(xe, wu_ref[0, :, cols].astype(BF16), preferred_element_type=F32)
        hid_ref[:, cols] = (gg * jax.nn.sigmoid(gg) * uu).astype(BF16)
        prefetch(c)
    hid = hid_ref[...]
    for c in range(n_down):
        cols = slice(c * MOE_CN, (c + 1) * MOE_CN)
        part = jnp.dot(hid, wd_ref[0, :, cols].astype(BF16), preferred_element_type=F32)
        acc_ref[:, cols] = acc_ref[:, cols] + part
        prefetch(n_up + c)

    @pl.when(f == pl.num_programs(1) - 1)
    def _():
        o_ref[0] = acc_ref[...].reshape(slots, ROW_SUB, ROW_LANE)

    @pl.when(jnp.logical_and(i == pl.num_programs(0) - 1, f == pl.num_programs(1) - 1))
    def _():
        for_rows(0, slots, lambda s: row_copy(idx_next_ref, s, 1 - buf).wait())


def _moe_ffn(idx, xn_flat, w_gate, w_up, w_down, slots):
    be = idx.shape[0]
    nf = D_FF // FF_TF
    assert slots % MOE_UNROLL == 0 and slots % (nf * (FF_TF // MOE_CN + D_MODEL // MOE_CN)) == 0
    return pl.pallas_call(
        functools.partial(_moe_kernel, slots=slots, nf=nf),
        grid=(be, nf),
        in_specs=[pl.BlockSpec((1, 1, slots), lambda i, f: (i, 0, 0), memory_space=pltpu.SMEM),
                  pl.BlockSpec((1, 1, slots), lambda i, f: (jnp.minimum(i + 1, be - 1), 0, 0),
                               memory_space=pltpu.SMEM),
                  pl.BlockSpec(memory_space=pl.ANY),
                  pl.BlockSpec((1, D_MODEL, FF_TF), lambda i, f: (i % N_EXPERTS, 0, f)),
                  pl.BlockSpec((1, D_MODEL, FF_TF), lambda i, f: (i % N_EXPERTS, 0, f)),
                  pl.BlockSpec((1, FF_TF, D_MODEL), lambda i, f: (i % N_EXPERTS, f, 0))],
        out_specs=pl.BlockSpec((1, slots, ROW_SUB, ROW_LANE), lambda i, f: (i, 0, 0, 0)),
        out_shape=jax.ShapeDtypeStruct((be, slots, ROW_SUB, ROW_LANE), F32),
        scratch_shapes=[pltpu.VMEM((2, slots, ROW_SUB, ROW_LANE), F32), pltpu.VMEM((slots, D_MODEL), BF16),
                        pltpu.VMEM((slots, FF_TF), BF16), pltpu.VMEM((slots, D_MODEL), F32),
                        pltpu.SemaphoreType.DMA((2,))],
        compiler_params=_cparams(("arbitrary", "arbitrary"), VMEM_LIMIT_BIG),
        name="moe_ffn",
    )(idx, idx, xn_flat, w_gate, w_up, w_down)


CB_U = 6
CB_NB = 512


def _combine_kernel(idx_ref, gate_ref, h1_hbm, ye_ref, g_ref, o_hbm, acc_ref, stage_ref, sem, osem, *, cap, tp):
    b = pl.program_id(0)
    e = pl.program_id(1)

    @pl.when(e == 0)
    def _():
        cp = pltpu.make_async_copy(h1_hbm.at[b], acc_ref, sem)
        cp.start()
        cp.wait()

    def rmw(g, carry):
        ts = [idx_ref[0, 0, g * CB_U + u] for u in range(CB_U)]
        vals = [acc_ref[ts[u]] + gate_ref[0, 0, g * CB_U + u] * ye_ref[g * CB_U + u] for u in range(CB_U)]
        for u in range(CB_U):
            acc_ref[ts[u]] = vals[u]
        return carry

    lax.fori_loop(0, cap // CB_U, rmw, 0)

    @pl.when(e == pl.num_programs(1) - 1)
    def _():
        gamma = g_ref[...]
        n_blocks = (tp - CH) // CB_NB

        def out_copy(k):
            return pltpu.make_async_copy(stage_ref.at[k % 2], o_hbm.at[b, pl.ds(k * CB_NB, CB_NB), :],
                                         osem.at[k % 2])

        for k in range(n_blocks):
            x = acc_ref[pl.ds(CH + k * CB_NB, CB_NB)]
            ms = jnp.sum(jnp.sum(x * x, axis=2, keepdims=True), axis=1, keepdims=True) * (1.0 / D_MODEL)
            y = x * lax.rsqrt(ms + RMS_EPS) * gamma
            if k >= 2:
                out_copy(k - 2).wait()
            stage_ref[k % 2] = y.reshape(CB_NB, D_MODEL)
            out_copy(k).start()
        for k in range(max(n_blocks - 2, 0), n_blocks):
            out_copy(k).wait()


def _combine(idx_local, gate, h1, ye, gf, cap, slots):
    b, tp = h1.shape[:2]
    assert cap % CB_U == 0 and (tp - CH) % CB_NB == 0
    return pl.pallas_call(
        functools.partial(_combine_kernel, cap=cap, tp=tp),
        grid=(b, N_EXPERTS),
        in_specs=[pl.BlockSpec((1, 1, slots), lambda bi, e: (bi * N_EXPERTS + e, 0, 0), memory_space=pltpu.SMEM),
                  pl.BlockSpec((1, 1, slots), lambda bi, e: (bi * N_EXPERTS + e, 0, 0), memory_space=pltpu.SMEM),
                  pl.BlockSpec(memory_space=pl.ANY),
                  pl.BlockSpec((None, slots, ROW_SUB, ROW_LANE), lambda bi, e: (bi * N_EXPERTS + e, 0, 0, 0)),
                  pl.BlockSpec((1, ROW_SUB, ROW_LANE), lambda bi, e: (0, 0, 0))],
        out_specs=pl.BlockSpec(memory_space=pl.ANY),
        out_shape=jax.ShapeDtypeStruct((b, tp - CH, D_MODEL), F32),
        scratch_shapes=[pltpu.VMEM((tp, ROW_SUB, ROW_LANE), F32), pltpu.VMEM((2, CB_NB, D_MODEL), F32),
                        pltpu.SemaphoreType.DMA(()), pltpu.SemaphoreType.DMA((2,))],
        compiler_params=_cparams(("arbitrary", "arbitrary"), VMEM_LIMIT_BIG),
        name="combine",
    )(idx_local, gate, h1, ye, gf.reshape(1, ROW_SUB, ROW_LANE))


def _rope_tables(tp, b):
    half = RET_QK_HEAD // 2
    pos = np.arange(tp, dtype=np.float64) - PAD
    inv = ROPE_BASE ** (-np.arange(half, dtype=np.float64) / half)
    ang = pos[:, None] * inv[None, :]
    cos, sin = np.cos(ang), np.sin(ang)
    tile = lambda t: jnp.asarray(np.tile(t, (b, 1)).astype(np.float32))
    return tile(np.concatenate([cos, cos], axis=1)), tile(np.concatenate([-sin, sin], axis=1))


def _filter_features(t_len):
    half = NFFT // 2
    q = np.arange(NFFT)
    r = CH * (q % CH) + q // CH
    p_main = np.where(r < half, r, NFFT - r)
    valid_main = (r != half).astype(np.float32)
    m = np.arange(FAR)
    pad = np.zeros(HALF - 2 * FAR, np.int64)
    p_far = np.concatenate([half + m, half - FAR + m, pad, half + m, half - m, pad])
    valid_far = np.ones(CH, np.float32)
    valid_far[HALF + FAR] = 0.0
    p = np.concatenate([p_main, p_far]).astype(np.float64)
    valid = np.concatenate([valid_main, valid_far]).astype(np.float64)
    t_norm = p / (t_len - 1)
    bands = (HY_EMB_DIM - 1) // 2
    fr = np.linspace(1e-4, bands - 1, bands)
    ang = (2.0 * math.pi * p / t_len)[:, None] * fr[None, :]
    feat = np.concatenate([t_norm[:, None], np.cos(ang), -np.sin(ang), valid[:, None]], axis=-1)
    feat = np.pad(feat, ((0, 0), (0, FEAT_W - feat.shape[1])))
    pairs = feat.reshape(-1, 2, HALF, FEAT_W).transpose(0, 2, 1, 3).reshape(-1, 2 * FEAT_W)
    return jnp.asarray(pairs.astype(np.float32))


def kernel(x, meta_tokens, norm1_g, w_in, ret_decay_fwd, ret_decay_bwd, ret_head_norm_g, w_ret_out,
           hy_conv_w, hy_conv_b, hy_filt_w1, hy_filt_b1, hy_filt_w2, hy_filt_b2, hy_filt_w3, hy_filt_b3,
           hy_filt_freq, hy_filt_w4, hy_skip, w_hy_out, w_o, norm2_g, w_router, w_exp_gate, w_exp_up,
           w_exp_down, final_norm_g):
    b, seq, d = x.shape
    t_len = seq + N_META
    tp = PAD + t_len
    assert d == D_MODEL and (b * tp) % IP_TM == 0 and tp % MG_TM == 0 and tp % CH == 0 and t_len - NFFT // 2 == FAR
    cap = EC_CAPACITY * t_len // N_EXPERTS
    slots = -(-cap // 16) * 16
    l = 0

    meta = jnp.broadcast_to(meta_tokens[None].astype(x.dtype), (b, N_META, d))
    h0 = jnp.concatenate([jnp.zeros((b, PAD, d), x.dtype), meta, x], axis=1)

    cs, sn = _rope_tables(tp, b)
    proj = _in_proj(h0.reshape(b * tp, d), norm1_g[l][None], w_in[l], cs, sn, tp)
    proj3 = proj.reshape(b, tp, IN_PROJ_W)

    lf = jax.nn.log_sigmoid(ret_decay_fwd[l].astype(F32))
    lb = jax.nn.log_sigmoid(ret_decay_bwd[l].astype(F32))
    og = _retention(proj3, lf, lb, ret_head_norm_g[l][None], tp)

    x0c, z, zp = _hy_prep(proj3, hy_conv_w[l], hy_conv_b[l][None], tp)

    feat = _filter_features(t_len)
    w1p = jnp.pad(hy_filt_w1[l].astype(F32), ((0, FEAT_W - HY_EMB_DIM), (0, 0)))
    max_decay = math.log(HY_DECAY_TARGET) / HY_FAST_DECAY_PCT
    min_decay = math.log(HY_DECAY_TARGET) / HY_SLOW_DECAY_PCT
    dl = jnp.abs(jnp.linspace(min_decay, max_decay, D_MODEL, dtype=F32))[None]
    fargs = (w1p, hy_filt_b1[l][None].astype(F32), hy_filt_w2[l].astype(F32), hy_filt_b2[l][None].astype(F32),
             hy_filt_w3[l].astype(F32), hy_filt_b3[l][None].astype(F32), hy_filt_freq[l][None].astype(F32),
             hy_filt_w4[l].astype(F32), dl)
    g = _filters(feat[:NFFT // 2], *fargs, FT_ROWS)
    h_far = _filters(feat[NFFT // 2:], *fargs, CH)
    yc = _fft_conv(zp, g)
    corr = _far_correction(h_far, z, tp)

    w_router_p = jnp.pad(w_router[l].astype(F32), ((0, 0), (0, ROUTER_W - N_EXPERTS)))
    w_router_p = _split_hi_lo(w_router_p)
    h1, xn2, aff, aff_packed = _merge(h0, og, x0c, z, yc, proj3, corr, hy_skip[l][None].astype(F32),
                          w_ret_out[l].astype(BF16), w_hy_out[l].astype(BF16), w_o[l].astype(BF16),
                          norm2_g[l][None].astype(F32), w_router_p, tp)

    idx, gate = _topk(aff, aff_packed, cap, slots)
    live = (jnp.arange(slots) < cap)[None, None, :]
    idx_local = jnp.where(live, idx, PAD)
    idx_flat = idx_local + (jnp.arange(b * N_EXPERTS, dtype=I32) // N_EXPERTS * tp)[:, None, None]
    ye = _moe_ffn(idx_flat, xn2.reshape(b * tp, ROW_SUB, ROW_LANE),
                  w_exp_gate[l], w_exp_up[l], w_exp_down[l], slots)
    return _combine(idx_local, gate, h1, ye, final_norm_g.astype(F32), cap, slots)
```

```python
import functools
import math

import numpy as np
import jax
import jax.numpy as jnp
from jax import lax
from jax.experimental import pallas as pl
from jax.experimental.pallas import tpu as pltpu

F32 = jnp.float32
BF16 = jnp.bfloat16
I32 = jnp.int32

D_MODEL = 1024
N_META = 16
RET_HEADS = 4
RET_QK_HEAD = 128
RET_V_HEAD = 256
ROPE_BASE = 10000.0
HY_EMB_DIM = 33
HY_FILTER_ORDER = 64
HY_FAST_DECAY_PCT = 0.3
HY_SLOW_DECAY_PCT = 1.5
HY_DECAY_TARGET = 1e-2
N_EXPERTS = 16
EC_CAPACITY = 2
D_FF = 2 * D_MODEL
RMS_EPS = 1e-6
IN_PROJ_W = 8192

CH = 128
PAD = CH - N_META
NFFT = 16384
ZCH = 80
FAR = 16

VMEM_LIMIT_BIG = 56 * 1024 * 1024
VMEM_LIMIT_MID = 40 * 1024 * 1024


def _cparams(sem, vmem=VMEM_LIMIT_MID):
    return pltpu.CompilerParams(dimension_semantics=sem, vmem_limit_bytes=vmem)


def _split_hi_lo(w):
    w = w.astype(F32)
    hi = w.astype(BF16)
    return jnp.stack([hi, (w - hi.astype(F32)).astype(BF16)])


def _dot3(a, w_hi, w_lo):
    a_hi = a.astype(BF16)
    a_lo = (a - a_hi.astype(F32)).astype(BF16)
    return (jnp.dot(a_hi, w_hi, preferred_element_type=F32) + jnp.dot(a_lo, w_hi, preferred_element_type=F32)
            + jnp.dot(a_hi, w_lo, preferred_element_type=F32))


IP_TM = 1280
IP_TN = 1024
IP_CN = 256


def _inproj_kernel(x_ref, g_ref, w_ref, cs_ref, sn_ref, o_ref, xn_ref):
    j = pl.program_id(1)

    @pl.when(j == 0)
    def _():
        x = x_ref[...]
        ms = jnp.mean(x * x, axis=-1, keepdims=True)
        xn_ref[...] = (x * lax.rsqrt(ms + RMS_EPS) * g_ref[...]).astype(BF16)

    def run(epilogue):
        for c in range(IP_TN // IP_CN):
            cols = slice(c * IP_CN, (c + 1) * IP_CN)
            acc = jnp.dot(xn_ref[...], w_ref[:, cols].astype(BF16), preferred_element_type=F32)
            epilogue(c, cols, acc)

    def rotary(c, cols, acc):
        scale = 1.0 if c < (IP_TN // IP_CN) // 2 else RET_QK_HEAD ** -0.5
        cs = cs_ref[...] * scale
        sn = sn_ref[...] * scale
        for hh in range(IP_CN // RET_QK_HEAD):
            xh = acc[:, hh * RET_QK_HEAD:(hh + 1) * RET_QK_HEAD]
            rot = xh * cs + pltpu.roll(xh, RET_QK_HEAD // 2, axis=1) * sn
            lo = c * IP_CN + hh * RET_QK_HEAD
            o_ref[:, lo:lo + RET_QK_HEAD] = rot.astype(BF16)

    def raw(c, cols, acc):
        o_ref[:, cols] = acc.astype(BF16)

    def swish(c, cols, acc):
        o_ref[:, cols] = (acc * jax.nn.sigmoid(acc)).astype(BF16)

    def sigm(c, cols, acc):
        o_ref[:, cols] = jax.nn.sigmoid(acc).astype(BF16)

    pl.when(j == 0)(lambda: run(rotary))
    pl.when(jnp.logical_or(j == 1, jnp.logical_and(j >= 3, j < 6)))(lambda: run(raw))
    pl.when(j == 2)(lambda: run(swish))
    pl.when(j >= 6)(lambda: run(sigm))


def _in_proj(h0, g1, w_in_bf, cs, sn, tp):
    n_rows = h0.shape[0]
    return pl.pallas_call(
        _inproj_kernel,
        grid=(n_rows // IP_TM, IN_PROJ_W // IP_TN),
        in_specs=[
            pl.BlockSpec((IP_TM, D_MODEL), lambda i, j: (i, 0)),
            pl.BlockSpec((1, D_MODEL), lambda i, j: (0, 0)),
            pl.BlockSpec((D_MODEL, IP_TN), lambda i, j: (0, j)),
            pl.BlockSpec((IP_TM, RET_QK_HEAD), lambda i, j: (i, 0)),
            pl.BlockSpec((IP_TM, RET_QK_HEAD), lambda i, j: (i, 0)),
        ],
        out_specs=pl.BlockSpec((IP_TM, IP_TN), lambda i, j: (i, j)),
        out_shape=jax.ShapeDtypeStruct((n_rows, IN_PROJ_W), BF16),
        scratch_shapes=[pltpu.VMEM((IP_TM, D_MODEL), BF16)],
        compiler_params=_cparams(("arbitrary", "arbitrary")),
        name="in_proj",
    )(h0, g1, w_in_bf, cs, sn)


RET_UNROLL = 13


def _ret_kernel(lf_ref, lb_ref, q_ref, k_ref, v_ref, gr_ref, gn_ref, o_ref, ob_ref, s_ref, *, nch):
    h = pl.program_id(1)
    lf = lf_ref[h]
    lb = lb_ref[h]
    ri = lax.broadcasted_iota(I32, (CH, CH), 0).astype(F32)
    ci = lax.broadcasted_iota(I32, (CH, CH), 1).astype(F32)
    diff = ri - ci
    mask = jnp.exp(jnp.where(diff >= 0, lf * diff, -lb * diff))
    w_end = jnp.exp(lf * (CH - 1.0 - ri))
    w_start = jnp.exp(lb * ri)
    qw_f = jnp.exp(lf * (ri + 1.0))
    qw_b = jnp.exp(lb * (CH - ri))
    dec_f = jnp.exp(jnp.full((CH, RET_V_HEAD), lf * CH, F32))
    dec_b = jnp.exp(jnp.full((CH, RET_V_HEAD), lb * CH, F32))
    tn_dims = (((0,), (0,)), ((), ()))
    nt_dims = (((1,), (1,)), ((), ()))

    s_ref[...] = jnp.zeros_like(s_ref)

    def bwd(it, carry):
        s = s_ref[...]
        for u in range(RET_UNROLL):
            n = nch - 1 - (it * RET_UNROLL + u)
            r0 = pl.multiple_of(n * CH, CH)
            q = q_ref[0, pl.ds(r0, CH), :].astype(F32)
            k = k_ref[0, pl.ds(r0, CH), :].astype(F32)
            v = v_ref[0, pl.ds(r0, CH), :]
            ob_ref[pl.ds(r0, CH), :] = jnp.dot((q * qw_b).astype(BF16), s.astype(BF16),
                                               preferred_element_type=F32)
            a = lax.dot_general((k * w_start).astype(BF16), v, tn_dims, preferred_element_type=F32)
            s = s * dec_b + a
        s_ref[...] = s
        return carry

    lax.fori_loop(0, nch // RET_UNROLL, bwd, 0)

    s_ref[...] = jnp.zeros_like(s_ref)
    gn = gn_ref[...]

    def fwd(it, carry):
        s = s_ref[...]
        for u in range(RET_UNROLL):
            n = it * RET_UNROLL + u
            r0 = pl.multiple_of(n * CH, CH)
            qb = q_ref[0, pl.ds(r0, CH), :]
            kb = k_ref[0, pl.ds(r0, CH), :]
            v = v_ref[0, pl.ds(r0, CH), :]
            q = qb.astype(F32)
            k = kb.astype(F32)
            scores = lax.dot_general(qb, kb, nt_dims, preferred_element_type=F32) * mask
            o = jnp.dot(scores.astype(BF16), v, preferred_element_type=F32)
            o = o + jnp.dot((q * qw_f).astype(BF16), s.astype(BF16), preferred_element_type=F32)
            o = o + ob_ref[pl.ds(r0, CH), :]
            a = lax.dot_general((k * w_end).astype(BF16), v, tn_dims, preferred_element_type=F32)
            s = s * dec_f + a
            y = o * lax.rsqrt(jnp.mean(o * o, axis=-1, keepdims=True) + RMS_EPS) * gn
            o_ref[0, pl.ds(r0, CH), :] = (y * gr_ref[0, pl.ds(r0, CH), :].astype(F32)).astype(BF16)
        s_ref[...] = s
        return carry

    lax.fori_loop(0, nch // RET_UNROLL, fwd, 0)


def _retention(proj3, lf, lb, gn, tp):
    b = proj3.shape[0]
    nch = tp // CH
    qk_blocks = (RET_HEADS * RET_QK_HEAD) // RET_QK_HEAD
    v_blk0 = (2 * RET_HEADS * RET_QK_HEAD) // RET_V_HEAD
    g_blk0 = v_blk0 + RET_HEADS
    smem = pl.BlockSpec(memory_space=pltpu.SMEM)
    return pl.pallas_call(
        functools.partial(_ret_kernel, nch=nch),
        grid=(b, RET_HEADS),
        in_specs=[
            smem, smem,
            pl.BlockSpec((1, tp, RET_QK_HEAD), lambda bi, h: (bi, 0, h)),
            pl.BlockSpec((1, tp, RET_QK_HEAD), lambda bi, h: (bi, 0, qk_blocks + h)),
            pl.BlockSpec((1, tp, RET_V_HEAD), lambda bi, h: (bi, 0, v_blk0 + h)),
            pl.BlockSpec((1, tp, RET_V_HEAD), lambda bi, h: (bi, 0, g_blk0 + h)),
            pl.BlockSpec((1, RET_V_HEAD), lambda bi, h: (0, h)),
        ],
        out_specs=pl.BlockSpec((1, tp, RET_V_HEAD), lambda bi, h: (bi, 0, h)),
        out_shape=jax.ShapeDtypeStruct((b, tp, RET_HEADS * RET_V_HEAD), BF16),
        scratch_shapes=[pltpu.VMEM((tp, RET_V_HEAD), F32), pltpu.VMEM((RET_QK_HEAD, RET_V_HEAD), F32)],
        compiler_params=_cparams(("arbitrary", "arbitrary"), VMEM_LIMIT_BIG),
        name="retention",
    )(lf, lb, proj3, proj3, proj3, proj3, gn)


HP_CW = 128


def _hyprep_kernel(u0_ref, u1_ref, u2_ref, w0_ref, w1_ref, w2_ref, b0_ref, b1_ref, b2_ref,
                   x0_ref, z_ref, zp_ref, *, nch):
    rows = lax.broadcasted_iota(I32, (CH, HP_CW), 0)
    halo = 16
    zp_ref[...] = jnp.zeros_like(zp_ref)

    def conv(u_ref, w_ref, b_ref, n, r0):
        cur = u_ref[0, pl.ds(r0, CH), :].astype(F32)
        rp = pl.multiple_of(jnp.maximum(r0 - halo, 0), halo)
        rn = pl.multiple_of(jnp.minimum(r0 + CH, (nch - 1) * CH), halo)
        prev = u_ref[0, pl.ds(rp, halo), :].astype(F32)[halo - 1:halo, :]
        nxt = u_ref[0, pl.ds(rn, halo), :].astype(F32)[0:1, :]
        prev = jnp.where(n > 0, prev, 0.0)
        nxt = jnp.where(n < nch - 1, nxt, 0.0)
        up = jnp.where(rows == 0, prev, pltpu.roll(cur, 1, axis=0))
        dn = jnp.where(rows == CH - 1, nxt, pltpu.roll(cur, CH - 1, axis=0))
        w = w_ref[...]
        return up * w[0:1, :] + cur * w[1:2, :] + dn * w[2:3, :] + b_ref[...]

    def body(n, carry):
        r0 = pl.multiple_of(n * CH, CH)
        x0 = conv(u0_ref, w0_ref, b0_ref, n, r0)
        x1 = conv(u1_ref, w1_ref, b1_ref, n, r0)
        vv = conv(u2_ref, w2_ref, b2_ref, n, r0)
        z = jnp.where(rows + r0 >= PAD, x1 * vv, 0.0)
        x0_ref[0, pl.ds(r0, CH), :] = x0.astype(BF16)
        z_ref[0, pl.ds(r0, CH), :] = z.astype(BF16)
        zp_ref[0, pl.ds(n, CH, stride=ZCH), :] = z
        return carry

    lax.fori_loop(0, nch, body, 0)


def _hy_prep(proj3, conv_w, conv_b, tp):
    b = proj3.shape[0]
    nch = tp // CH
    ncb = D_MODEL // HP_CW
    u_blk0 = 3072 // HP_CW
    uspec = lambda s: pl.BlockSpec((1, tp, HP_CW), lambda bi, c: (bi, 0, u_blk0 + s * ncb + c))
    wspec = lambda s: pl.BlockSpec((3, HP_CW), lambda bi, c: (0, s * ncb + c))
    bspec = lambda s: pl.BlockSpec((1, HP_CW), lambda bi, c: (0, s * ncb + c))
    return pl.pallas_call(
        functools.partial(_hyprep_kernel, nch=nch),
        grid=(b, ncb),
        in_specs=[uspec(0), uspec(1), uspec(2), wspec(0), wspec(1), wspec(2), bspec(0), bspec(1), bspec(2)],
        out_specs=[pl.BlockSpec((1, tp, HP_CW), lambda bi, c: (bi, 0, c)),
                   pl.BlockSpec((1, tp, HP_CW), lambda bi, c: (bi, 0, c)),
                   pl.BlockSpec((1, CH * ZCH, HP_CW), lambda bi, c: (bi, 0, c))],
        out_shape=[jax.ShapeDtypeStruct((b, tp, D_MODEL), BF16),
                   jax.ShapeDtypeStruct((b, tp, D_MODEL), BF16),
                   jax.ShapeDtypeStruct((b, CH * ZCH, D_MODEL), F32)],
        compiler_params=_cparams(("arbitrary", "arbitrary")),
        name="hy_prep",
    )(proj3, proj3, proj3, conv_w, conv_w, conv_w, conv_b, conv_b, conv_b)


FEAT_W = 128


FT_ROWS = 512
HALF = CH // 2


def _filter_kernel(feat_ref, w1_ref, b1_ref, w2_ref, b2_ref, w3_ref, b3_ref, fq_ref, w4_ref, dl_ref, o_ref,
                   *, groups):
    feat = feat_ref[...]
    fq = fq_ref[...]
    hdn = jnp.sin(fq * (_dot3(feat, w1_ref[0], w1_ref[1]) + b1_ref[...]))
    hdn = jnp.sin(fq * (_dot3(hdn, w2_ref[0], w2_ref[1]) + b2_ref[...]))
    hdn = jnp.sin(fq * (_dot3(hdn, w3_ref[0], w3_ref[1]) + b3_ref[...]))
    filt = _dot3(hdn, w4_ref[0], w4_ref[1])
    dl = dl_ref[...]
    for d in range(2):
        f0 = d * FEAT_W
        scale = jnp.exp(-feat[:, f0:f0 + 1] * dl) * feat[:, f0 + HY_EMB_DIM:f0 + HY_EMB_DIM + 1]
        fd = filt[:, d * D_MODEL:(d + 1) * D_MODEL] * scale
        for g in range(groups):
            o_ref[g * CH + d * HALF:g * CH + (d + 1) * HALF, :] = fd[g * HALF:(g + 1) * HALF]


def _block_diag2(w):
    z = jnp.zeros_like(w)
    return jnp.concatenate([jnp.concatenate([w, z], axis=1), jnp.concatenate([z, w], axis=1)], axis=0)


def _filters(feat_pairs, w1p, b1, w2, b2, w3, b3, fq, w4, dl, rows):
    n_rows = 2 * feat_pairs.shape[0]
    twice = lambda v: jnp.concatenate([v, v], axis=1)
    c = w4.shape[1] // 2
    w4bd = jnp.concatenate([jnp.concatenate([w4[:, :c], jnp.zeros_like(w4[:, c:])], axis=1),
                            jnp.concatenate([jnp.zeros_like(w4[:, :c]), w4[:, c:]], axis=1)], axis=0)
    args = (_split_hi_lo(_block_diag2(w1p)), twice(b1), _split_hi_lo(_block_diag2(w2)), twice(b2),
            _split_hi_lo(_block_diag2(w3)), twice(b3), twice(fq), _split_hi_lo(w4bd), dl)
    full = lambda a: pl.BlockSpec(a.shape, lambda i: (0,) * a.ndim)
    return pl.pallas_call(
        functools.partial(_filter_kernel, groups=rows // CH),
        grid=(n_rows // rows,),
        in_specs=[pl.BlockSpec((rows // 2, 2 * FEAT_W), lambda i: (i, 0))] + [full(a) for a in args],
        out_specs=pl.BlockSpec((rows, D_MODEL), lambda i: (i, 0)),
        out_shape=jax.ShapeDtypeStruct((n_rows, D_MODEL), F32),
        compiler_params=_cparams(("arbitrary",)),
        name="hy_filter",
    )(feat_pairs, *args)


FFT_G = 8


def _lanes(j, c):
    return slice(j * c, (j + 1) * c)


def _fft_a_data_kernel(m_ref, z_ref, o_ref):
    for j in range(FFT_G):
        rows = slice(j * ZCH, (j + 1) * ZCH)
        x = jnp.concatenate([z_ref[0, rows, :], z_ref[1, rows, :]], axis=0).astype(BF16)
        o_ref[j] = jnp.dot(m_ref[j], x, preferred_element_type=F32).astype(BF16)


def _fft_a_filt_kernel(m_ref, g_ref, o_ref):
    for j in range(FFT_G):
        gj = g_ref[j * CH:(j + 1) * CH, :].astype(BF16)
        o_ref[j] = jnp.dot(m_ref[j], gj, preferred_element_type=F32).astype(BF16)


def _fft_b_kernel(f_ref, fi_ref, yr_ref, yi_ref, gr_ref, gi_ref, o_ref):
    c = o_ref.shape[-1]
    for j in range(FFT_G):
        y = jnp.concatenate([yr_ref[:, _lanes(j, c)], yi_ref[:, _lanes(j, c)]], axis=0)
        yg = jnp.concatenate([gr_ref[:, _lanes(j, c)], gi_ref[:, _lanes(j, c)]], axis=0)
        x = jnp.dot(f_ref[...], y, preferred_element_type=F32)
        g = jnp.dot(f_ref[...], yg, preferred_element_type=F32) * (1.0 / NFFT)
        xr, xi = x[:CH], x[CH:]
        gr, gi = g[:CH], g[CH:]
        p = jnp.concatenate([xr * gr - xi * gi, xr * gi + xi * gr], axis=0).astype(BF16)
        o_ref[j] = jnp.dot(fi_ref[...], p, preferred_element_type=F32).astype(BF16)


def _fft_a_inv_kernel(m_ref, ur_ref, ui_ref, o_ref):
    c = ur_ref.shape[-1] // FFT_G
    for j in range(FFT_G):
        u = jnp.concatenate([ur_ref[:, _lanes(j, c)], ui_ref[:, _lanes(j, c)]], axis=0)
        y = jnp.dot(m_ref[j], u, preferred_element_type=F32)
        o_ref[:, :, j, :] = y.reshape(2, ZCH, c)


def _dft_tables():
    n2 = np.arange(CH)[:, None, None]
    k1 = np.arange(CH)[None, :, None]

    def theta(n1_count):
        n1 = np.arange(n1_count)[None, None, :]
        return 2.0 * np.pi * ((k1 * (CH * n1 + n2)) % NFFT) / NFFT

    th = theta(ZCH)
    c, s = np.cos(th), np.sin(th)
    m_a = np.concatenate([np.concatenate([c, s], axis=2), np.concatenate([-s, c], axis=2)], axis=1)
    m_ainv = np.transpose(m_a, (0, 2, 1))
    th = theta(CH)
    m_af = np.concatenate([np.cos(th), -np.sin(th)], axis=1)
    a = 2.0 * np.pi * ((np.arange(CH)[:, None] * np.arange(CH)[None, :]) % CH) / CH
    c, s = np.cos(a), np.sin(a)
    f2 = np.block([[c, s], [-s, c]])
    f2i = np.block([[c, -s], [s, c]])
    f = lambda t: jnp.asarray(t.astype(np.float32)).astype(BF16)
    return f(m_a), f(m_ainv), f(m_af), f(f2), f(f2i)


def _fft_conv(zp, g):
    c = zp.shape[-1]
    m_a, m_ainv, m_af, f2, f2i = _dft_tables()
    cp = _cparams(("arbitrary",), VMEM_LIMIT_BIG)
    steps = CH // FFT_G
    gc = FFT_G * c
    full2 = pl.BlockSpec((2 * CH, 2 * CH), lambda i: (0, 0))
    col_re = pl.BlockSpec((CH, gc), lambda i: (0, i))
    col_im = pl.BlockSpec((CH, gc), lambda i: (0, steps + i))
    blk = pl.BlockSpec((FFT_G, 2 * CH, c), lambda i: (i, 0, 0))
    spec_shape = jax.ShapeDtypeStruct((CH, 2 * CH, c), BF16)

    yg = pl.pallas_call(
        _fft_a_filt_kernel, grid=(steps,),
        in_specs=[pl.BlockSpec((FFT_G, 2 * CH, CH), lambda i: (i, 0, 0)),
                  pl.BlockSpec((FFT_G * CH, c), lambda i: (i, 0))],
        out_specs=blk, out_shape=spec_shape, compiler_params=cp, name="fft_a_filt",
    )(m_af, g)
    yg2 = yg.reshape(CH, 2 * CH * c)

    y = pl.pallas_call(
        _fft_a_data_kernel, grid=(steps,),
        in_specs=[pl.BlockSpec((FFT_G, 2 * CH, 2 * ZCH), lambda i: (i, 0, 0)),
                  pl.BlockSpec((2, FFT_G * ZCH, c), lambda i: (0, i, 0))],
        out_specs=blk, out_shape=spec_shape, compiler_params=cp, name="fft_a_data",
    )(m_a, zp)
    y2 = y.reshape(CH, 2 * CH * c)
    u = pl.pallas_call(
        _fft_b_kernel, grid=(steps,),
        in_specs=[full2, full2, col_re, col_im, col_re, col_im],
        out_specs=blk, out_shape=spec_shape, compiler_params=cp, name="fft_b",
    )(f2, f2i, y2, y2, yg2, yg2)
    u2 = u.reshape(CH, 2 * CH * c)
    yc = pl.pallas_call(
        _fft_a_inv_kernel, grid=(steps,),
        in_specs=[pl.BlockSpec((FFT_G, 2 * ZCH, 2 * CH), lambda i: (i, 0, 0)), col_re, col_im],
        out_specs=pl.BlockSpec((2, ZCH, FFT_G, c), lambda i: (0, 0, i, 0)),
        out_shape=jax.ShapeDtypeStruct((2, ZCH, CH, c), F32),
        compiler_params=cp, name="fft_a_inv",
    )(m_ainv, u2, u2)
    return yc.reshape(2, ZCH * CH, c)


def _far_kernel(far_ref, zm_ref, zl_ref, o_ref):
    g_hi = far_ref[HALF + FAR:HALF + 2 * FAR, :]
    ef = far_ref[0:FAR, :] - g_hi
    g_lo = far_ref[FAR:2 * FAR, :]
    hb = far_ref[HALF:HALF + FAR, :]
    zm = zm_ref[0, PAD:CH, :].astype(F32)
    zl = zl_ref[0, PAD:CH, :].astype(F32)
    row = lambda a, i: a[i:i + 1, :]
    eb = [row(hb, 0) - row(g_hi, 0)] + [row(hb, m) - row(g_lo, FAR - m) for m in range(1, FAR)]
    for j in range(FAR):
        acc_f = row(ef, j) * row(zm, 0)
        for i in range(1, j + 1):
            acc_f = acc_f + row(ef, j - i) * row(zm, i)
        o_ref[0, 0, j:j + 1, :] = acc_f
        acc_b = eb[0] * row(zl, j)
        for i in range(j + 1, FAR):
            acc_b = acc_b + eb[i - j] * row(zl, i)
        o_ref[0, 1, j:j + 1, :] = acc_b


def _far_correction(h_far, z, tp):
    b, _, c = z.shape
    last = tp // CH - 1
    return pl.pallas_call(
        _far_kernel, grid=(b,),
        in_specs=[pl.BlockSpec((CH, c), lambda bi: (0, 0)),
                  pl.BlockSpec((1, CH, c), lambda bi: (bi, 0, 0)),
                  pl.BlockSpec((1, CH, c), lambda bi: (bi, last, 0))],
        out_specs=pl.BlockSpec((1, 2, FAR, c), lambda bi: (bi, 0, 0, 0)),
        out_shape=jax.ShapeDtypeStruct((b, 2, FAR, c), F32),
        compiler_params=_cparams(("arbitrary",)),
        name="hy_far",
    )(h_far, z, z)


MG_TM = 640
MG_CN = 256
ROUTER_W = 128
AFF_PACK = ROUTER_W // N_EXPERTS
ROW_SUB, ROW_LANE = 8, 128


def _merge_kernel(h_ref, og_ref, x0_ref, z_ref, yc_ref, ga_ref, gb_ref, corr_ref, skip_ref,
                  wr_ref, wh_ref, wo_ref, g2_ref, wrt_ref,
                  h1_ref, xn_ref, aff_ref, affp_ref, pre_ref, mix_ref, h1s_ref, *, nt_b):
    i = pl.program_id(0)
    ib = i % nt_b
    pre_ref[...] = yc_ref[0].astype(F32) + z_ref[0].astype(F32) * skip_ref[...]

    @pl.when(ib == 0)
    def _():
        pre_ref[PAD:CH, :] += corr_ref[0, 1]

    @pl.when(ib == nt_b - 1)
    def _():
        pre_ref[MG_TM - FAR:MG_TM, :] += corr_ref[0, 0]

    chunks = [slice(c * MG_CN, (c + 1) * MG_CN) for c in range(D_MODEL // MG_CN)]
    og = og_ref[0]
    pre = (x0_ref[0].astype(F32) * pre_ref[...]).astype(BF16)
    for cols in chunks:
        ya = jnp.dot(og, wr_ref[:, cols], preferred_element_type=F32)
        yb = jnp.dot(pre, wh_ref[:, cols], preferred_element_type=F32)
        mix_ref[:, cols] = (ga_ref[0, :, cols].astype(F32) * ya + gb_ref[0, :, cols].astype(F32) * yb).astype(BF16)
    mixed = mix_ref[...]
    ss = jnp.zeros((MG_TM, 1), F32)
    for cols in chunks:
        h1 = h_ref[0, :, cols] + jnp.dot(mixed, wo_ref[:, cols], preferred_element_type=F32)
        h1s_ref[:, cols] = h1
        ss = ss + jnp.sum(h1 * h1, axis=-1, keepdims=True)
    rinv = lax.rsqrt(ss * (1.0 / D_MODEL) + RMS_EPS)
    h1 = h1s_ref[...]
    h1_ref[0] = h1.reshape(MG_TM, ROW_SUB, ROW_LANE)
    xn = h1 * rinv * g2_ref[...]
    xn_ref[0] = xn.reshape(MG_TM, ROW_SUB, ROW_LANE)
    logits = _dot3(xn, wrt_ref[0], wrt_ref[1])
    lane = lax.broadcasted_iota(I32, logits.shape, 1)
    logits = jnp.where(lane < N_EXPERTS, logits, -jnp.inf)
    m = jnp.max(logits, axis=-1, keepdims=True)
    e = jnp.exp(logits - m)
    aff = e / jnp.sum(e, axis=-1, keepdims=True)
    rows = lax.broadcasted_iota(I32, logits.shape, 0) + ib * MG_TM
    aff_ref[0] = jnp.where(rows >= PAD, aff, -1.0)
    lane_grp = lax.broadcasted_iota(I32, (MG_TM // AFF_PACK, ROUTER_W), 1) // N_EXPERTS
    packed = jnp.zeros((MG_TM // AFF_PACK, ROUTER_W), F32)
    for g in range(AFF_PACK):
        blk = aff_ref[0, pl.ds(g, MG_TM // AFF_PACK, stride=AFF_PACK), :]
        if g:
            blk = pltpu.roll(blk, N_EXPERTS * g, axis=1)
        packed = jnp.where(lane_grp == g, blk, packed)
    affp_ref[0] = packed


def _merge(h0, og, x0c, z, yc, proj3, corr, skip, w_ret, w_hy, w_o, g2, w_router_p, tp):
    b = h0.shape[0]
    nt_b = tp // MG_TM
    ga_blk0 = 6144 // D_MODEL
    row = lambda w: pl.BlockSpec((1, MG_TM, w), lambda i: (i // nt_b, i % nt_b, 0))
    full = lambda a: pl.BlockSpec(a.shape, lambda i: (0,) * a.ndim)
    return pl.pallas_call(
        functools.partial(_merge_kernel, nt_b=nt_b),
        grid=(b * nt_b,),
        in_specs=[row(D_MODEL), row(D_MODEL), row(D_MODEL), row(D_MODEL), row(D_MODEL),
                  pl.BlockSpec((1, MG_TM, D_MODEL), lambda i: (i // nt_b, i % nt_b, ga_blk0)),
                  pl.BlockSpec((1, MG_TM, D_MODEL), lambda i: (i // nt_b, i % nt_b, ga_blk0 + 1)),
                  pl.BlockSpec((1, 2, FAR, D_MODEL), lambda i: (i // nt_b, 0, 0, 0)),
                  full(skip), full(w_ret), full(w_hy), full(w_o), full(g2), full(w_router_p)],
        out_specs=[pl.BlockSpec((1, MG_TM, ROW_SUB, ROW_LANE), lambda i: (i // nt_b, i % nt_b, 0, 0)),
                   pl.BlockSpec((1, MG_TM, ROW_SUB, ROW_LANE), lambda i: (i // nt_b, i % nt_b, 0, 0)),
                   row(ROUTER_W),
                   pl.BlockSpec((1, MG_TM // AFF_PACK, ROUTER_W), lambda i: (i // nt_b, i % nt_b, 0))],
        out_shape=[jax.ShapeDtypeStruct((b, tp, ROW_SUB, ROW_LANE), F32),
                   jax.ShapeDtypeStruct((b, tp, ROW_SUB, ROW_LANE), F32),
                   jax.ShapeDtypeStruct((b, tp, ROUTER_W), F32),
                   jax.ShapeDtypeStruct((b, tp // AFF_PACK, ROUTER_W), F32)],
        scratch_shapes=[pltpu.VMEM((MG_TM, D_MODEL), F32), pltpu.VMEM((MG_TM, D_MODEL), BF16),
                        pltpu.VMEM((MG_TM, D_MODEL), F32)],
        compiler_params=_cparams(("arbitrary",), VMEM_LIMIT_BIG),
        name="merge",
    )(h0, og, x0c, z, yc, proj3, proj3, corr, skip, w_ret, w_hy, w_o, g2, w_router_p)


TK_W = 128


TK_UNROLL = 5


def _select_kernel(aff_ref, affp_ref, low_ref, slot_ref, offs_ref, *, cap, nch):
    def chunk(c):
        r0 = pl.multiple_of(c * CH, CH)
        return aff_ref[0, pl.ds(r0, CH), :]

    def count(pred):
        cnt = jnp.sum(pred(affp_ref[0]).astype(I32), axis=0, keepdims=True)
        for shift in (N_EXPERTS, 2 * N_EXPERTS, 4 * N_EXPERTS):
            cnt = cnt + pltpu.roll(cnt, shift, axis=1)
        return cnt

    def search(it, bits):
        cand = bits | jnp.left_shift(1, 29 - it)
        cand_f = pltpu.bitcast(cand, F32)
        return jnp.where(count(lambda a: a >= cand_f) >= cap, cand, bits)

    thr = pltpu.bitcast(lax.fori_loop(0, 30, search, jnp.zeros((1, ROUTER_W), I32)), F32)
    need = (cap - count(lambda a: a > thr)).astype(F32)
    low = low_ref[...]

    def scan(it, carry):
        c_eq, c_sel = carry
        for u in range(TK_UNROLL):
            c = it * TK_UNROLL + u
            a = chunk(c)
            eq = a == thr
            eq_f = eq.astype(F32)
            eq_rank = jnp.dot(low, eq_f, preferred_element_type=F32) + c_eq
            sel = jnp.logical_or(a > thr, jnp.logical_and(eq, eq_rank < need))
            sel_f = sel.astype(F32)
            slot = jnp.dot(low, sel_f, preferred_element_type=F32) + c_sel
            r0 = pl.multiple_of(c * CH, CH)
            slot_ref[0, pl.ds(r0, CH), :] = jnp.where(sel, slot, -1.0).astype(I32)
            offs_ref[0, c] = c_sel.astype(I32)
            c_eq = c_eq + jnp.sum(eq_f, axis=0, keepdims=True)
            c_sel = c_sel + jnp.sum(sel_f, axis=0, keepdims=True)
        return c_eq, c_sel

    zero = jnp.zeros((1, ROUTER_W), F32)
    lax.fori_loop(0, nch // TK_UNROLL, scan, (zero, zero))


def _extract_kernel(offs_ref, slot_ref, aff_ref, idx_ref, gate_ref, *, nch, nwin):
    idx_ref[...] = jnp.zeros_like(idx_ref)
    gate_ref[...] = jnp.zeros_like(gate_ref)
    lane = lax.broadcasted_iota(I32, (CH, TK_W), 1)
    trow = lax.broadcasted_iota(I32, (CH, TK_W), 0)

    def per_chunk(c, carry):
        r0 = pl.multiple_of(c * CH, CH)
        slots = slot_ref[0, pl.ds(r0, CH), :]
        affs = aff_ref[0, pl.ds(r0, CH), :]
        tpos = (trow + r0).astype(F32)
        for e in range(N_EXPERTS):
            col = jnp.broadcast_to(slots[:, e:e + 1], (CH, TK_W))
            gcol = jnp.broadcast_to(affs[:, e:e + 1], (CH, TK_W))
            w0 = offs_ref[0, 0, c * N_EXPERTS + e] // TK_W
            for dw in range(2):
                w = w0 + dw
                base = jnp.where(w < nwin, w * TK_W, -2 * TK_W)
                hit = col == lane + base
                row = e * nwin + jnp.minimum(w, nwin - 1)
                idx_ref[row] += jnp.sum(jnp.where(hit, tpos, 0.0), axis=0, keepdims=True)
                gate_ref[row] += jnp.sum(jnp.where(hit, gcol, 0.0), axis=0, keepdims=True)
        return carry

    lax.fori_loop(0, nch, per_chunk, 0)


def _topk(aff, aff_packed, cap, slots):
    b, tp, _ = aff.shape
    nch = tp // CH
    nwin = -(-slots // TK_W)
    low = jnp.asarray(np.tril(np.ones((CH, CH), np.float32), k=-1))
    pack = AFF_PACK
    assert nch % TK_UNROLL == 0 and tp % pack == 0
    slot, offs = pl.pallas_call(
        functools.partial(_select_kernel, cap=cap, nch=nch),
        grid=(b,),
        in_specs=[pl.BlockSpec((1, tp, ROUTER_W), lambda bi: (bi, 0, 0)),
                  pl.BlockSpec((1, tp // pack, ROUTER_W), lambda bi: (bi, 0, 0)),
                  pl.BlockSpec((CH, CH), lambda bi: (0, 0))],
        out_specs=[pl.BlockSpec((1, tp, ROUTER_W), lambda bi: (bi, 0, 0)),
                   pl.BlockSpec((1, nch, 1, ROUTER_W), lambda bi: (bi, 0, 0, 0))],
        out_shape=[jax.ShapeDtypeStruct((b, tp, ROUTER_W), I32),
                   jax.ShapeDtypeStruct((b, nch, 1, ROUTER_W), I32)],
        compiler_params=_cparams(("arbitrary",)),
        name="topk_select",
    )(aff, aff_packed, low)
    offs_s = offs[:, :, 0, :N_EXPERTS].reshape(b, 1, nch * N_EXPERTS)
    rows = pl.BlockSpec((None, N_EXPERTS * nwin, 1, TK_W), lambda bi: (bi, 0, 0, 0))
    out = jax.ShapeDtypeStruct((b, N_EXPERTS * nwin, 1, TK_W), F32)
    idx, gate = pl.pallas_call(
        functools.partial(_extract_kernel, nch=nch, nwin=nwin),
        grid=(b,),
        in_specs=[pl.BlockSpec((1, 1, nch * N_EXPERTS), lambda bi: (bi, 0, 0), memory_space=pltpu.SMEM),
                  pl.BlockSpec((1, tp, ROUTER_W), lambda bi: (bi, 0, 0)),
                  pl.BlockSpec((1, tp, ROUTER_W), lambda bi: (bi, 0, 0))],
        out_specs=[rows, rows],
        out_shape=[out, out],
        compiler_params=_cparams(("arbitrary",)),
        name="topk_extract",
    )(offs_s, slot, aff)
    idx = idx.reshape(b * N_EXPERTS, 1, nwin * TK_W)[:, :, :slots].astype(I32)
    gate = gate.reshape(b * N_EXPERTS, 1, nwin * TK_W)[:, :, :slots]
    return idx, gate


FF_TF = 1024
MOE_CN = 256


MOE_UNROLL = 4


def _moe_kernel(idx_ref, idx_next_ref, xn_hbm, wg_ref, wu_ref, wd_ref, o_ref,
                xe32_ref, xe_ref, hid_ref, acc_ref, sem,
                *, slots, nf):
    i = pl.program_id(0)
    f = pl.program_id(1)
    buf = i % 2
    share = slots // nf

    def row_copy(ids_ref, s, b):
        return pltpu.make_async_copy(xn_hbm.at[ids_ref[0, 0, s]], xe32_ref.at[b, s], sem.at[b])

    def for_rows(lo, n, fn):
        def body(k, carry):
            for u in range(MOE_UNROLL):
                fn(lo + k * MOE_UNROLL + u)
            return carry

        lax.fori_loop(0, n // MOE_UNROLL, body, 0)

    @pl.when(jnp.logical_and(i == 0, f == 0))
    def _():
        for_rows(0, slots, lambda s: row_copy(idx_ref, s, 0).start())

    @pl.when(f == 0)
    def _():
        for_rows(0, slots, lambda s: row_copy(idx_ref, s, buf).wait())
        xe_ref[...] = xe32_ref[buf].reshape(slots, D_MODEL).astype(BF16)

    @pl.when(f == 0)
    def _():
        acc_ref[...] = jnp.zeros_like(acc_ref)

    n_up, n_down = FF_TF // MOE_CN, D_MODEL // MOE_CN
    per_chunk = share // (n_up + n_down)

    def prefetch(chunk):
        base = f * share + chunk * per_chunk
        for u in range(per_chunk):
            row_copy(idx_next_ref, base + u, 1 - buf).start()

    xe = xe_ref[...]
    for c in range(n_up):
        cols = slice(c * MOE_CN, (c + 1) * MOE_CN)
        gg = jnp.dot(xe, wg_ref[0, :, cols].astype(BF16), preferred_element_type=F32)
        uu = jnp.dot(xe, wu_ref[0, :, cols].astype(BF16), preferred_element_type=F32)
        hid_ref[:, cols] = (gg * jax.nn.sigmoid(gg) * uu).astype(BF16)
        prefetch(c)
    hid = hid_ref[...]
    for c in range(n_down):
        cols = slice(c * MOE_CN, (c + 1) * MOE_CN)
        part = jnp.dot(hid, wd_ref[0, :, cols].astype(BF16), preferred_element_type=F32)
        acc_ref[:, cols] = acc_ref[:, cols] + part
        prefetch(n_up + c)

    @pl.when(f == pl.num_programs(1) - 1)
    def _():
        o_ref[0] = acc_ref[...].reshape(slots, ROW_SUB, ROW_LANE)

    @pl.when(jnp.logical_and(i == pl.num_programs(0) - 1, f == pl.num_programs(1) - 1))
    def _():
        for_rows(0, slots, lambda s: row_copy(idx_next_ref, s, 1 - buf).wait())


def _moe_ffn(idx, xn_flat, w_gate, w_up, w_down, slots):
    be = idx.shape[0]
    nf = D_FF // FF_TF
    assert slots % MOE_UNROLL == 0 and slots % (nf * (FF_TF // MOE_CN + D_MODEL // MOE_CN)) == 0
    return pl.pallas_call(
        functools.partial(_moe_kernel, slots=slots, nf=nf),
        grid=(be, nf),
        in_specs=[pl.BlockSpec((1, 1, slots), lambda i, f: (i, 0, 0), memory_space=pltpu.SMEM),
                  pl.BlockSpec((1, 1, slots), lambda i, f: (jnp.minimum(i + 1, be - 1), 0, 0),
                               memory_space=pltpu.SMEM),
                  pl.BlockSpec(memory_space=pl.ANY),
                  pl.BlockSpec((1, D_MODEL, FF_TF), lambda i, f: (i % N_EXPERTS, 0, f)),
                  pl.BlockSpec((1, D_MODEL, FF_TF), lambda i, f: (i % N_EXPERTS, 0, f)),
                  pl.BlockSpec((1, FF_TF, D_MODEL), lambda i, f: (i % N_EXPERTS, f, 0))],
        out_specs=pl.BlockSpec((1, slots, ROW_SUB, ROW_LANE), lambda i, f: (i, 0, 0, 0)),
        out_shape=jax.ShapeDtypeStruct((be, slots, ROW_SUB, ROW_LANE), F32),
        scratch_shapes=[pltpu.VMEM((2, slots, ROW_SUB, ROW_LANE), F32), pltpu.VMEM((slots, D_MODEL), BF16),
                        pltpu.VMEM((slots, FF_TF), BF16), pltpu.VMEM((slots, D_MODEL), F32),
                        pltpu.SemaphoreType.DMA((2,))],
        compiler_params=_cparams(("arbitrary", "arbitrary"), VMEM_LIMIT_BIG),
        name="moe_ffn",
    )(idx, idx, xn_flat, w_gate, w_up, w_down)


CB_U = 6
CB_NB = 512


def _combine_kernel(idx_ref, gate_ref, h1_hbm, ye_ref, g_ref, o_hbm, acc_ref, stage_ref, sem, osem, *, cap, tp):
    b = pl.program_id(0)
    e = pl.program_id(1)

    @pl.when(e == 0)
    def _():
        cp = pltpu.make_async_copy(h1_hbm.at[b], acc_ref, sem)
        cp.start()
        cp.wait()

    def rmw(g, carry):
        ts = [idx_ref[0, 0, g * CB_U + u] for u in range(CB_U)]
        vals = [acc_ref[ts[u]] + gate_ref[0, 0, g * CB_U + u] * ye_ref[g * CB_U + u] for u in range(CB_U)]
        for u in range(CB_U):
            acc_ref[ts[u]] = vals[u]
        return carry

    lax.fori_loop(0, cap // CB_U, rmw, 0)

    @pl.when(e == pl.num_programs(1) - 1)
    def _():
        gamma = g_ref[...]
        n_blocks = (tp - CH) // CB_NB

        def out_copy(k):
            return pltpu.make_async_copy(stage_ref.at[k % 2], o_hbm.at[b, pl.ds(k * CB_NB, CB_NB), :],
                                         osem.at[k % 2])

        for k in range(n_blocks):
            x = acc_ref[pl.ds(CH + k * CB_NB, CB_NB)]
            ms = jnp.sum(jnp.sum(x * x, axis=2, keepdims=True), axis=1, keepdims=True) * (1.0 / D_MODEL)
            y = x * lax.rsqrt(ms + RMS_EPS) * gamma
            if k >= 2:
                out_copy(k - 2).wait()
            stage_ref[k % 2] = y.reshape(CB_NB, D_MODEL)
            out_copy(k).start()
        for k in range(max(n_blocks - 2, 0), n_blocks):
            out_copy(k).wait()


def _combine(idx_local, gate, h1, ye, gf, cap, slots):
    b, tp = h1.shape[:2]
    assert cap % CB_U == 0 and (tp - CH) % CB_NB == 0
    return pl.pallas_call(
        functools.partial(_combine_kernel, cap=cap, tp=tp),
        grid=(b, N_EXPERTS),
        in_specs=[pl.BlockSpec((1, 1, slots), lambda bi, e: (bi * N_EXPERTS + e, 0, 0), memory_space=pltpu.SMEM),
                  pl.BlockSpec((1, 1, slots), lambda bi, e: (bi * N_EXPERTS + e, 0, 0), memory_space=pltpu.SMEM),
                  pl.BlockSpec(memory_space=pl.ANY),
                  pl.BlockSpec((None, slots, ROW_SUB, ROW_LANE), lambda bi, e: (bi * N_EXPERTS + e, 0, 0, 0)),
                  pl.BlockSpec((1, ROW_SUB, ROW_LANE), lambda bi, e: (0, 0, 0))],
        out_specs=pl.BlockSpec(memory_space=pl.ANY),
        out_shape=jax.ShapeDtypeStruct((b, tp - CH, D_MODEL), F32),
        scratch_shapes=[pltpu.VMEM((tp, ROW_SUB, ROW_LANE), F32), pltpu.VMEM((2, CB_NB, D_MODEL), F32),
                        pltpu.SemaphoreType.DMA(()), pltpu.SemaphoreType.DMA((2,))],
        compiler_params=_cparams(("arbitrary", "arbitrary"), VMEM_LIMIT_BIG),
        name="combine",
    )(idx_local, gate, h1, ye, gf.reshape(1, ROW_SUB, ROW_LANE))


def _rope_tables(tp, b):
    half = RET_QK_HEAD // 2
    pos = np.arange(tp, dtype=np.float64) - PAD
    inv = ROPE_BASE ** (-np.arange(half, dtype=np.float64) / half)
    ang = pos[:, None] * inv[None, :]
    cos, sin = np.cos(ang), np.sin(ang)
    tile = lambda t: jnp.asarray(np.tile(t, (b, 1)).astype(np.float32))
    return tile(np.concatenate([cos, cos], axis=1)), tile(np.concatenate([-sin, sin], axis=1))


def _filter_features(t_len):
    half = NFFT // 2
    q = np.arange(NFFT)
    r = CH * (q % CH) + q // CH
    p_main = np.where(r < half, r, NFFT - r)
    valid_main = (r != half).astype(np.float32)
    m = np.arange(FAR)
    pad = np.zeros(HALF - 2 * FAR, np.int64)
    p_far = np.concatenate([half + m, half - FAR + m, pad, half + m, half - m, pad])
    valid_far = np.ones(CH, np.float32)
    valid_far[HALF + FAR] = 0.0
    p = np.concatenate([p_main, p_far]).astype(np.float64)
    valid = np.concatenate([valid_main, valid_far]).astype(np.float64)
    t_norm = p / (t_len - 1)
    bands = (HY_EMB_DIM - 1) // 2
    fr = np.linspace(1e-4, bands - 1, bands)
    ang = (2.0 * math.pi * p / t_len)[:, None] * fr[None, :]
    feat = np.concatenate([t_norm[:, None], np.cos(ang), -np.sin(ang), valid[:, None]], axis=-1)
    feat = np.pad(feat, ((0, 0), (0, FEAT_W - feat.shape[1])))
    pairs = feat.reshape(-1, 2, HALF, FEAT_W).transpose(0, 2, 1, 3).reshape(-1, 2 * FEAT_W)
    return jnp.asarray(pairs.astype(np.float32))


def kernel(x, meta_tokens, norm1_g, w_in, ret_decay_fwd, ret_decay_bwd, ret_head_norm_g, w_ret_out,
           hy_conv_w, hy_conv_b, hy_filt_w1, hy_filt_b1, hy_filt_w2, hy_filt_b2, hy_filt_w3, hy_filt_b3,
           hy_filt_freq, hy_filt_w4, hy_skip, w_hy_out, w_o, norm2_g, w_router, w_exp_gate, w_exp_up,
           w_exp_down, final_norm_g):
    b, seq, d = x.shape
    t_len = seq + N_META
    tp = PAD + t_len
    assert d == D_MODEL and (b * tp) % IP_TM == 0 and tp % MG_TM == 0 and tp % CH == 0 and t_len - NFFT // 2 == FAR
    cap = EC_CAPACITY * t_len // N_EXPERTS
    slots = -(-cap // 16) * 16
    l = 0

    meta = jnp.broadcast_to(meta_tokens[None].astype(x.dtype), (b, N_META, d))
    h0 = jnp.concatenate([jnp.zeros((b, PAD, d), x.dtype), meta, x], axis=1)

    cs, sn = _rope_tables(tp, b)
    proj = _in_proj(h0.reshape(b * tp, d), norm1_g[l][None], w_in[l], cs, sn, tp)
    proj3 = proj.reshape(b, tp, IN_PROJ_W)

    lf = jax.nn.log_sigmoid(ret_decay_fwd[l].astype(F32))
    lb = jax.nn.log_sigmoid(ret_decay_bwd[l].astype(F32))
    og = _retention(proj3, lf, lb, ret_head_norm_g[l][None], tp)

    x0c, z, zp = _hy_prep(proj3, hy_conv_w[l], hy_conv_b[l][None], tp)

    feat = _filter_features(t_len)
    w1p = jnp.pad(hy_filt_w1[l].astype(F32), ((0, FEAT_W - HY_EMB_DIM), (0, 0)))
    max_decay = math.log(HY_DECAY_TARGET) / HY_FAST_DECAY_PCT
    min_decay = math.log(HY_DECAY_TARGET) / HY_SLOW_DECAY_PCT
    dl = jnp.abs(jnp.linspace(min_decay, max_decay, D_MODEL, dtype=F32))[None]
    fargs = (w1p, hy_filt_b1[l][None].astype(F32), hy_filt_w2[l].astype(F32), hy_filt_b2[l][None].astype(F32),
             hy_filt_w3[l].astype(F32), hy_filt_b3[l][None].astype(F32), hy_filt_freq[l][None].astype(F32),
             hy_filt_w4[l].astype(F32), dl)
    g = _filters(feat[:NFFT // 2], *fargs, FT_ROWS)
    h_far = _filters(feat[NFFT // 2:], *fargs, CH)
    yc = _fft_conv(zp, g)
    corr = _far_correction(h_far, z, tp)

    w_router_p = jnp.pad(w_router[l].astype(F32), ((0, 0), (0, ROUTER_W - N_EXPERTS)))
    w_router_p = _split_hi_lo(w_router_p)
    h1, xn2, aff, aff_packed = _merge(h0, og, x0c, z, yc, proj3, corr, hy_skip[l][None].astype(F32),
                          w_ret_out[l].astype(BF16), w_hy_out[l].astype(BF16), w_o[l].astype(BF16),
                          norm2_g[l][None].astype(F32), w_router_p, tp)

    idx, gate = _topk(aff, aff_packed, cap, slots)
    live = (jnp.arange(slots) < cap)[None, None, :]
    idx_local = jnp.where(live, idx, PAD)
    idx_flat = idx_local + (jnp.arange(b * N_EXPERTS, dtype=I32) // N_EXPERTS * tp)[:, None, None]
    ye = _moe_ffn(idx_flat, xn2.reshape(b * tp, ROW_SUB, ROW_LANE),
                  w_exp_gate[l], w_exp_up[l], w_exp_down[l], slots)
    return _combine(idx_local, gate, h1, ye, final_norm_g.astype(F32), cap, slots)
```

```python
import functools
import math

import numpy as np
import jax
import jax.numpy as jnp
from jax import lax
from jax.experimental import pallas as pl
from jax.experimental.pallas import tpu as pltpu

F32 = jnp.float32
BF16 = jnp.bfloat16
I32 = jnp.int32

D_MODEL = 1024
N_META = 16
RET_HEADS = 4
RET_QK_HEAD = 128
RET_V_HEAD = 256
ROPE_BASE = 10000.0
HY_EMB_DIM = 33
HY_FILTER_ORDER = 64
HY_FAST_DECAY_PCT = 0.3
HY_SLOW_DECAY_PCT = 1.5
HY_DECAY_TARGET = 1e-2
N_EXPERTS = 16
EC_CAPACITY = 2
D_FF = 2 * D_MODEL
RMS_EPS = 1e-6
IN_PROJ_W = 8192

CH = 128
PAD = CH - N_META
NFFT = 16384
ZCH = 80
FAR = 16

VMEM_LIMIT_BIG = 56 * 1024 * 1024
VMEM_LIMIT_MID = 40 * 1024 * 1024


def _cparams(sem, vmem=VMEM_LIMIT_MID):
    return pltpu.CompilerParams(dimension_semantics=sem, vmem_limit_bytes=vmem)


def _split_hi_lo(w):
    w = w.astype(F32)
    hi = w.astype(BF16)
    return jnp.stack([hi, (w - hi.astype(F32)).astype(BF16)])


def _dot3(a, w_hi, w_lo):
    a_hi = a.astype(BF16)
    a_lo = (a - a_hi.astype(F32)).astype(BF16)
    return (jnp.dot(a_hi, w_hi, preferred_element_type=F32) + jnp.dot(a_lo, w_hi, preferred_element_type=F32)
            + jnp.dot(a_hi, w_lo, preferred_element_type=F32))


IP_TM = 1280
IP_TN = 1024
IP_CN = 256


def _inproj_kernel(x_ref, g_ref, w_ref, cs_ref, sn_ref, o_ref, xn_ref):
    j = pl.program_id(1)

    @pl.when(j == 0)
    def _():
        x = x_ref[...]
        ms = jnp.mean(x * x, axis=-1, keepdims=True)
        xn_ref[...] = (x * lax.rsqrt(ms + RMS_EPS) * g_ref[...]).astype(BF16)

    def run(epilogue):
        for c in range(IP_TN // IP_CN):
            cols = slice(c * IP_CN, (c + 1) * IP_CN)
            acc = jnp.dot(xn_ref[...], w_ref[:, cols].astype(BF16), preferred_element_type=F32)
            epilogue(c, cols, acc)

    def rotary(c, cols, acc):
        scale = 1.0 if c < (IP_TN // IP_CN) // 2 else RET_QK_HEAD ** -0.5
        cs = cs_ref[...] * scale
        sn = sn_ref[...] * scale
        for hh in range(IP_CN // RET_QK_HEAD):
            xh = acc[:, hh * RET_QK_HEAD:(hh + 1) * RET_QK_HEAD]
            rot = xh * cs + pltpu.roll(xh, RET_QK_HEAD // 2, axis=1) * sn
            lo = c * IP_CN + hh * RET_QK_HEAD
            o_ref[:, lo:lo + RET_QK_HEAD] = rot.astype(BF16)

    def raw(c, cols, acc):
        o_ref[:, cols] = acc.astype(BF16)

    def swish(c, cols, acc):
        o_ref[:, cols] = (acc * jax.nn.sigmoid(acc)).astype(BF16)

    def sigm(c, cols, acc):
        o_ref[:, cols] = jax.nn.sigmoid(acc).astype(BF16)

    pl.when(j == 0)(lambda: run(rotary))
    pl.when(jnp.logical_or(j == 1, jnp.logical_and(j >= 3, j < 6)))(lambda: run(raw))
    pl.when(j == 2)(lambda: run(swish))
    pl.when(j >= 6)(lambda: run(sigm))


def _in_proj(h0, g1, w_in_bf, cs, sn, tp):
    n_rows = h0.shape[0]
    return pl.pallas_call(
        _inproj_kernel,
        grid=(n_rows // IP_TM, IN_PROJ_W // IP_TN),
        in_specs=[
            pl.BlockSpec((IP_TM, D_MODEL), lambda i, j: (i, 0)),
            pl.BlockSpec((1, D_MODEL), lambda i, j: (0, 0)),
            pl.BlockSpec((D_MODEL, IP_TN), lambda i, j: (0, j)),
            pl.BlockSpec((IP_TM, RET_QK_HEAD), lambda i, j: (i, 0)),
            pl.BlockSpec((IP_TM, RET_QK_HEAD), lambda i, j: (i, 0)),
        ],
        out_specs=pl.BlockSpec((IP_TM, IP_TN), lambda i, j: (i, j)),
        out_shape=jax.ShapeDtypeStruct((n_rows, IN_PROJ_W), BF16),
        scratch_shapes=[pltpu.VMEM((IP_TM, D_MODEL), BF16)],
        compiler_params=_cparams(("arbitrary", "arbitrary")),
        name="in_proj",
    )(h0, g1, w_in_bf, cs, sn)


RET_UNROLL = 13


def _ret_kernel(lf_ref, lb_ref, q_ref, k_ref, v_ref, gr_ref, gn_ref, o_ref, ob_ref, s_ref, *, nch):
    h = pl.program_id(1)
    lf = lf_ref[h]
    lb = lb_ref[h]
    ri = lax.broadcasted_iota(I32, (CH, CH), 0).astype(F32)
    ci = lax.broadcasted_iota(I32, (CH, CH), 1).astype(F32)
    diff = ri - ci
    mask = jnp.exp(jnp.where(diff >= 0, lf * diff, -lb * diff))
    w_end = jnp.exp(lf * (CH - 1.0 - ri))
    w_start = jnp.exp(lb * ri)
    qw_f = jnp.exp(lf * (ri + 1.0))
    qw_b = jnp.exp(lb * (CH - ri))
    dec_f = jnp.exp(jnp.full((CH, RET_V_HEAD), lf * CH, F32))
    dec_b = jnp.exp(jnp.full((CH, RET_V_HEAD), lb * CH, F32))
    tn_dims = (((0,), (0,)), ((), ()))
    nt_dims = (((1,), (1,)), ((), ()))

    s_ref[...] = jnp.zeros_like(s_ref)

    def bwd(it, carry):
        s = s_ref[...]
        for u in range(RET_UNROLL):
            n = nch - 1 - (it * RET_UNROLL + u)
            r0 = pl.multiple_of(n * CH, CH)
            q = q_ref[0, pl.ds(r0, CH), :].astype(F32)
            k = k_ref[0, pl.ds(r0, CH), :].astype(F32)
            v = v_ref[0, pl.ds(r0, CH), :]
            ob_ref[pl.ds(r0, CH), :] = jnp.dot((q * qw_b).astype(BF16), s.astype(BF16),
                                               preferred_element_type=F32)
            a = lax.dot_general((k * w_start).astype(BF16), v, tn_dims, preferred_element_type=F32)
            s = s * dec_b + a
        s_ref[...] = s
        return carry

    lax.fori_loop(0, nch // RET_UNROLL, bwd, 0)

    s_ref[...] = jnp.zeros_like(s_ref)
    gn = gn_ref[...]

    def fwd(it, carry):
        s = s_ref[...]
        for u in range(RET_UNROLL):
            n = it * RET_UNROLL + u
            r0 = pl.multiple_of(n * CH, CH)
            qb = q_ref[0, pl.ds(r0, CH), :]
            kb = k_ref[0, pl.ds(r0, CH), :]
            v = v_ref[0, pl.ds(r0, CH), :]
            q = qb.astype(F32)
            k = kb.astype(F32)
            scores = lax.dot_general(qb, kb, nt_dims, preferred_element_type=F32) * mask
            o = jnp.dot(scores.astype(BF16), v, preferred_element_type=F32)
            o = o + jnp.dot((q * qw_f).astype(BF16), s.astype(BF16), preferred_element_type=F32)
            o = o + ob_ref[pl.ds(r0, CH), :]
            a = lax.dot_general((k * w_end).astype(BF16), v, tn_dims, preferred_element_type=F32)
            s = s * dec_f + a
            y = o * lax.rsqrt(jnp.mean(o * o, axis=-1, keepdims=True) + RMS_EPS) * gn
            o_ref[0, pl.ds(r0, CH), :] = (y * gr_ref[0, pl.ds(r0, CH), :].astype(F32)).astype(BF16)
        s_ref[...] = s
        return carry

    lax.fori_loop(0, nch // RET_UNROLL, fwd, 0)


def _retention(proj3, lf, lb, gn, tp):
    b = proj3.shape[0]
    nch = tp // CH
    qk_blocks = (RET_HEADS * RET_QK_HEAD) // RET_QK_HEAD
    v_blk0 = (2 * RET_HEADS * RET_QK_HEAD) // RET_V_HEAD
    g_blk0 = v_blk0 + RET_HEADS
    smem = pl.BlockSpec(memory_space=pltpu.SMEM)
    return pl.pallas_call(
        functools.partial(_ret_kernel, nch=nch),
        grid=(b, RET_HEADS),
        in_specs=[
            smem, smem,
            pl.BlockSpec((1, tp, RET_QK_HEAD), lambda bi, h: (bi, 0, h)),
            pl.BlockSpec((1, tp, RET_QK_HEAD), lambda bi, h: (bi, 0, qk_blocks + h)),
            pl.BlockSpec((1, tp, RET_V_HEAD), lambda bi, h: (bi, 0, v_blk0 + h)),
            pl.BlockSpec((1, tp, RET_V_HEAD), lambda bi, h: (bi, 0, g_blk0 + h)),
            pl.BlockSpec((1, RET_V_HEAD), lambda bi, h: (0, h)),
        ],
        out_specs=pl.BlockSpec((1, tp, RET_V_HEAD), lambda bi, h: (bi, 0, h)),
        out_shape=jax.ShapeDtypeStruct((b, tp, RET_HEADS * RET_V_HEAD), BF16),
        scratch_shapes=[pltpu.VMEM((tp, RET_V_HEAD), F32), pltpu.VMEM((RET_QK_HEAD, RET_V_HEAD), F32)],
        compiler_params=_cparams(("arbitrary", "arbitrary"), VMEM_LIMIT_BIG),
        name="retention",
    )(lf, lb, proj3, proj3, proj3, proj3, gn)


HP_CW = 128


def _hyprep_kernel(u0_ref, u1_ref, u2_ref, w0_ref, w1_ref, w2_ref, b0_ref, b1_ref, b2_ref,
                   x0_ref, z_ref, zp_ref, *, nch):
    rows = lax.broadcasted_iota(I32, (CH, HP_CW), 0)
    halo = 16
    zp_ref[...] = jnp.zeros_like(zp_ref)

    def conv(u_ref, w_ref, b_ref, n, r0):
        cur = u_ref[0, pl.ds(r0, CH), :].astype(F32)
        rp = pl.multiple_of(jnp.maximum(r0 - halo, 0), halo)
        rn = pl.multiple_of(jnp.minimum(r0 + CH, (nch - 1) * CH), halo)
        prev = u_ref[0, pl.ds(rp, halo), :].astype(F32)[halo - 1:halo, :]
        nxt = u_ref[0, pl.ds(rn, halo), :].astype(F32)[0:1, :]
        prev = jnp.where(n > 0, prev, 0.0)
        nxt = jnp.where(n < nch - 1, nxt, 0.0)
        up = jnp.where(rows == 0, prev, pltpu.roll(cur, 1, axis=0))
        dn = jnp.where(rows == CH - 1, nxt, pltpu.roll(cur, CH - 1, axis=0))
        w = w_ref[...]
        return up * w[0:1, :] + cur * w[1:2, :] + dn * w[2:3, :] + b_ref[...]

    def body(n, carry):
        r0 = pl.multiple_of(n * CH, CH)
        x0 = conv(u0_ref, w0_ref, b0_ref, n, r0)
        x1 = conv(u1_ref, w1_ref, b1_ref, n, r0)
        vv = conv(u2_ref, w2_ref, b2_ref, n, r0)
        z = jnp.where(rows + r0 >= PAD, x1 * vv, 0.0)
        x0_ref[0, pl.ds(r0, CH), :] = x0.astype(BF16)
        z_ref[0, pl.ds(r0, CH), :] = z.astype(BF16)
        zp_ref[0, pl.ds(n, CH, stride=ZCH), :] = z
        return carry

    lax.fori_loop(0, nch, body, 0)


def _hy_prep(proj3, conv_w, conv_b, tp):
    b = proj3.shape[0]
    nch = tp // CH
    ncb = D_MODEL // HP_CW
    u_blk0 = 3072 // HP_CW
    uspec = lambda s: pl.BlockSpec((1, tp, HP_CW), lambda bi, c: (bi, 0, u_blk0 + s * ncb + c))
    wspec = lambda s: pl.BlockSpec((3, HP_CW), lambda bi, c: (0, s * ncb + c))
    bspec = lambda s: pl.BlockSpec((1, HP_CW), lambda bi, c: (0, s * ncb + c))
    return pl.pallas_call(
        functools.partial(_hyprep_kernel, nch=nch),
        grid=(b, ncb),
        in_specs=[uspec(0), uspec(1), uspec(2), wspec(0), wspec(1), wspec(2), bspec(0), bspec(1), bspec(2)],
        out_specs=[pl.BlockSpec((1, tp, HP_CW), lambda bi, c: (bi, 0, c)),
                   pl.BlockSpec((1, tp, HP_CW), lambda bi, c: (bi, 0, c)),
                   pl.BlockSpec((1, CH * ZCH, HP_CW), lambda bi, c: (bi, 0, c))],
        out_shape=[jax.ShapeDtypeStruct((b, tp, D_MODEL), BF16),
                   jax.ShapeDtypeStruct((b, tp, D_MODEL), BF16),
                   jax.ShapeDtypeStruct((b, CH * ZCH, D_MODEL), F32)],
        compiler_params=_cparams(("arbitrary", "arbitrary")),
        name="hy_prep",
    )(proj3, proj3, proj3, conv_w, conv_w, conv_w, conv_b, conv_b, conv_b)


FEAT_W = 128


FT_ROWS = 512
HALF = CH // 2


def _filter_kernel(feat_ref, w1_ref, b1_ref, w2_ref, b2_ref, w3_ref, b3_ref, fq_ref, w4_ref, dl_ref, o_ref,
                   *, groups):
    feat = feat_ref[...]
    fq = fq_ref[...]
    hdn = jnp.sin(fq * (_dot3(feat, w1_ref[0], w1_ref[1]) + b1_ref[...]))
    hdn = jnp.sin(fq * (_dot3(hdn, w2_ref[0], w2_ref[1]) + b2_ref[...]))
    hdn = jnp.sin(fq * (_dot3(hdn, w3_ref[0], w3_ref[1]) + b3_ref[...]))
    filt = _dot3(hdn, w4_ref[0], w4_ref[1])
    dl = dl_ref[...]
    for d in range(2):
        f0 = d * FEAT_W
        scale = jnp.exp(-feat[:, f0:f0 + 1] * dl) * feat[:, f0 + HY_EMB_DIM:f0 + HY_EMB_DIM + 1]
        fd = filt[:, d * D_MODEL:(d + 1) * D_MODEL] * scale
        for g in range(groups):
            o_ref[g * CH + d * HALF:g * CH + (d + 1) * HALF, :] = fd[g * HALF:(g + 1) * HALF]


def _block_diag2(w):
    z = jnp.zeros_like(w)
    return jnp.concatenate([jnp.concatenate([w, z], axis=1), jnp.concatenate([z, w], axis=1)], axis=0)


def _filters(feat_pairs, w1p, b1, w2, b2, w3, b3, fq, w4, dl, rows):
    n_rows = 2 * feat_pairs.shape[0]
    twice = lambda v: jnp.concatenate([v, v], axis=1)
    c = w4.shape[1] // 2
    w4bd = jnp.concatenate([jnp.concatenate([w4[:, :c], jnp.zeros_like(w4[:, c:])], axis=1),
                            jnp.concatenate([jnp.zeros_like(w4[:, :c]), w4[:, c:]], axis=1)], axis=0)
    args = (_split_hi_lo(_block_diag2(w1p)), twice(b1), _split_hi_lo(_block_diag2(w2)), twice(b2),
            _split_hi_lo(_block_diag2(w3)), twice(b3), twice(fq), _split_hi_lo(w4bd), dl)
    full = lambda a: pl.BlockSpec(a.shape, lambda i: (0,) * a.ndim)
    return pl.pallas_call(
        functools.partial(_filter_kernel, groups=rows // CH),
        grid=(n_rows // rows,),
        in_specs=[pl.BlockSpec((rows // 2, 2 * FEAT_W), lambda i: (i, 0))] + [full(a) for a in args],
        out_specs=pl.BlockSpec((rows, D_MODEL), lambda i: (i, 0)),
        out_shape=jax.ShapeDtypeStruct((n_rows, D_MODEL), F32),
        compiler_params=_cparams(("arbitrary",)),
        name="hy_filter",
    )(feat_pairs, *args)


FFT_G = 8


def _lanes(j, c):
    return slice(j * c, (j + 1) * c)


def _fft_a_data_kernel(m_ref, z_ref, o_ref):
    for j in range(FFT_G):
        rows = slice(j * ZCH, (j + 1) * ZCH)
        x = jnp.concatenate([z_ref[0, rows, :], z_ref[1, rows, :]], axis=0).astype(BF16)
        o_ref[j] = jnp.dot(m_ref[j], x, preferred_element_type=F32).astype(BF16)


def _fft_a_filt_kernel(m_ref, g_ref, o_ref):
    for j in range(FFT_G):
        gj = g_ref[j * CH:(j + 1) * CH, :].astype(BF16)
        o_ref[j] = jnp.dot(m_ref[j], gj, preferred_element_type=F32).astype(BF16)


def _fft_b_kernel(f_ref, fi_ref, yr_ref, yi_ref, gr_ref, gi_ref, o_ref):
    c = o_ref.shape[-1]
    for j in range(FFT_G):
        y = jnp.concatenate([yr_ref[:, _lanes(j, c)], yi_ref[:, _lanes(j, c)]], axis=0)
        yg = jnp.concatenate([gr_ref[:, _lanes(j, c)], gi_ref[:, _lanes(j, c)]], axis=0)
        x = jnp.dot(f_ref[...], y, preferred_element_type=F32)
        g = jnp.dot(f_ref[...], yg, preferred_element_type=F32) * (1.0 / NFFT)
        xr, xi = x[:CH], x[CH:]
        gr, gi = g[:CH], g[CH:]
        p = jnp.concatenate([xr * gr - xi * gi, xr * gi + xi * gr], axis=0).astype(BF16)
        o_ref[j] = jnp.dot(fi_ref[...], p, preferred_element_type=F32).astype(BF16)


def _fft_a_inv_kernel(m_ref, ur_ref, ui_ref, o_ref):
    c = ur_ref.shape[-1] // FFT_G
    for j in range(FFT_G):
        u = jnp.concatenate([ur_ref[:, _lanes(j, c)], ui_ref[:, _lanes(j, c)]], axis=0)
        y = jnp.dot(m_ref[j], u, preferred_element_type=F32)
        o_ref[:, :, j, :] = y.reshape(2, ZCH, c)


def _dft_tables():
    n2 = np.arange(CH)[:, None, None]
    k1 = np.arange(CH)[None, :, None]

    def theta(n1_count):
        n1 = np.arange(n1_count)[None, None, :]
        return 2.0 * np.pi * ((k1 * (CH * n1 + n2)) % NFFT) / NFFT

    th = theta(ZCH)
    c, s = np.cos(th), np.sin(th)
    m_a = np.concatenate([np.concatenate([c, s], axis=2), np.concatenate([-s, c], axis=2)], axis=1)
    m_ainv = np.transpose(m_a, (0, 2, 1))
    th = theta(CH)
    m_af = np.concatenate([np.cos(th), -np.sin(th)], axis=1)
    a = 2.0 * np.pi * ((np.arange(CH)[:, None] * np.arange(CH)[None, :]) % CH) / CH
    c, s = np.cos(a), np.sin(a)
    f2 = np.block([[c, s], [-s, c]])
    f2i = np.block([[c, -s], [s, c]])
    f = lambda t: jnp.asarray(t.astype(np.float32)).astype(BF16)
    return f(m_a), f(m_ainv), f(m_af), f(f2), f(f2i)


def _fft_conv(zp, g):
    c = zp.shape[-1]
    m_a, m_ainv, m_af, f2, f2i = _dft_tables()
    cp = _cparams(("arbitrary",), VMEM_LIMIT_BIG)
    steps = CH // FFT_G
    gc = FFT_G * c
    full2 = pl.BlockSpec((2 * CH, 2 * CH), lambda i: (0, 0))
    col_re = pl.BlockSpec((CH, gc), lambda i: (0, i))
    col_im = pl.BlockSpec((CH, gc), lambda i: (0, steps + i))
    blk = pl.BlockSpec((FFT_G, 2 * CH, c), lambda i: (i, 0, 0))
    spec_shape = jax.ShapeDtypeStruct((CH, 2 * CH, c), BF16)

    yg = pl.pallas_call(
        _fft_a_filt_kernel, grid=(steps,),
        in_specs=[pl.BlockSpec((FFT_G, 2 * CH, CH), lambda i: (i, 0, 0)),
                  pl.BlockSpec((FFT_G * CH, c), lambda i: (i, 0))],
        out_specs=blk, out_shape=spec_shape, compiler_params=cp, name="fft_a_filt",
    )(m_af, g)
    yg2 = yg.reshape(CH, 2 * CH * c)

    y = pl.pallas_call(
        _fft_a_data_kernel, grid=(steps,),
        in_specs=[pl.BlockSpec((FFT_G, 2 * CH, 2 * ZCH), lambda i: (i, 0, 0)),
                  pl.BlockSpec((2, FFT_G * ZCH, c), lambda i: (0, i, 0))],
        out_specs=blk, out_shape=spec_shape, compiler_params=cp, name="fft_a_data",
    )(m_a, zp)
    y2 = y.reshape(CH, 2 * CH * c)
    u = pl.pallas_call(
        _fft_b_kernel, grid=(steps,),
        in_specs=[full2, full2, col_re, col_im, col_re, col_im],
        out_specs=blk, out_shape=spec_shape, compiler_params=cp, name="fft_b",
    )(f2, f2i, y2, y2, yg2, yg2)
    u2 = u.reshape(CH, 2 * CH * c)
    yc = pl.pallas_call(
        _fft_a_inv_kernel, grid=(steps,),
        in_specs=[pl.BlockSpec((FFT_G, 2 * ZCH, 2 * CH), lambda i: (i, 0, 0)), col_re, col_im],
        out_specs=pl.BlockSpec((2, ZCH, FFT_G, c), lambda i: (0, 0, i, 0)),
        out_shape=jax.ShapeDtypeStruct((2, ZCH, CH, c), F32),
        compiler_params=cp, name="fft_a_inv",
    )(m_ainv, u2, u2)
    return yc.reshape(2, ZCH * CH, c)


def _far_kernel(far_ref, zm_ref, zl_ref, o_ref):
    g_hi = far_ref[HALF + FAR:HALF + 2 * FAR, :]
    ef = far_ref[0:FAR, :] - g_hi
    g_lo = far_ref[FAR:2 * FAR, :]
    hb = far_ref[HALF:HALF + FAR, :]
    zm = zm_ref[0, PAD:CH, :].astype(F32)
    zl = zl_ref[0, PAD:CH, :].astype(F32)
    row = lambda a, i: a[i:i + 1, :]
    eb = [row(hb, 0) - row(g_hi, 0)] + [row(hb, m) - row(g_lo, FAR - m) for m in range(1, FAR)]
    for j in range(FAR):
        acc_f = row(ef, j) * row(zm, 0)
        for i in range(1, j + 1):
            acc_f = acc_f + row(ef, j - i) * row(zm, i)
        o_ref[0, 0, j:j + 1, :] = acc_f
        acc_b = eb[0] * row(zl, j)
        for i in range(j + 1, FAR):
            acc_b = acc_b + eb[i - j] * row(zl, i)
        o_ref[0, 1, j:j + 1, :] = acc_b


def _far_correction(h_far, z, tp):
    b, _, c = z.shape
    last = tp // CH - 1
    return pl.pallas_call(
        _far_kernel, grid=(b,),
        in_specs=[pl.BlockSpec((CH, c), lambda bi: (0, 0)),
                  pl.BlockSpec((1, CH, c), lambda bi: (bi, 0, 0)),
                  pl.BlockSpec((1, CH, c), lambda bi: (bi, last, 0))],
        out_specs=pl.BlockSpec((1, 2, FAR, c), lambda bi: (bi, 0, 0, 0)),
        out_shape=jax.ShapeDtypeStruct((b, 2, FAR, c), F32),
        compiler_params=_cparams(("arbitrary",)),
        name="hy_far",
    )(h_far, z, z)


MG_TM = 640
MG_CN = 256
ROUTER_W = 128
AFF_PACK = ROUTER_W // N_EXPERTS
ROW_SUB, ROW_LANE = 8, 128


def _merge_kernel(h_ref, og_ref, x0_ref, z_ref, yc_ref, ga_ref, gb_ref, corr_ref, skip_ref,
                  wr_ref, wh_ref, wo_ref, g2_ref, wrt_ref,
                  h1_ref, xn_ref, aff_ref, affp_ref, afft_ref, pre_ref, mix_ref, h1s_ref, *, nt_b):
    i = pl.program_id(0)
    ib = i % nt_b
    pre_ref[...] = yc_ref[0].astype(F32) + z_ref[0].astype(F32) * skip_ref[...]

    @pl.when(ib == 0)
    def _():
        pre_ref[PAD:CH, :] += corr_ref[0, 1]

    @pl.when(ib == nt_b - 1)
    def _():
        pre_ref[MG_TM - FAR:MG_TM, :] += corr_ref[0, 0]

    chunks = [slice(c * MG_CN, (c + 1) * MG_CN) for c in range(D_MODEL // MG_CN)]
    og = og_ref[0]
    pre = (x0_ref[0].astype(F32) * pre_ref[...]).astype(BF16)
    for cols in chunks:
        ya = jnp.dot(og, wr_ref[:, cols], preferred_element_type=F32)
        yb = jnp.dot(pre, wh_ref[:, cols], preferred_element_type=F32)
        mix_ref[:, cols] = (ga_ref[0, :, cols].astype(F32) * ya + gb_ref[0, :, cols].astype(F32) * yb).astype(BF16)
    mixed = mix_ref[...]
    ss = jnp.zeros((MG_TM, 1), F32)
    for cols in chunks:
        h1 = h_ref[0, :, cols] + jnp.dot(mixed, wo_ref[:, cols], preferred_element_type=F32)
        h1s_ref[:, cols] = h1
        ss = ss + jnp.sum(h1 * h1, axis=-1, keepdims=True)
    rinv = lax.rsqrt(ss * (1.0 / D_MODEL) + RMS_EPS)
    h1 = h1s_ref[...]
    h1_ref[0] = h1.reshape(MG_TM, ROW_SUB, ROW_LANE)
    xn = h1 * rinv * g2_ref[...]
    xn_ref[0] = xn.reshape(MG_TM, ROW_SUB, ROW_LANE)
    logits = _dot3(xn, wrt_ref[0], wrt_ref[1])
    lane = lax.broadcasted_iota(I32, logits.shape, 1)
    logits = jnp.where(lane < N_EXPERTS, logits, -jnp.inf)
    m = jnp.max(logits, axis=-1, keepdims=True)
    e = jnp.exp(logits - m)
    aff = e / jnp.sum(e, axis=-1, keepdims=True)
    rows = lax.broadcasted_iota(I32, logits.shape, 0) + ib * MG_TM
    aff_ref[0] = jnp.where(rows >= PAD, aff, -1.0)
    lane_grp = lax.broadcasted_iota(I32, (MG_TM // AFF_PACK, ROUTER_W), 1) // N_EXPERTS
    packed = jnp.zeros((MG_TM // AFF_PACK, ROUTER_W), F32)
    for g in range(AFF_PACK):
        blk = aff_ref[0, pl.ds(g, MG_TM // AFF_PACK, stride=AFF_PACK), :]
        if g:
            blk = pltpu.roll(blk, N_EXPERTS * g, axis=1)
        packed = jnp.where(lane_grp == g, blk, packed)
    affp_ref[0] = packed
    afft_ref[0] = jnp.transpose(aff_ref[0])[:N_EXPERTS]


def _merge(h0, og, x0c, z, yc, proj3, corr, skip, w_ret, w_hy, w_o, g2, w_router_p, tp):
    b = h0.shape[0]
    nt_b = tp // MG_TM
    ga_blk0 = 6144 // D_MODEL
    row = lambda w: pl.BlockSpec((1, MG_TM, w), lambda i: (i // nt_b, i % nt_b, 0))
    full = lambda a: pl.BlockSpec(a.shape, lambda i: (0,) * a.ndim)
    return pl.pallas_call(
        functools.partial(_merge_kernel, nt_b=nt_b),
        grid=(b * nt_b,),
        in_specs=[row(D_MODEL), row(D_MODEL), row(D_MODEL), row(D_MODEL), row(D_MODEL),
                  pl.BlockSpec((1, MG_TM, D_MODEL), lambda i: (i // nt_b, i % nt_b, ga_blk0)),
                  pl.BlockSpec((1, MG_TM, D_MODEL), lambda i: (i // nt_b, i % nt_b, ga_blk0 + 1)),
                  pl.BlockSpec((1, 2, FAR, D_MODEL), lambda i: (i // nt_b, 0, 0, 0)),
                  full(skip), full(w_ret), full(w_hy), full(w_o), full(g2), full(w_router_p)],
        out_specs=[pl.BlockSpec((1, MG_TM, ROW_SUB, ROW_LANE), lambda i: (i // nt_b, i % nt_b, 0, 0)),
                   pl.BlockSpec((1, MG_TM, ROW_SUB, ROW_LANE), lambda i: (i // nt_b, i % nt_b, 0, 0)),
                   row(ROUTER_W),
                   pl.BlockSpec((1, MG_TM // AFF_PACK, ROUTER_W), lambda i: (i // nt_b, i % nt_b, 0)),
                   pl.BlockSpec((1, N_EXPERTS, MG_TM), lambda i: (i // nt_b, 0, i % nt_b))],
        out_shape=[jax.ShapeDtypeStruct((b, tp, ROW_SUB, ROW_LANE), F32),
                   jax.ShapeDtypeStruct((b, tp, ROW_SUB, ROW_LANE), F32),
                   jax.ShapeDtypeStruct((b, tp, ROUTER_W), F32),
                   jax.ShapeDtypeStruct((b, tp // AFF_PACK, ROUTER_W), F32),
                   jax.ShapeDtypeStruct((b, N_EXPERTS, tp), F32)],
        scratch_shapes=[pltpu.VMEM((MG_TM, D_MODEL), F32), pltpu.VMEM((MG_TM, D_MODEL), BF16),
                        pltpu.VMEM((MG_TM, D_MODEL), F32)],
        compiler_params=_cparams(("arbitrary",), VMEM_LIMIT_BIG),
        name="merge",
    )(h0, og, x0c, z, yc, proj3, proj3, corr, skip, w_ret, w_hy, w_o, g2, w_router_p)


TK_W = 128


TK_UNROLL = 5


def _select_kernel(aff_ref, affp_ref, low_ref, slot_ref, offs_ref, *, cap, nch):
    def chunk(c):
        r0 = pl.multiple_of(c * CH, CH)
        return aff_ref[0, pl.ds(r0, CH), :]

    def count(pred):
        cnt = jnp.sum(pred(affp_ref[0]).astype(I32), axis=0, keepdims=True)
        for shift in (N_EXPERTS, 2 * N_EXPERTS, 4 * N_EXPERTS):
            cnt = cnt + pltpu.roll(cnt, shift, axis=1)
        return cnt

    def search(it, bits):
        cand = bits | jnp.left_shift(1, 29 - it)
        cand_f = pltpu.bitcast(cand, F32)
        return jnp.where(count(lambda a: a >= cand_f) >= cap, cand, bits)

    thr = pltpu.bitcast(lax.fori_loop(0, 30, search, jnp.zeros((1, ROUTER_W), I32)), F32)
    need = (cap - count(lambda a: a > thr)).astype(F32)
    low = low_ref[...]

    def scan(it, carry):
        c_eq, c_sel = carry
        for u in range(TK_UNROLL):
            c = it * TK_UNROLL + u
            a = chunk(c)
            eq = a == thr
            eq_f = eq.astype(F32)
            eq_rank = jnp.dot(low, eq_f, preferred_element_type=F32) + c_eq
            sel = jnp.logical_or(a > thr, jnp.logical_and(eq, eq_rank < need))
            sel_f = sel.astype(F32)
            slot = jnp.dot(low, sel_f, preferred_element_type=F32) + c_sel
            r0 = pl.multiple_of(c * CH, CH)
            slot_ref[0, pl.ds(r0, CH), :] = jnp.where(sel, slot, -1.0).astype(I32)
            offs_ref[0, c] = c_sel.astype(I32)
            c_eq = c_eq + jnp.sum(eq_f, axis=0, keepdims=True)
            c_sel = c_sel + jnp.sum(sel_f, axis=0, keepdims=True)
        return c_eq, c_sel

    zero = jnp.zeros((1, ROUTER_W), F32)
    lax.fori_loop(0, nch // TK_UNROLL, scan, (zero, zero))


def _extract_kernel(offs_ref, slot_ref, idx_ref, *, nch, nwin):
    idx_ref[...] = jnp.zeros_like(idx_ref)
    lane = lax.broadcasted_iota(I32, (CH, TK_W), 1)
    trow = lax.broadcasted_iota(I32, (CH, TK_W), 0)

    def per_chunk(c, carry):
        r0 = pl.multiple_of(c * CH, CH)
        slots = slot_ref[0, pl.ds(r0, CH), :]
        tpos = (trow + r0).astype(F32)
        for e in range(N_EXPERTS):
            col = jnp.broadcast_to(slots[:, e:e + 1], (CH, TK_W))
            w0 = offs_ref[0, 0, c * N_EXPERTS + e] // TK_W
            for dw in range(2):
                w = w0 + dw
                base = jnp.where(w < nwin, w * TK_W, -2 * TK_W)
                hit = col == lane + base
                row = e * nwin + jnp.minimum(w, nwin - 1)
                idx_ref[row] += jnp.sum(jnp.where(hit, tpos, 0.0), axis=0, keepdims=True)
        return carry

    lax.fori_loop(0, nch, per_chunk, 0)


def _topk(aff, aff_packed, cap, slots):
    b, tp, _ = aff.shape
    nch = tp // CH
    nwin = -(-slots // TK_W)
    low = jnp.asarray(np.tril(np.ones((CH, CH), np.float32), k=-1))
    pack = AFF_PACK
    assert nch % TK_UNROLL == 0 and tp % pack == 0
    slot, offs = pl.pallas_call(
        functools.partial(_select_kernel, cap=cap, nch=nch),
        grid=(b,),
        in_specs=[pl.BlockSpec((1, tp, ROUTER_W), lambda bi: (bi, 0, 0)),
                  pl.BlockSpec((1, tp // pack, ROUTER_W), lambda bi: (bi, 0, 0)),
                  pl.BlockSpec((CH, CH), lambda bi: (0, 0))],
        out_specs=[pl.BlockSpec((1, tp, ROUTER_W), lambda bi: (bi, 0, 0)),
                   pl.BlockSpec((1, nch, 1, ROUTER_W), lambda bi: (bi, 0, 0, 0))],
        out_shape=[jax.ShapeDtypeStruct((b, tp, ROUTER_W), I32),
                   jax.ShapeDtypeStruct((b, nch, 1, ROUTER_W), I32)],
        compiler_params=_cparams(("arbitrary",)),
        name="topk_select",
    )(aff, aff_packed, low)
    offs_s = offs[:, :, 0, :N_EXPERTS].reshape(b, 1, nch * N_EXPERTS)
    rows = pl.BlockSpec((None, N_EXPERTS * nwin, 1, TK_W), lambda bi: (bi, 0, 0, 0))
    out = jax.ShapeDtypeStruct((b, N_EXPERTS * nwin, 1, TK_W), F32)
    idx = pl.pallas_call(
        functools.partial(_extract_kernel, nch=nch, nwin=nwin),
        grid=(b,),
        in_specs=[pl.BlockSpec((1, 1, nch * N_EXPERTS), lambda bi: (bi, 0, 0), memory_space=pltpu.SMEM),
                  pl.BlockSpec((1, tp, ROUTER_W), lambda bi: (bi, 0, 0))],
        out_specs=rows,
        out_shape=out,
        compiler_params=_cparams(("arbitrary",)),
        name="topk_extract",
    )(offs_s, slot)
    idx = idx.reshape(b * N_EXPERTS, 1, nwin * TK_W)[:, :, :slots].astype(I32)
    return idx


FF_TF = 1024
MOE_CN = 256


MOE_UNROLL = 4


def _moe_kernel(idx_ref, idx_next_ref, xn_hbm, wg_ref, wu_ref, wd_ref, o_ref,
                xe32_ref, xe_ref, hid_ref, acc_ref, sem,
                *, slots, nf):
    i = pl.program_id(0)
    f = pl.program_id(1)
    buf = i % 2
    share = slots // nf

    def row_copy(ids_ref, s, b):
        return pltpu.make_async_copy(xn_hbm.at[ids_ref[0, 0, s]], xe32_ref.at[b, s], sem.at[b])

    def for_rows(lo, n, fn):
        def body(k, carry):
            for u in range(MOE_UNROLL):
                fn(lo + k * MOE_UNROLL + u)
            return carry

        lax.fori_loop(0, n // MOE_UNROLL, body, 0)

    @pl.when(jnp.logical_and(i == 0, f == 0))
    def _():
        for_rows(0, slots, lambda s: row_copy(idx_ref, s, 0).start())

    @pl.when(f == 0)
    def _():
        for_rows(0, slots, lambda s: row_copy(idx_ref, s, buf).wait())
        xe_ref[...] = xe32_ref[buf].reshape(slots, D_MODEL).astype(BF16)

    @pl.when(f == 0)
    def _():
        acc_ref[...] = jnp.zeros_like(acc_ref)

    n_up, n_down = FF_TF // MOE_CN, D_MODEL // MOE_CN
    per_chunk = share // (n_up + n_down)

    def prefetch(chunk):
        base = f * share + chunk * per_chunk
        for u in range(per_chunk):
            row_copy(idx_next_ref, base + u, 1 - buf).start()

    xe = xe_ref[...]
    for c in range(n_up):
        cols = slice(c * MOE_CN, (c + 1) * MOE_CN)
        gg = jnp.dot(xe, wg_ref[0, :, cols].astype(BF16), preferred_element_type=F32)
        uu = jnp.dot(xe, wu_ref[0, :, cols].astype(BF16), preferred_element_type=F32)
        hid_ref[:, cols] = (gg * jax.nn.sigmoid(gg) * uu).astype(BF16)
        prefetch(c)
    hid = hid_ref[...]
    for c in range(n_down):
        cols = slice(c * MOE_CN, (c + 1) * MOE_CN)
        part = jnp.dot(hid, wd_ref[0, :, cols].astype(BF16), preferred_element_type=F32)
        acc_ref[:, cols] = acc_ref[:, cols] + part
        prefetch(n_up + c)

    @pl.when(f == pl.num_programs(1) - 1)
    def _():
        o_ref[0] = acc_ref[...].reshape(slots, ROW_SUB, ROW_LANE)

    @pl.when(jnp.logical_and(i == pl.num_programs(0) - 1, f == pl.num_programs(1) - 1))
    def _():
        for_rows(0, slots, lambda s: row_copy(idx_next_ref, s, 1 - buf).wait())


def _moe_ffn(idx, xn_flat, w_gate, w_up, w_down, slots):
    be = idx.shape[0]
    nf = D_FF // FF_TF
    assert slots % MOE_UNROLL == 0 and slots % (nf * (FF_TF // MOE_CN + D_MODEL // MOE_CN)) == 0
    return pl.pallas_call(
        functools.partial(_moe_kernel, slots=slots, nf=nf),
        grid=(be, nf),
        in_specs=[pl.BlockSpec((1, 1, slots), lambda i, f: (i, 0, 0), memory_space=pltpu.SMEM),
                  pl.BlockSpec((1, 1, slots), lambda i, f: (jnp.minimum(i + 1, be - 1), 0, 0),
                               memory_space=pltpu.SMEM),
                  pl.BlockSpec(memory_space=pl.ANY),
                  pl.BlockSpec((1, D_MODEL, FF_TF), lambda i, f: (i % N_EXPERTS, 0, f)),
                  pl.BlockSpec((1, D_MODEL, FF_TF), lambda i, f: (i % N_EXPERTS, 0, f)),
                  pl.BlockSpec((1, FF_TF, D_MODEL), lambda i, f: (i % N_EXPERTS, f, 0))],
        out_specs=pl.BlockSpec((1, slots, ROW_SUB, ROW_LANE), lambda i, f: (i, 0, 0, 0)),
        out_shape=jax.ShapeDtypeStruct((be, slots, ROW_SUB, ROW_LANE), F32),
        scratch_shapes=[pltpu.VMEM((2, slots, ROW_SUB, ROW_LANE), F32), pltpu.VMEM((slots, D_MODEL), BF16),
                        pltpu.VMEM((slots, FF_TF), BF16), pltpu.VMEM((slots, D_MODEL), F32),
                        pltpu.SemaphoreType.DMA((2,))],
        compiler_params=_cparams(("arbitrary", "arbitrary"), VMEM_LIMIT_BIG),
        name="moe_ffn",
    )(idx, idx, xn_flat, w_gate, w_up, w_down)


CB_U = 6
CB_NB = 512


def _combine_kernel(idx_ref, gate_ref, h1_hbm, ye_ref, g_ref, o_hbm, acc_ref, stage_ref, sem, osem, *, cap, tp):
    b = pl.program_id(0)
    e = pl.program_id(1)

    @pl.when(e == 0)
    def _():
        cp = pltpu.make_async_copy(h1_hbm.at[b], acc_ref, sem)
        cp.start()
        cp.wait()

    def rmw(g, carry):
        ts = [idx_ref[0, 0, g * CB_U + u] for u in range(CB_U)]
        vals = [acc_ref[ts[u]] + gate_ref[0, 0, ts[u]] * ye_ref[g * CB_U + u] for u in range(CB_U)]
        for u in range(CB_U):
            acc_ref[ts[u]] = vals[u]
        return carry

    lax.fori_loop(0, cap // CB_U, rmw, 0)

    @pl.when(e == pl.num_programs(1) - 1)
    def _():
        gamma = g_ref[...]
        n_blocks = (tp - CH) // CB_NB

        def out_copy(k):
            return pltpu.make_async_copy(stage_ref.at[k % 2], o_hbm.at[b, pl.ds(k * CB_NB, CB_NB), :],
                                         osem.at[k % 2])

        for k in range(n_blocks):
            x = acc_ref[pl.ds(CH + k * CB_NB, CB_NB)]
            ms = jnp.sum(jnp.sum(x * x, axis=2, keepdims=True), axis=1, keepdims=True) * (1.0 / D_MODEL)
            y = x * lax.rsqrt(ms + RMS_EPS) * gamma
            if k >= 2:
                out_copy(k - 2).wait()
            stage_ref[k % 2] = y.reshape(CB_NB, D_MODEL)
            out_copy(k).start()
        for k in range(max(n_blocks - 2, 0), n_blocks):
            out_copy(k).wait()


def _combine(idx_local, gate, h1, ye, gf, cap, slots):
    b, tp = h1.shape[:2]
    assert cap % CB_U == 0 and (tp - CH) % CB_NB == 0
    return pl.pallas_call(
        functools.partial(_combine_kernel, cap=cap, tp=tp),
        grid=(b, N_EXPERTS),
        in_specs=[pl.BlockSpec((1, 1, slots), lambda bi, e: (bi * N_EXPERTS + e, 0, 0), memory_space=pltpu.SMEM),
                  pl.BlockSpec((1, 1, tp), lambda bi, e: (bi * N_EXPERTS + e, 0, 0), memory_space=pltpu.SMEM),
                  pl.BlockSpec(memory_space=pl.ANY),
                  pl.BlockSpec((None, slots, ROW_SUB, ROW_LANE), lambda bi, e: (bi * N_EXPERTS + e, 0, 0, 0)),
                  pl.BlockSpec((1, ROW_SUB, ROW_LANE), lambda bi, e: (0, 0, 0))],
        out_specs=pl.BlockSpec(memory_space=pl.ANY),
        out_shape=jax.ShapeDtypeStruct((b, tp - CH, D_MODEL), F32),
        scratch_shapes=[pltpu.VMEM((tp, ROW_SUB, ROW_LANE), F32), pltpu.VMEM((2, CB_NB, D_MODEL), F32),
                        pltpu.SemaphoreType.DMA(()), pltpu.SemaphoreType.DMA((2,))],
        compiler_params=_cparams(("arbitrary", "arbitrary"), VMEM_LIMIT_BIG),
        name="combine",
    )(idx_local, gate, h1, ye, gf.reshape(1, ROW_SUB, ROW_LANE))


def _rope_tables(tp, b):
    half = RET_QK_HEAD // 2
    pos = np.arange(tp, dtype=np.float64) - PAD
    inv = ROPE_BASE ** (-np.arange(half, dtype=np.float64) / half)
    ang = pos[:, None] * inv[None, :]
    cos, sin = np.cos(ang), np.sin(ang)
    tile = lambda t: jnp.asarray(np.tile(t, (b, 1)).astype(np.float32))
    return tile(np.concatenate([cos, cos], axis=1)), tile(np.concatenate([-sin, sin], axis=1))


def _filter_features(t_len):
    half = NFFT // 2
    q = np.arange(NFFT)
    r = CH * (q % CH) + q // CH
    p_main = np.where(r < half, r, NFFT - r)
    valid_main = (r != half).astype(np.float32)
    m = np.arange(FAR)
    pad = np.zeros(HALF - 2 * FAR, np.int64)
    p_far = np.concatenate([half + m, half - FAR + m, pad, half + m, half - m, pad])
    valid_far = np.ones(CH, np.float32)
    valid_far[HALF + FAR] = 0.0
    p = np.concatenate([p_main, p_far]).astype(np.float64)
    valid = np.concatenate([valid_main, valid_far]).astype(np.float64)
    t_norm = p / (t_len - 1)
    bands = (HY_EMB_DIM - 1) // 2
    fr = np.linspace(1e-4, bands - 1, bands)
    ang = (2.0 * math.pi * p / t_len)[:, None] * fr[None, :]
    feat = np.concatenate([t_norm[:, None], np.cos(ang), -np.sin(ang), valid[:, None]], axis=-1)
    feat = np.pad(feat, ((0, 0), (0, FEAT_W - feat.shape[1])))
    pairs = feat.reshape(-1, 2, HALF, FEAT_W).transpose(0, 2, 1, 3).reshape(-1, 2 * FEAT_W)
    return jnp.asarray(pairs.astype(np.float32))


def kernel(x, meta_tokens, norm1_g, w_in, ret_decay_fwd, ret_decay_bwd, ret_head_norm_g, w_ret_out,
           hy_conv_w, hy_conv_b, hy_filt_w1, hy_filt_b1, hy_filt_w2, hy_filt_b2, hy_filt_w3, hy_filt_b3,
           hy_filt_freq, hy_filt_w4, hy_skip, w_hy_out, w_o, norm2_g, w_router, w_exp_gate, w_exp_up,
           w_exp_down, final_norm_g):
    b, seq, d = x.shape
    t_len = seq + N_META
    tp = PAD + t_len
    assert d == D_MODEL and (b * tp) % IP_TM == 0 and tp % MG_TM == 0 and tp % CH == 0 and t_len - NFFT // 2 == FAR
    cap = EC_CAPACITY * t_len // N_EXPERTS
    slots = -(-cap // 16) * 16
    l = 0

    meta = jnp.broadcast_to(meta_tokens[None].astype(x.dtype), (b, N_META, d))
    h0 = jnp.concatenate([jnp.zeros((b, PAD, d), x.dtype), meta, x], axis=1)

    cs, sn = _rope_tables(tp, b)
    proj = _in_proj(h0.reshape(b * tp, d), norm1_g[l][None], w_in[l], cs, sn, tp)
    proj3 = proj.reshape(b, tp, IN_PROJ_W)

    lf = jax.nn.log_sigmoid(ret_decay_fwd[l].astype(F32))
    lb = jax.nn.log_sigmoid(ret_decay_bwd[l].astype(F32))
    og = _retention(proj3, lf, lb, ret_head_norm_g[l][None], tp)

    x0c, z, zp = _hy_prep(proj3, hy_conv_w[l], hy_conv_b[l][None], tp)

    feat = _filter_features(t_len)
    w1p = jnp.pad(hy_filt_w1[l].astype(F32), ((0, FEAT_W - HY_EMB_DIM), (0, 0)))
    max_decay = math.log(HY_DECAY_TARGET) / HY_FAST_DECAY_PCT
    min_decay = math.log(HY_DECAY_TARGET) / HY_SLOW_DECAY_PCT
    dl = jnp.abs(jnp.linspace(min_decay, max_decay, D_MODEL, dtype=F32))[None]
    fargs = (w1p, hy_filt_b1[l][None].astype(F32), hy_filt_w2[l].astype(F32), hy_filt_b2[l][None].astype(F32),
             hy_filt_w3[l].astype(F32), hy_filt_b3[l][None].astype(F32), hy_filt_freq[l][None].astype(F32),
             hy_filt_w4[l].astype(F32), dl)
    g = _filters(feat[:NFFT // 2], *fargs, FT_ROWS)
    h_far = _filters(feat[NFFT // 2:], *fargs, CH)
    yc = _fft_conv(zp, g)
    corr = _far_correction(h_far, z, tp)

    w_router_p = jnp.pad(w_router[l].astype(F32), ((0, 0), (0, ROUTER_W - N_EXPERTS)))
    w_router_p = _split_hi_lo(w_router_p)
    h1, xn2, aff, aff_packed, aff_t = _merge(h0, og, x0c, z, yc, proj3, corr, hy_skip[l][None].astype(F32),
                          w_ret_out[l].astype(BF16), w_hy_out[l].astype(BF16), w_o[l].astype(BF16),
                          norm2_g[l][None].astype(F32), w_router_p, tp)

    idx = _topk(aff, aff_packed, cap, slots)
    live = (jnp.arange(slots) < cap)[None, None, :]
    idx_local = jnp.where(live, idx, PAD)
    idx_flat = idx_local + (jnp.arange(b * N_EXPERTS, dtype=I32) // N_EXPERTS * tp)[:, None, None]
    ye = _moe_ffn(idx_flat, xn2.reshape(b * tp, ROW_SUB, ROW_LANE),
                  w_exp_gate[l], w_exp_up[l], w_exp_down[l], slots)
    return _combine(idx_local, aff_t.reshape(b * N_EXPERTS, 1, tp), h1, ye, final_norm_g.astype(F32), cap, slots)
```

```python
import functools
import math

import numpy as np
import jax
import jax.numpy as jnp
from jax import lax
from jax.experimental import pallas as pl
from jax.experimental.pallas import tpu as pltpu

F32 = jnp.float32
BF16 = jnp.bfloat16
I32 = jnp.int32

D_MODEL = 1024
N_META = 16
RET_HEADS = 4
RET_QK_HEAD = 128
RET_V_HEAD = 256
ROPE_BASE = 10000.0
HY_EMB_DIM = 33
HY_FILTER_ORDER = 64
HY_FAST_DECAY_PCT = 0.3
HY_SLOW_DECAY_PCT = 1.5
HY_DECAY_TARGET = 1e-2
N_EXPERTS = 16
EC_CAPACITY = 2
D_FF = 2 * D_MODEL
RMS_EPS = 1e-6
IN_PROJ_W = 8192

CH = 128
PAD = CH - N_META
NFFT = 16384
ZCH = 80
FAR = 16

VMEM_LIMIT_BIG = 56 * 1024 * 1024
VMEM_LIMIT_MID = 40 * 1024 * 1024


def _cparams(sem, vmem=VMEM_LIMIT_MID):
    return pltpu.CompilerParams(dimension_semantics=sem, vmem_limit_bytes=vmem)


def _split_hi_lo(w):
    w = w.astype(F32)
    hi = w.astype(BF16)
    return jnp.stack([hi, (w - hi.astype(F32)).astype(BF16)])


def _dot3(a, w_hi, w_lo):
    a_hi = a.astype(BF16)
    a_lo = (a - a_hi.astype(F32)).astype(BF16)
    return (jnp.dot(a_hi, w_hi, preferred_element_type=F32) + jnp.dot(a_lo, w_hi, preferred_element_type=F32)
            + jnp.dot(a_hi, w_lo, preferred_element_type=F32))


IP_TM = 1280
IP_TN = 1024
IP_CN = 256


def _padded_chunk(flat_chunk, nch, head_ref, x_ref):
    return jnp.where(flat_chunk % nch == 0, head_ref[...], x_ref[...])


def _x_chunk_index(flat_chunk, nch, n_x_chunks):
    return jnp.clip(flat_chunk - flat_chunk // nch - 1, 0, n_x_chunks - 1)


def _inproj_kernel(*refs, nch):
    n_in = IP_TM // CH
    x_refs, (head_ref, g_ref, w_ref, cs_ref, sn_ref, o_ref, xn_ref) = refs[:n_in], refs[n_in:]
    i = pl.program_id(0)
    j = pl.program_id(1)

    @pl.when(j == 0)
    def _():
        for k in range(n_in):
            x = _padded_chunk(i * n_in + k, nch, head_ref, x_refs[k])
            ms = jnp.mean(x * x, axis=-1, keepdims=True)
            xn_ref[k * CH:(k + 1) * CH, :] = (x * lax.rsqrt(ms + RMS_EPS) * g_ref[...]).astype(BF16)

    def run(epilogue):
        for c in range(IP_TN // IP_CN):
            cols = slice(c * IP_CN, (c + 1) * IP_CN)
            acc = jnp.dot(xn_ref[...], w_ref[:, cols].astype(BF16), preferred_element_type=F32)
            epilogue(c, cols, acc)

    def rotary(c, cols, acc):
        scale = 1.0 if c < (IP_TN // IP_CN) // 2 else RET_QK_HEAD ** -0.5
        cs = cs_ref[...] * scale
        sn = sn_ref[...] * scale
        for hh in range(IP_CN // RET_QK_HEAD):
            xh = acc[:, hh * RET_QK_HEAD:(hh + 1) * RET_QK_HEAD]
            rot = xh * cs + pltpu.roll(xh, RET_QK_HEAD // 2, axis=1) * sn
            lo = c * IP_CN + hh * RET_QK_HEAD
            o_ref[:, lo:lo + RET_QK_HEAD] = rot.astype(BF16)

    def raw(c, cols, acc):
        o_ref[:, cols] = acc.astype(BF16)

    def swish(c, cols, acc):
        o_ref[:, cols] = (acc * jax.nn.sigmoid(acc)).astype(BF16)

    def sigm(c, cols, acc):
        o_ref[:, cols] = jax.nn.sigmoid(acc).astype(BF16)

    pl.when(j == 0)(lambda: run(rotary))
    pl.when(jnp.logical_or(j == 1, jnp.logical_and(j >= 3, j < 6)))(lambda: run(raw))
    pl.when(j == 2)(lambda: run(swish))
    pl.when(j >= 6)(lambda: run(sigm))


def _in_proj(x_flat, head, g1, w_in_bf, cs, sn, tp, b):
    n_rows = b * tp
    nch = tp // CH
    n_in = IP_TM // CH
    n_x = x_flat.shape[0] // CH
    x_spec = lambda k: pl.BlockSpec((CH, D_MODEL), lambda i, j: (_x_chunk_index(i * n_in + k, nch, n_x), 0))
    return pl.pallas_call(
        functools.partial(_inproj_kernel, nch=nch),
        grid=(n_rows // IP_TM, IN_PROJ_W // IP_TN),
        in_specs=[x_spec(k) for k in range(n_in)] + [
            pl.BlockSpec((CH, D_MODEL), lambda i, j: (0, 0)),
            pl.BlockSpec((1, D_MODEL), lambda i, j: (0, 0)),
            pl.BlockSpec((D_MODEL, IP_TN), lambda i, j: (0, j)),
            pl.BlockSpec((IP_TM, RET_QK_HEAD), lambda i, j: (i, 0)),
            pl.BlockSpec((IP_TM, RET_QK_HEAD), lambda i, j: (i, 0)),
        ],
        out_specs=pl.BlockSpec((IP_TM, IP_TN), lambda i, j: (i, j)),
        out_shape=jax.ShapeDtypeStruct((n_rows, IN_PROJ_W), BF16),
        scratch_shapes=[pltpu.VMEM((IP_TM, D_MODEL), BF16)],
        compiler_params=_cparams(("arbitrary", "arbitrary")),
        name="in_proj",
    )(*([x_flat] * n_in), head, g1, w_in_bf, cs, sn)


RET_UNROLL = 13


def _ret_kernel(lf_ref, lb_ref, q_ref, k_ref, v_ref, gr_ref, gn_ref, o_ref, ob_ref, s_ref, *, nch):
    h = pl.program_id(1)
    lf = lf_ref[h]
    lb = lb_ref[h]
    ri = lax.broadcasted_iota(I32, (CH, CH), 0).astype(F32)
    ci = lax.broadcasted_iota(I32, (CH, CH), 1).astype(F32)
    diff = ri - ci
    mask = jnp.exp(jnp.where(diff >= 0, lf * diff, -lb * diff))
    w_end = jnp.exp(lf * (CH - 1.0 - ri))
    w_start = jnp.exp(lb * ri)
    qw_f = jnp.exp(lf * (ri + 1.0))
    qw_b = jnp.exp(lb * (CH - ri))
    dec_f = jnp.exp(jnp.full((CH, RET_V_HEAD), lf * CH, F32))
    dec_b = jnp.exp(jnp.full((CH, RET_V_HEAD), lb * CH, F32))
    tn_dims = (((0,), (0,)), ((), ()))
    nt_dims = (((1,), (1,)), ((), ()))

    s_ref[...] = jnp.zeros_like(s_ref)

    def bwd(it, carry):
        s = s_ref[...]
        for u in range(RET_UNROLL):
            n = nch - 1 - (it * RET_UNROLL + u)
            r0 = pl.multiple_of(n * CH, CH)
            q = q_ref[0, pl.ds(r0, CH), :].astype(F32)
            k = k_ref[0, pl.ds(r0, CH), :].astype(F32)
            v = v_ref[0, pl.ds(r0, CH), :]
            ob_ref[pl.ds(r0, CH), :] = jnp.dot((q * qw_b).astype(BF16), s.astype(BF16),
                                               preferred_element_type=F32)
            a = lax.dot_general((k * w_start).astype(BF16), v, tn_dims, preferred_element_type=F32)
            s = s * dec_b + a
        s_ref[...] = s
        return carry

    lax.fori_loop(0, nch // RET_UNROLL, bwd, 0)

    s_ref[...] = jnp.zeros_like(s_ref)
    gn = gn_ref[...]

    def fwd(it, carry):
        s = s_ref[...]
        for u in range(RET_UNROLL):
            n = it * RET_UNROLL + u
            r0 = pl.multiple_of(n * CH, CH)
            qb = q_ref[0, pl.ds(r0, CH), :]
            kb = k_ref[0, pl.ds(r0, CH), :]
            v = v_ref[0, pl.ds(r0, CH), :]
            q = qb.astype(F32)
            k = kb.astype(F32)
            scores = lax.dot_general(qb, kb, nt_dims, preferred_element_type=F32) * mask
            o = jnp.dot(scores.astype(BF16), v, preferred_element_type=F32)
            o = o + jnp.dot((q * qw_f).astype(BF16), s.astype(BF16), preferred_element_type=F32)
            o = o + ob_ref[pl.ds(r0, CH), :]
            a = lax.dot_general((k * w_end).astype(BF16), v, tn_dims, preferred_element_type=F32)
            s = s * dec_f + a
            y = o * lax.rsqrt(jnp.mean(o * o, axis=-1, keepdims=True) + RMS_EPS) * gn
            o_ref[0, pl.ds(r0, CH), :] = (y * gr_ref[0, pl.ds(r0, CH), :].astype(F32)).astype(BF16)
        s_ref[...] = s
        return carry

    lax.fori_loop(0, nch // RET_UNROLL, fwd, 0)


def _retention(proj3, lf, lb, gn, tp):
    b = proj3.shape[0]
    nch = tp // CH
    qk_blocks = (RET_HEADS * RET_QK_HEAD) // RET_QK_HEAD
    v_blk0 = (2 * RET_HEADS * RET_QK_HEAD) // RET_V_HEAD
    g_blk0 = v_blk0 + RET_HEADS
    smem = pl.BlockSpec(memory_space=pltpu.SMEM)
    return pl.pallas_call(
        functools.partial(_ret_kernel, nch=nch),
        grid=(b, RET_HEADS),
        in_specs=[
            smem, smem,
            pl.BlockSpec((1, tp, RET_QK_HEAD), lambda bi, h: (bi, 0, h)),
            pl.BlockSpec((1, tp, RET_QK_HEAD), lambda bi, h: (bi, 0, qk_blocks + h)),
            pl.BlockSpec((1, tp, RET_V_HEAD), lambda bi, h: (bi, 0, v_blk0 + h)),
            pl.BlockSpec((1, tp, RET_V_HEAD), lambda bi, h: (bi, 0, g_blk0 + h)),
            pl.BlockSpec((1, RET_V_HEAD), lambda bi, h: (0, h)),
        ],
        out_specs=pl.BlockSpec((1, tp, RET_V_HEAD), lambda bi, h: (bi, 0, h)),
        out_shape=jax.ShapeDtypeStruct((b, tp, RET_HEADS * RET_V_HEAD), BF16),
        scratch_shapes=[pltpu.VMEM((tp, RET_V_HEAD), F32), pltpu.VMEM((RET_QK_HEAD, RET_V_HEAD), F32)],
        compiler_params=_cparams(("arbitrary", "arbitrary"), VMEM_LIMIT_BIG),
        name="retention",
    )(lf, lb, proj3, proj3, proj3, proj3, gn)


HP_CW = 128


def _hyprep_kernel(u0_ref, u1_ref, u2_ref, w0_ref, w1_ref, w2_ref, b0_ref, b1_ref, b2_ref,
                   x0_ref, z_ref, zp_ref, *, nch):
    rows = lax.broadcasted_iota(I32, (CH, HP_CW), 0)
    halo = 16
    zp_ref[...] = jnp.zeros_like(zp_ref)

    def conv(u_ref, w_ref, b_ref, n, r0):
        cur = u_ref[0, pl.ds(r0, CH), :].astype(F32)
        rp = pl.multiple_of(jnp.maximum(r0 - halo, 0), halo)
        rn = pl.multiple_of(jnp.minimum(r0 + CH, (nch - 1) * CH), halo)
        prev = u_ref[0, pl.ds(rp, halo), :].astype(F32)[halo - 1:halo, :]
        nxt = u_ref[0, pl.ds(rn, halo), :].astype(F32)[0:1, :]
        prev = jnp.where(n > 0, prev, 0.0)
        nxt = jnp.where(n < nch - 1, nxt, 0.0)
        up = jnp.where(rows == 0, prev, pltpu.roll(cur, 1, axis=0))
        dn = jnp.where(rows == CH - 1, nxt, pltpu.roll(cur, CH - 1, axis=0))
        w = w_ref[...]
        return up * w[0:1, :] + cur * w[1:2, :] + dn * w[2:3, :] + b_ref[...]

    def body(n, carry):
        r0 = pl.multiple_of(n * CH, CH)
        x0 = conv(u0_ref, w0_ref, b0_ref, n, r0)
        x1 = conv(u1_ref, w1_ref, b1_ref, n, r0)
        vv = conv(u2_ref, w2_ref, b2_ref, n, r0)
        z = jnp.where(rows + r0 >= PAD, x1 * vv, 0.0)
        x0_ref[0, pl.ds(r0, CH), :] = x0.astype(BF16)
        z_ref[0, pl.ds(r0, CH), :] = z.astype(BF16)
        zp_ref[0, pl.ds(n, CH, stride=ZCH), :] = z
        return carry

    lax.fori_loop(0, nch, body, 0)


def _hy_prep(proj3, conv_w, conv_b, tp):
    b = proj3.shape[0]
    nch = tp // CH
    ncb = D_MODEL // HP_CW
    u_blk0 = 3072 // HP_CW
    uspec = lambda s: pl.BlockSpec((1, tp, HP_CW), lambda bi, c: (bi, 0, u_blk0 + s * ncb + c))
    wspec = lambda s: pl.BlockSpec((3, HP_CW), lambda bi, c: (0, s * ncb + c))
    bspec = lambda s: pl.BlockSpec((1, HP_CW), lambda bi, c: (0, s * ncb + c))
    return pl.pallas_call(
        functools.partial(_hyprep_kernel, nch=nch),
        grid=(b, ncb),
        in_specs=[uspec(0), uspec(1), uspec(2), wspec(0), wspec(1), wspec(2), bspec(0), bspec(1), bspec(2)],
        out_specs=[pl.BlockSpec((1, tp, HP_CW), lambda bi, c: (bi, 0, c)),
                   pl.BlockSpec((1, tp, HP_CW), lambda bi, c: (bi, 0, c)),
                   pl.BlockSpec((1, CH * ZCH, HP_CW), lambda bi, c: (bi, 0, c))],
        out_shape=[jax.ShapeDtypeStruct((b, tp, D_MODEL), BF16),
                   jax.ShapeDtypeStruct((b, tp, D_MODEL), BF16),
                   jax.ShapeDtypeStruct((b, CH * ZCH, D_MODEL), F32)],
        compiler_params=_cparams(("arbitrary", "arbitrary")),
        name="hy_prep",
    )(proj3, proj3, proj3, conv_w, conv_w, conv_w, conv_b, conv_b, conv_b)


FEAT_W = 128


FT_ROWS = 512
HALF = CH // 2


def _filter_kernel(feat_ref, w1_ref, b1_ref, w2_ref, b2_ref, w3_ref, b3_ref, fq_ref, w4_ref, dl_ref, o_ref,
                   *, groups):
    feat = feat_ref[...]
    fq = fq_ref[...]
    hdn = jnp.sin(fq * (_dot3(feat, w1_ref[0], w1_ref[1]) + b1_ref[...]))
    hdn = jnp.sin(fq * (_dot3(hdn, w2_ref[0], w2_ref[1]) + b2_ref[...]))
    hdn = jnp.sin(fq * (_dot3(hdn, w3_ref[0], w3_ref[1]) + b3_ref[...]))
    filt = _dot3(hdn, w4_ref[0], w4_ref[1])
    dl = dl_ref[...]
    for d in range(2):
        f0 = d * FEAT_W
        scale = jnp.exp(-feat[:, f0:f0 + 1] * dl) * feat[:, f0 + HY_EMB_DIM:f0 + HY_EMB_DIM + 1]
        fd = filt[:, d * D_MODEL:(d + 1) * D_MODEL] * scale
        for g in range(groups):
            o_ref[g * CH + d * HALF:g * CH + (d + 1) * HALF, :] = fd[g * HALF:(g + 1) * HALF]


def _block_diag2(w):
    z = jnp.zeros_like(w)
    return jnp.concatenate([jnp.concatenate([w, z], axis=1), jnp.concatenate([z, w], axis=1)], axis=0)


def _filters(feat_pairs, w1p, b1, w2, b2, w3, b3, fq, w4, dl, rows):
    n_rows = 2 * feat_pairs.shape[0]
    twice = lambda v: jnp.concatenate([v, v], axis=1)
    c = w4.shape[1] // 2
    w4bd = jnp.concatenate([jnp.concatenate([w4[:, :c], jnp.zeros_like(w4[:, c:])], axis=1),
                            jnp.concatenate([jnp.zeros_like(w4[:, :c]), w4[:, c:]], axis=1)], axis=0)
    args = (_split_hi_lo(_block_diag2(w1p)), twice(b1), _split_hi_lo(_block_diag2(w2)), twice(b2),
            _split_hi_lo(_block_diag2(w3)), twice(b3), twice(fq), _split_hi_lo(w4bd), dl)
    full = lambda a: pl.BlockSpec(a.shape, lambda i: (0,) * a.ndim)
    return pl.pallas_call(
        functools.partial(_filter_kernel, groups=rows // CH),
        grid=(n_rows // rows,),
        in_specs=[pl.BlockSpec((rows // 2, 2 * FEAT_W), lambda i: (i, 0))] + [full(a) for a in args],
        out_specs=pl.BlockSpec((rows, D_MODEL), lambda i: (i, 0)),
        out_shape=jax.ShapeDtypeStruct((n_rows, D_MODEL), F32),
        compiler_params=_cparams(("arbitrary",)),
        name="hy_filter",
    )(feat_pairs, *args)


FFT_G = 8


def _lanes(j, c):
    return slice(j * c, (j + 1) * c)


def _fft_a_data_kernel(m_ref, z_ref, o_ref):
    for j in range(FFT_G):
        rows = slice(j * ZCH, (j + 1) * ZCH)
        x = jnp.concatenate([z_ref[0, rows, :], z_ref[1, rows, :]], axis=0).astype(BF16)
        o_ref[j] = jnp.dot(m_ref[j], x, preferred_element_type=F32).astype(BF16)


def _fft_a_filt_kernel(m_ref, g_ref, o_ref):
    for j in range(FFT_G):
        gj = g_ref[j * CH:(j + 1) * CH, :].astype(BF16)
        o_ref[j] = jnp.dot(m_ref[j], gj, preferred_element_type=F32).astype(BF16)


def _fft_b_kernel(f_ref, fi_ref, yr_ref, yi_ref, gr_ref, gi_ref, o_ref):
    c = o_ref.shape[-1]
    for j in range(FFT_G):
        y = jnp.concatenate([yr_ref[:, _lanes(j, c)], yi_ref[:, _lanes(j, c)]], axis=0)
        yg = jnp.concatenate([gr_ref[:, _lanes(j, c)], gi_ref[:, _lanes(j, c)]], axis=0)
        x = jnp.dot(f_ref[...], y, preferred_element_type=F32)
        g = jnp.dot(f_ref[...], yg, preferred_element_type=F32) * (1.0 / NFFT)
        xr, xi = x[:CH], x[CH:]
        gr, gi = g[:CH], g[CH:]
        p = jnp.concatenate([xr * gr - xi * gi, xr * gi + xi * gr], axis=0).astype(BF16)
        o_ref[j] = jnp.dot(fi_ref[...], p, preferred_element_type=F32).astype(BF16)


def _fft_a_inv_kernel(m_ref, ur_ref, ui_ref, o_ref):
    c = ur_ref.shape[-1] // FFT_G
    for j in range(FFT_G):
        u = jnp.concatenate([ur_ref[:, _lanes(j, c)], ui_ref[:, _lanes(j, c)]], axis=0)
        y = jnp.dot(m_ref[j], u, preferred_element_type=F32)
        o_ref[:, :, j, :] = y.reshape(2, ZCH, c)


def _dft_tables():
    n2 = np.arange(CH)[:, None, None]
    k1 = np.arange(CH)[None, :, None]

    def theta(n1_count):
        n1 = np.arange(n1_count)[None, None, :]
        return 2.0 * np.pi * ((k1 * (CH * n1 + n2)) % NFFT) / NFFT

    th = theta(ZCH)
    c, s = np.cos(th), np.sin(th)
    m_a = np.concatenate([np.concatenate([c, s], axis=2), np.concatenate([-s, c], axis=2)], axis=1)
    m_ainv = np.transpose(m_a, (0, 2, 1))
    th = theta(CH)
    m_af = np.concatenate([np.cos(th), -np.sin(th)], axis=1)
    a = 2.0 * np.pi * ((np.arange(CH)[:, None] * np.arange(CH)[None, :]) % CH) / CH
    c, s = np.cos(a), np.sin(a)
    f2 = np.block([[c, s], [-s, c]])
    f2i = np.block([[c, -s], [s, c]])
    f = lambda t: jnp.asarray(t.astype(np.float32)).astype(BF16)
    return f(m_a), f(m_ainv), f(m_af), f(f2), f(f2i)


def _fft_conv(zp, g):
    c = zp.shape[-1]
    m_a, m_ainv, m_af, f2, f2i = _dft_tables()
    cp = _cparams(("arbitrary",), VMEM_LIMIT_BIG)
    steps = CH // FFT_G
    gc = FFT_G * c
    full2 = pl.BlockSpec((2 * CH, 2 * CH), lambda i: (0, 0))
    col_re = pl.BlockSpec((CH, gc), lambda i: (0, i))
    col_im = pl.BlockSpec((CH, gc), lambda i: (0, steps + i))
    blk = pl.BlockSpec((FFT_G, 2 * CH, c), lambda i: (i, 0, 0))
    spec_shape = jax.ShapeDtypeStruct((CH, 2 * CH, c), BF16)

    yg = pl.pallas_call(
        _fft_a_filt_kernel, grid=(steps,),
        in_specs=[pl.BlockSpec((FFT_G, 2 * CH, CH), lambda i: (i, 0, 0)),
                  pl.BlockSpec((FFT_G * CH, c), lambda i: (i, 0))],
        out_specs=blk, out_shape=spec_shape, compiler_params=cp, name="fft_a_filt",
    )(m_af, g)
    yg2 = yg.reshape(CH, 2 * CH * c)

    y = pl.pallas_call(
        _fft_a_data_kernel, grid=(steps,),
        in_specs=[pl.BlockSpec((FFT_G, 2 * CH, 2 * ZCH), lambda i: (i, 0, 0)),
                  pl.BlockSpec((2, FFT_G * ZCH, c), lambda i: (0, i, 0))],
        out_specs=blk, out_shape=spec_shape, compiler_params=cp, name="fft_a_data",
    )(m_a, zp)
    y2 = y.reshape(CH, 2 * CH * c)
    u = pl.pallas_call(
        _fft_b_kernel, grid=(steps,),
        in_specs=[full2, full2, col_re, col_im, col_re, col_im],
        out_specs=blk, out_shape=spec_shape, compiler_params=cp, name="fft_b",
    )(f2, f2i, y2, y2, yg2, yg2)
    u2 = u.reshape(CH, 2 * CH * c)
    yc = pl.pallas_call(
        _fft_a_inv_kernel, grid=(steps,),
        in_specs=[pl.BlockSpec((FFT_G, 2 * ZCH, 2 * CH), lambda i: (i, 0, 0)), col_re, col_im],
        out_specs=pl.BlockSpec((2, ZCH, FFT_G, c), lambda i: (0, 0, i, 0)),
        out_shape=jax.ShapeDtypeStruct((2, ZCH, CH, c), F32),
        compiler_params=cp, name="fft_a_inv",
    )(m_ainv, u2, u2)
    return yc.reshape(2, ZCH * CH, c)


def _far_kernel(far_ref, zm_ref, zl_ref, o_ref):
    g_hi = far_ref[HALF + FAR:HALF + 2 * FAR, :]
    ef = far_ref[0:FAR, :] - g_hi
    g_lo = far_ref[FAR:2 * FAR, :]
    hb = far_ref[HALF:HALF + FAR, :]
    zm = zm_ref[0, PAD:CH, :].astype(F32)
    zl = zl_ref[0, PAD:CH, :].astype(F32)
    row = lambda a, i: a[i:i + 1, :]
    eb = [row(hb, 0) - row(g_hi, 0)] + [row(hb, m) - row(g_lo, FAR - m) for m in range(1, FAR)]
    for j in range(FAR):
        acc_f = row(ef, j) * row(zm, 0)
        for i in range(1, j + 1):
            acc_f = acc_f + row(ef, j - i) * row(zm, i)
        o_ref[0, 0, j:j + 1, :] = acc_f
        acc_b = eb[0] * row(zl, j)
        for i in range(j + 1, FAR):
            acc_b = acc_b + eb[i - j] * row(zl, i)
        o_ref[0, 1, j:j + 1, :] = acc_b


def _far_correction(h_far, z, tp):
    b, _, c = z.shape
    last = tp // CH - 1
    return pl.pallas_call(
        _far_kernel, grid=(b,),
        in_specs=[pl.BlockSpec((CH, c), lambda bi: (0, 0)),
                  pl.BlockSpec((1, CH, c), lambda bi: (bi, 0, 0)),
                  pl.BlockSpec((1, CH, c), lambda bi: (bi, last, 0))],
        out_specs=pl.BlockSpec((1, 2, FAR, c), lambda bi: (bi, 0, 0, 0)),
        out_shape=jax.ShapeDtypeStruct((b, 2, FAR, c), F32),
        compiler_params=_cparams(("arbitrary",)),
        name="hy_far",
    )(h_far, z, z)


MG_TM = 640
MG_CN = 256
ROUTER_W = 128
AFF_PACK = ROUTER_W // N_EXPERTS
ROW_SUB, ROW_LANE = 8, 128


def _merge_kernel(*refs, nt_b, nch):
    n_in = MG_TM // CH
    x_refs = refs[:n_in]
    (head_ref, og_ref, x0_ref, z_ref, yc_ref, ga_ref, gb_ref, corr_ref, skip_ref,
     wr_ref, wh_ref, wo_ref, g2_ref, wrt_ref,
     h1_ref, xn_ref, aff_ref, affp_ref, afft_ref, pre_ref, mix_ref, h1s_ref) = refs[n_in:]
    i = pl.program_id(0)
    for k in range(n_in):
        h1s_ref[k * CH:(k + 1) * CH, :] = _padded_chunk(i * n_in + k, nch, head_ref, x_refs[k])
    ib = i % nt_b
    pre_ref[...] = yc_ref[0].astype(F32) + z_ref[0].astype(F32) * skip_ref[...]

    @pl.when(ib == 0)
    def _():
        pre_ref[PAD:CH, :] += corr_ref[0, 1]

    @pl.when(ib == nt_b - 1)
    def _():
        pre_ref[MG_TM - FAR:MG_TM, :] += corr_ref[0, 0]

    chunks = [slice(c * MG_CN, (c + 1) * MG_CN) for c in range(D_MODEL // MG_CN)]
    og = og_ref[0]
    pre = (x0_ref[0].astype(F32) * pre_ref[...]).astype(BF16)
    for cols in chunks:
        ya = jnp.dot(og, wr_ref[:, cols], preferred_element_type=F32)
        yb = jnp.dot(pre, wh_ref[:, cols], preferred_element_type=F32)
        mix_ref[:, cols] = (ga_ref[0, :, cols].astype(F32) * ya + gb_ref[0, :, cols].astype(F32) * yb).astype(BF16)
    mixed = mix_ref[...]
    ss = jnp.zeros((MG_TM, 1), F32)
    for cols in chunks:
        h1 = h1s_ref[:, cols] + jnp.dot(mixed, wo_ref[:, cols], preferred_element_type=F32)
        h1s_ref[:, cols] = h1
        ss = ss + jnp.sum(h1 * h1, axis=-1, keepdims=True)
    rinv = lax.rsqrt(ss * (1.0 / D_MODEL) + RMS_EPS)
    h1 = h1s_ref[...]
    h1_ref[0] = h1.reshape(MG_TM, ROW_SUB, ROW_LANE)
    xn = h1 * rinv * g2_ref[...]
    xn_ref[0] = xn.reshape(MG_TM, ROW_SUB, ROW_LANE)
    logits = _dot3(xn, wrt_ref[0], wrt_ref[1])
    lane = lax.broadcasted_iota(I32, logits.shape, 1)
    logits = jnp.where(lane < N_EXPERTS, logits, -jnp.inf)
    m = jnp.max(logits, axis=-1, keepdims=True)
    e = jnp.exp(logits - m)
    aff = e / jnp.sum(e, axis=-1, keepdims=True)
    rows = lax.broadcasted_iota(I32, logits.shape, 0) + ib * MG_TM
    aff_ref[0] = jnp.where(rows >= PAD, aff, -1.0)
    lane_grp = lax.broadcasted_iota(I32, (MG_TM // AFF_PACK, ROUTER_W), 1) // N_EXPERTS
    packed = jnp.zeros((MG_TM // AFF_PACK, ROUTER_W), F32)
    for g in range(AFF_PACK):
        blk = aff_ref[0, pl.ds(g, MG_TM // AFF_PACK, stride=AFF_PACK), :]
        if g:
            blk = pltpu.roll(blk, N_EXPERTS * g, axis=1)
        packed = jnp.where(lane_grp == g, blk, packed)
    affp_ref[0] = packed
    afft_ref[0] = jnp.transpose(aff_ref[0])[:N_EXPERTS]


def _merge(x_flat, head, og, x0c, z, yc, proj3, corr, skip, w_ret, w_hy, w_o, g2, w_router_p, tp):
    b = og.shape[0]
    nt_b = tp // MG_TM
    nch = tp // CH
    n_in = MG_TM // CH
    n_x = x_flat.shape[0] // CH
    ga_blk0 = 6144 // D_MODEL
    row = lambda w: pl.BlockSpec((1, MG_TM, w), lambda i: (i // nt_b, i % nt_b, 0))
    full = lambda a: pl.BlockSpec(a.shape, lambda i: (0,) * a.ndim)
    x_spec = lambda k: pl.BlockSpec((CH, D_MODEL), lambda i: (_x_chunk_index(i * n_in + k, nch, n_x), 0))
    return pl.pallas_call(
        functools.partial(_merge_kernel, nt_b=nt_b, nch=nch),
        grid=(b * nt_b,),
        in_specs=[x_spec(k) for k in range(n_in)] + [
                  full(head), row(D_MODEL), row(D_MODEL), row(D_MODEL), row(D_MODEL),
                  pl.BlockSpec((1, MG_TM, D_MODEL), lambda i: (i // nt_b, i % nt_b, ga_blk0)),
                  pl.BlockSpec((1, MG_TM, D_MODEL), lambda i: (i // nt_b, i % nt_b, ga_blk0 + 1)),
                  pl.BlockSpec((1, 2, FAR, D_MODEL), lambda i: (i // nt_b, 0, 0, 0)),
                  full(skip), full(w_ret), full(w_hy), full(w_o), full(g2), full(w_router_p)],
        out_specs=[pl.BlockSpec((1, MG_TM, ROW_SUB, ROW_LANE), lambda i: (i // nt_b, i % nt_b, 0, 0)),
                   pl.BlockSpec((1, MG_TM, ROW_SUB, ROW_LANE), lambda i: (i // nt_b, i % nt_b, 0, 0)),
                   row(ROUTER_W),
                   pl.BlockSpec((1, MG_TM // AFF_PACK, ROUTER_W), lambda i: (i // nt_b, i % nt_b, 0)),
                   pl.BlockSpec((1, N_EXPERTS, MG_TM), lambda i: (i // nt_b, 0, i % nt_b))],
        out_shape=[jax.ShapeDtypeStruct((b, tp, ROW_SUB, ROW_LANE), F32),
                   jax.ShapeDtypeStruct((b, tp, ROW_SUB, ROW_LANE), F32),
                   jax.ShapeDtypeStruct((b, tp, ROUTER_W), F32),
                   jax.ShapeDtypeStruct((b, tp // AFF_PACK, ROUTER_W), F32),
                   jax.ShapeDtypeStruct((b, N_EXPERTS, tp), F32)],
        scratch_shapes=[pltpu.VMEM((MG_TM, D_MODEL), F32), pltpu.VMEM((MG_TM, D_MODEL), BF16),
                        pltpu.VMEM((MG_TM, D_MODEL), F32)],
        compiler_params=_cparams(("arbitrary",), VMEM_LIMIT_BIG),
        name="merge",
    )(*([x_flat] * n_in), head, og, x0c, z, yc, proj3, proj3, corr, skip, w_ret, w_hy, w_o, g2, w_router_p)


TK_W = 128


TK_UNROLL = 5


def _select_kernel(aff_ref, affp_ref, low_ref, slot_ref, offs_ref, *, cap, nch):
    def chunk(c):
        r0 = pl.multiple_of(c * CH, CH)
        return aff_ref[0, pl.ds(r0, CH), :]

    def count(pred):
        cnt = jnp.sum(pred(affp_ref[0]).astype(I32), axis=0, keepdims=True)
        for shift in (N_EXPERTS, 2 * N_EXPERTS, 4 * N_EXPERTS):
            cnt = cnt + pltpu.roll(cnt, shift, axis=1)
        return cnt

    def search(it, bits):
        cand = bits | jnp.left_shift(1, 29 - it)
        cand_f = pltpu.bitcast(cand, F32)
        return jnp.where(count(lambda a: a >= cand_f) >= cap, cand, bits)

    thr = pltpu.bitcast(lax.fori_loop(0, 30, search, jnp.zeros((1, ROUTER_W), I32)), F32)
    need = (cap - count(lambda a: a > thr)).astype(F32)
    low = low_ref[...]

    def scan(it, carry):
        c_eq, c_sel = carry
        for u in range(TK_UNROLL):
            c = it * TK_UNROLL + u
            a = chunk(c)
            eq = a == thr
            eq_f = eq.astype(F32)
            eq_rank = jnp.dot(low, eq_f, preferred_element_type=F32) + c_eq
            sel = jnp.logical_or(a > thr, jnp.logical_and(eq, eq_rank < need))
            sel_f = sel.astype(F32)
            slot = jnp.dot(low, sel_f, preferred_element_type=F32) + c_sel
            r0 = pl.multiple_of(c * CH, CH)
            slot_ref[0, pl.ds(r0, CH), :] = jnp.where(sel, slot, -1.0).astype(I32)
            offs_ref[0, c] = c_sel.astype(I32)
            c_eq = c_eq + jnp.sum(eq_f, axis=0, keepdims=True)
            c_sel = c_sel + jnp.sum(sel_f, axis=0, keepdims=True)
        return c_eq, c_sel

    zero = jnp.zeros((1, ROUTER_W), F32)
    lax.fori_loop(0, nch // TK_UNROLL, scan, (zero, zero))


def _extract_kernel(offs_ref, slot_ref, idx_ref, *, nch, nwin):
    idx_ref[...] = jnp.zeros_like(idx_ref)
    lane = lax.broadcasted_iota(I32, (CH, TK_W), 1)
    trow = lax.broadcasted_iota(I32, (CH, TK_W), 0)

    def per_chunk(c, carry):
        r0 = pl.multiple_of(c * CH, CH)
        slots = slot_ref[0, pl.ds(r0, CH), :]
        tpos = (trow + r0).astype(F32)
        for e in range(N_EXPERTS):
            col = jnp.broadcast_to(slots[:, e:e + 1], (CH, TK_W))
            w0 = offs_ref[0, 0, c * N_EXPERTS + e] // TK_W
            for dw in range(2):
                w = w0 + dw
                base = jnp.where(w < nwin, w * TK_W, -2 * TK_W)
                hit = col == lane + base
                row = e * nwin + jnp.minimum(w, nwin - 1)
                idx_ref[row] += jnp.sum(jnp.where(hit, tpos, 0.0), axis=0, keepdims=True)
        return carry

    lax.fori_loop(0, nch, per_chunk, 0)


def _topk(aff, aff_packed, cap, slots):
    b, tp, _ = aff.shape
    nch = tp // CH
    nwin = -(-slots // TK_W)
    low = jnp.asarray(np.tril(np.ones((CH, CH), np.float32), k=-1))
    pack = AFF_PACK
    assert nch % TK_UNROLL == 0 and tp % pack == 0
    slot, offs = pl.pallas_call(
        functools.partial(_select_kernel, cap=cap, nch=nch),
        grid=(b,),
        in_specs=[pl.BlockSpec((1, tp, ROUTER_W), lambda bi: (bi, 0, 0)),
                  pl.BlockSpec((1, tp // pack, ROUTER_W), lambda bi: (bi, 0, 0)),
                  pl.BlockSpec((CH, CH), lambda bi: (0, 0))],
        out_specs=[pl.BlockSpec((1, tp, ROUTER_W), lambda bi: (bi, 0, 0)),
                   pl.BlockSpec((1, nch, 1, ROUTER_W), lambda bi: (bi, 0, 0, 0))],
        out_shape=[jax.ShapeDtypeStruct((b, tp, ROUTER_W), I32),
                   jax.ShapeDtypeStruct((b, nch, 1, ROUTER_W), I32)],
        compiler_params=_cparams(("arbitrary",)),
        name="topk_select",
    )(aff, aff_packed, low)
    offs_s = offs[:, :, 0, :N_EXPERTS].reshape(b, 1, nch * N_EXPERTS)
    rows = pl.BlockSpec((None, N_EXPERTS * nwin, 1, TK_W), lambda bi: (bi, 0, 0, 0))
    out = jax.ShapeDtypeStruct((b, N_EXPERTS * nwin, 1, TK_W), F32)
    idx = pl.pallas_call(
        functools.partial(_extract_kernel, nch=nch, nwin=nwin),
        grid=(b,),
        in_specs=[pl.BlockSpec((1, 1, nch * N_EXPERTS), lambda bi: (bi, 0, 0), memory_space=pltpu.SMEM),
                  pl.BlockSpec((1, tp, ROUTER_W), lambda bi: (bi, 0, 0))],
        out_specs=rows,
        out_shape=out,
        compiler_params=_cparams(("arbitrary",)),
        name="topk_extract",
    )(offs_s, slot)
    idx = idx.reshape(b * N_EXPERTS, 1, nwin * TK_W)[:, :, :slots].astype(I32)
    return idx


FF_TF = 1024
MOE_CN = 256


MOE_UNROLL = 4


def _moe_kernel(idx_ref, idx_next_ref, xn_hbm, wg_ref, wu_ref, wd_ref, o_ref,
                xe32_ref, xe_ref, hid_ref, acc_ref, sem,
                *, slots, nf):
    i = pl.program_id(0)
    f = pl.program_id(1)
    buf = i % 2
    share = slots // nf

    def row_copy(ids_ref, s, b):
        return pltpu.make_async_copy(xn_hbm.at[ids_ref[0, 0, s]], xe32_ref.at[b, s], sem.at[b])

    def for_rows(lo, n, fn):
        def body(k, carry):
            for u in range(MOE_UNROLL):
                fn(lo + k * MOE_UNROLL + u)
            return carry

        lax.fori_loop(0, n // MOE_UNROLL, body, 0)

    @pl.when(jnp.logical_and(i == 0, f == 0))
    def _():
        for_rows(0, slots, lambda s: row_copy(idx_ref, s, 0).start())

    @pl.when(f == 0)
    def _():
        for_rows(0, slots, lambda s: row_copy(idx_ref, s, buf).wait())
        xe_ref[...] = xe32_ref[buf].reshape(slots, D_MODEL).astype(BF16)

    @pl.when(f == 0)
    def _():
        acc_ref[...] = jnp.zeros_like(acc_ref)

    n_up, n_down = FF_TF // MOE_CN, D_MODEL // MOE_CN
    per_chunk = share // (n_up + n_down)

    def prefetch(chunk):
        base = f * share + chunk * per_chunk
        for u in range(per_chunk):
            row_copy(idx_next_ref, base + u, 1 - buf).start()

    xe = xe_ref[...]
    for c in range(n_up):
        cols = slice(c * MOE_CN, (c + 1) * MOE_CN)
        gg = jnp.dot(xe, wg_ref[0, :, cols].astype(BF16), preferred_element_type=F32)
        uu = jnp.dot(xe, wu_ref[0, :, cols].astype(BF16), preferred_element_type=F32)
        hid_ref[:, cols] = (gg * jax.nn.sigmoid(gg) * uu).astype(BF16)
        prefetch(c)
    hid = hid_ref[...]
    for c in range(n_down):
        cols = slice(c * MOE_CN, (c + 1) * MOE_CN)
        part = jnp.dot(hid, wd_ref[0, :, cols].astype(BF16), preferred_element_type=F32)
        acc_ref[:, cols] = acc_ref[:, cols] + part
        prefetch(n_up + c)

    @pl.when(f == pl.num_programs(1) - 1)
    def _():
        o_ref[0] = acc_ref[...].reshape(slots, ROW_SUB, ROW_LANE)

    @pl.when(jnp.logical_and(i == pl.num_programs(0) - 1, f == pl.num_programs(1) - 1))
    def _():
        for_rows(0, slots, lambda s: row_copy(idx_next_ref, s, 1 - buf).wait())


def _moe_ffn(idx, xn_flat, w_gate, w_up, w_down, slots):
    be = idx.shape[0]
    nf = D_FF // FF_TF
    assert slots % MOE_UNROLL == 0 and slots % (nf * (FF_TF // MOE_CN + D_MODEL // MOE_CN)) == 0
    return pl.pallas_call(
        functools.partial(_moe_kernel, slots=slots, nf=nf),
        grid=(be, nf),
        in_specs=[pl.BlockSpec((1, 1, slots), lambda i, f: (i, 0, 0), memory_space=pltpu.SMEM),
                  pl.BlockSpec((1, 1, slots), lambda i, f: (jnp.minimum(i + 1, be - 1), 0, 0),
                               memory_space=pltpu.SMEM),
                  pl.BlockSpec(memory_space=pl.ANY),
                  pl.BlockSpec((1, D_MODEL, FF_TF), lambda i, f: (i % N_EXPERTS, 0, f)),
                  pl.BlockSpec((1, D_MODEL, FF_TF), lambda i, f: (i % N_EXPERTS, 0, f)),
                  pl.BlockSpec((1, FF_TF, D_MODEL), lambda i, f: (i % N_EXPERTS, f, 0))],
        out_specs=pl.BlockSpec((1, slots, ROW_SUB, ROW_LANE), lambda i, f: (i, 0, 0, 0)),
        out_shape=jax.ShapeDtypeStruct((be, slots, ROW_SUB, ROW_LANE), F32),
        scratch_shapes=[pltpu.VMEM((2, slots, ROW_SUB, ROW_LANE), F32), pltpu.VMEM((slots, D_MODEL), BF16),
                        pltpu.VMEM((slots, FF_TF), BF16), pltpu.VMEM((slots, D_MODEL), F32),
                        pltpu.SemaphoreType.DMA((2,))],
        compiler_params=_cparams(("arbitrary", "arbitrary"), VMEM_LIMIT_BIG),
        name="moe_ffn",
    )(idx, idx, xn_flat, w_gate, w_up, w_down)


CB_U = 6
CB_NB = 512


def _combine_kernel(idx_ref, gate_ref, h1_hbm, ye_ref, g_ref, o_hbm, acc_ref, stage_ref, sem, osem, *, cap, tp):
    b = pl.program_id(0)
    e = pl.program_id(1)

    @pl.when(e == 0)
    def _():
        cp = pltpu.make_async_copy(h1_hbm.at[b], acc_ref, sem)
        cp.start()
        cp.wait()

    def rmw(g, carry):
        ts = [idx_ref[0, 0, g * CB_U + u] for u in range(CB_U)]
        vals = [acc_ref[ts[u]] + gate_ref[0, 0, ts[u]] * ye_ref[g * CB_U + u] for u in range(CB_U)]
        for u in range(CB_U):
            acc_ref[ts[u]] = vals[u]
        return carry

    lax.fori_loop(0, cap // CB_U, rmw, 0)

    @pl.when(e == pl.num_programs(1) - 1)
    def _():
        gamma = g_ref[...]
        n_blocks = (tp - CH) // CB_NB

        def out_copy(k):
            return pltpu.make_async_copy(stage_ref.at[k % 2], o_hbm.at[b, pl.ds(k * CB_NB, CB_NB), :],
                                         osem.at[k % 2])

        for k in range(n_blocks):
            x = acc_ref[pl.ds(CH + k * CB_NB, CB_NB)]
            ms = jnp.sum(jnp.sum(x * x, axis=2, keepdims=True), axis=1, keepdims=True) * (1.0 / D_MODEL)
            y = x * lax.rsqrt(ms + RMS_EPS) * gamma
            if k >= 2:
                out_copy(k - 2).wait()
            stage_ref[k % 2] = y.reshape(CB_NB, D_MODEL)
            out_copy(k).start()
        for k in range(max(n_blocks - 2, 0), n_blocks):
            out_copy(k).wait()


def _combine(idx_local, gate, h1, ye, gf, cap, slots):
    b, tp = h1.shape[:2]
    assert cap % CB_U == 0 and (tp - CH) % CB_NB == 0
    return pl.pallas_call(
        functools.partial(_combine_kernel, cap=cap, tp=tp),
        grid=(b, N_EXPERTS),
        in_specs=[pl.BlockSpec((1, 1, slots), lambda bi, e: (bi * N_EXPERTS + e, 0, 0), memory_space=pltpu.SMEM),
                  pl.BlockSpec((1, 1, tp), lambda bi, e: (bi * N_EXPERTS + e, 0, 0), memory_space=pltpu.SMEM),
                  pl.BlockSpec(memory_space=pl.ANY),
                  pl.BlockSpec((None, slots, ROW_SUB, ROW_LANE), lambda bi, e: (bi * N_EXPERTS + e, 0, 0, 0)),
                  pl.BlockSpec((1, ROW_SUB, ROW_LANE), lambda bi, e: (0, 0, 0))],
        out_specs=pl.BlockSpec(memory_space=pl.ANY),
        out_shape=jax.ShapeDtypeStruct((b, tp - CH, D_MODEL), F32),
        scratch_shapes=[pltpu.VMEM((tp, ROW_SUB, ROW_LANE), F32), pltpu.VMEM((2, CB_NB, D_MODEL), F32),
                        pltpu.SemaphoreType.DMA(()), pltpu.SemaphoreType.DMA((2,))],
        compiler_params=_cparams(("arbitrary", "arbitrary"), VMEM_LIMIT_BIG),
        name="combine",
    )(idx_local, gate, h1, ye, gf.reshape(1, ROW_SUB, ROW_LANE))


def _rope_tables(tp, b):
    half = RET_QK_HEAD // 2
    pos = np.arange(tp, dtype=np.float64) - PAD
    inv = ROPE_BASE ** (-np.arange(half, dtype=np.float64) / half)
    ang = pos[:, None] * inv[None, :]
    cos, sin = np.cos(ang), np.sin(ang)
    tile = lambda t: jnp.asarray(np.tile(t, (b, 1)).astype(np.float32))
    return tile(np.concatenate([cos, cos], axis=1)), tile(np.concatenate([-sin, sin], axis=1))


def _filter_features(t_len):
    half = NFFT // 2
    q = np.arange(NFFT)
    r = CH * (q % CH) + q // CH
    p_main = np.where(r < half, r, NFFT - r)
    valid_main = (r != half).astype(np.float32)
    m = np.arange(FAR)
    pad = np.zeros(HALF - 2 * FAR, np.int64)
    p_far = np.concatenate([half + m, half - FAR + m, pad, half + m, half - m, pad])
    valid_far = np.ones(CH, np.float32)
    valid_far[HALF + FAR] = 0.0
    p = np.concatenate([p_main, p_far]).astype(np.float64)
    valid = np.concatenate([valid_main, valid_far]).astype(np.float64)
    t_norm = p / (t_len - 1)
    bands = (HY_EMB_DIM - 1) // 2
    fr = np.linspace(1e-4, bands - 1, bands)
    ang = (2.0 * math.pi * p / t_len)[:, None] * fr[None, :]
    feat = np.concatenate([t_norm[:, None], np.cos(ang), -np.sin(ang), valid[:, None]], axis=-1)
    feat = np.pad(feat, ((0, 0), (0, FEAT_W - feat.shape[1])))
    pairs = feat.reshape(-1, 2, HALF, FEAT_W).transpose(0, 2, 1, 3).reshape(-1, 2 * FEAT_W)
    return jnp.asarray(pairs.astype(np.float32))


def kernel(x, meta_tokens, norm1_g, w_in, ret_decay_fwd, ret_decay_bwd, ret_head_norm_g, w_ret_out,
           hy_conv_w, hy_conv_b, hy_filt_w1, hy_filt_b1, hy_filt_w2, hy_filt_b2, hy_filt_w3, hy_filt_b3,
           hy_filt_freq, hy_filt_w4, hy_skip, w_hy_out, w_o, norm2_g, w_router, w_exp_gate, w_exp_up,
           w_exp_down, final_norm_g):
    b, seq, d = x.shape
    t_len = seq + N_META
    tp = PAD + t_len
    assert d == D_MODEL and (b * tp) % IP_TM == 0 and tp % MG_TM == 0 and tp % CH == 0 and t_len - NFFT // 2 == FAR
    cap = EC_CAPACITY * t_len // N_EXPERTS
    slots = -(-cap // 16) * 16
    l = 0

    assert seq % CH == 0
    head = jnp.concatenate([jnp.zeros((PAD, d), x.dtype), meta_tokens.astype(x.dtype)], axis=0)
    x_flat = x.reshape(b * seq, d)

    cs, sn = _rope_tables(tp, b)
    proj = _in_proj(x_flat, head, norm1_g[l][None], w_in[l], cs, sn, tp, b)
    proj3 = proj.reshape(b, tp, IN_PROJ_W)

    lf = jax.nn.log_sigmoid(ret_decay_fwd[l].astype(F32))
    lb = jax.nn.log_sigmoid(ret_decay_bwd[l].astype(F32))
    og = _retention(proj3, lf, lb, ret_head_norm_g[l][None], tp)

    x0c, z, zp = _hy_prep(proj3, hy_conv_w[l], hy_conv_b[l][None], tp)

    feat = _filter_features(t_len)
    w1p = jnp.pad(hy_filt_w1[l].astype(F32), ((0, FEAT_W - HY_EMB_DIM), (0, 0)))
    max_decay = math.log(HY_DECAY_TARGET) / HY_FAST_DECAY_PCT
    min_decay = math.log(HY_DECAY_TARGET) / HY_SLOW_DECAY_PCT
    dl = jnp.abs(jnp.linspace(min_decay, max_decay, D_MODEL, dtype=F32))[None]
    fargs = (w1p, hy_filt_b1[l][None].astype(F32), hy_filt_w2[l].astype(F32), hy_filt_b2[l][None].astype(F32),
             hy_filt_w3[l].astype(F32), hy_filt_b3[l][None].astype(F32), hy_filt_freq[l][None].astype(F32),
             hy_filt_w4[l].astype(F32), dl)
    g = _filters(feat[:NFFT // 2], *fargs, FT_ROWS)
    h_far = _filters(feat[NFFT // 2:], *fargs, CH)
    yc = _fft_conv(zp, g)
    corr = _far_correction(h_far, z, tp)

    w_router_p = jnp.pad(w_router[l].astype(F32), ((0, 0), (0, ROUTER_W - N_EXPERTS)))
    w_router_p = _split_hi_lo(w_router_p)
    h1, xn2, aff, aff_packed, aff_t = _merge(x_flat, head, og, x0c, z, yc, proj3, corr, hy_skip[l][None].astype(F32),
                          w_ret_out[l].astype(BF16), w_hy_out[l].astype(BF16), w_o[l].astype(BF16),
                          norm2_g[l][None].astype(F32), w_router_p, tp)

    idx = _topk(aff, aff_packed, cap, slots)
    live = (jnp.arange(slots) < cap)[None, None, :]
    idx_local = jnp.where(live, idx, PAD)
    idx_flat = idx_local + (jnp.arange(b * N_EXPERTS, dtype=I32) // N_EXPERTS * tp)[:, None, None]
    ye = _moe_ffn(idx_flat, xn2.reshape(b * tp, ROW_SUB, ROW_LANE),
                  w_exp_gate[l], w_exp_up[l], w_exp_down[l], slots)
    return _combine(idx_local, aff_t.reshape(b * N_EXPERTS, 1, tp), h1, ye, final_norm_g.astype(F32), cap, slots)
```

```python
import functools
import math

import numpy as np
import jax
import jax.numpy as jnp
from jax import lax
from jax.experimental import pallas as pl
from jax.experimental.pallas import tpu as pltpu

F32 = jnp.float32
BF16 = jnp.bfloat16
I32 = jnp.int32

D_MODEL = 1024
N_META = 16
RET_HEADS = 4
RET_QK_HEAD = 128
RET_V_HEAD = 256
ROPE_BASE = 10000.0
HY_EMB_DIM = 33
HY_FILTER_ORDER = 64
HY_FAST_DECAY_PCT = 0.3
HY_SLOW_DECAY_PCT = 1.5
HY_DECAY_TARGET = 1e-2
N_EXPERTS = 16
EC_CAPACITY = 2
D_FF = 2 * D_MODEL
RMS_EPS = 1e-6
IN_PROJ_W = 8192

CH = 128
PAD = CH - N_META
NFFT = 16384
ZCH = 80
FAR = 16

VMEM_LIMIT_BIG = 56 * 1024 * 1024
VMEM_LIMIT_MID = 40 * 1024 * 1024


def _cparams(sem, vmem=VMEM_LIMIT_MID):
    return pltpu.CompilerParams(dimension_semantics=sem, vmem_limit_bytes=vmem)


def _split_hi_lo(w):
    w = w.astype(F32)
    hi = w.astype(BF16)
    return jnp.stack([hi, (w - hi.astype(F32)).astype(BF16)])


def _dot3(a, w_hi, w_lo):
    a_hi = a.astype(BF16)
    a_lo = (a - a_hi.astype(F32)).astype(BF16)
    return (jnp.dot(a_hi, w_hi, preferred_element_type=F32) + jnp.dot(a_lo, w_hi, preferred_element_type=F32)
            + jnp.dot(a_hi, w_lo, preferred_element_type=F32))


IP_TM = 1280
IP_TN = 1024
IP_CN = 256


def _padded_chunk(flat_chunk, nch, head_ref, x_ref):
    return jnp.where(flat_chunk % nch == 0, head_ref[...], x_ref[...])


def _x_chunk_index(flat_chunk, nch, n_x_chunks):
    return jnp.clip(flat_chunk - flat_chunk // nch - 1, 0, n_x_chunks - 1)


def _inproj_kernel(*refs, nch):
    n_in = IP_TM // CH
    x_refs, (head_ref, g_ref, w_ref, cs_ref, sn_ref, o_ref, xn_ref) = refs[:n_in], refs[n_in:]
    i = pl.program_id(0)
    j = pl.program_id(1)

    @pl.when(j == 0)
    def _():
        for k in range(n_in):
            x = _padded_chunk(i * n_in + k, nch, head_ref, x_refs[k])
            ms = jnp.mean(x * x, axis=-1, keepdims=True)
            xn_ref[k * CH:(k + 1) * CH, :] = (x * lax.rsqrt(ms + RMS_EPS) * g_ref[...]).astype(BF16)

    def run(epilogue):
        for c in range(IP_TN // IP_CN):
            cols = slice(c * IP_CN, (c + 1) * IP_CN)
            acc = jnp.dot(xn_ref[...], w_ref[:, cols].astype(BF16), preferred_element_type=F32)
            epilogue(c, cols, acc)

    def rotary(c, cols, acc):
        scale = 1.0 if c < (IP_TN // IP_CN) // 2 else RET_QK_HEAD ** -0.5
        cs = cs_ref[...] * scale
        sn = sn_ref[...] * scale
        for hh in range(IP_CN // RET_QK_HEAD):
            xh = acc[:, hh * RET_QK_HEAD:(hh + 1) * RET_QK_HEAD]
            rot = xh * cs + pltpu.roll(xh, RET_QK_HEAD // 2, axis=1) * sn
            lo = c * IP_CN + hh * RET_QK_HEAD
            o_ref[:, lo:lo + RET_QK_HEAD] = rot.astype(BF16)

    def raw(c, cols, acc):
        o_ref[:, cols] = acc.astype(BF16)

    def swish(c, cols, acc):
        o_ref[:, cols] = (acc * jax.nn.sigmoid(acc)).astype(BF16)

    def sigm(c, cols, acc):
        o_ref[:, cols] = jax.nn.sigmoid(acc).astype(BF16)

    pl.when(j == 0)(lambda: run(rotary))
    pl.when(jnp.logical_or(j == 1, jnp.logical_and(j >= 3, j < 6)))(lambda: run(raw))
    pl.when(j == 2)(lambda: run(swish))
    pl.when(j >= 6)(lambda: run(sigm))


def _in_proj(x_flat, head, g1, w_in_bf, cs, sn, tp, b):
    n_rows = b * tp
    nch = tp // CH
    n_in = IP_TM // CH
    n_x = x_flat.shape[0] // CH
    x_spec = lambda k: pl.BlockSpec((CH, D_MODEL), lambda i, j: (_x_chunk_index(i * n_in + k, nch, n_x), 0))
    return pl.pallas_call(
        functools.partial(_inproj_kernel, nch=nch),
        grid=(n_rows // IP_TM, IN_PROJ_W // IP_TN),
        in_specs=[x_spec(k) for k in range(n_in)] + [
            pl.BlockSpec((CH, D_MODEL), lambda i, j: (0, 0)),
            pl.BlockSpec((1, D_MODEL), lambda i, j: (0, 0)),
            pl.BlockSpec((D_MODEL, IP_TN), lambda i, j: (0, j)),
            pl.BlockSpec((IP_TM, RET_QK_HEAD), lambda i, j: (i, 0)),
            pl.BlockSpec((IP_TM, RET_QK_HEAD), lambda i, j: (i, 0)),
        ],
        out_specs=pl.BlockSpec((IP_TM, IP_TN), lambda i, j: (i, j)),
        out_shape=jax.ShapeDtypeStruct((n_rows, IN_PROJ_W), BF16),
        scratch_shapes=[pltpu.VMEM((IP_TM, D_MODEL), BF16)],
        compiler_params=_cparams(("arbitrary", "arbitrary")),
        name="in_proj",
    )(*([x_flat] * n_in), head, g1, w_in_bf, cs, sn)


RET_UNROLL = 13


def _ret_kernel(lf_ref, lb_ref, q_ref, k_ref, v_ref, gr_ref, gn_ref, o_ref, ob_ref, s_ref, *, nch):
    h = pl.program_id(1)
    lf = lf_ref[h]
    lb = lb_ref[h]
    ri = lax.broadcasted_iota(I32, (CH, CH), 0).astype(F32)
    ci = lax.broadcasted_iota(I32, (CH, CH), 1).astype(F32)
    diff = ri - ci
    mask = jnp.exp(jnp.where(diff >= 0, lf * diff, -lb * diff))
    w_end = jnp.exp(lf * (CH - 1.0 - ri))
    w_start = jnp.exp(lb * ri)
    qw_f = jnp.exp(lf * (ri + 1.0))
    qw_b = jnp.exp(lb * (CH - ri))
    dec_f = jnp.exp(jnp.full((CH, RET_V_HEAD), lf * CH, F32))
    dec_b = jnp.exp(jnp.full((CH, RET_V_HEAD), lb * CH, F32))
    tn_dims = (((0,), (0,)), ((), ()))
    nt_dims = (((1,), (1,)), ((), ()))

    s_ref[...] = jnp.zeros_like(s_ref)

    def bwd(it, carry):
        s = s_ref[...]
        for u in range(RET_UNROLL):
            n = nch - 1 - (it * RET_UNROLL + u)
            r0 = pl.multiple_of(n * CH, CH)
            q = q_ref[0, pl.ds(r0, CH), :].astype(F32)
            k = k_ref[0, pl.ds(r0, CH), :].astype(F32)
            v = v_ref[0, pl.ds(r0, CH), :]
            ob_ref[pl.ds(r0, CH), :] = jnp.dot((q * qw_b).astype(BF16), s.astype(BF16),
                                               preferred_element_type=F32)
            a = lax.dot_general((k * w_start).astype(BF16), v, tn_dims, preferred_element_type=F32)
            s = s * dec_b + a
        s_ref[...] = s
        return carry

    lax.fori_loop(0, nch // RET_UNROLL, bwd, 0)

    s_ref[...] = jnp.zeros_like(s_ref)
    gn = gn_ref[...]

    def fwd(it, carry):
        s = s_ref[...]
        for u in range(RET_UNROLL):
            n = it * RET_UNROLL + u
            r0 = pl.multiple_of(n * CH, CH)
            qb = q_ref[0, pl.ds(r0, CH), :]
            kb = k_ref[0, pl.ds(r0, CH), :]
            v = v_ref[0, pl.ds(r0, CH), :]
            q = qb.astype(F32)
            k = kb.astype(F32)
            scores = lax.dot_general(qb, kb, nt_dims, preferred_element_type=F32) * mask
            o = jnp.dot(scores.astype(BF16), v, preferred_element_type=F32)
            o = o + jnp.dot((q * qw_f).astype(BF16), s.astype(BF16), preferred_element_type=F32)
            o = o + ob_ref[pl.ds(r0, CH), :]
            a = lax.dot_general((k * w_end).astype(BF16), v, tn_dims, preferred_element_type=F32)
            s = s * dec_f + a
            y = o * lax.rsqrt(jnp.mean(o * o, axis=-1, keepdims=True) + RMS_EPS) * gn
            o_ref[0, pl.ds(r0, CH), :] = (y * gr_ref[0, pl.ds(r0, CH), :].astype(F32)).astype(BF16)
        s_ref[...] = s
        return carry

    lax.fori_loop(0, nch // RET_UNROLL, fwd, 0)


def _retention(proj3, lf, lb, gn, tp):
    b = proj3.shape[0]
    nch = tp // CH
    qk_blocks = (RET_HEADS * RET_QK_HEAD) // RET_QK_HEAD
    v_blk0 = (2 * RET_HEADS * RET_QK_HEAD) // RET_V_HEAD
    g_blk0 = v_blk0 + RET_HEADS
    smem = pl.BlockSpec(memory_space=pltpu.SMEM)
    return pl.pallas_call(
        functools.partial(_ret_kernel, nch=nch),
        grid=(b, RET_HEADS),
        in_specs=[
            smem, smem,
            pl.BlockSpec((1, tp, RET_QK_HEAD), lambda bi, h: (bi, 0, h)),
            pl.BlockSpec((1, tp, RET_QK_HEAD), lambda bi, h: (bi, 0, qk_blocks + h)),
            pl.BlockSpec((1, tp, RET_V_HEAD), lambda bi, h: (bi, 0, v_blk0 + h)),
            pl.BlockSpec((1, tp, RET_V_HEAD), lambda bi, h: (bi, 0, g_blk0 + h)),
            pl.BlockSpec((1, RET_V_HEAD), lambda bi, h: (0, h)),
        ],
        out_specs=pl.BlockSpec((1, tp, RET_V_HEAD), lambda bi, h: (bi, 0, h)),
        out_shape=jax.ShapeDtypeStruct((b, tp, RET_HEADS * RET_V_HEAD), BF16),
        scratch_shapes=[pltpu.VMEM((tp, RET_V_HEAD), F32), pltpu.VMEM((RET_QK_HEAD, RET_V_HEAD), F32)],
        compiler_params=_cparams(("arbitrary", "arbitrary"), VMEM_LIMIT_BIG),
        name="retention",
    )(lf, lb, proj3, proj3, proj3, proj3, gn)


HP_CW = 128


def _hyprep_kernel(u0_ref, u1_ref, u2_ref, w0_ref, w1_ref, w2_ref, b0_ref, b1_ref, b2_ref,
                   x0_ref, z_ref, zp_ref, *, nch):
    rows = lax.broadcasted_iota(I32, (CH, HP_CW), 0)
    halo = 16
    zp_ref[...] = jnp.zeros_like(zp_ref)

    def conv(u_ref, w_ref, b_ref, n, r0):
        cur = u_ref[0, pl.ds(r0, CH), :].astype(F32)
        rp = pl.multiple_of(jnp.maximum(r0 - halo, 0), halo)
        rn = pl.multiple_of(jnp.minimum(r0 + CH, (nch - 1) * CH), halo)
        prev = u_ref[0, pl.ds(rp, halo), :].astype(F32)[halo - 1:halo, :]
        nxt = u_ref[0, pl.ds(rn, halo), :].astype(F32)[0:1, :]
        prev = jnp.where(n > 0, prev, 0.0)
        nxt = jnp.where(n < nch - 1, nxt, 0.0)
        up = jnp.where(rows == 0, prev, pltpu.roll(cur, 1, axis=0))
        dn = jnp.where(rows == CH - 1, nxt, pltpu.roll(cur, CH - 1, axis=0))
        w = w_ref[...]
        return up * w[0:1, :] + cur * w[1:2, :] + dn * w[2:3, :] + b_ref[...]

    def body(n, carry):
        r0 = pl.multiple_of(n * CH, CH)
        x0 = conv(u0_ref, w0_ref, b0_ref, n, r0)
        x1 = conv(u1_ref, w1_ref, b1_ref, n, r0)
        vv = conv(u2_ref, w2_ref, b2_ref, n, r0)
        z = jnp.where(rows + r0 >= PAD, x1 * vv, 0.0)
        x0_ref[0, pl.ds(r0, CH), :] = x0.astype(BF16)
        z_ref[0, pl.ds(r0, CH), :] = z.astype(BF16)
        zp_ref[0, pl.ds(n, CH, stride=ZCH), :] = z
        return carry

    lax.fori_loop(0, nch, body, 0)


def _hy_prep(proj3, conv_w, conv_b, tp):
    b = proj3.shape[0]
    nch = tp // CH
    ncb = D_MODEL // HP_CW
    u_blk0 = 3072 // HP_CW
    uspec = lambda s: pl.BlockSpec((1, tp, HP_CW), lambda bi, c: (bi, 0, u_blk0 + s * ncb + c))
    wspec = lambda s: pl.BlockSpec((3, HP_CW), lambda bi, c: (0, s * ncb + c))
    bspec = lambda s: pl.BlockSpec((1, HP_CW), lambda bi, c: (0, s * ncb + c))
    return pl.pallas_call(
        functools.partial(_hyprep_kernel, nch=nch),
        grid=(b, ncb),
        in_specs=[uspec(0), uspec(1), uspec(2), wspec(0), wspec(1), wspec(2), bspec(0), bspec(1), bspec(2)],
        out_specs=[pl.BlockSpec((1, tp, HP_CW), lambda bi, c: (bi, 0, c)),
                   pl.BlockSpec((1, tp, HP_CW), lambda bi, c: (bi, 0, c)),
                   pl.BlockSpec((1, CH * ZCH, HP_CW), lambda bi, c: (bi, 0, c))],
        out_shape=[jax.ShapeDtypeStruct((b, tp, D_MODEL), BF16),
                   jax.ShapeDtypeStruct((b, tp, D_MODEL), BF16),
                   jax.ShapeDtypeStruct((b, CH * ZCH, D_MODEL), F32)],
        compiler_params=_cparams(("arbitrary", "arbitrary")),
        name="hy_prep",
    )(proj3, proj3, proj3, conv_w, conv_w, conv_w, conv_b, conv_b, conv_b)


FEAT_W = 128


FT_ROWS = 512
HALF = CH // 2


def _filter_kernel(feat_ref, w1_ref, b1_ref, w2_ref, b2_ref, w3_ref, b3_ref, fq_ref, w4_ref, dl_ref, o_ref,
                   *, groups):
    feat = feat_ref[...]
    fq = fq_ref[...]
    hdn = jnp.sin(fq * (_dot3(feat, w1_ref[0], w1_ref[1]) + b1_ref[...]))
    hdn = jnp.sin(fq * (_dot3(hdn, w2_ref[0], w2_ref[1]) + b2_ref[...]))
    hdn = jnp.sin(fq * (_dot3(hdn, w3_ref[0], w3_ref[1]) + b3_ref[...]))
    filt = _dot3(hdn, w4_ref[0], w4_ref[1])
    dl = dl_ref[...]
    for d in range(2):
        f0 = d * FEAT_W
        scale = jnp.exp(-feat[:, f0:f0 + 1] * dl) * feat[:, f0 + HY_EMB_DIM:f0 + HY_EMB_DIM + 1]
        fd = filt[:, d * D_MODEL:(d + 1) * D_MODEL] * scale
        for g in range(groups):
            o_ref[g * CH + d * HALF:g * CH + (d + 1) * HALF, :] = fd[g * HALF:(g + 1) * HALF]


def _block_diag2(w):
    z = jnp.zeros_like(w)
    return jnp.concatenate([jnp.concatenate([w, z], axis=1), jnp.concatenate([z, w], axis=1)], axis=0)


def _filters(feat_pairs, w1p, b1, w2, b2, w3, b3, fq, w4, dl, rows):
    n_rows = 2 * feat_pairs.shape[0]
    twice = lambda v: jnp.concatenate([v, v], axis=1)
    c = w4.shape[1] // 2
    w4bd = jnp.concatenate([jnp.concatenate([w4[:, :c], jnp.zeros_like(w4[:, c:])], axis=1),
                            jnp.concatenate([jnp.zeros_like(w4[:, :c]), w4[:, c:]], axis=1)], axis=0)
    args = (_split_hi_lo(_block_diag2(w1p)), twice(b1), _split_hi_lo(_block_diag2(w2)), twice(b2),
            _split_hi_lo(_block_diag2(w3)), twice(b3), twice(fq), _split_hi_lo(w4bd), dl)
    full = lambda a: pl.BlockSpec(a.shape, lambda i: (0,) * a.ndim)
    return pl.pallas_call(
        functools.partial(_filter_kernel, groups=rows // CH),
        grid=(n_rows // rows,),
        in_specs=[pl.BlockSpec((rows // 2, 2 * FEAT_W), lambda i: (i, 0))] + [full(a) for a in args],
        out_specs=pl.BlockSpec((rows, D_MODEL), lambda i: (i, 0)),
        out_shape=jax.ShapeDtypeStruct((n_rows, D_MODEL), F32),
        compiler_params=_cparams(("arbitrary",)),
        name="hy_filter",
    )(feat_pairs, *args)


FFT_G = 16


def _lanes(j, c):
    return slice(j * c, (j + 1) * c)


def _fft_a_data_kernel(m_ref, z_ref, o_ref):
    for j in range(FFT_G):
        rows = slice(j * ZCH, (j + 1) * ZCH)
        x = jnp.concatenate([z_ref[0, rows, :], z_ref[1, rows, :]], axis=0).astype(BF16)
        o_ref[j] = jnp.dot(m_ref[j], x, preferred_element_type=F32).astype(BF16)


def _fft_a_filt_kernel(m_ref, g_ref, o_ref):
    for j in range(FFT_G):
        gj = g_ref[j * CH:(j + 1) * CH, :].astype(BF16)
        o_ref[j] = jnp.dot(m_ref[j], gj, preferred_element_type=F32).astype(BF16)


def _fft_b_kernel(f_ref, fi_ref, yr_ref, yi_ref, gr_ref, gi_ref, o_ref):
    c = o_ref.shape[-1]
    for j in range(FFT_G):
        y = jnp.concatenate([yr_ref[:, _lanes(j, c)], yi_ref[:, _lanes(j, c)]], axis=0)
        yg = jnp.concatenate([gr_ref[:, _lanes(j, c)], gi_ref[:, _lanes(j, c)]], axis=0)
        x = jnp.dot(f_ref[...], y, preferred_element_type=F32)
        g = jnp.dot(f_ref[...], yg, preferred_element_type=F32) * (1.0 / NFFT)
        xr, xi = x[:CH], x[CH:]
        gr, gi = g[:CH], g[CH:]
        p = jnp.concatenate([xr * gr - xi * gi, xr * gi + xi * gr], axis=0).astype(BF16)
        o_ref[j] = jnp.dot(fi_ref[...], p, preferred_element_type=F32).astype(BF16)


def _fft_a_inv_kernel(m_ref, ur_ref, ui_ref, o_ref):
    c = ur_ref.shape[-1] // FFT_G
    for j in range(FFT_G):
        u = jnp.concatenate([ur_ref[:, _lanes(j, c)], ui_ref[:, _lanes(j, c)]], axis=0)
        y = jnp.dot(m_ref[j], u, preferred_element_type=F32)
        o_ref[:, :, j, :] = y.reshape(2, ZCH, c)


def _dft_tables():
    n2 = np.arange(CH)[:, None, None]
    k1 = np.arange(CH)[None, :, None]

    def theta(n1_count):
        n1 = np.arange(n1_count)[None, None, :]
        return 2.0 * np.pi * ((k1 * (CH * n1 + n2)) % NFFT) / NFFT

    th = theta(ZCH)
    c, s = np.cos(th), np.sin(th)
    m_a = np.concatenate([np.concatenate([c, s], axis=2), np.concatenate([-s, c], axis=2)], axis=1)
    m_ainv = np.transpose(m_a, (0, 2, 1))
    th = theta(CH)
    m_af = np.concatenate([np.cos(th), -np.sin(th)], axis=1)
    a = 2.0 * np.pi * ((np.arange(CH)[:, None] * np.arange(CH)[None, :]) % CH) / CH
    c, s = np.cos(a), np.sin(a)
    f2 = np.block([[c, s], [-s, c]])
    f2i = np.block([[c, -s], [s, c]])
    f = lambda t: jnp.asarray(t.astype(np.float32)).astype(BF16)
    return f(m_a), f(m_ainv), f(m_af), f(f2), f(f2i)


def _fft_conv(zp, g):
    c = zp.shape[-1]
    m_a, m_ainv, m_af, f2, f2i = _dft_tables()
    cp = _cparams(("arbitrary",), VMEM_LIMIT_BIG)
    steps = CH // FFT_G
    gc = FFT_G * c
    full2 = pl.BlockSpec((2 * CH, 2 * CH), lambda i: (0, 0))
    col_re = pl.BlockSpec((CH, gc), lambda i: (0, i))
    col_im = pl.BlockSpec((CH, gc), lambda i: (0, steps + i))
    blk = pl.BlockSpec((FFT_G, 2 * CH, c), lambda i: (i, 0, 0))
    spec_shape = jax.ShapeDtypeStruct((CH, 2 * CH, c), BF16)

    yg = pl.pallas_call(
        _fft_a_filt_kernel, grid=(steps,),
        in_specs=[pl.BlockSpec((FFT_G, 2 * CH, CH), lambda i: (i, 0, 0)),
                  pl.BlockSpec((FFT_G * CH, c), lambda i: (i, 0))],
        out_specs=blk, out_shape=spec_shape, compiler_params=cp, name="fft_a_filt",
    )(m_af, g)
    yg2 = yg.reshape(CH, 2 * CH * c)

    y = pl.pallas_call(
        _fft_a_data_kernel, grid=(steps,),
        in_specs=[pl.BlockSpec((FFT_G, 2 * CH, 2 * ZCH), lambda i: (i, 0, 0)),
                  pl.BlockSpec((2, FFT_G * ZCH, c), lambda i: (0, i, 0))],
        out_specs=blk, out_shape=spec_shape, compiler_params=cp, name="fft_a_data",
    )(m_a, zp)
    y2 = y.reshape(CH, 2 * CH * c)
    u = pl.pallas_call(
        _fft_b_kernel, grid=(steps,),
        in_specs=[full2, full2, col_re, col_im, col_re, col_im],
        out_specs=blk, out_shape=spec_shape, compiler_params=cp, name="fft_b",
    )(f2, f2i, y2, y2, yg2, yg2)
    u2 = u.reshape(CH, 2 * CH * c)
    yc = pl.pallas_call(
        _fft_a_inv_kernel, grid=(steps,),
        in_specs=[pl.BlockSpec((FFT_G, 2 * ZCH, 2 * CH), lambda i: (i, 0, 0)), col_re, col_im],
        out_specs=pl.BlockSpec((2, ZCH, FFT_G, c), lambda i: (0, 0, i, 0)),
        out_shape=jax.ShapeDtypeStruct((2, ZCH, CH, c), F32),
        compiler_params=cp, name="fft_a_inv",
    )(m_ainv, u2, u2)
    return yc.reshape(2, ZCH * CH, c)


def _far_kernel(far_ref, zm_ref, zl_ref, o_ref):
    g_hi = far_ref[HALF + FAR:HALF + 2 * FAR, :]
    ef = far_ref[0:FAR, :] - g_hi
    g_lo = far_ref[FAR:2 * FAR, :]
    hb = far_ref[HALF:HALF + FAR, :]
    zm = zm_ref[0, PAD:CH, :].astype(F32)
    zl = zl_ref[0, PAD:CH, :].astype(F32)
    row = lambda a, i: a[i:i + 1, :]
    eb = [row(hb, 0) - row(g_hi, 0)] + [row(hb, m) - row(g_lo, FAR - m) for m in range(1, FAR)]
    for j in range(FAR):
        acc_f = row(ef, j) * row(zm, 0)
        for i in range(1, j + 1):
            acc_f = acc_f + row(ef, j - i) * row(zm, i)
        o_ref[0, 0, j:j + 1, :] = acc_f
        acc_b = eb[0] * row(zl, j)
        for i in range(j + 1, FAR):
            acc_b = acc_b + eb[i - j] * row(zl, i)
        o_ref[0, 1, j:j + 1, :] = acc_b


def _far_correction(h_far, z, tp):
    b, _, c = z.shape
    last = tp // CH - 1
    return pl.pallas_call(
        _far_kernel, grid=(b,),
        in_specs=[pl.BlockSpec((CH, c), lambda bi: (0, 0)),
                  pl.BlockSpec((1, CH, c), lambda bi: (bi, 0, 0)),
                  pl.BlockSpec((1, CH, c), lambda bi: (bi, last, 0))],
        out_specs=pl.BlockSpec((1, 2, FAR, c), lambda bi: (bi, 0, 0, 0)),
        out_shape=jax.ShapeDtypeStruct((b, 2, FAR, c), F32),
        compiler_params=_cparams(("arbitrary",)),
        name="hy_far",
    )(h_far, z, z)


MG_TM = 640
MG_CN = 256
ROUTER_W = 128
AFF_PACK = ROUTER_W // N_EXPERTS
ROW_SUB, ROW_LANE = 8, 128


def _merge_kernel(*refs, nt_b, nch):
    n_in = MG_TM // CH
    x_refs = refs[:n_in]
    (head_ref, og_ref, x0_ref, z_ref, yc_ref, ga_ref, gb_ref, corr_ref, skip_ref,
     wr_ref, wh_ref, wo_ref, g2_ref, wrt_ref,
     h1_ref, xn_ref, aff_ref, affp_ref, afft_ref, pre_ref, mix_ref, h1s_ref) = refs[n_in:]
    i = pl.program_id(0)
    for k in range(n_in):
        h1s_ref[k * CH:(k + 1) * CH, :] = _padded_chunk(i * n_in + k, nch, head_ref, x_refs[k])
    ib = i % nt_b
    pre_ref[...] = yc_ref[0].astype(F32) + z_ref[0].astype(F32) * skip_ref[...]

    @pl.when(ib == 0)
    def _():
        pre_ref[PAD:CH, :] += corr_ref[0, 1]

    @pl.when(ib == nt_b - 1)
    def _():
        pre_ref[MG_TM - FAR:MG_TM, :] += corr_ref[0, 0]

    chunks = [slice(c * MG_CN, (c + 1) * MG_CN) for c in range(D_MODEL // MG_CN)]
    og = og_ref[0]
    pre = (x0_ref[0].astype(F32) * pre_ref[...]).astype(BF16)
    for cols in chunks:
        ya = jnp.dot(og, wr_ref[:, cols], preferred_element_type=F32)
        yb = jnp.dot(pre, wh_ref[:, cols], preferred_element_type=F32)
        mix_ref[:, cols] = (ga_ref[0, :, cols].astype(F32) * ya + gb_ref[0, :, cols].astype(F32) * yb).astype(BF16)
    mixed = mix_ref[...]
    ss = jnp.zeros((MG_TM, 1), F32)
    for cols in chunks:
        h1 = h1s_ref[:, cols] + jnp.dot(mixed, wo_ref[:, cols], preferred_element_type=F32)
        h1s_ref[:, cols] = h1
        ss = ss + jnp.sum(h1 * h1, axis=-1, keepdims=True)
    rinv = lax.rsqrt(ss * (1.0 / D_MODEL) + RMS_EPS)
    h1 = h1s_ref[...]
    h1_ref[0] = h1.reshape(MG_TM, ROW_SUB, ROW_LANE)
    xn = h1 * rinv * g2_ref[...]
    xn_ref[0] = xn.reshape(MG_TM, ROW_SUB, ROW_LANE)
    logits = _dot3(xn, wrt_ref[0], wrt_ref[1])
    lane = lax.broadcasted_iota(I32, logits.shape, 1)
    logits = jnp.where(lane < N_EXPERTS, logits, -jnp.inf)
    m = jnp.max(logits, axis=-1, keepdims=True)
    e = jnp.exp(logits - m)
    aff = e / jnp.sum(e, axis=-1, keepdims=True)
    rows = lax.broadcasted_iota(I32, logits.shape, 0) + ib * MG_TM
    aff_ref[0] = jnp.where(rows >= PAD, aff, -1.0)
    lane_grp = lax.broadcasted_iota(I32, (MG_TM // AFF_PACK, ROUTER_W), 1) // N_EXPERTS
    packed = jnp.zeros((MG_TM // AFF_PACK, ROUTER_W), F32)
    for g in range(AFF_PACK):
        blk = aff_ref[0, pl.ds(g, MG_TM // AFF_PACK, stride=AFF_PACK), :]
        if g:
            blk = pltpu.roll(blk, N_EXPERTS * g, axis=1)
        packed = jnp.where(lane_grp == g, blk, packed)
    affp_ref[0] = packed
    afft_ref[0] = jnp.transpose(aff_ref[0])[:N_EXPERTS]


def _merge(x_flat, head, og, x0c, z, yc, proj3, corr, skip, w_ret, w_hy, w_o, g2, w_router_p, tp):
    b = og.shape[0]
    nt_b = tp // MG_TM
    nch = tp // CH
    n_in = MG_TM // CH
    n_x = x_flat.shape[0] // CH
    ga_blk0 = 6144 // D_MODEL
    row = lambda w: pl.BlockSpec((1, MG_TM, w), lambda i: (i // nt_b, i % nt_b, 0))
    full = lambda a: pl.BlockSpec(a.shape, lambda i: (0,) * a.ndim)
    x_spec = lambda k: pl.BlockSpec((CH, D_MODEL), lambda i: (_x_chunk_index(i * n_in + k, nch, n_x), 0))
    return pl.pallas_call(
        functools.partial(_merge_kernel, nt_b=nt_b, nch=nch),
        grid=(b * nt_b,),
        in_specs=[x_spec(k) for k in range(n_in)] + [
                  full(head), row(D_MODEL), row(D_MODEL), row(D_MODEL), row(D_MODEL),
                  pl.BlockSpec((1, MG_TM, D_MODEL), lambda i: (i // nt_b, i % nt_b, ga_blk0)),
                  pl.BlockSpec((1, MG_TM, D_MODEL), lambda i: (i // nt_b, i % nt_b, ga_blk0 + 1)),
                  pl.BlockSpec((1, 2, FAR, D_MODEL), lambda i: (i // nt_b, 0, 0, 0)),
                  full(skip), full(w_ret), full(w_hy), full(w_o), full(g2), full(w_router_p)],
        out_specs=[pl.BlockSpec((1, MG_TM, ROW_SUB, ROW_LANE), lambda i: (i // nt_b, i % nt_b, 0, 0)),
                   pl.BlockSpec((1, MG_TM, ROW_SUB, ROW_LANE), lambda i: (i // nt_b, i % nt_b, 0, 0)),
                   row(ROUTER_W),
                   pl.BlockSpec((1, MG_TM // AFF_PACK, ROUTER_W), lambda i: (i // nt_b, i % nt_b, 0)),
                   pl.BlockSpec((1, N_EXPERTS, MG_TM), lambda i: (i // nt_b, 0, i % nt_b))],
        out_shape=[jax.ShapeDtypeStruct((b, tp, ROW_SUB, ROW_LANE), F32),
                   jax.ShapeDtypeStruct((b, tp, ROW_SUB, ROW_LANE), F32),
                   jax.ShapeDtypeStruct((b, tp, ROUTER_W), F32),
                   jax.ShapeDtypeStruct((b, tp // AFF_PACK, ROUTER_W), F32),
                   jax.ShapeDtypeStruct((b, N_EXPERTS, tp), F32)],
        scratch_shapes=[pltpu.VMEM((MG_TM, D_MODEL), F32), pltpu.VMEM((MG_TM, D_MODEL), BF16),
                        pltpu.VMEM((MG_TM, D_MODEL), F32)],
        compiler_params=_cparams(("arbitrary",), VMEM_LIMIT_BIG),
        name="merge",
    )(*([x_flat] * n_in), head, og, x0c, z, yc, proj3, proj3, corr, skip, w_ret, w_hy, w_o, g2, w_router_p)


TK_W = 128


TK_UNROLL = 5


def _select_kernel(aff_ref, affp_ref, low_ref, slot_ref, offs_ref, *, cap, nch):
    def chunk(c):
        r0 = pl.multiple_of(c * CH, CH)
        return aff_ref[0, pl.ds(r0, CH), :]

    def count(pred):
        cnt = jnp.sum(pred(affp_ref[0]).astype(I32), axis=0, keepdims=True)
        for shift in (N_EXPERTS, 2 * N_EXPERTS, 4 * N_EXPERTS):
            cnt = cnt + pltpu.roll(cnt, shift, axis=1)
        return cnt

    def search(it, bits):
        cand = bits | jnp.left_shift(1, 29 - it)
        cand_f = pltpu.bitcast(cand, F32)
        return jnp.where(count(lambda a: a >= cand_f) >= cap, cand, bits)

    thr = pltpu.bitcast(lax.fori_loop(0, 30, search, jnp.zeros((1, ROUTER_W), I32)), F32)
    need = (cap - count(lambda a: a > thr)).astype(F32)
    low = low_ref[...]

    def scan(it, carry):
        c_eq, c_sel = carry
        for u in range(TK_UNROLL):
            c = it * TK_UNROLL + u
            a = chunk(c)
            eq = a == thr
            eq_f = eq.astype(F32)
            eq_rank = jnp.dot(low, eq_f, preferred_element_type=F32) + c_eq
            sel = jnp.logical_or(a > thr, jnp.logical_and(eq, eq_rank < need))
            sel_f = sel.astype(F32)
            slot = jnp.dot(low, sel_f, preferred_element_type=F32) + c_sel
            r0 = pl.multiple_of(c * CH, CH)
            slot_ref[0, pl.ds(r0, CH), :] = jnp.where(sel, slot, -1.0).astype(I32)
            offs_ref[0, c] = c_sel.astype(I32)
            c_eq = c_eq + jnp.sum(eq_f, axis=0, keepdims=True)
            c_sel = c_sel + jnp.sum(sel_f, axis=0, keepdims=True)
        return c_eq, c_sel

    zero = jnp.zeros((1, ROUTER_W), F32)
    lax.fori_loop(0, nch // TK_UNROLL, scan, (zero, zero))


def _extract_kernel(offs_ref, slot_ref, idx_ref, *, nch, nwin):
    idx_ref[...] = jnp.zeros_like(idx_ref)
    lane = lax.broadcasted_iota(I32, (CH, TK_W), 1)
    trow = lax.broadcasted_iota(I32, (CH, TK_W), 0)

    def per_chunk(c, carry):
        r0 = pl.multiple_of(c * CH, CH)
        slots = slot_ref[0, pl.ds(r0, CH), :]
        tpos = (trow + r0).astype(F32)
        for e in range(N_EXPERTS):
            col = jnp.broadcast_to(slots[:, e:e + 1], (CH, TK_W))
            w0 = offs_ref[0, 0, c * N_EXPERTS + e] // TK_W
            for dw in range(2):
                w = w0 + dw
                base = jnp.where(w < nwin, w * TK_W, -2 * TK_W)
                hit = col == lane + base
                row = e * nwin + jnp.minimum(w, nwin - 1)
                idx_ref[row] += jnp.sum(jnp.where(hit, tpos, 0.0), axis=0, keepdims=True)
        return carry

    lax.fori_loop(0, nch, per_chunk, 0)


def _topk(aff, aff_packed, cap, slots):
    b, tp, _ = aff.shape
    nch = tp // CH
    nwin = -(-slots // TK_W)
    low = jnp.asarray(np.tril(np.ones((CH, CH), np.float32), k=-1))
    pack = AFF_PACK
    assert nch % TK_UNROLL == 0 and tp % pack == 0
    slot, offs = pl.pallas_call(
        functools.partial(_select_kernel, cap=cap, nch=nch),
        grid=(b,),
        in_specs=[pl.BlockSpec((1, tp, ROUTER_W), lambda bi: (bi, 0, 0)),
                  pl.BlockSpec((1, tp // pack, ROUTER_W), lambda bi: (bi, 0, 0)),
                  pl.BlockSpec((CH, CH), lambda bi: (0, 0))],
        out_specs=[pl.BlockSpec((1, tp, ROUTER_W), lambda bi: (bi, 0, 0)),
                   pl.BlockSpec((1, nch, 1, ROUTER_W), lambda bi: (bi, 0, 0, 0))],
        out_shape=[jax.ShapeDtypeStruct((b, tp, ROUTER_W), I32),
                   jax.ShapeDtypeStruct((b, nch, 1, ROUTER_W), I32)],
        compiler_params=_cparams(("arbitrary",)),
        name="topk_select",
    )(aff, aff_packed, low)
    offs_s = offs[:, :, 0, :N_EXPERTS].reshape(b, 1, nch * N_EXPERTS)
    rows = pl.BlockSpec((None, N_EXPERTS * nwin, 1, TK_W), lambda bi: (bi, 0, 0, 0))
    out = jax.ShapeDtypeStruct((b, N_EXPERTS * nwin, 1, TK_W), F32)
    idx = pl.pallas_call(
        functools.partial(_extract_kernel, nch=nch, nwin=nwin),
        grid=(b,),
        in_specs=[pl.BlockSpec((1, 1, nch * N_EXPERTS), lambda bi: (bi, 0, 0), memory_space=pltpu.SMEM),
                  pl.BlockSpec((1, tp, ROUTER_W), lambda bi: (bi, 0, 0))],
        out_specs=rows,
        out_shape=out,
        compiler_params=_cparams(("arbitrary",)),
        name="topk_extract",
    )(offs_s, slot)
    idx = idx.reshape(b * N_EXPERTS, 1, nwin * TK_W)[:, :, :slots].astype(I32)
    return idx


FF_TF = 1024
MOE_CN = 256


MOE_UNROLL = 4


def _moe_kernel(idx_ref, idx_next_ref, xn_hbm, wg_ref, wu_ref, wd_ref, o_ref,
                xe32_ref, xe_ref, hid_ref, acc_ref, sem,
                *, slots, nf):
    i = pl.program_id(0)
    f = pl.program_id(1)
    buf = i % 2
    share = slots // nf

    def row_copy(ids_ref, s, b):
        return pltpu.make_async_copy(xn_hbm.at[ids_ref[0, 0, s]], xe32_ref.at[b, s], sem.at[b])

    def for_rows(lo, n, fn):
        def body(k, carry):
            for u in range(MOE_UNROLL):
                fn(lo + k * MOE_UNROLL + u)
            return carry

        lax.fori_loop(0, n // MOE_UNROLL, body, 0)

    @pl.when(jnp.logical_and(i == 0, f == 0))
    def _():
        for_rows(0, slots, lambda s: row_copy(idx_ref, s, 0).start())

    @pl.when(f == 0)
    def _():
        for_rows(0, slots, lambda s: row_copy(idx_ref, s, buf).wait())
        xe_ref[...] = xe32_ref[buf].reshape(slots, D_MODEL).astype(BF16)

    @pl.when(f == 0)
    def _():
        acc_ref[...] = jnp.zeros_like(acc_ref)

    n_up, n_down = FF_TF // MOE_CN, D_MODEL // MOE_CN
    per_chunk = share // (n_up + n_down)

    def prefetch(chunk):
        base = f * share + chunk * per_chunk
        for u in range(per_chunk):
            row_copy(idx_next_ref, base + u, 1 - buf).start()

    xe = xe_ref[...]
    for c in range(n_up):
        cols = slice(c * MOE_CN, (c + 1) * MOE_CN)
        gg = jnp.dot(xe, wg_ref[0, :, cols].astype(BF16), preferred_element_type=F32)
        uu = jnp.dot(xe, wu_ref[0, :, cols].astype(BF16), preferred_element_type=F32)
        hid_ref[:, cols] = (gg * jax.nn.sigmoid(gg) * uu).astype(BF16)
        prefetch(c)
    hid = hid_ref[...]
    for c in range(n_down):
        cols = slice(c * MOE_CN, (c + 1) * MOE_CN)
        part = jnp.dot(hid, wd_ref[0, :, cols].astype(BF16), preferred_element_type=F32)
        acc_ref[:, cols] = acc_ref[:, cols] + part
        prefetch(n_up + c)

    @pl.when(f == pl.num_programs(1) - 1)
    def _():
        o_ref[0] = acc_ref[...].reshape(slots, ROW_SUB, ROW_LANE)

    @pl.when(jnp.logical_and(i == pl.num_programs(0) - 1, f == pl.num_programs(1) - 1))
    def _():
        for_rows(0, slots, lambda s: row_copy(idx_next_ref, s, 1 - buf).wait())


def _moe_ffn(idx, xn_flat, w_gate, w_up, w_down, slots):
    be = idx.shape[0]
    nf = D_FF // FF_TF
    assert slots % MOE_UNROLL == 0 and slots % (nf * (FF_TF // MOE_CN + D_MODEL // MOE_CN)) == 0
    return pl.pallas_call(
        functools.partial(_moe_kernel, slots=slots, nf=nf),
        grid=(be, nf),
        in_specs=[pl.BlockSpec((1, 1, slots), lambda i, f: (i, 0, 0), memory_space=pltpu.SMEM),
                  pl.BlockSpec((1, 1, slots), lambda i, f: (jnp.minimum(i + 1, be - 1), 0, 0),
                               memory_space=pltpu.SMEM),
                  pl.BlockSpec(memory_space=pl.ANY),
                  pl.BlockSpec((1, D_MODEL, FF_TF), lambda i, f: (i % N_EXPERTS, 0, f)),
                  pl.BlockSpec((1, D_MODEL, FF_TF), lambda i, f: (i % N_EXPERTS, 0, f)),
                  pl.BlockSpec((1, FF_TF, D_MODEL), lambda i, f: (i % N_EXPERTS, f, 0))],
        out_specs=pl.BlockSpec((1, slots, ROW_SUB, ROW_LANE), lambda i, f: (i, 0, 0, 0)),
        out_shape=jax.ShapeDtypeStruct((be, slots, ROW_SUB, ROW_LANE), F32),
        scratch_shapes=[pltpu.VMEM((2, slots, ROW_SUB, ROW_LANE), F32), pltpu.VMEM((slots, D_MODEL), BF16),
                        pltpu.VMEM((slots, FF_TF), BF16), pltpu.VMEM((slots, D_MODEL), F32),
                        pltpu.SemaphoreType.DMA((2,))],
        compiler_params=_cparams(("arbitrary", "arbitrary"), VMEM_LIMIT_BIG),
        name="moe_ffn",
    )(idx, idx, xn_flat, w_gate, w_up, w_down)


CB_U = 6
CB_NB = 512


def _combine_kernel(idx_ref, gate_ref, h1_hbm, ye_ref, g_ref, o_hbm, acc_ref, stage_ref, sem, osem, *, cap, tp):
    b = pl.program_id(0)
    e = pl.program_id(1)

    @pl.when(e == 0)
    def _():
        cp = pltpu.make_async_copy(h1_hbm.at[b], acc_ref, sem)
        cp.start()
        cp.wait()

    def rmw(g, carry):
        ts = [idx_ref[0, 0, g * CB_U + u] for u in range(CB_U)]
        vals = [acc_ref[ts[u]] + gate_ref[0, 0, ts[u]] * ye_ref[g * CB_U + u] for u in range(CB_U)]
        for u in range(CB_U):
            acc_ref[ts[u]] = vals[u]
        return carry

    lax.fori_loop(0, cap // CB_U, rmw, 0)

    @pl.when(e == pl.num_programs(1) - 1)
    def _():
        gamma = g_ref[...]
        n_blocks = (tp - CH) // CB_NB

        def out_copy(k):
            return pltpu.make_async_copy(stage_ref.at[k % 2], o_hbm.at[b, pl.ds(k * CB_NB, CB_NB), :],
                                         osem.at[k % 2])

        for k in range(n_blocks):
            x = acc_ref[pl.ds(CH + k * CB_NB, CB_NB)]
            ms = jnp.sum(jnp.sum(x * x, axis=2, keepdims=True), axis=1, keepdims=True) * (1.0 / D_MODEL)
            y = x * lax.rsqrt(ms + RMS_EPS) * gamma
            if k >= 2:
                out_copy(k - 2).wait()
            stage_ref[k % 2] = y.reshape(CB_NB, D_MODEL)
            out_copy(k).start()
        for k in range(max(n_blocks - 2, 0), n_blocks):
            out_copy(k).wait()


def _combine(idx_local, gate, h1, ye, gf, cap, slots):
    b, tp = h1.shape[:2]
    assert cap % CB_U == 0 and (tp - CH) % CB_NB == 0
    return pl.pallas_call(
        functools.partial(_combine_kernel, cap=cap, tp=tp),
        grid=(b, N_EXPERTS),
        in_specs=[pl.BlockSpec((1, 1, slots), lambda bi, e: (bi * N_EXPERTS + e, 0, 0), memory_space=pltpu.SMEM),
                  pl.BlockSpec((1, 1, tp), lambda bi, e: (bi * N_EXPERTS + e, 0, 0), memory_space=pltpu.SMEM),
                  pl.BlockSpec(memory_space=pl.ANY),
                  pl.BlockSpec((None, slots, ROW_SUB, ROW_LANE), lambda bi, e: (bi * N_EXPERTS + e, 0, 0, 0)),
                  pl.BlockSpec((1, ROW_SUB, ROW_LANE), lambda bi, e: (0, 0, 0))],
        out_specs=pl.BlockSpec(memory_space=pl.ANY),
        out_shape=jax.ShapeDtypeStruct((b, tp - CH, D_MODEL), F32),
        scratch_shapes=[pltpu.VMEM((tp, ROW_SUB, ROW_LANE), F32), pltpu.VMEM((2, CB_NB, D_MODEL), F32),
                        pltpu.SemaphoreType.DMA(()), pltpu.SemaphoreType.DMA((2,))],
        compiler_params=_cparams(("arbitrary", "arbitrary"), VMEM_LIMIT_BIG),
        name="combine",
    )(idx_local, gate, h1, ye, gf.reshape(1, ROW_SUB, ROW_LANE))


def _rope_tables(tp, b):
    half = RET_QK_HEAD // 2
    pos = np.arange(tp, dtype=np.float64) - PAD
    inv = ROPE_BASE ** (-np.arange(half, dtype=np.float64) / half)
    ang = pos[:, None] * inv[None, :]
    cos, sin = np.cos(ang), np.sin(ang)
    tile = lambda t: jnp.asarray(np.tile(t, (b, 1)).astype(np.float32))
    return tile(np.concatenate([cos, cos], axis=1)), tile(np.concatenate([-sin, sin], axis=1))


def _filter_features(t_len):
    half = NFFT // 2
    q = np.arange(NFFT)
    r = CH * (q % CH) + q // CH
    p_main = np.where(r < half, r, NFFT - r)
    valid_main = (r != half).astype(np.float32)
    m = np.arange(FAR)
    pad = np.zeros(HALF - 2 * FAR, np.int64)
    p_far = np.concatenate([half + m, half - FAR + m, pad, half + m, half - m, pad])
    valid_far = np.ones(CH, np.float32)
    valid_far[HALF + FAR] = 0.0
    p = np.concatenate([p_main, p_far]).astype(np.float64)
    valid = np.concatenate([valid_main, valid_far]).astype(np.float64)
    t_norm = p / (t_len - 1)
    bands = (HY_EMB_DIM - 1) // 2
    fr = np.linspace(1e-4, bands - 1, bands)
    ang = (2.0 * math.pi * p / t_len)[:, None] * fr[None, :]
    feat = np.concatenate([t_norm[:, None], np.cos(ang), -np.sin(ang), valid[:, None]], axis=-1)
    feat = np.pad(feat, ((0, 0), (0, FEAT_W - feat.shape[1])))
    pairs = feat.reshape(-1, 2, HALF, FEAT_W).transpose(0, 2, 1, 3).reshape(-1, 2 * FEAT_W)
    return jnp.asarray(pairs.astype(np.float32))


def kernel(x, meta_tokens, norm1_g, w_in, ret_decay_fwd, ret_decay_bwd, ret_head_norm_g, w_ret_out,
           hy_conv_w, hy_conv_b, hy_filt_w1, hy_filt_b1, hy_filt_w2, hy_filt_b2, hy_filt_w3, hy_filt_b3,
           hy_filt_freq, hy_filt_w4, hy_skip, w_hy_out, w_o, norm2_g, w_router, w_exp_gate, w_exp_up,
           w_exp_down, final_norm_g):
    b, seq, d = x.shape
    t_len = seq + N_META
    tp = PAD + t_len
    assert d == D_MODEL and (b * tp) % IP_TM == 0 and tp % MG_TM == 0 and tp % CH == 0 and t_len - NFFT // 2 == FAR
    cap = EC_CAPACITY * t_len // N_EXPERTS
    slots = -(-cap // 16) * 16
    l = 0

    assert seq % CH == 0
    head = jnp.concatenate([jnp.zeros((PAD, d), x.dtype), meta_tokens.astype(x.dtype)], axis=0)
    x_flat = x.reshape(b * seq, d)

    cs, sn = _rope_tables(tp, b)
    proj = _in_proj(x_flat, head, norm1_g[l][None], w_in[l], cs, sn, tp, b)
    proj3 = proj.reshape(b, tp, IN_PROJ_W)

    lf = jax.nn.log_sigmoid(ret_decay_fwd[l].astype(F32))
    lb = jax.nn.log_sigmoid(ret_decay_bwd[l].astype(F32))
    og = _retention(proj3, lf, lb, ret_head_norm_g[l][None], tp)

    x0c, z, zp = _hy_prep(proj3, hy_conv_w[l], hy_conv_b[l][None], tp)

    feat = _filter_features(t_len)
    w1p = jnp.pad(hy_filt_w1[l].astype(F32), ((0, FEAT_W - HY_EMB_DIM), (0, 0)))
    max_decay = math.log(HY_DECAY_TARGET) / HY_FAST_DECAY_PCT
    min_decay = math.log(HY_DECAY_TARGET) / HY_SLOW_DECAY_PCT
    dl = jnp.abs(jnp.linspace(min_decay, max_decay, D_MODEL, dtype=F32))[None]
    fargs = (w1p, hy_filt_b1[l][None].astype(F32), hy_filt_w2[l].astype(F32), hy_filt_b2[l][None].astype(F32),
             hy_filt_w3[l].astype(F32), hy_filt_b3[l][None].astype(F32), hy_filt_freq[l][None].astype(F32),
             hy_filt_w4[l].astype(F32), dl)
    g = _filters(feat[:NFFT // 2], *fargs, FT_ROWS)
    h_far = _filters(feat[NFFT // 2:], *fargs, CH)
    yc = _fft_conv(zp, g)
    corr = _far_correction(h_far, z, tp)

    w_router_p = jnp.pad(w_router[l].astype(F32), ((0, 0), (0, ROUTER_W - N_EXPERTS)))
    w_router_p = _split_hi_lo(w_router_p)
    h1, xn2, aff, aff_packed, aff_t = _merge(x_flat, head, og, x0c, z, yc, proj3, corr, hy_skip[l][None].astype(F32),
                          w_ret_out[l].astype(BF16), w_hy_out[l].astype(BF16), w_o[l].astype(BF16),
                          norm2_g[l][None].astype(F32), w_router_p, tp)

    idx = _topk(aff, aff_packed, cap, slots)
    live = (jnp.arange(slots) < cap)[None, None, :]
    idx_local = jnp.where(live, idx, PAD)
    idx_flat = idx_local + (jnp.arange(b * N_EXPERTS, dtype=I32) // N_EXPERTS * tp)[:, None, None]
    ye = _moe_ffn(idx_flat, xn2.reshape(b * tp, ROW_SUB, ROW_LANE),
                  w_exp_gate[l], w_exp_up[l], w_exp_down[l], slots)
    return _combine(idx_local, aff_t.reshape(b * N_EXPERTS, 1, tp), h1, ye, final_norm_g.astype(F32), cap, slots)
```

```python
import functools
import math

import numpy as np
import jax
import jax.numpy as jnp
from jax import lax
from jax.experimental import pallas as pl
from jax.experimental.pallas import tpu as pltpu

F32 = jnp.float32
BF16 = jnp.bfloat16
I32 = jnp.int32

D_MODEL = 1024
N_META = 16
RET_HEADS = 4
RET_QK_HEAD = 128
RET_V_HEAD = 256
ROPE_BASE = 10000.0
HY_EMB_DIM = 33
HY_FILTER_ORDER = 64
HY_FAST_DECAY_PCT = 0.3
HY_SLOW_DECAY_PCT = 1.5
HY_DECAY_TARGET = 1e-2
N_EXPERTS = 16
EC_CAPACITY = 2
D_FF = 2 * D_MODEL
RMS_EPS = 1e-6
IN_PROJ_W = 8192

CH = 128
PAD = CH - N_META
NFFT = 16384
ZCH = 80
FAR = 16

VMEM_LIMIT_BIG = 56 * 1024 * 1024
VMEM_LIMIT_MID = 40 * 1024 * 1024


def _cparams(sem, vmem=VMEM_LIMIT_MID):
    return pltpu.CompilerParams(dimension_semantics=sem, vmem_limit_bytes=vmem)


def _split_hi_lo(w):
    w = w.astype(F32)
    hi = w.astype(BF16)
    return jnp.stack([hi, (w - hi.astype(F32)).astype(BF16)])


def _dot3(a, w_hi, w_lo):
    a_hi = a.astype(BF16)
    a_lo = (a - a_hi.astype(F32)).astype(BF16)
    return (jnp.dot(a_hi, w_hi, preferred_element_type=F32) + jnp.dot(a_lo, w_hi, preferred_element_type=F32)
            + jnp.dot(a_hi, w_lo, preferred_element_type=F32))


IP_TM = 1280
IP_TN = 1024
IP_CN = 256


def _padded_chunk(flat_chunk, nch, head_ref, x_ref):
    return jnp.where(flat_chunk % nch == 0, head_ref[...], x_ref[...])


def _x_chunk_index(flat_chunk, nch, n_x_chunks):
    return jnp.clip(flat_chunk - flat_chunk // nch - 1, 0, n_x_chunks - 1)


def _inproj_kernel(*refs, nch):
    n_in = IP_TM // CH
    x_refs, (head_ref, g_ref, w_ref, cs_ref, sn_ref, o_ref, xn_ref) = refs[:n_in], refs[n_in:]
    i = pl.program_id(0)
    j = pl.program_id(1)

    @pl.when(j == 0)
    def _():
        for k in range(n_in):
            x = _padded_chunk(i * n_in + k, nch, head_ref, x_refs[k])
            ms = jnp.mean(x * x, axis=-1, keepdims=True)
            xn_ref[k * CH:(k + 1) * CH, :] = (x * lax.rsqrt(ms + RMS_EPS) * g_ref[...]).astype(BF16)

    def run(epilogue):
        for c in range(IP_TN // IP_CN):
            cols = slice(c * IP_CN, (c + 1) * IP_CN)
            acc = jnp.dot(xn_ref[...], w_ref[:, cols].astype(BF16), preferred_element_type=F32)
            epilogue(c, cols, acc)

    def rotary(c, cols, acc):
        scale = 1.0 if c < (IP_TN // IP_CN) // 2 else RET_QK_HEAD ** -0.5
        cs = cs_ref[...] * scale
        sn = sn_ref[...] * scale
        for hh in range(IP_CN // RET_QK_HEAD):
            xh = acc[:, hh * RET_QK_HEAD:(hh + 1) * RET_QK_HEAD]
            rot = xh * cs + pltpu.roll(xh, RET_QK_HEAD // 2, axis=1) * sn
            lo = c * IP_CN + hh * RET_QK_HEAD
            o_ref[:, lo:lo + RET_QK_HEAD] = rot.astype(BF16)

    def raw(c, cols, acc):
        o_ref[:, cols] = acc.astype(BF16)

    def swish(c, cols, acc):
        o_ref[:, cols] = (acc * jax.nn.sigmoid(acc)).astype(BF16)

    def sigm(c, cols, acc):
        o_ref[:, cols] = jax.nn.sigmoid(acc).astype(BF16)

    pl.when(j == 0)(lambda: run(rotary))
    pl.when(jnp.logical_or(j == 1, jnp.logical_and(j >= 3, j < 6)))(lambda: run(raw))
    pl.when(j == 2)(lambda: run(swish))
    pl.when(j >= 6)(lambda: run(sigm))


def _in_proj(x_flat, head, g1, w_in_bf, cs, sn, tp, b):
    n_rows = b * tp
    nch = tp // CH
    n_in = IP_TM // CH
    n_x = x_flat.shape[0] // CH
    x_spec = lambda k: pl.BlockSpec((CH, D_MODEL), lambda i, j: (_x_chunk_index(i * n_in + k, nch, n_x), 0))
    return pl.pallas_call(
        functools.partial(_inproj_kernel, nch=nch),
        grid=(n_rows // IP_TM, IN_PROJ_W // IP_TN),
        in_specs=[x_spec(k) for k in range(n_in)] + [
            pl.BlockSpec((CH, D_MODEL), lambda i, j: (0, 0)),
            pl.BlockSpec((1, D_MODEL), lambda i, j: (0, 0)),
            pl.BlockSpec((D_MODEL, IP_TN), lambda i, j: (0, j)),
            pl.BlockSpec((IP_TM, RET_QK_HEAD), lambda i, j: (i, 0)),
            pl.BlockSpec((IP_TM, RET_QK_HEAD), lambda i, j: (i, 0)),
        ],
        out_specs=pl.BlockSpec((IP_TM, IP_TN), lambda i, j: (i, j)),
        out_shape=jax.ShapeDtypeStruct((n_rows, IN_PROJ_W), BF16),
        scratch_shapes=[pltpu.VMEM((IP_TM, D_MODEL), BF16)],
        compiler_params=_cparams(("arbitrary", "arbitrary")),
        name="in_proj",
    )(*([x_flat] * n_in), head, g1, w_in_bf, cs, sn)


RET_UNROLL = 65


def _ret_kernel(lf_ref, lb_ref, q_ref, k_ref, v_ref, gr_ref, gn_ref, o_ref, ob_ref, s_ref, *, nch):
    h = pl.program_id(1)
    lf = lf_ref[h]
    lb = lb_ref[h]
    ri = lax.broadcasted_iota(I32, (CH, CH), 0).astype(F32)
    ci = lax.broadcasted_iota(I32, (CH, CH), 1).astype(F32)
    diff = ri - ci
    mask = jnp.exp(jnp.where(diff >= 0, lf * diff, -lb * diff))
    w_end = jnp.exp(lf * (CH - 1.0 - ri))
    w_start = jnp.exp(lb * ri)
    qw_f = jnp.exp(lf * (ri + 1.0))
    qw_b = jnp.exp(lb * (CH - ri))
    dec_f = jnp.exp(jnp.full((CH, RET_V_HEAD), lf * CH, F32))
    dec_b = jnp.exp(jnp.full((CH, RET_V_HEAD), lb * CH, F32))
    tn_dims = (((0,), (0,)), ((), ()))
    nt_dims = (((1,), (1,)), ((), ()))

    s_ref[...] = jnp.zeros_like(s_ref)

    def bwd(it, carry):
        s = s_ref[...]
        for u in range(RET_UNROLL):
            n = nch - 1 - (it * RET_UNROLL + u)
            r0 = pl.multiple_of(n * CH, CH)
            q = q_ref[0, pl.ds(r0, CH), :].astype(F32)
            k = k_ref[0, pl.ds(r0, CH), :].astype(F32)
            v = v_ref[0, pl.ds(r0, CH), :]
            ob_ref[pl.ds(r0, CH), :] = jnp.dot((q * qw_b).astype(BF16), s.astype(BF16),
                                               preferred_element_type=F32)
            a = lax.dot_general((k * w_start).astype(BF16), v, tn_dims, preferred_element_type=F32)
            s = s * dec_b + a
        s_ref[...] = s
        return carry

    lax.fori_loop(0, nch // RET_UNROLL, bwd, 0)

    s_ref[...] = jnp.zeros_like(s_ref)
    gn = gn_ref[...]

    def fwd(it, carry):
        s = s_ref[...]
        for u in range(RET_UNROLL):
            n = it * RET_UNROLL + u
            r0 = pl.multiple_of(n * CH, CH)
            qb = q_ref[0, pl.ds(r0, CH), :]
            kb = k_ref[0, pl.ds(r0, CH), :]
            v = v_ref[0, pl.ds(r0, CH), :]
            q = qb.astype(F32)
            k = kb.astype(F32)
            scores = lax.dot_general(qb, kb, nt_dims, preferred_element_type=F32) * mask
            o = jnp.dot(scores.astype(BF16), v, preferred_element_type=F32)
            o = o + jnp.dot((q * qw_f).astype(BF16), s.astype(BF16), preferred_element_type=F32)
            o = o + ob_ref[pl.ds(r0, CH), :]
            a = lax.dot_general((k * w_end).astype(BF16), v, tn_dims, preferred_element_type=F32)
            s = s * dec_f + a
            y = o * lax.rsqrt(jnp.mean(o * o, axis=-1, keepdims=True) + RMS_EPS) * gn
            o_ref[0, pl.ds(r0, CH), :] = (y * gr_ref[0, pl.ds(r0, CH), :].astype(F32)).astype(BF16)
        s_ref[...] = s
        return carry

    lax.fori_loop(0, nch // RET_UNROLL, fwd, 0)


def _retention(proj3, lf, lb, gn, tp):
    b = proj3.shape[0]
    nch = tp // CH
    qk_blocks = (RET_HEADS * RET_QK_HEAD) // RET_QK_HEAD
    v_blk0 = (2 * RET_HEADS * RET_QK_HEAD) // RET_V_HEAD
    g_blk0 = v_blk0 + RET_HEADS
    smem = pl.BlockSpec(memory_space=pltpu.SMEM)
    return pl.pallas_call(
        functools.partial(_ret_kernel, nch=nch),
        grid=(b, RET_HEADS),
        in_specs=[
            smem, smem,
            pl.BlockSpec((1, tp, RET_QK_HEAD), lambda bi, h: (bi, 0, h)),
            pl.BlockSpec((1, tp, RET_QK_HEAD), lambda bi, h: (bi, 0, qk_blocks + h)),
            pl.BlockSpec((1, tp, RET_V_HEAD), lambda bi, h: (bi, 0, v_blk0 + h)),
            pl.BlockSpec((1, tp, RET_V_HEAD), lambda bi, h: (bi, 0, g_blk0 + h)),
            pl.BlockSpec((1, RET_V_HEAD), lambda bi, h: (0, h)),
        ],
        out_specs=pl.BlockSpec((1, tp, RET_V_HEAD), lambda bi, h: (bi, 0, h)),
        out_shape=jax.ShapeDtypeStruct((b, tp, RET_HEADS * RET_V_HEAD), BF16),
        scratch_shapes=[pltpu.VMEM((tp, RET_V_HEAD), F32), pltpu.VMEM((RET_QK_HEAD, RET_V_HEAD), F32)],
        compiler_params=_cparams(("arbitrary", "arbitrary"), VMEM_LIMIT_BIG),
        name="retention",
    )(lf, lb, proj3, proj3, proj3, proj3, gn)


HP_CW = 128


def _hyprep_kernel(u0_ref, u1_ref, u2_ref, w0_ref, w1_ref, w2_ref, b0_ref, b1_ref, b2_ref,
                   x0_ref, z_ref, zp_ref, *, nch):
    rows = lax.broadcasted_iota(I32, (CH, HP_CW), 0)
    halo = 16
    zp_ref[...] = jnp.zeros_like(zp_ref)

    def conv(u_ref, w_ref, b_ref, n, r0):
        cur = u_ref[0, pl.ds(r0, CH), :].astype(F32)
        rp = pl.multiple_of(jnp.maximum(r0 - halo, 0), halo)
        rn = pl.multiple_of(jnp.minimum(r0 + CH, (nch - 1) * CH), halo)
        prev = u_ref[0, pl.ds(rp, halo), :].astype(F32)[halo - 1:halo, :]
        nxt = u_ref[0, pl.ds(rn, halo), :].astype(F32)[0:1, :]
        prev = jnp.where(n > 0, prev, 0.0)
        nxt = jnp.where(n < nch - 1, nxt, 0.0)
        up = jnp.where(rows == 0, prev, pltpu.roll(cur, 1, axis=0))
        dn = jnp.where(rows == CH - 1, nxt, pltpu.roll(cur, CH - 1, axis=0))
        w = w_ref[...]
        return up * w[0:1, :] + cur * w[1:2, :] + dn * w[2:3, :] + b_ref[...]

    def body(n, carry):
        r0 = pl.multiple_of(n * CH, CH)
        x0 = conv(u0_ref, w0_ref, b0_ref, n, r0)
        x1 = conv(u1_ref, w1_ref, b1_ref, n, r0)
        vv = conv(u2_ref, w2_ref, b2_ref, n, r0)
        z = jnp.where(rows + r0 >= PAD, x1 * vv, 0.0)
        x0_ref[0, pl.ds(r0, CH), :] = x0.astype(BF16)
        z_ref[0, pl.ds(r0, CH), :] = z.astype(BF16)
        zp_ref[0, pl.ds(n, CH, stride=ZCH), :] = z
        return carry

    lax.fori_loop(0, nch, body, 0)


def _hy_prep(proj3, conv_w, conv_b, tp):
    b = proj3.shape[0]
    nch = tp // CH
    ncb = D_MODEL // HP_CW
    u_blk0 = 3072 // HP_CW
    uspec = lambda s: pl.BlockSpec((1, tp, HP_CW), lambda bi, c: (bi, 0, u_blk0 + s * ncb + c))
    wspec = lambda s: pl.BlockSpec((3, HP_CW), lambda bi, c: (0, s * ncb + c))
    bspec = lambda s: pl.BlockSpec((1, HP_CW), lambda bi, c: (0, s * ncb + c))
    return pl.pallas_call(
        functools.partial(_hyprep_kernel, nch=nch),
        grid=(b, ncb),
        in_specs=[uspec(0), uspec(1), uspec(2), wspec(0), wspec(1), wspec(2), bspec(0), bspec(1), bspec(2)],
        out_specs=[pl.BlockSpec((1, tp, HP_CW), lambda bi, c: (bi, 0, c)),
                   pl.BlockSpec((1, tp, HP_CW), lambda bi, c: (bi, 0, c)),
                   pl.BlockSpec((1, CH * ZCH, HP_CW), lambda bi, c: (bi, 0, c))],
        out_shape=[jax.ShapeDtypeStruct((b, tp, D_MODEL), BF16),
                   jax.ShapeDtypeStruct((b, tp, D_MODEL), BF16),
                   jax.ShapeDtypeStruct((b, CH * ZCH, D_MODEL), F32)],
        compiler_params=_cparams(("arbitrary", "arbitrary")),
        name="hy_prep",
    )(proj3, proj3, proj3, conv_w, conv_w, conv_w, conv_b, conv_b, conv_b)


FEAT_W = 128


FT_ROWS = 512
HALF = CH // 2


def _filter_kernel(feat_ref, w1_ref, b1_ref, w2_ref, b2_ref, w3_ref, b3_ref, fq_ref, w4_ref, dl_ref, o_ref,
                   *, groups):
    feat = feat_ref[...]
    fq = fq_ref[...]
    hdn = jnp.sin(fq * (_dot3(feat, w1_ref[0], w1_ref[1]) + b1_ref[...]))
    hdn = jnp.sin(fq * (_dot3(hdn, w2_ref[0], w2_ref[1]) + b2_ref[...]))
    hdn = jnp.sin(fq * (_dot3(hdn, w3_ref[0], w3_ref[1]) + b3_ref[...]))
    filt = _dot3(hdn, w4_ref[0], w4_ref[1])
    dl = dl_ref[...]
    for d in range(2):
        f0 = d * FEAT_W
        scale = jnp.exp(-feat[:, f0:f0 + 1] * dl) * feat[:, f0 + HY_EMB_DIM:f0 + HY_EMB_DIM + 1]
        fd = filt[:, d * D_MODEL:(d + 1) * D_MODEL] * scale
        for g in range(groups):
            o_ref[g * CH + d * HALF:g * CH + (d + 1) * HALF, :] = fd[g * HALF:(g + 1) * HALF]


def _block_diag2(w):
    z = jnp.zeros_like(w)
    return jnp.concatenate([jnp.concatenate([w, z], axis=1), jnp.concatenate([z, w], axis=1)], axis=0)


def _filters(feat_pairs, w1p, b1, w2, b2, w3, b3, fq, w4, dl, rows):
    n_rows = 2 * feat_pairs.shape[0]
    twice = lambda v: jnp.concatenate([v, v], axis=1)
    c = w4.shape[1] // 2
    w4bd = jnp.concatenate([jnp.concatenate([w4[:, :c], jnp.zeros_like(w4[:, c:])], axis=1),
                            jnp.concatenate([jnp.zeros_like(w4[:, :c]), w4[:, c:]], axis=1)], axis=0)
    args = (_split_hi_lo(_block_diag2(w1p)), twice(b1), _split_hi_lo(_block_diag2(w2)), twice(b2),
            _split_hi_lo(_block_diag2(w3)), twice(b3), twice(fq), _split_hi_lo(w4bd), dl)
    full = lambda a: pl.BlockSpec(a.shape, lambda i: (0,) * a.ndim)
    return pl.pallas_call(
        functools.partial(_filter_kernel, groups=rows // CH),
        grid=(n_rows // rows,),
        in_specs=[pl.BlockSpec((rows // 2, 2 * FEAT_W), lambda i: (i, 0))] + [full(a) for a in args],
        out_specs=pl.BlockSpec((rows, D_MODEL), lambda i: (i, 0)),
        out_shape=jax.ShapeDtypeStruct((n_rows, D_MODEL), F32),
        compiler_params=_cparams(("arbitrary",)),
        name="hy_filter",
    )(feat_pairs, *args)


FFT_G = 8


def _lanes(j, c):
    return slice(j * c, (j + 1) * c)


def _fft_a_data_kernel(m_ref, z_ref, o_ref):
    for j in range(FFT_G):
        rows = slice(j * ZCH, (j + 1) * ZCH)
        x = jnp.concatenate([z_ref[0, rows, :], z_ref[1, rows, :]], axis=0).astype(BF16)
        o_ref[j] = jnp.dot(m_ref[j], x, preferred_element_type=F32).astype(BF16)


def _fft_a_filt_kernel(m_ref, g_ref, o_ref):
    for j in range(FFT_G):
        gj = g_ref[j * CH:(j + 1) * CH, :].astype(BF16)
        o_ref[j] = jnp.dot(m_ref[j], gj, preferred_element_type=F32).astype(BF16)


def _fft_b_kernel(f_ref, fi_ref, yr_ref, yi_ref, gr_ref, gi_ref, o_ref):
    c = o_ref.shape[-1]
    for j in range(FFT_G):
        y = jnp.concatenate([yr_ref[:, _lanes(j, c)], yi_ref[:, _lanes(j, c)]], axis=0)
        yg = jnp.concatenate([gr_ref[:, _lanes(j, c)], gi_ref[:, _lanes(j, c)]], axis=0)
        x = jnp.dot(f_ref[...], y, preferred_element_type=F32)
        g = jnp.dot(f_ref[...], yg, preferred_element_type=F32) * (1.0 / NFFT)
        xr, xi = x[:CH], x[CH:]
        gr, gi = g[:CH], g[CH:]
        p = jnp.concatenate([xr * gr - xi * gi, xr * gi + xi * gr], axis=0).astype(BF16)
        o_ref[j] = jnp.dot(fi_ref[...], p, preferred_element_type=F32).astype(BF16)


def _fft_a_inv_kernel(m_ref, ur_ref, ui_ref, o_ref):
    c = ur_ref.shape[-1] // FFT_G
    for j in range(FFT_G):
        u = jnp.concatenate([ur_ref[:, _lanes(j, c)], ui_ref[:, _lanes(j, c)]], axis=0)
        y = jnp.dot(m_ref[j], u, preferred_element_type=F32)
        o_ref[:, :, j, :] = y.reshape(2, ZCH, c)


def _dft_tables():
    n2 = np.arange(CH)[:, None, None]
    k1 = np.arange(CH)[None, :, None]

    def theta(n1_count):
        n1 = np.arange(n1_count)[None, None, :]
        return 2.0 * np.pi * ((k1 * (CH * n1 + n2)) % NFFT) / NFFT

    th = theta(ZCH)
    c, s = np.cos(th), np.sin(th)
    m_a = np.concatenate([np.concatenate([c, s], axis=2), np.concatenate([-s, c], axis=2)], axis=1)
    m_ainv = np.transpose(m_a, (0, 2, 1))
    th = theta(CH)
    m_af = np.concatenate([np.cos(th), -np.sin(th)], axis=1)
    a = 2.0 * np.pi * ((np.arange(CH)[:, None] * np.arange(CH)[None, :]) % CH) / CH
    c, s = np.cos(a), np.sin(a)
    f2 = np.block([[c, s], [-s, c]])
    f2i = np.block([[c, -s], [s, c]])
    f = lambda t: jnp.asarray(t.astype(np.float32)).astype(BF16)
    return f(m_a), f(m_ainv), f(m_af), f(f2), f(f2i)


def _fft_conv(zp, g):
    c = zp.shape[-1]
    m_a, m_ainv, m_af, f2, f2i = _dft_tables()
    cp = _cparams(("arbitrary",), VMEM_LIMIT_BIG)
    steps = CH // FFT_G
    gc = FFT_G * c
    full2 = pl.BlockSpec((2 * CH, 2 * CH), lambda i: (0, 0))
    col_re = pl.BlockSpec((CH, gc), lambda i: (0, i))
    col_im = pl.BlockSpec((CH, gc), lambda i: (0, steps + i))
    blk = pl.BlockSpec((FFT_G, 2 * CH, c), lambda i: (i, 0, 0))
    spec_shape = jax.ShapeDtypeStruct((CH, 2 * CH, c), BF16)

    yg = pl.pallas_call(
        _fft_a_filt_kernel, grid=(steps,),
        in_specs=[pl.BlockSpec((FFT_G, 2 * CH, CH), lambda i: (i, 0, 0)),
                  pl.BlockSpec((FFT_G * CH, c), lambda i: (i, 0))],
        out_specs=blk, out_shape=spec_shape, compiler_params=cp, name="fft_a_filt",
    )(m_af, g)
    yg2 = yg.reshape(CH, 2 * CH * c)

    y = pl.pallas_call(
        _fft_a_data_kernel, grid=(steps,),
        in_specs=[pl.BlockSpec((FFT_G, 2 * CH, 2 * ZCH), lambda i: (i, 0, 0)),
                  pl.BlockSpec((2, FFT_G * ZCH, c), lambda i: (0, i, 0))],
        out_specs=blk, out_shape=spec_shape, compiler_params=cp, name="fft_a_data",
    )(m_a, zp)
    y2 = y.reshape(CH, 2 * CH * c)
    u = pl.pallas_call(
        _fft_b_kernel, grid=(steps,),
        in_specs=[full2, full2, col_re, col_im, col_re, col_im],
        out_specs=blk, out_shape=spec_shape, compiler_params=cp, name="fft_b",
    )(f2, f2i, y2, y2, yg2, yg2)
    u2 = u.reshape(CH, 2 * CH * c)
    yc = pl.pallas_call(
        _fft_a_inv_kernel, grid=(steps,),
        in_specs=[pl.BlockSpec((FFT_G, 2 * ZCH, 2 * CH), lambda i: (i, 0, 0)), col_re, col_im],
        out_specs=pl.BlockSpec((2, ZCH, FFT_G, c), lambda i: (0, 0, i, 0)),
        out_shape=jax.ShapeDtypeStruct((2, ZCH, CH, c), F32),
        compiler_params=cp, name="fft_a_inv",
    )(m_ainv, u2, u2)
    return yc.reshape(2, ZCH * CH, c)


def _far_kernel(far_ref, zm_ref, zl_ref, o_ref):
    g_hi = far_ref[HALF + FAR:HALF + 2 * FAR, :]
    ef = far_ref[0:FAR, :] - g_hi
    g_lo = far_ref[FAR:2 * FAR, :]
    hb = far_ref[HALF:HALF + FAR, :]
    zm = zm_ref[0, PAD:CH, :].astype(F32)
    zl = zl_ref[0, PAD:CH, :].astype(F32)
    row = lambda a, i: a[i:i + 1, :]
    eb = [row(hb, 0) - row(g_hi, 0)] + [row(hb, m) - row(g_lo, FAR - m) for m in range(1, FAR)]
    for j in range(FAR):
        acc_f = row(ef, j) * row(zm, 0)
        for i in range(1, j + 1):
            acc_f = acc_f + row(ef, j - i) * row(zm, i)
        o_ref[0, 0, j:j + 1, :] = acc_f
        acc_b = eb[0] * row(zl, j)
        for i in range(j + 1, FAR):
            acc_b = acc_b + eb[i - j] * row(zl, i)
        o_ref[0, 1, j:j + 1, :] = acc_b


def _far_correction(h_far, z, tp):
    b, _, c = z.shape
    last = tp // CH - 1
    return pl.pallas_call(
        _far_kernel, grid=(b,),
        in_specs=[pl.BlockSpec((CH, c), lambda bi: (0, 0)),
                  pl.BlockSpec((1, CH, c), lambda bi: (bi, 0, 0)),
                  pl.BlockSpec((1, CH, c), lambda bi: (bi, last, 0))],
        out_specs=pl.BlockSpec((1, 2, FAR, c), lambda bi: (bi, 0, 0, 0)),
        out_shape=jax.ShapeDtypeStruct((b, 2, FAR, c), F32),
        compiler_params=_cparams(("arbitrary",)),
        name="hy_far",
    )(h_far, z, z)


MG_TM = 640
MG_CN = 256
ROUTER_W = 128
AFF_PACK = ROUTER_W // N_EXPERTS
ROW_SUB, ROW_LANE = 8, 128


def _merge_kernel(*refs, nt_b, nch):
    n_in = MG_TM // CH
    x_refs = refs[:n_in]
    (head_ref, og_ref, x0_ref, z_ref, yc_ref, ga_ref, gb_ref, corr_ref, skip_ref,
     wr_ref, wh_ref, wo_ref, g2_ref, wrt_ref,
     h1_ref, xn_ref, aff_ref, affp_ref, afft_ref, pre_ref, mix_ref, h1s_ref) = refs[n_in:]
    i = pl.program_id(0)
    for k in range(n_in):
        h1s_ref[k * CH:(k + 1) * CH, :] = _padded_chunk(i * n_in + k, nch, head_ref, x_refs[k])
    ib = i % nt_b
    pre_ref[...] = yc_ref[0].astype(F32) + z_ref[0].astype(F32) * skip_ref[...]

    @pl.when(ib == 0)
    def _():
        pre_ref[PAD:CH, :] += corr_ref[0, 1]

    @pl.when(ib == nt_b - 1)
    def _():
        pre_ref[MG_TM - FAR:MG_TM, :] += corr_ref[0, 0]

    chunks = [slice(c * MG_CN, (c + 1) * MG_CN) for c in range(D_MODEL // MG_CN)]
    og = og_ref[0]
    pre = (x0_ref[0].astype(F32) * pre_ref[...]).astype(BF16)
    for cols in chunks:
        ya = jnp.dot(og, wr_ref[:, cols], preferred_element_type=F32)
        yb = jnp.dot(pre, wh_ref[:, cols], preferred_element_type=F32)
        mix_ref[:, cols] = (ga_ref[0, :, cols].astype(F32) * ya + gb_ref[0, :, cols].astype(F32) * yb).astype(BF16)
    mixed = mix_ref[...]
    ss = jnp.zeros((MG_TM, 1), F32)
    for cols in chunks:
        h1 = h1s_ref[:, cols] + jnp.dot(mixed, wo_ref[:, cols], preferred_element_type=F32)
        h1s_ref[:, cols] = h1
        ss = ss + jnp.sum(h1 * h1, axis=-1, keepdims=True)
    rinv = lax.rsqrt(ss * (1.0 / D_MODEL) + RMS_EPS)
    h1 = h1s_ref[...]
    h1_ref[0] = h1.reshape(MG_TM, ROW_SUB, ROW_LANE)
    xn = h1 * rinv * g2_ref[...]
    xn_ref[0] = xn.reshape(MG_TM, ROW_SUB, ROW_LANE)
    logits = _dot3(xn, wrt_ref[0], wrt_ref[1])
    lane = lax.broadcasted_iota(I32, logits.shape, 1)
    logits = jnp.where(lane < N_EXPERTS, logits, -jnp.inf)
    m = jnp.max(logits, axis=-1, keepdims=True)
    e = jnp.exp(logits - m)
    aff = e / jnp.sum(e, axis=-1, keepdims=True)
    rows = lax.broadcasted_iota(I32, logits.shape, 0) + ib * MG_TM
    aff_ref[0] = jnp.where(rows >= PAD, aff, -1.0)
    lane_grp = lax.broadcasted_iota(I32, (MG_TM // AFF_PACK, ROUTER_W), 1) // N_EXPERTS
    packed = jnp.zeros((MG_TM // AFF_PACK, ROUTER_W), F32)
    for g in range(AFF_PACK):
        blk = aff_ref[0, pl.ds(g, MG_TM // AFF_PACK, stride=AFF_PACK), :]
        if g:
            blk = pltpu.roll(blk, N_EXPERTS * g, axis=1)
        packed = jnp.where(lane_grp == g, blk, packed)
    affp_ref[0] = packed
    afft_ref[0] = jnp.transpose(aff_ref[0])[:N_EXPERTS]


def _merge(x_flat, head, og, x0c, z, yc, proj3, corr, skip, w_ret, w_hy, w_o, g2, w_router_p, tp):
    b = og.shape[0]
    nt_b = tp // MG_TM
    nch = tp // CH
    n_in = MG_TM // CH
    n_x = x_flat.shape[0] // CH
    ga_blk0 = 6144 // D_MODEL
    row = lambda w: pl.BlockSpec((1, MG_TM, w), lambda i: (i // nt_b, i % nt_b, 0))
    full = lambda a: pl.BlockSpec(a.shape, lambda i: (0,) * a.ndim)
    x_spec = lambda k: pl.BlockSpec((CH, D_MODEL), lambda i: (_x_chunk_index(i * n_in + k, nch, n_x), 0))
    return pl.pallas_call(
        functools.partial(_merge_kernel, nt_b=nt_b, nch=nch),
        grid=(b * nt_b,),
        in_specs=[x_spec(k) for k in range(n_in)] + [
                  full(head), row(D_MODEL), row(D_MODEL), row(D_MODEL), row(D_MODEL),
                  pl.BlockSpec((1, MG_TM, D_MODEL), lambda i: (i // nt_b, i % nt_b, ga_blk0)),
                  pl.BlockSpec((1, MG_TM, D_MODEL), lambda i: (i // nt_b, i % nt_b, ga_blk0 + 1)),
                  pl.BlockSpec((1, 2, FAR, D_MODEL), lambda i: (i // nt_b, 0, 0, 0)),
                  full(skip), full(w_ret), full(w_hy), full(w_o), full(g2), full(w_router_p)],
        out_specs=[pl.BlockSpec((1, MG_TM, ROW_SUB, ROW_LANE), lambda i: (i // nt_b, i % nt_b, 0, 0)),
                   pl.BlockSpec((1, MG_TM, ROW_SUB, ROW_LANE), lambda i: (i // nt_b, i % nt_b, 0, 0)),
                   row(ROUTER_W),
                   pl.BlockSpec((1, MG_TM // AFF_PACK, ROUTER_W), lambda i: (i // nt_b, i % nt_b, 0)),
                   pl.BlockSpec((1, N_EXPERTS, MG_TM), lambda i: (i // nt_b, 0, i % nt_b))],
        out_shape=[jax.ShapeDtypeStruct((b, tp, ROW_SUB, ROW_LANE), F32),
                   jax.ShapeDtypeStruct((b, tp, ROW_SUB, ROW_LANE), F32),
                   jax.ShapeDtypeStruct((b, tp, ROUTER_W), F32),
                   jax.ShapeDtypeStruct((b, tp // AFF_PACK, ROUTER_W), F32),
                   jax.ShapeDtypeStruct((b, N_EXPERTS, tp), F32)],
        scratch_shapes=[pltpu.VMEM((MG_TM, D_MODEL), F32), pltpu.VMEM((MG_TM, D_MODEL), BF16),
                        pltpu.VMEM((MG_TM, D_MODEL), F32)],
        compiler_params=_cparams(("arbitrary",), VMEM_LIMIT_BIG),
        name="merge",
    )(*([x_flat] * n_in), head, og, x0c, z, yc, proj3, proj3, corr, skip, w_ret, w_hy, w_o, g2, w_router_p)


TK_W = 128


TK_UNROLL = 5


def _select_kernel(aff_ref, affp_ref, low_ref, slot_ref, offs_ref, *, cap, nch):
    def chunk(c):
        r0 = pl.multiple_of(c * CH, CH)
        return aff_ref[0, pl.ds(r0, CH), :]

    def count(pred):
        cnt = jnp.sum(pred(affp_ref[0]).astype(I32), axis=0, keepdims=True)
        for shift in (N_EXPERTS, 2 * N_EXPERTS, 4 * N_EXPERTS):
            cnt = cnt + pltpu.roll(cnt, shift, axis=1)
        return cnt

    def search(it, bits):
        cand = bits | jnp.left_shift(1, 29 - it)
        cand_f = pltpu.bitcast(cand, F32)
        return jnp.where(count(lambda a: a >= cand_f) >= cap, cand, bits)

    thr = pltpu.bitcast(lax.fori_loop(0, 30, search, jnp.zeros((1, ROUTER_W), I32)), F32)
    need = (cap - count(lambda a: a > thr)).astype(F32)
    low = low_ref[...]

    def scan(it, carry):
        c_eq, c_sel = carry
        for u in range(TK_UNROLL):
            c = it * TK_UNROLL + u
            a = chunk(c)
            eq = a == thr
            eq_f = eq.astype(F32)
            eq_rank = jnp.dot(low, eq_f, preferred_element_type=F32) + c_eq
            sel = jnp.logical_or(a > thr, jnp.logical_and(eq, eq_rank < need))
            sel_f = sel.astype(F32)
            slot = jnp.dot(low, sel_f, preferred_element_type=F32) + c_sel
            r0 = pl.multiple_of(c * CH, CH)
            slot_ref[0, pl.ds(r0, CH), :] = jnp.where(sel, slot, -1.0).astype(I32)
            offs_ref[0, c] = c_sel.astype(I32)
            c_eq = c_eq + jnp.sum(eq_f, axis=0, keepdims=True)
            c_sel = c_sel + jnp.sum(sel_f, axis=0, keepdims=True)
        return c_eq, c_sel

    zero = jnp.zeros((1, ROUTER_W), F32)
    lax.fori_loop(0, nch // TK_UNROLL, scan, (zero, zero))


def _extract_kernel(offs_ref, slot_ref, idx_ref, *, nch, nwin):
    idx_ref[...] = jnp.zeros_like(idx_ref)
    lane = lax.broadcasted_iota(I32, (CH, TK_W), 1)
    trow = lax.broadcasted_iota(I32, (CH, TK_W), 0)

    def per_chunk(c, carry):
        r0 = pl.multiple_of(c * CH, CH)
        slots = slot_ref[0, pl.ds(r0, CH), :]
        tpos = (trow + r0).astype(F32)
        for e in range(N_EXPERTS):
            col = jnp.broadcast_to(slots[:, e:e + 1], (CH, TK_W))
            w0 = offs_ref[0, 0, c * N_EXPERTS + e] // TK_W
            for dw in range(2):
                w = w0 + dw
                base = jnp.where(w < nwin, w * TK_W, -2 * TK_W)
                hit = col == lane + base
                row = e * nwin + jnp.minimum(w, nwin - 1)
                idx_ref[row] += jnp.sum(jnp.where(hit, tpos, 0.0), axis=0, keepdims=True)
        return carry

    lax.fori_loop(0, nch, per_chunk, 0)


def _topk(aff, aff_packed, cap, slots):
    b, tp, _ = aff.shape
    nch = tp // CH
    nwin = -(-slots // TK_W)
    low = jnp.asarray(np.tril(np.ones((CH, CH), np.float32), k=-1))
    pack = AFF_PACK
    assert nch % TK_UNROLL == 0 and tp % pack == 0
    slot, offs = pl.pallas_call(
        functools.partial(_select_kernel, cap=cap, nch=nch),
        grid=(b,),
        in_specs=[pl.BlockSpec((1, tp, ROUTER_W), lambda bi: (bi, 0, 0)),
                  pl.BlockSpec((1, tp // pack, ROUTER_W), lambda bi: (bi, 0, 0)),
                  pl.BlockSpec((CH, CH), lambda bi: (0, 0))],
        out_specs=[pl.BlockSpec((1, tp, ROUTER_W), lambda bi: (bi, 0, 0)),
                   pl.BlockSpec((1, nch, 1, ROUTER_W), lambda bi: (bi, 0, 0, 0))],
        out_shape=[jax.ShapeDtypeStruct((b, tp, ROUTER_W), I32),
                   jax.ShapeDtypeStruct((b, nch, 1, ROUTER_W), I32)],
        compiler_params=_cparams(("arbitrary",)),
        name="topk_select",
    )(aff, aff_packed, low)
    offs_s = offs[:, :, 0, :N_EXPERTS].reshape(b, 1, nch * N_EXPERTS)
    rows = pl.BlockSpec((None, N_EXPERTS * nwin, 1, TK_W), lambda bi: (bi, 0, 0, 0))
    out = jax.ShapeDtypeStruct((b, N_EXPERTS * nwin, 1, TK_W), F32)
    idx = pl.pallas_call(
        functools.partial(_extract_kernel, nch=nch, nwin=nwin),
        grid=(b,),
        in_specs=[pl.BlockSpec((1, 1, nch * N_EXPERTS), lambda bi: (bi, 0, 0), memory_space=pltpu.SMEM),
                  pl.BlockSpec((1, tp, ROUTER_W), lambda bi: (bi, 0, 0))],
        out_specs=rows,
        out_shape=out,
        compiler_params=_cparams(("arbitrary",)),
        name="topk_extract",
    )(offs_s, slot)
    idx = idx.reshape(b * N_EXPERTS, 1, nwin * TK_W)[:, :, :slots].astype(I32)
    return idx


FF_TF = 1024
MOE_CN = 256


MOE_UNROLL = 4


def _moe_kernel(idx_ref, idx_next_ref, xn_hbm, wg_ref, wu_ref, wd_ref, o_ref,
                xe32_ref, xe_ref, hid_ref, acc_ref, sem,
                *, slots, nf):
    i = pl.program_id(0)
    f = pl.program_id(1)
    buf = i % 2
    share = slots // nf

    def row_copy(ids_ref, s, b):
        return pltpu.make_async_copy(xn_hbm.at[ids_ref[0, 0, s]], xe32_ref.at[b, s], sem.at[b])

    def for_rows(lo, n, fn):
        def body(k, carry):
            for u in range(MOE_UNROLL):
                fn(lo + k * MOE_UNROLL + u)
            return carry

        lax.fori_loop(0, n // MOE_UNROLL, body, 0)

    @pl.when(jnp.logical_and(i == 0, f == 0))
    def _():
        for_rows(0, slots, lambda s: row_copy(idx_ref, s, 0).start())

    @pl.when(f == 0)
    def _():
        for_rows(0, slots, lambda s: row_copy(idx_ref, s, buf).wait())
        xe_ref[...] = xe32_ref[buf].reshape(slots, D_MODEL).astype(BF16)

    @pl.when(f == 0)
    def _():
        acc_ref[...] = jnp.zeros_like(acc_ref)

    n_up, n_down = FF_TF // MOE_CN, D_MODEL // MOE_CN
    per_chunk = share // (n_up + n_down)

    def prefetch(chunk):
        base = f * share + chunk * per_chunk
        for u in range(per_chunk):
            row_copy(idx_next_ref, base + u, 1 - buf).start()

    xe = xe_ref[...]
    for c in range(n_up):
        cols = slice(c * MOE_CN, (c + 1) * MOE_CN)
        gg = jnp.dot(xe, wg_ref[0, :, cols].astype(BF16), preferred_element_type=F32)
        uu = jnp.dot(xe, wu_ref[0, :, cols].astype(BF16), preferred_element_type=F32)
        hid_ref[:, cols] = (gg * jax.nn.sigmoid(gg) * uu).astype(BF16)
        prefetch(c)
    hid = hid_ref[...]
    for c in range(n_down):
        cols = slice(c * MOE_CN, (c + 1) * MOE_CN)
        part = jnp.dot(hid, wd_ref[0, :, cols].astype(BF16), preferred_element_type=F32)
        acc_ref[:, cols] = acc_ref[:, cols] + part
        prefetch(n_up + c)

    @pl.when(f == pl.num_programs(1) - 1)
    def _():
        o_ref[0] = acc_ref[...].reshape(slots, ROW_SUB, ROW_LANE)

    @pl.when(jnp.logical_and(i == pl.num_programs(0) - 1, f == pl.num_programs(1) - 1))
    def _():
        for_rows(0, slots, lambda s: row_copy(idx_next_ref, s, 1 - buf).wait())


def _moe_ffn(idx, xn_flat, w_gate, w_up, w_down, slots):
    be = idx.shape[0]
    nf = D_FF // FF_TF
    assert slots % MOE_UNROLL == 0 and slots % (nf * (FF_TF // MOE_CN + D_MODEL // MOE_CN)) == 0
    return pl.pallas_call(
        functools.partial(_moe_kernel, slots=slots, nf=nf),
        grid=(be, nf),
        in_specs=[pl.BlockSpec((1, 1, slots), lambda i, f: (i, 0, 0), memory_space=pltpu.SMEM),
                  pl.BlockSpec((1, 1, slots), lambda i, f: (jnp.minimum(i + 1, be - 1), 0, 0),
                               memory_space=pltpu.SMEM),
                  pl.BlockSpec(memory_space=pl.ANY),
                  pl.BlockSpec((1, D_MODEL, FF_TF), lambda i, f: (i % N_EXPERTS, 0, f)),
                  pl.BlockSpec((1, D_MODEL, FF_TF), lambda i, f: (i % N_EXPERTS, 0, f)),
                  pl.BlockSpec((1, FF_TF, D_MODEL), lambda i, f: (i % N_EXPERTS, f, 0))],
        out_specs=pl.BlockSpec((1, slots, ROW_SUB, ROW_LANE), lambda i, f: (i, 0, 0, 0)),
        out_shape=jax.ShapeDtypeStruct((be, slots, ROW_SUB, ROW_LANE), F32),
        scratch_shapes=[pltpu.VMEM((2, slots, ROW_SUB, ROW_LANE), F32), pltpu.VMEM((slots, D_MODEL), BF16),
                        pltpu.VMEM((slots, FF_TF), BF16), pltpu.VMEM((slots, D_MODEL), F32),
                        pltpu.SemaphoreType.DMA((2,))],
        compiler_params=_cparams(("arbitrary", "arbitrary"), VMEM_LIMIT_BIG),
        name="moe_ffn",
    )(idx, idx, xn_flat, w_gate, w_up, w_down)


CB_U = 6
CB_NB = 512


def _combine_kernel(idx_ref, gate_ref, h1_hbm, ye_ref, g_ref, o_hbm, acc_ref, stage_ref, sem, osem, *, cap, tp):
    b = pl.program_id(0)
    e = pl.program_id(1)

    @pl.when(e == 0)
    def _():
        cp = pltpu.make_async_copy(h1_hbm.at[b], acc_ref, sem)
        cp.start()
        cp.wait()

    def rmw(g, carry):
        ts = [idx_ref[0, 0, g * CB_U + u] for u in range(CB_U)]
        vals = [acc_ref[ts[u]] + gate_ref[0, 0, ts[u]] * ye_ref[g * CB_U + u] for u in range(CB_U)]
        for u in range(CB_U):
            acc_ref[ts[u]] = vals[u]
        return carry

    lax.fori_loop(0, cap // CB_U, rmw, 0)

    @pl.when(e == pl.num_programs(1) - 1)
    def _():
        gamma = g_ref[...]
        n_blocks = (tp - CH) // CB_NB

        def out_copy(k):
            return pltpu.make_async_copy(stage_ref.at[k % 2], o_hbm.at[b, pl.ds(k * CB_NB, CB_NB), :],
                                         osem.at[k % 2])

        for k in range(n_blocks):
            x = acc_ref[pl.ds(CH + k * CB_NB, CB_NB)]
            ms = jnp.sum(jnp.sum(x * x, axis=2, keepdims=True), axis=1, keepdims=True) * (1.0 / D_MODEL)
            y = x * lax.rsqrt(ms + RMS_EPS) * gamma
            if k >= 2:
                out_copy(k - 2).wait()
            stage_ref[k % 2] = y.reshape(CB_NB, D_MODEL)
            out_copy(k).start()
        for k in range(max(n_blocks - 2, 0), n_blocks):
            out_copy(k).wait()


def _combine(idx_local, gate, h1, ye, gf, cap, slots):
    b, tp = h1.shape[:2]
    assert cap % CB_U == 0 and (tp - CH) % CB_NB == 0
    return pl.pallas_call(
        functools.partial(_combine_kernel, cap=cap, tp=tp),
        grid=(b, N_EXPERTS),
        in_specs=[pl.BlockSpec((1, 1, slots), lambda bi, e: (bi * N_EXPERTS + e, 0, 0), memory_space=pltpu.SMEM),
                  pl.BlockSpec((1, 1, tp), lambda bi, e: (bi * N_EXPERTS + e, 0, 0), memory_space=pltpu.SMEM),
                  pl.BlockSpec(memory_space=pl.ANY),
                  pl.BlockSpec((None, slots, ROW_SUB, ROW_LANE), lambda bi, e: (bi * N_EXPERTS + e, 0, 0, 0)),
                  pl.BlockSpec((1, ROW_SUB, ROW_LANE), lambda bi, e: (0, 0, 0))],
        out_specs=pl.BlockSpec(memory_space=pl.ANY),
        out_shape=jax.ShapeDtypeStruct((b, tp - CH, D_MODEL), F32),
        scratch_shapes=[pltpu.VMEM((tp, ROW_SUB, ROW_LANE), F32), pltpu.VMEM((2, CB_NB, D_MODEL), F32),
                        pltpu.SemaphoreType.DMA(()), pltpu.SemaphoreType.DMA((2,))],
        compiler_params=_cparams(("arbitrary", "arbitrary"), VMEM_LIMIT_BIG),
        name="combine",
    )(idx_local, gate, h1, ye, gf.reshape(1, ROW_SUB, ROW_LANE))


def _rope_tables(tp, b):
    half = RET_QK_HEAD // 2
    pos = np.arange(tp, dtype=np.float64) - PAD
    inv = ROPE_BASE ** (-np.arange(half, dtype=np.float64) / half)
    ang = pos[:, None] * inv[None, :]
    cos, sin = np.cos(ang), np.sin(ang)
    tile = lambda t: jnp.asarray(np.tile(t, (b, 1)).astype(np.float32))
    return tile(np.concatenate([cos, cos], axis=1)), tile(np.concatenate([-sin, sin], axis=1))


def _filter_features(t_len):
    half = NFFT // 2
    q = np.arange(NFFT)
    r = CH * (q % CH) + q // CH
    p_main = np.where(r < half, r, NFFT - r)
    valid_main = (r != half).astype(np.float32)
    m = np.arange(FAR)
    pad = np.zeros(HALF - 2 * FAR, np.int64)
    p_far = np.concatenate([half + m, half - FAR + m, pad, half + m, half - m, pad])
    valid_far = np.ones(CH, np.float32)
    valid_far[HALF + FAR] = 0.0
    p = np.concatenate([p_main, p_far]).astype(np.float64)
    valid = np.concatenate([valid_main, valid_far]).astype(np.float64)
    t_norm = p / (t_len - 1)
    bands = (HY_EMB_DIM - 1) // 2
    fr = np.linspace(1e-4, bands - 1, bands)
    ang = (2.0 * math.pi * p / t_len)[:, None] * fr[None, :]
    feat = np.concatenate([t_norm[:, None], np.cos(ang), -np.sin(ang), valid[:, None]], axis=-1)
    feat = np.pad(feat, ((0, 0), (0, FEAT_W - feat.shape[1])))
    pairs = feat.reshape(-1, 2, HALF, FEAT_W).transpose(0, 2, 1, 3).reshape(-1, 2 * FEAT_W)
    return jnp.asarray(pairs.astype(np.float32))


def kernel(x, meta_tokens, norm1_g, w_in, ret_decay_fwd, ret_decay_bwd, ret_head_norm_g, w_ret_out,
           hy_conv_w, hy_conv_b, hy_filt_w1, hy_filt_b1, hy_filt_w2, hy_filt_b2, hy_filt_w3, hy_filt_b3,
           hy_filt_freq, hy_filt_w4, hy_skip, w_hy_out, w_o, norm2_g, w_router, w_exp_gate, w_exp_up,
           w_exp_down, final_norm_g):
    b, seq, d = x.shape
    t_len = seq + N_META
    tp = PAD + t_len
    assert d == D_MODEL and (b * tp) % IP_TM == 0 and tp % MG_TM == 0 and tp % CH == 0 and t_len - NFFT // 2 == FAR
    cap = EC_CAPACITY * t_len // N_EXPERTS
    slots = -(-cap // 16) * 16
    l = 0

    assert seq % CH == 0
    head = jnp.concatenate([jnp.zeros((PAD, d), x.dtype), meta_tokens.astype(x.dtype)], axis=0)
    x_flat = x.reshape(b * seq, d)

    cs, sn = _rope_tables(tp, b)
    proj = _in_proj(x_flat, head, norm1_g[l][None], w_in[l], cs, sn, tp, b)
    proj3 = proj.reshape(b, tp, IN_PROJ_W)

    lf = jax.nn.log_sigmoid(ret_decay_fwd[l].astype(F32))
    lb = jax.nn.log_sigmoid(ret_decay_bwd[l].astype(F32))
    og = _retention(proj3, lf, lb, ret_head_norm_g[l][None], tp)

    x0c, z, zp = _hy_prep(proj3, hy_conv_w[l], hy_conv_b[l][None], tp)

    feat = _filter_features(t_len)
    w1p = jnp.pad(hy_filt_w1[l].astype(F32), ((0, FEAT_W - HY_EMB_DIM), (0, 0)))
    max_decay = math.log(HY_DECAY_TARGET) / HY_FAST_DECAY_PCT
    min_decay = math.log(HY_DECAY_TARGET) / HY_SLOW_DECAY_PCT
    dl = jnp.abs(jnp.linspace(min_decay, max_decay, D_MODEL, dtype=F32))[None]
    fargs = (w1p, hy_filt_b1[l][None].astype(F32), hy_filt_w2[l].astype(F32), hy_filt_b2[l][None].astype(F32),
             hy_filt_w3[l].astype(F32), hy_filt_b3[l][None].astype(F32), hy_filt_freq[l][None].astype(F32),
             hy_filt_w4[l].astype(F32), dl)
    g = _filters(feat[:NFFT // 2], *fargs, FT_ROWS)
    h_far = _filters(feat[NFFT // 2:], *fargs, CH)
    yc = _fft_conv(zp, g)
    corr = _far_correction(h_far, z, tp)

    w_router_p = jnp.pad(w_router[l].astype(F32), ((0, 0), (0, ROUTER_W - N_EXPERTS)))
    w_router_p = _split_hi_lo(w_router_p)
    h1, xn2, aff, aff_packed, aff_t = _merge(x_flat, head, og, x0c, z, yc, proj3, corr, hy_skip[l][None].astype(F32),
                          w_ret_out[l].astype(BF16), w_hy_out[l].astype(BF16), w_o[l].astype(BF16),
                          norm2_g[l][None].astype(F32), w_router_p, tp)

    idx = _topk(aff, aff_packed, cap, slots)
    live = (jnp.arange(slots) < cap)[None, None, :]
    idx_local = jnp.where(live, idx, PAD)
    idx_flat = idx_local + (jnp.arange(b * N_EXPERTS, dtype=I32) // N_EXPERTS * tp)[:, None, None]
    ye = _moe_ffn(idx_flat, xn2.reshape(b * tp, ROW_SUB, ROW_LANE),
                  w_exp_gate[l], w_exp_up[l], w_exp_down[l], slots)
    return _combine(idx_local, aff_t.reshape(b * N_EXPERTS, 1, tp), h1, ye, final_norm_g.astype(F32), cap, slots)
```
